```python
import math
import jax
import jax.numpy as jnp
from jax import lax
import numpy as np

D_MODEL = 1024
BATCH = 2
SEQ = 8192
DEPTH = 2
DEC_BATCH = 32
DEC_SEQ = 4
PAST_LEN = 8192
PAGE_SIZE = 128

HEAD_DIM = 64
NSA_HEADS = 8
FOX_HEADS = 8
MIX_WIDTH = (NSA_HEADS + FOX_HEADS) * HEAD_DIM
NSA_KV_HEADS = 2
NSA_GROUP = NSA_HEADS // NSA_KV_HEADS
CMP_BLOCK = 32
CMP_STRIDE = 16
CMP_HIDDEN = 2 * HEAD_DIM
SLC_BLOCK = 64
N_SELECT = 16
WINDOW = 512
T5_BUCKETS = 32
T5_EXACT = T5_BUCKETS // 2
T5_MAX_DIST = 128
D_FF = -(-8 * D_MODEL // (3 * 256)) * 256
Q_BLOCK = 128
NORM_EPS = 1e-6
NEG_INF = -1e30
FORCE_SCORE = 1e4
FORGET_BIAS_INIT = 3.0
IN_SIZES = (NSA_HEADS * HEAD_DIM,) + (NSA_KV_HEADS * HEAD_DIM,) * 6 + (3 * NSA_HEADS,) + (FOX_HEADS * HEAD_DIM,) * 3 + (FOX_HEADS,)
IN_COLS = sum(IN_SIZES)

kernel_name = 'nsa_fox_hybrid_step'


def rms_norm(x, g):
    xf = x.astype(jnp.float32)
    y = xf * lax.rsqrt(jnp.mean(xf * xf, axis=-1, keepdims=True) + NORM_EPS)
    return (y * g.astype(jnp.float32)).astype(x.dtype)


def masked_softmax(logits, mask):
    p = jax.nn.softmax(jnp.where(mask, logits, NEG_INF), axis=-1)
    return jnp.where(mask, p, 0.0)


def t5_bucket(dist):
    n = jnp.maximum(dist, 0)
    nf = jnp.maximum(n, 1).astype(jnp.float32)
    large = T5_EXACT + (jnp.log(nf / T5_EXACT) / math.log(T5_MAX_DIST / T5_EXACT) * (T5_BUCKETS - T5_EXACT)).astype(jnp.int32)
    large = jnp.minimum(large, T5_BUCKETS - 1)
    return jnp.where(n < T5_EXACT, n, large)


def t5_bias_2d(dist, rel_bias):
    b = rel_bias[t5_bucket(dist)].astype(jnp.float32)
    b = b.reshape(dist.shape + (NSA_KV_HEADS, NSA_GROUP))
    return jnp.transpose(b, (2, 3, 0, 1))


def project(h, w_in, b_gate, b_forget):
    B, T, _ = h.shape
    z = jnp.einsum('btd,dc->btc', h, w_in)
    splits = np.cumsum(IN_SIZES)[:-1].tolist()
    q_n, k_c, v_c, k_s, v_s, k_w, v_w, g, q_f, k_f, v_f, f = jnp.split(z, splits, axis=-1)
    nsa_kv = lambda a: a.reshape(B, T, NSA_KV_HEADS, HEAD_DIM)
    fox = lambda a: a.reshape(B, T, FOX_HEADS, HEAD_DIM)
    gates = jax.nn.sigmoid(g.reshape(B, T, 3, NSA_HEADS) + b_gate)
    logf = jax.nn.log_sigmoid((f + b_forget).astype(jnp.float32))
    return (q_n.reshape(B, T, NSA_HEADS, HEAD_DIM), nsa_kv(k_c), nsa_kv(v_c), nsa_kv(k_s), nsa_kv(v_s),
            nsa_kv(k_w), nsa_kv(v_w), gates, fox(q_f), fox(k_f), fox(v_f), logf)


def compress(k, pe, w1, w2):
    B, L, G, Dh = k.shape
    n_chunks = L // CMP_STRIDE
    r = CMP_BLOCK // CMP_STRIDE
    n_cmp = n_chunks - r + 1
    c = k[:, : n_chunks * CMP_STRIDE].reshape(B, n_chunks, CMP_STRIDE, G, Dh)
    blk = jnp.concatenate([c[:, i:i + n_cmp] for i in range(r)], axis=2)
    blk = blk + pe[None, None, :, None, :]
    flat = jnp.transpose(blk, (0, 1, 3, 2, 4)).reshape(B, n_cmp, G, CMP_BLOCK * Dh)
    return jax.nn.gelu(flat @ w1) @ w2


def to_blocks(k):
    B, L, G, Dh = k.shape
    n_slc = -(-L // SLC_BLOCK)
    k = jnp.pad(k, ((0, 0), (0, n_slc * SLC_BLOCK - L), (0, 0), (0, 0)))
    return jnp.transpose(k.reshape(B, n_slc, SLC_BLOCK, G, Dh), (0, 3, 1, 2, 4))


def cmp_to_slc(n_cmp, n_slc):
    start = jnp.arange(n_cmp, dtype=jnp.int32) * CMP_STRIDE
    j0 = jnp.arange(n_slc, dtype=jnp.int32) * SLC_BLOCK
    return ((start[:, None] < j0[None, :] + SLC_BLOCK) & (start[:, None] + CMP_BLOCK > j0[None, :])).astype(jnp.float32)


def nsa_full_keys(k_c, v_c, k_s, v_s, cmp_pe, cmp_w1, cmp_w2):
    kc = compress(k_c, cmp_pe[0], cmp_w1[0], cmp_w2[0])
    vc = compress(v_c, cmp_pe[1], cmp_w1[1], cmp_w2[1])
    n_cmp = kc.shape[1]
    c_end = jnp.arange(n_cmp, dtype=jnp.int32) * CMP_STRIDE + (CMP_BLOCK - 1)
    ksb, vsb = to_blocks(k_s), to_blocks(v_s)
    ovl = cmp_to_slc(n_cmp, ksb.shape[2])
    return kc, vc, c_end, ovl, ksb, vsb


def attend_block(t_pos, q_n, gates, q_f, cq, kw, vw, w_pos, kc, vc, c_end, ovl, ksb, vsb, kf, vf, ck, rel_bias):
    B, Q = q_n.shape[:2]
    G, R = NSA_KV_HEADS, NSA_GROUP
    scale = HEAD_DIM ** -0.5
    qg = q_n.reshape(B, Q, G, R, HEAD_DIM)

    dist_c = t_pos[:, None] - c_end[None, :]
    lc = jnp.einsum('bqgrd,bcgd->bgrqc', qg, kc).astype(jnp.float32) * scale + t5_bias_2d(dist_c, rel_bias)
    pc = masked_softmax(lc, dist_c >= 0)
    o_c = jnp.einsum('bgrqc,bcgd->bqgrd', pc.astype(vc.dtype), vc)

    n_slc = ksb.shape[2]
    n_top = min(N_SELECT, n_slc)
    imp = jnp.einsum('bgrqc,cs->bgqs', pc, ovl)
    j = jnp.arange(n_slc, dtype=jnp.int32)[None, :]
    cur = (t_pos // SLC_BLOCK)[:, None]
    forced = (j == 0) | (j == cur) | (j == cur - 1)
    score = jnp.where(forced, FORCE_SCORE, imp)
    score = jnp.where(j * SLC_BLOCK > t_pos[:, None], -FORCE_SCORE, score)
    _, idx = lax.top_k(score, n_top)

    bi = jnp.arange(B)[:, None, None, None]
    gi = jnp.arange(G)[None, :, None, None]
    M = n_top * SLC_BLOCK
    ks_sel = ksb[bi, gi, idx].reshape(B, G, Q, M, HEAD_DIM)
    vs_sel = vsb[bi, gi, idx].reshape(B, G, Q, M, HEAD_DIM)
    s_pos = (idx[..., None] * SLC_BLOCK + jnp.arange(SLC_BLOCK, dtype=jnp.int32)).reshape(B, G, Q, M)
    dist_s = t_pos[None, None, :, None] - s_pos
    bias_s = jnp.moveaxis(rel_bias.reshape(T5_BUCKETS, G, R)[t5_bucket(dist_s), gi], -1, 2)
    ls = jnp.einsum('bqgrd,bgqmd->bgrqm', qg, ks_sel).astype(jnp.float32) * scale + bias_s.astype(jnp.float32)
    ps = masked_softmax(ls, (dist_s >= 0)[:, :, None])
    o_s = jnp.einsum('bgrqm,bgqmd->bqgrd', ps.astype(vs_sel.dtype), vs_sel)

    dist_w = t_pos[:, None] - w_pos[None, :]
    valid_w = (dist_w >= 0) & (dist_w < WINDOW) & (w_pos[None, :] >= 0)
    lw = jnp.einsum('bqgrd,bkgd->bgrqk', qg, kw).astype(jnp.float32) * scale + t5_bias_2d(dist_w, rel_bias)
    pw = masked_softmax(lw, valid_w)
    o_w = jnp.einsum('bgrqk,bkgd->bqgrd', pw.astype(vw.dtype), vw)

    g = gates.reshape(B, Q, 3, G, R)[..., None]
    o_n = (g[:, :, 0] * o_c + g[:, :, 1] * o_s + g[:, :, 2] * o_w).reshape(B, Q, NSA_HEADS * HEAD_DIM)

    k_pos = jnp.arange(kf.shape[1], dtype=jnp.int32)
    decay = jnp.transpose(cq, (0, 2, 1))[..., None] - jnp.transpose(ck, (0, 2, 1))[:, :, None, :]
    lf = jnp.einsum('bqhd,bkhd->bhqk', q_f, kf).astype(jnp.float32) * scale + decay
    pf = masked_softmax(lf, k_pos[None, :] <= t_pos[:, None])
    o_f = jnp.einsum('bhqk,bkhd->bqhd', pf.astype(vf.dtype), vf).reshape(B, Q, FOX_HEADS * HEAD_DIM)
    return o_n, o_f


def mixer_prompt(h, w_in, b_gate, b_forget, cmp_pe, cmp_w1, cmp_w2, rel_bias):
    B, T, _ = h.shape
    q_n, k_c, v_c, k_s, v_s, k_w, v_w, gates, q_f, k_f, v_f, logf = project(h, w_in, b_gate, b_forget)
    kc, vc, c_end, ovl, ksb, vsb = nsa_full_keys(k_c, v_c, k_s, v_s, cmp_pe, cmp_w1, cmp_w2)
    pad = ((0, 0), (WINDOW, 0), (0, 0), (0, 0))
    kw_pad, vw_pad = jnp.pad(k_w, pad), jnp.pad(v_w, pad)
    ck = jnp.cumsum(logf, axis=1)
    q_blk = min(Q_BLOCK, T)

    def one_block(i):
        q0 = i * q_blk
        t_pos = q0 + jnp.arange(q_blk, dtype=jnp.int32)
        rows = lambda a: lax.dynamic_slice_in_dim(a, q0, q_blk, axis=1)
        band = lambda a: lax.dynamic_slice_in_dim(a, q0, WINDOW + q_blk, axis=1)
        w_pos = q0 - WINDOW + jnp.arange(WINDOW + q_blk, dtype=jnp.int32)
        return attend_block(t_pos, rows(q_n), rows(gates), rows(q_f), rows(ck), band(kw_pad), band(vw_pad), w_pos,
                            kc, vc, c_end, ovl, ksb, vsb, k_f, v_f, ck, rel_bias)

    o_n, o_f = lax.map(one_block, jnp.arange(T // q_blk, dtype=jnp.int32))
    o_n = jnp.swapaxes(o_n, 0, 1).reshape(B, T, NSA_HEADS * HEAD_DIM)
    o_f = jnp.swapaxes(o_f, 0, 1).reshape(B, T, FOX_HEADS * HEAD_DIM)
    n_win = min(WINDOW, T)
    state = (jnp.stack([k_c, v_c, k_s, v_s], axis=2), jnp.stack([k_f, v_f], axis=2), logf,
             jnp.stack([k_w, v_w], axis=2)[:, T - n_win:])
    return o_n, o_f, state


def mixer_sample(h, past_nsa, past_fox, past_logf, win_buf, w_in, b_gate, b_forget, cmp_pe, cmp_w1, cmp_w2, rel_bias):
    B, T, _ = h.shape
    P = past_nsa.shape[1]
    n_win = win_buf.shape[1]
    q_n, k_c, v_c, k_s, v_s, k_w, v_w, gates, q_f, k_f, v_f, logf = project(h, w_in, b_gate, b_forget)
    cat = lambda past, new: jnp.concatenate([past.astype(new.dtype), new], axis=1)
    kc, vc, c_end, ovl, ksb, vsb = nsa_full_keys(cat(past_nsa[:, :, 0], k_c), cat(past_nsa[:, :, 1], v_c),
                                                 cat(past_nsa[:, :, 2], k_s), cat(past_nsa[:, :, 3], v_s),
                                                 cmp_pe, cmp_w1, cmp_w2)
    kw = cat(win_buf[:, :, 0], k_w)
    vw = cat(win_buf[:, :, 1], v_w)
    w_pos = P - n_win + jnp.arange(n_win + T, dtype=jnp.int32)
    t_pos = P + jnp.arange(T, dtype=jnp.int32)
    ck = jnp.cumsum(cat(past_logf, logf), axis=1)
    o_n, o_f = attend_block(t_pos, q_n, gates, q_f, ck[:, P:], kw, vw, w_pos, kc, vc, c_end, ovl, ksb, vsb,
                            cat(past_fox[:, :, 0], k_f), cat(past_fox[:, :, 1], v_f), ck, rel_bias)
    state = (jnp.stack([k_c, v_c, k_s, v_s], axis=2), jnp.stack([k_f, v_f], axis=2), logf,
             jnp.stack([kw, vw], axis=2)[:, T:])
    return o_n, o_f, state


def finish_layer(x, o_n, o_f, g_n, g_f, w_o, g_post_mix, g_pre_ffn, g_post_ffn, w_gate, w_up, w_down):
    mixed = jnp.concatenate([rms_norm(o_n, g_n), rms_norm(o_f, g_f)], axis=-1) @ w_o
    x = x + rms_norm(mixed, g_post_mix)
    hf = rms_norm(x, g_pre_ffn)
    ffn = (jax.nn.silu(hf @ w_gate) * (hf @ w_up)) @ w_down
    return x + rms_norm(ffn, g_post_ffn)


def setup_inputs(seed: int = 0) -> dict:
    key = jax.random.key(seed)
    ks = jax.random.split(key, 32)

    def nrm(k, shape, scale=1.0):
        return scale * jax.random.normal(k, shape, jnp.float32)

    def gain(k, shape):
        return 1.0 + nrm(k, shape, 0.05)

    n_pages = PAST_LEN // PAGE_SIZE
    n_pool = (5 * DEC_BATCH * n_pages) // 4
    n_win = min(WINDOW, PAST_LEN)
    page_table = jax.random.permutation(ks[6], n_pool)[: DEC_BATCH * n_pages].reshape(DEC_BATCH, n_pages).astype(jnp.int32)
    return {
        'x_prompt': nrm(ks[0], (BATCH, SEQ, D_MODEL)),
        'x_sample': nrm(ks[1], (DEC_BATCH, DEC_SEQ, D_MODEL)),
        'cache_nsa_kv': nrm(ks[2], (DEPTH, n_pool, PAGE_SIZE, 4, NSA_KV_HEADS, HEAD_DIM)),
        'cache_fox_kv': nrm(ks[3], (DEPTH, n_pool, PAGE_SIZE, 2, FOX_HEADS, HEAD_DIM)),
        'cache_fox_logf': jax.nn.log_sigmoid(FORGET_BIAS_INIT + nrm(ks[4], (DEPTH, n_pool, PAGE_SIZE, FOX_HEADS))),
        'state_win_kv': nrm(ks[5], (DEPTH, DEC_BATCH, n_win, 2, NSA_KV_HEADS, HEAD_DIM)),
        'page_table': page_table,
        'rel_bias': nrm(ks[7], (T5_BUCKETS, NSA_HEADS), 0.5),
        'norm_mix_pre': gain(ks[8], (DEPTH, D_MODEL)),
        'norm_mix_post': gain(ks[9], (DEPTH, D_MODEL)),
        'norm_ffn_pre': gain(ks[10], (DEPTH, D_MODEL)),
        'norm_ffn_post': gain(ks[11], (DEPTH, D_MODEL)),
        'w_in': nrm(ks[12], (DEPTH, D_MODEL, IN_COLS), D_MODEL ** -0.5),
        'b_gate': nrm(ks[13], (DEPTH, 3, NSA_HEADS), 0.1),
        'b_forget': FORGET_BIAS_INIT + nrm(ks[14], (DEPTH, FOX_HEADS), 0.5),
        'cmp_pe': nrm(ks[15], (DEPTH, 2, CMP_BLOCK, HEAD_DIM), 0.1),
        'cmp_w1': nrm(ks[16], (DEPTH, 2, CMP_BLOCK * HEAD_DIM, CMP_HIDDEN), (CMP_BLOCK * HEAD_DIM) ** -0.5),
        'cmp_w2': nrm(ks[17], (DEPTH, 2, CMP_HIDDEN, HEAD_DIM), CMP_HIDDEN ** -0.5),
        'grp_norm_nsa': gain(ks[18], (DEPTH, NSA_HEADS * HEAD_DIM)),
        'grp_norm_fox': gain(ks[19], (DEPTH, FOX_HEADS * HEAD_DIM)),
        'w_o': nrm(ks[20], (DEPTH, MIX_WIDTH, D_MODEL), MIX_WIDTH ** -0.5),
        'w_ffn_gate': nrm(ks[21], (DEPTH, D_MODEL, D_FF), D_MODEL ** -0.5),
        'w_ffn_up': nrm(ks[22], (DEPTH, D_MODEL, D_FF), D_MODEL ** -0.5),
        'w_ffn_down': nrm(ks[23], (DEPTH, D_FF, D_MODEL), D_FF ** -0.5),
    }


def reference(x_prompt, x_sample, cache_nsa_kv, cache_fox_kv, cache_fox_logf, state_win_kv, page_table,
              rel_bias, norm_mix_pre, norm_mix_post, norm_ffn_pre, norm_ffn_post, w_in, b_gate, b_forget,
              cmp_pe, cmp_w1, cmp_w2, grp_norm_nsa, grp_norm_fox, w_o, w_ffn_gate, w_ffn_up, w_ffn_down):
    n_seq, n_pages = page_table.shape

    def gather_pages(cache):
        return cache[page_table].reshape((n_seq, n_pages * PAGE_SIZE) + cache.shape[2:])

    xp, xs = x_prompt, x_sample
    st_p, st_s = [], []
    for l in range(DEPTH):
        o_n, o_f, s_p = mixer_prompt(rms_norm(xp, norm_mix_pre[l]), w_in[l], b_gate[l], b_forget[l],
                                     cmp_pe[l], cmp_w1[l], cmp_w2[l], rel_bias)
        xp = finish_layer(xp, o_n, o_f, grp_norm_nsa[l], grp_norm_fox[l], w_o[l], norm_mix_post[l],
                          norm_ffn_pre[l], norm_ffn_post[l], w_ffn_gate[l], w_ffn_up[l], w_ffn_down[l])
        o_n, o_f, s_s = mixer_sample(rms_norm(xs, norm_mix_pre[l]), gather_pages(cache_nsa_kv[l]),
                                     gather_pages(cache_fox_kv[l]), gather_pages(cache_fox_logf[l]),
                                     state_win_kv[l], w_in[l], b_gate[l], b_forget[l],
                                     cmp_pe[l], cmp_w1[l], cmp_w2[l], rel_bias)
        xs = finish_layer(xs, o_n, o_f, grp_norm_nsa[l], grp_norm_fox[l], w_o[l], norm_mix_post[l],
                          norm_ffn_pre[l], norm_ffn_post[l], w_ffn_gate[l], w_ffn_up[l], w_ffn_down[l])
        st_p.append(s_p)
        st_s.append(s_s)

    stack = lambda states, i: jnp.stack([s[i] for s in states], axis=0)
    return (xp, xs, stack(st_p, 0), stack(st_p, 1), stack(st_p, 2), stack(st_p, 3),
            stack(st_s, 0), stack(st_s, 1), stack(st_s, 2), stack(st_s, 3))
```

```python
import functools
import math

import numpy as np
import jax
import jax.numpy as jnp
from jax import lax
from jax.experimental import pallas as pl
from jax.experimental.pallas import tpu as pltpu

HEAD_DIM = 64
NSA_HEADS = 8
FOX_HEADS = 8
NSA_KV_HEADS = 2
NSA_GROUP = NSA_HEADS // NSA_KV_HEADS
CMP_BLOCK = 32
CMP_STRIDE = 16
CMP_HIDDEN = 2 * HEAD_DIM
SLC_BLOCK = 64
N_SELECT = 16
WINDOW = 512
T5_BUCKETS = 32
T5_EXACT = T5_BUCKETS // 2
T5_MAX_DIST = 128
PAGE_SIZE = 128
NORM_EPS = 1e-6
FORCE_SCORE = 1e4
SCALE = HEAD_DIM ** -0.5

LANES = 128
SUBLANES = 8
VMEM_LIMIT = 56 * 1024 * 1024

_F32 = jnp.float32
_BF = jnp.bfloat16
NEG = -(2.0 ** 100)
NEG_HALF = -(2.0 ** 99)
M_INIT = -3.0e38
REMOVED = -3.4e38

def _t5_thresholds():
    n = np.arange(1, 4 * T5_MAX_DIST)
    large = T5_EXACT + (np.log(n / T5_EXACT) / math.log(T5_MAX_DIST / T5_EXACT) * (T5_BUCKETS - T5_EXACT)).astype(np.int64)
    return tuple(int(n[np.argmax(large >= k)]) for k in range(T5_EXACT + 1, T5_BUCKETS))


_T5_THR = _t5_thresholds()
T5_FAR = _T5_THR[-1]


def _log2(n):
    assert n & (n - 1) == 0
    return n.bit_length() - 1


TQ = 256
TQF = 512
TR_PROJ = 256
TR_POST = 512
TR_FFN = 256
CPT = TQ // CMP_STRIDE
NEAR_BACK = -(-(T5_FAR + CMP_BLOCK - 1) // CMP_STRIDE) - 1
NEAR_U = CPT + NEAR_BACK

C_QN, C_NSA, C_KW, C_QF, C_FOX, C_MISC = 0, 512, 1024, 1280, 1792, 2816
C_TOT = 2944
L_GATE, L_LOGF, L_CUM = 0, 24, 32


def _dot(a, b):
    return jnp.dot(a, b, preferred_element_type=_F32)


def _dot_nt(a, b):
    return lax.dot_general(a, b, (((1,), (1,)), ((), ())), preferred_element_type=_F32)


def _split3(x):
    hi = x.astype(_BF)
    r1 = x - hi.astype(_F32)
    mid = r1.astype(_BF)
    lo = (r1 - mid.astype(_F32)).astype(_BF)
    return hi, mid, lo


def _dot3(a_bf, x):
    hi, mid, lo = _split3(x)
    return _dot(a_bf, hi) + _dot(a_bf, mid) + _dot(a_bf, lo)


def _rms(x, g):
    ms = jnp.mean(x * x, axis=-1, keepdims=True)
    return x * lax.rsqrt(ms + NORM_EPS) * g


def _gelu_tanh(x):
    c = math.sqrt(2.0 / math.pi)
    return x * (0.5 * (1.0 + jnp.tanh(c * (x + 0.044715 * (x * x * x)))))


def _iota(shape, dim):
    return lax.broadcasted_iota(jnp.int32, shape, dim)


def _div(x, n):
    return jnp.right_shift(x, _log2(n))


def _mod(x, n):
    return x & (n - 1)


def _cparams(sem):
    return pltpu.CompilerParams(dimension_semantics=sem, vmem_limit_bytes=VMEM_LIMIT)


def _t5_rel(dist, rel_ref, h):
    d = jnp.minimum(dist, T5_MAX_DIST - 1)
    big = jnp.full(d.shape, T5_EXACT, jnp.int32)
    for thr in _T5_THR:
        big = big + jnp.where(d >= thr, 1, 0)
    bkt = jnp.where(d < T5_EXACT, d, big)
    far = rel_ref[T5_BUCKETS - 1, h]
    val = jnp.zeros(d.shape, _F32)
    for k in range(T5_BUCKETS - 1):
        val = jnp.where(bkt == k, rel_ref[k, h] - far, val)
    return val


def _t5_masked(dist, rel_ref, h):
    return jnp.where(dist < 0, NEG, _t5_rel(jnp.maximum(dist, 0), rel_ref, h))


def _bias_prompt_kernel(rel_ref, tz_ref, nb_ref):
    def body(h, c):
        i = _iota((TQ, 2 * TQ), 0)
        m = _iota((TQ, 2 * TQ), 1)
        tz_ref[h] = _t5_masked(i + TQ - m, rel_ref, h)
        i = _iota((TQ, LANES), 0)
        u = _iota((TQ, LANES), 1)
        near = _t5_masked(i + (CMP_STRIDE * NEAR_BACK - (CMP_BLOCK - 1)) - CMP_STRIDE * u, rel_ref, h)
        nb = jnp.where(u < NEAR_U, near, jnp.where(u == NEAR_U, NEG, 0.0))
        nb_ref[pl.ds(pl.multiple_of(h * TQ, TQ), TQ), :] = nb
        return c
    lax.fori_loop(0, NSA_HEADS, body, 0)


def _bias_prompt(rel_bias):
    return pl.pallas_call(
        _bias_prompt_kernel,
        out_shape=(jax.ShapeDtypeStruct((NSA_HEADS, TQ, 2 * TQ), _F32),
                   jax.ShapeDtypeStruct((NSA_HEADS * TQ, LANES), _F32)),
        in_specs=[pl.BlockSpec(memory_space=pltpu.SMEM)],
        name="t5_bias_prompt",
    )(rel_bias)


def _bias_sample_kernel(rel_ref, bc_ref, bs_ref, bw_ref, *, past, n_new, n_cmp):
    def table(shape, dist_fn, extra_invalid=None):
        out = jnp.zeros(shape, _F32)
        r = _iota(shape, 0)
        c = _iota(shape, 1)
        i = _div(r, NSA_HEADS)
        dist = dist_fn(i, c)
        for h in range(NSA_HEADS):
            v = _t5_masked(dist, rel_ref, h)
            out = jnp.where(_mod(r, NSA_HEADS) == h, v, out)
        if extra_invalid is not None:
            out = jnp.where(extra_invalid(i, c, dist), NEG, out)
        return out

    bc_ref[...] = table(bc_ref.shape, lambda i, c: past + i - (CMP_STRIDE * c + CMP_BLOCK - 1),
                        lambda i, c, d: c >= n_cmp)
    bs_ref[...] = table(bs_ref.shape, lambda i, c: i + WINDOW - c)
    bw_ref[...] = table(bw_ref.shape, lambda i, c: i + WINDOW - c,
                        lambda i, c, d: (d >= WINDOW) | (c >= WINDOW + n_new))


def _bias_sample(rel_bias, past, n_new, n_cmp, ncp):
    rows = NSA_HEADS * n_new
    return pl.pallas_call(
        functools.partial(_bias_sample_kernel, past=past, n_new=n_new, n_cmp=n_cmp),
        out_shape=(jax.ShapeDtypeStruct((rows, ncp), _F32),
                   jax.ShapeDtypeStruct((rows, 2 * WINDOW), _F32),
                   jax.ShapeDtypeStruct((rows, WINDOW + LANES), _F32)),
        in_specs=[pl.BlockSpec(memory_space=pltpu.SMEM)],
        name="t5_bias_sample",
    )(rel_bias)


def _proj_kernel(x_ref, g_ref, w_ref, b_ref, qn_ref, nsa_ref, nsab_ref, kw_ref, kwb_ref, qf_ref,
                 fox_ref, foxb_ref, misc_ref, ckt_ref, carry_ref, *, tr, cum):
    x = x_ref[0]
    h = _rms(x, g_ref[...]).astype(_BF)
    qn_ref[0] = (_dot(h, w_ref[:, C_QN:C_NSA]) * SCALE).astype(_BF)
    z = _dot(h, w_ref[:, C_NSA:C_KW])
    nsa_ref[0] = z
    nsab_ref[0] = z.astype(_BF)
    z = _dot(h, w_ref[:, C_KW:C_QF])
    kw_ref[0] = z
    kwb_ref[0] = z.astype(_BF)
    qf_ref[0] = (_dot(h, w_ref[:, C_QF:C_FOX]) * SCALE).astype(_BF)
    z = _dot(h, w_ref[:, C_FOX:C_MISC])
    fox_ref[0] = z
    foxb_ref[0] = z.astype(_BF)
    zm = _dot(h, w_ref[:, C_MISC:C_TOT]) + b_ref[...]
    lane = _iota((tr, LANES), 1)
    sg = jax.nn.sigmoid(zm)
    ls = jnp.minimum(zm, 0.0) - jnp.log1p(jnp.exp(-jnp.abs(zm)))
    if cum:
        @pl.when(pl.program_id(1) == 0)
        def _():
            carry_ref[...] = jnp.zeros(carry_ref.shape, _F32)
        row = _iota((tr, tr), 0)
        col = _iota((tr, tr), 1)
        tri = jnp.where(col <= row, 1.0, 0.0).astype(_BF)
        cs = _dot3(tri, ls) + carry_ref[0:1, :]
        carry_ref[...] = jnp.broadcast_to(cs[tr - 1:tr, :], carry_ref.shape)
        ckt_ref[0] = cs.T[L_CUM:L_CUM + FOX_HEADS, :]
    else:
        cs = jnp.zeros((tr, LANES), _F32)
        ckt_ref[0] = jnp.zeros(ckt_ref.shape[1:], _F32)
    misc_ref[0] = jnp.where(lane < L_LOGF, sg, jnp.where(lane < L_CUM, ls, jnp.where(lane < L_CUM + FOX_HEADS, cs, 0.0)))


def _proj(x, g, w, b, tr, cum):
    nb, t, d = x.shape
    grid = (nb, t // tr)
    row = lambda width: pl.BlockSpec((1, tr, width), lambda bi, i: (bi, i, 0))
    const = lambda shape: pl.BlockSpec(shape, lambda bi, i: (0,) * len(shape))
    shp = lambda width, dt: jax.ShapeDtypeStruct((nb, t, width), dt)
    return pl.pallas_call(
        functools.partial(_proj_kernel, tr=tr, cum=cum),
        grid=grid,
        in_specs=[row(d), const((1, d)), const((d, C_TOT)), const((1, LANES))],
        out_specs=(row(512), row(512), row(512), row(256), row(256), row(512), row(1024), row(1024), row(LANES),
                   pl.BlockSpec((1, FOX_HEADS, tr), lambda bi, i: (bi, 0, i))),
        out_shape=(shp(512, _BF), shp(512, _F32), shp(512, _BF), shp(256, _F32), shp(256, _BF), shp(512, _BF),
                   shp(1024, _F32), shp(1024, _BF), shp(LANES, _F32),
                   jax.ShapeDtypeStruct((nb, FOX_HEADS, t), _F32)),
        scratch_shapes=[pltpu.VMEM((SUBLANES, LANES), _F32)],
        compiler_params=_cparams(("arbitrary", "arbitrary")),
        name="in_proj",
    )(x, g, w, b)


def _compress(lhs_bf, w1p, pe8, w1raw, w2p, n_valid):
    nc = lhs_bf.shape[0]
    hcat = _dot(lhs_bf, w1p)
    cst = _dot(pe8, w1raw)[0:1]

    def hidden(g):
        a = hcat[:, g * 256:g * 256 + CMP_HIDDEN]
        b = hcat[:, g * 256 + CMP_HIDDEN:(g + 1) * 256]
        return _gelu_tanh(a + pltpu.roll(b, nc - 1, 0) + cst)

    hh = jnp.concatenate([hidden(0), hidden(1)], axis=1).astype(_BF)
    out = _dot(hh, w2p)
    return jnp.where(_iota(out.shape, 0) < n_valid, out, 0.0)


def _compress_prompt_kernel(xk_ref, xv_ref, w1k_ref, w1v_ref, pe_ref, w1r_ref, w2_ref, kc_ref, vc_ref, *, nc, ncp):
    for idx, (x_ref, w1, out_ref) in enumerate(((xk_ref, w1k_ref, kc_ref), (xv_ref, w1v_ref, vc_ref))):
        pieces = [x_ref[0, pl.ds(p, nc, stride=CMP_STRIDE), :] for p in range(CMP_STRIDE)]
        lhs = jnp.concatenate(pieces, axis=1).astype(_BF)
        out = _compress(lhs, w1[...], pe_ref[idx], w1r_ref[idx], w2_ref[idx], nc - 1)
        if ncp > nc:
            out = jnp.concatenate([out, jnp.zeros((ncp - nc, LANES), _F32)], axis=0)
        out_ref[0] = out.astype(_BF)


def _compress_prompt(nsa_state, cw, nc, ncp):
    b, t, _ = nsa_state.shape
    const = lambda a: pl.BlockSpec(a.shape, lambda bi: (0,) * a.ndim)
    return pl.pallas_call(
        functools.partial(_compress_prompt_kernel, nc=nc, ncp=ncp),
        grid=(b,),
        in_specs=[pl.BlockSpec((1, t, LANES), lambda bi: (bi, 0, 0)), pl.BlockSpec((1, t, LANES), lambda bi: (bi, 0, 1)),
                  const(cw["w1k"]), const(cw["w1v"]), const(cw["pe8"]), const(cw["w1raw"]), const(cw["w2p"])],
        out_specs=(pl.BlockSpec((1, ncp, LANES), lambda bi: (bi, 0, 0)),) * 2,
        out_shape=(jax.ShapeDtypeStruct((b, ncp, LANES), _BF),) * 2,
        compiler_params=_cparams(("arbitrary",)),
        name="compress_prompt",
    )(nsa_state, nsa_state, cw["w1k"], cw["w1v"], cw["pe8"], cw["w1raw"], cw["w2p"])


def _softmax_step(s, v_bf, m_ref, l_ref, acc_ref):
    m_old = m_ref[...]
    m_new = jnp.maximum(m_old, jnp.max(s, axis=-1, keepdims=True))
    alpha = jnp.exp(m_old - m_new)
    p = jnp.exp(s - m_new)
    l_ref[...] = alpha * l_ref[...] + jnp.sum(p, axis=-1, keepdims=True)
    acc_ref[...] = alpha * acc_ref[...] + _dot(p.astype(_BF), v_bf)
    m_ref[...] = m_new


def _softmax_init(m_ref, l_ref, acc_ref):
    m_ref[...] = jnp.full(m_ref.shape, M_INIT, _F32)
    l_ref[...] = jnp.zeros(l_ref.shape, _F32)
    acc_ref[...] = jnp.zeros(acc_ref.shape, _F32)


def _masked_softmax_full(s):
    m = jnp.max(s, axis=-1, keepdims=True)
    p = jnp.exp(s - m)
    l = jnp.sum(p, axis=-1, keepdims=True)
    return jnp.where(m > NEG_HALF, p / l, 0.0)


def _overlap(ncp, n_cmp, n_lanes):
    c = _iota((ncp, n_lanes), 0)
    j = _iota((ncp, n_lanes), 1)
    r = SLC_BLOCK // CMP_STRIDE
    hit = (c >= r * j - (CMP_BLOCK // CMP_STRIDE - 1)) & (c <= r * j + r - 1) & (c < n_cmp)
    return jnp.where(hit, 1.0, 0.0).astype(_BF)


def _select_blocks(imp, t_pos, n_slc, n_top):
    j = _iota(imp.shape, 1)
    cur = _div(t_pos, SLC_BLOCK)
    forced = (j == 0) | (j == cur) | (j == cur - 1)
    score = jnp.where(forced, FORCE_SCORE, imp)
    score = jnp.where(j * SLC_BLOCK > t_pos, -FORCE_SCORE, score)
    score = jnp.where(j >= n_slc, M_INIT, score)
    sel = jnp.zeros(imp.shape, jnp.bool_)
    for _ in range(n_top):
        mx = jnp.max(score, axis=-1, keepdims=True)
        idx = jnp.min(jnp.where(score == mx, j, 1 << 20), axis=-1, keepdims=True)
        hit = j == idx
        sel = sel | hit
        score = jnp.where(hit, REMOVED, score)
    return jnp.where(sel, 0.0, NEG)


def _block_onehot(k0, tk, n_lanes):
    s = _iota((tk, n_lanes), 0)
    j = _iota((tk, n_lanes), 1)
    return jnp.where(j == _div(k0 + s, SLC_BLOCK), 1.0, 0.0).astype(_BF)


def _nsa_prompt_kernel(qn_ref, misc_ref, kc_ref, vc_ref, ks_ref, vs_ref, kw_ref, vw_ref, tz_ref, nb_ref,
                       on_ref, lhs_ref, ms_ref, ls_ref, as_ref, mw_ref, lw_ref, aw_ref, *, n_cmp, ncp, n_slc):
    qt = pl.program_id(1)
    q0 = qt * TQ
    rows = NSA_HEADS * TQ
    lane = _iota((TQ, LANES), 1)
    low = lane < HEAD_DIM

    q = qn_ref[0]
    for g in range(NSA_KV_HEADS):
        for r in range(NSA_GROUP):
            blk = q[:, r * LANES:(r + 1) * LANES]
            blk = jnp.where(low if g == 0 else jnp.logical_not(low), blk, jnp.zeros_like(blk))
            lhs_ref[(g * NSA_GROUP + r) * TQ:(g * NSA_GROUP + r + 1) * TQ, 0:LANES] = blk
    lq = lhs_ref[:, 0:LANES]

    u = _iota((LANES, ncp), 0)
    c = _iota((LANES, ncp), 1)
    place = ((u < NEAR_U) & (c == CPT * qt - NEAR_BACK + u)) | ((u == NEAR_U) & (c >= CPT * qt + CPT))
    place = jnp.where(place, 1.0, 0.0).astype(_BF)
    nbv = nb_ref[...]
    nb_hi = nbv.astype(_BF)
    nb_lo = (nbv - nb_hi.astype(_F32)).astype(_BF)
    s_c = _dot_nt(lq, kc_ref[0]) + _dot(nb_hi, place) + _dot(nb_lo, place)
    pc = _masked_softmax_full(s_c)
    o_c = _dot(pc.astype(_BF), vc_ref[0])

    ovl = _overlap(ncp, n_cmp, LANES)
    imps = []
    for g in range(NSA_KV_HEADS):
        acc = pc[g * NSA_GROUP * TQ:(g * NSA_GROUP + 1) * TQ]
        for r in range(1, NSA_GROUP):
            acc = acc + pc[(g * NSA_GROUP + r) * TQ:(g * NSA_GROUP + r + 1) * TQ]
        hi = acc.astype(_BF)
        lo = (acc - hi.astype(_F32)).astype(_BF)
        imps.append(_dot(hi, ovl) + _dot(lo, ovl))
    imp = jnp.concatenate(imps, axis=0)
    t_pos = q0 + _mod(_iota((NSA_KV_HEADS * TQ, 1), 0), TQ)
    msel = _select_blocks(imp, t_pos, n_slc, min(N_SELECT, n_slc)).astype(_BF)
    for g in range(NSA_KV_HEADS):
        for r in range(NSA_GROUP):
            lhs_ref[(g * NSA_GROUP + r) * TQ:(g * NSA_GROUP + r + 1) * TQ, LANES:2 * LANES] = msel[g * TQ:(g + 1) * TQ]

    def add_bias(s, half):
        s3 = s.reshape(NSA_HEADS, TQ, TQ) + tz_ref[:, :, half * TQ:(half + 1) * TQ]
        return s3.reshape(rows, TQ)

    _softmax_init(ms_ref, ls_ref, as_ref)

    def sel_tile(k0, half):
        kaug = jnp.concatenate([ks_ref[0, pl.ds(k0, TQ), :], _block_onehot(k0, TQ, LANES)], axis=1)
        s = _dot_nt(lhs_ref[...], kaug)
        if half is not None:
            s = add_bias(s, half)
        _softmax_step(s, vs_ref[0, pl.ds(k0, TQ), :], ms_ref, ls_ref, as_ref)

    def far_body(kt, carry):
        sel_tile(pl.multiple_of(kt * TQ, TQ), None)
        return carry
    lax.fori_loop(0, jnp.maximum(qt - 1, 0), far_body, 0)

    @pl.when(qt >= 1)
    def _():
        sel_tile(pl.multiple_of(q0 - TQ, TQ), 0)
    sel_tile(pl.multiple_of(q0, TQ), 1)

    _softmax_init(mw_ref, lw_ref, aw_ref)
    n_wt = WINDOW // TQ
    for w in range(n_wt + 1):
        def win_tile(w=w):
            k0 = pl.multiple_of(q0 - (n_wt - w) * TQ, TQ)
            s = _dot_nt(lq, kw_ref[0, pl.ds(k0, TQ), :])
            if w == 0:
                i2 = _iota((TQ, TQ), 0)
                c2 = _iota((TQ, TQ), 1)
                edge = jnp.where(c2 > i2, 0.0, NEG)
                s = (s.reshape(NSA_HEADS, TQ, TQ) + edge[None]).reshape(rows, TQ)
            if w == n_wt - 1 and n_wt >= 2:
                s = add_bias(s, 0)
            if w == n_wt:
                s = add_bias(s, 1)
            _softmax_step(s, vw_ref[0, pl.ds(k0, TQ), :], mw_ref, lw_ref, aw_ref)
        if w == n_wt:
            win_tile()
        else:
            pl.when(qt >= n_wt - w)(win_tile)

    o_s = as_ref[...] / ls_ref[...]
    o_w = aw_ref[...] / lw_ref[...]
    gates = misc_ref[0]
    for r in range(NSA_GROUP):
        halves = []
        for g in range(NSA_KV_HEADS):
            h = g * NSA_GROUP + r
            sl = slice(h * TQ, (h + 1) * TQ)
            halves.append(gates[:, L_GATE + h:L_GATE + h + 1] * o_c[sl]
                          + gates[:, L_GATE + NSA_HEADS + h:L_GATE + NSA_HEADS + h + 1] * o_s[sl]
                          + gates[:, L_GATE + 2 * NSA_HEADS + h:L_GATE + 2 * NSA_HEADS + h + 1] * o_w[sl])
        on_ref[0, :, r * LANES:(r + 1) * LANES] = jnp.where(low, halves[0], halves[1])


def _nsa_prompt(qn, misc, kcmp, vcmp, nsab, kwb, tz, nb, n_cmp, ncp, n_slc):
    b, t, _ = qn.shape
    rows = NSA_HEADS * TQ
    seq = lambda arr, blk: pl.BlockSpec((1, t, LANES), lambda bi, qi, blk=blk: (bi, 0, blk))
    const = lambda a: pl.BlockSpec(a.shape, lambda bi, qi: (0,) * a.ndim)
    return pl.pallas_call(
        functools.partial(_nsa_prompt_kernel, n_cmp=n_cmp, ncp=ncp, n_slc=n_slc),
        grid=(b, t // TQ),
        in_specs=[pl.BlockSpec((1, TQ, 512), lambda bi, qi: (bi, qi, 0)),
                  pl.BlockSpec((1, TQ, LANES), lambda bi, qi: (bi, qi, 0)),
                  pl.BlockSpec((1, ncp, LANES), lambda bi, qi: (bi, 0, 0)),
                  pl.BlockSpec((1, ncp, LANES), lambda bi, qi: (bi, 0, 0)),
                  seq(nsab, 2), seq(nsab, 3), seq(kwb, 0), seq(kwb, 1), const(tz), const(nb)],
        out_specs=pl.BlockSpec((1, TQ, 512), lambda bi, qi: (bi, qi, 0)),
        out_shape=jax.ShapeDtypeStruct((b, t, 512), _F32),
        scratch_shapes=[pltpu.VMEM((rows, 2 * LANES), _BF),
                        pltpu.VMEM((rows, 1), _F32), pltpu.VMEM((rows, 1), _F32), pltpu.VMEM((rows, LANES), _F32),
                        pltpu.VMEM((rows, 1), _F32), pltpu.VMEM((rows, 1), _F32), pltpu.VMEM((rows, LANES), _F32)],
        compiler_params=_cparams(("arbitrary", "arbitrary")),
        name="nsa_prompt",
    )(qn, misc, kcmp, vcmp, nsab, nsab, kwb, kwb, tz, nb)


def _fox_prompt_kernel(qf_ref, misc_ref, kf_ref, vf_ref, ckt_ref, of_ref, lhs_ref, m_ref, l_ref, acc_ref):
    p = pl.program_id(1)
    qt = pl.program_id(2)
    q0 = qt * TQF
    lane = _iota((TQF, LANES), 1)
    low = lane < HEAD_DIM
    q = qf_ref[0]
    zero = jnp.zeros_like(q)
    lhs_ref[0:TQF, :] = jnp.where(low, q, zero)
    lhs_ref[TQF:2 * TQF, :] = jnp.where(low, zero, q)
    misc = misc_ref[0]
    cq = [jnp.sum(jnp.where(lane == L_CUM + 2 * p + a, misc, 0.0), axis=-1, keepdims=True) for a in range(2)]
    _softmax_init(m_ref, l_ref, acc_ref)

    def tile(k0, diag):
        s = _dot_nt(lhs_ref[...], kf_ref[0, pl.ds(k0, TQF), :])
        parts = []
        for a in range(2):
            ck = ckt_ref[0, pl.ds(2 * p + a, 1), pl.ds(k0, TQF)]
            sa = s[a * TQF:(a + 1) * TQF] + (cq[a] - ck)
            if diag:
                i2 = _iota((TQF, TQF), 0)
                c2 = _iota((TQF, TQF), 1)
                sa = jnp.where(c2 <= i2, sa, NEG)
            parts.append(sa)
        _softmax_step(jnp.concatenate(parts, axis=0), vf_ref[0, pl.ds(k0, TQF), :], m_ref, l_ref, acc_ref)

    def far_body(kt, carry):
        tile(pl.multiple_of(kt * TQF, TQF), False)
        return carry
    lax.fori_loop(0, qt, far_body, 0)
    tile(pl.multiple_of(q0, TQF), True)
    o = acc_ref[...] / l_ref[...]
    of_ref[0] = jnp.where(low, o[0:TQF], o[TQF:2 * TQF])


def _fox_prompt(qf, misc, foxb, ckt):
    b, t, _ = qf.shape
    n_pair = FOX_HEADS // 2
    return pl.pallas_call(
        _fox_prompt_kernel,
        grid=(b, n_pair, t // TQF),
        in_specs=[pl.BlockSpec((1, TQF, LANES), lambda bi, p, qi: (bi, qi, p)),
                  pl.BlockSpec((1, TQF, LANES), lambda bi, p, qi: (bi, qi, 0)),
                  pl.BlockSpec((1, t, LANES), lambda bi, p, qi: (bi, 0, p)),
                  pl.BlockSpec((1, t, LANES), lambda bi, p, qi: (bi, 0, n_pair + p)),
                  pl.BlockSpec((1, FOX_HEADS, t), lambda bi, p, qi: (bi, 0, 0))],
        out_specs=pl.BlockSpec((1, TQF, LANES), lambda bi, p, qi: (bi, qi, p)),
        out_shape=jax.ShapeDtypeStruct((b, t, 512), _F32),
        scratch_shapes=[pltpu.VMEM((2 * TQF, LANES), _BF), pltpu.VMEM((2 * TQF, 1), _F32),
                        pltpu.VMEM((2 * TQF, 1), _F32), pltpu.VMEM((2 * TQF, LANES), _F32)],
        compiler_params=_cparams(("arbitrary", "arbitrary", "arbitrary")),
        name="fox_prompt",
    )(qf, misc, foxb, foxb, ckt)


def _post_kernel(on_ref, of_ref, x_ref, gn_ref, gf_ref, wo_ref, gp_ref, y_ref):
    half = on_ref.shape[-1]
    a = _rms(on_ref[0], gn_ref[...]).astype(_BF)
    f = _rms(of_ref[0], gf_ref[...]).astype(_BF)
    mixed = _dot(a, wo_ref[0:half, :]) + _dot(f, wo_ref[half:2 * half, :])
    y_ref[0] = x_ref[0] + _rms(mixed, gp_ref[...])


def _post(o_n, o_f, x, gn, gf, wo, gp, tr):
    nb, t, d = x.shape
    half = o_n.shape[-1]
    row = lambda width: pl.BlockSpec((1, tr, width), lambda bi, i: (bi, i, 0))
    const = lambda shape: pl.BlockSpec(shape, lambda bi, i: (0,) * len(shape))
    return pl.pallas_call(
        _post_kernel,
        grid=(nb, t // tr),
        in_specs=[row(half), row(half), row(d), const((1, half)), const((1, half)), const((2 * half, d)), const((1, d))],
        out_specs=row(d),
        out_shape=jax.ShapeDtypeStruct((nb, t, d), _F32),
        compiler_params=_cparams(("arbitrary", "arbitrary")),
        name="out_proj",
    )(o_n, o_f, x, gn, gf, wo, gp)


def _ffn_kernel(x_ref, gpre_ref, wg_ref, wu_ref, wd_ref, gpost_ref, y_ref):
    x = x_ref[0]
    h = _rms(x, gpre_ref[...]).astype(_BF)
    act = (jax.nn.silu(_dot(h, wg_ref[...])) * _dot(h, wu_ref[...])).astype(_BF)
    y_ref[0] = x + _rms(_dot(act, wd_ref[...]), gpost_ref[...])


def _ffn(x, gpre, wg, wu, wd, gpost, tr):
    nb, t, d = x.shape
    dff = wg.shape[1]
    row = pl.BlockSpec((1, tr, d), lambda bi, i: (bi, i, 0))
    const = lambda shape: pl.BlockSpec(shape, lambda bi, i: (0,) * len(shape))
    return pl.pallas_call(
        _ffn_kernel,
        grid=(nb, t // tr),
        in_specs=[row, const((1, d)), const((d, dff)), const((d, dff)), const((dff, d)), const((1, d))],
        out_specs=row,
        out_shape=jax.ShapeDtypeStruct((nb, t, d), _F32),
        compiler_params=_cparams(("arbitrary", "arbitrary")),
        name="ffn",
    )(x, gpre, wg, wu, wd, gpost)


def _row_select(rows8, pieces):
    out = jnp.zeros((SUBLANES, pieces[0].shape[-1]), _F32)
    for i, piece in enumerate(pieces):
        out = jnp.where(rows8 == i, jnp.broadcast_to(piece, out.shape), out)
    return out


def _nsa_sample_kernel(pt_ref, *refs, pps, past, n_new, n_cmp, ncp, n_slc):
    page_refs = refs[:pps]
    (qn_ref, misc_ref, new_ref, win_ref, kwn_ref, w1k_ref, w1v_ref, pe_ref, w1r_ref, w2_ref,
     bc_ref, bs_ref, bw_ref, on_ref, lk_ref, lv_ref, ks_ref, vs_ref, sk_ref, sv_ref, m_ref, l_ref, acc_ref) = refs[pps:]
    del pt_ref
    j = pl.program_id(1)
    n_steps = pl.num_programs(1)
    chunk = WINDOW
    n_chunks = past // chunk + 1

    for k in range(pps):
        pg = j * pps + k
        ref = page_refs[k]
        r0 = pl.multiple_of(pg * PAGE_SIZE, PAGE_SIZE)
        ks_ref[pl.ds(r0, PAGE_SIZE), :] = ref[0, :, 2 * LANES:3 * LANES].astype(_BF)
        vs_ref[pl.ds(r0, PAGE_SIZE), :] = ref[0, :, 3 * LANES:4 * LANES].astype(_BF)
        c0 = pl.multiple_of(pg * (PAGE_SIZE // CMP_STRIDE), PAGE_SIZE // CMP_STRIDE)
        sk_ref[...] = ref[0, :, 0:LANES]
        sv_ref[...] = ref[0, :, LANES:2 * LANES]
        for p in range(CMP_STRIDE):
            rows_p = pl.ds(p, PAGE_SIZE // CMP_STRIDE, stride=CMP_STRIDE)
            lk_ref[pl.ds(c0, PAGE_SIZE // CMP_STRIDE), p * LANES:(p + 1) * LANES] = sk_ref[rows_p, :]
            lv_ref[pl.ds(c0, PAGE_SIZE // CMP_STRIDE), p * LANES:(p + 1) * LANES] = sv_ref[rows_p, :]

    @pl.when(j == n_steps - 1)
    def _():
        nc = past // CMP_STRIDE
        new = new_ref[0]
        pad = jnp.zeros((chunk - SUBLANES, LANES), _F32)
        ks_ref[past:past + chunk, :] = jnp.concatenate([new[:, 2 * LANES:3 * LANES], pad], axis=0).astype(_BF)
        vs_ref[past:past + chunk, :] = jnp.concatenate([new[:, 3 * LANES:4 * LANES], pad], axis=0).astype(_BF)

        def padc(a):
            return a if ncp == nc else jnp.concatenate([a, jnp.zeros((ncp - nc, LANES), _F32)], axis=0)
        kc = padc(_compress(lk_ref[...].astype(_BF), w1k_ref[...], pe_ref[0], w1r_ref[0], w2_ref[0], n_cmp)).astype(_BF)
        vc = padc(_compress(lv_ref[...].astype(_BF), w1v_ref[...], pe_ref[1], w1r_ref[1], w2_ref[1], n_cmp)).astype(_BF)

        rows8 = _iota((SUBLANES, LANES), 0)
        lane = _iota((SUBLANES, LANES), 1)
        grp_low = rows8 < NSA_GROUP
        q = qn_ref[0].astype(_F32)
        blocks = []
        for i in range(n_new):
            blk = jnp.zeros((SUBLANES, LANES), _F32)
            for r in range(NSA_GROUP):
                piece = jnp.broadcast_to(q[i:i + 1, r * LANES:(r + 1) * LANES], (SUBLANES, LANES))
                blk = jnp.where(_mod(rows8, NSA_GROUP) == r, piece, blk)
            on_group = jnp.logical_not(jnp.logical_xor(lane < HEAD_DIM, grp_low))
            blocks.append(jnp.where(on_group, blk, jnp.zeros_like(blk)))
        lq = jnp.concatenate(blocks, axis=0).astype(_BF)

        pc = _masked_softmax_full(_dot_nt(lq, kc) + bc_ref[...])
        o_c = _dot(pc.astype(_BF), vc)

        nl = 2 * LANES
        ovl = _overlap(ncp, n_cmp, nl)
        sums = []
        for i in range(n_new):
            blk = pc[i * SUBLANES:(i + 1) * SUBLANES]
            rr = _iota(blk.shape, 0)
            for g in range(NSA_KV_HEADS):
                in_group = (rr < NSA_GROUP) if g == 0 else (rr >= NSA_GROUP)
                sums.append(jnp.sum(jnp.where(in_group, blk, 0.0), axis=0, keepdims=True))
        rows8w = _iota((SUBLANES, ncp), 0)
        psum = _row_select(rows8w, sums)
        hi = psum.astype(_BF)
        lo = (psum - hi.astype(_F32)).astype(_BF)
        imp = _dot(hi, ovl) + _dot(lo, ovl)
        t_pos = past + _div(_iota((SUBLANES, 1), 0), NSA_KV_HEADS)
        msel = _select_blocks(imp, t_pos, n_slc, min(N_SELECT, n_slc))
        rows8n = _iota((SUBLANES, nl), 0)
        mrows = []
        for i in range(n_new):
            m0 = jnp.broadcast_to(msel[2 * i:2 * i + 1], (SUBLANES, nl))
            m1 = jnp.broadcast_to(msel[2 * i + 1:2 * i + 2], (SUBLANES, nl))
            mrows.append(jnp.where(rows8n < NSA_GROUP, m0, m1))
        mrows = jnp.concatenate(mrows, axis=0).astype(_BF)

        _softmax_init(m_ref, l_ref, acc_ref)

        def sel_chunk(k0, bias):
            s_i = _iota((nl, chunk), 1)
            j_i = _iota((nl, chunk), 0)
            expand = jnp.where(j_i == _div(k0 + s_i, SLC_BLOCK), 1.0, 0.0).astype(_BF)
            s = _dot_nt(lq, ks_ref[pl.ds(k0, chunk), :]) + _dot(mrows, expand)
            if bias is not None:
                s = s + bias
            _softmax_step(s, vs_ref[pl.ds(k0, chunk), :], m_ref, l_ref, acc_ref)

        def far_body(ci, carry):
            sel_chunk(pl.multiple_of(ci * chunk, chunk), None)
            return carry
        lax.fori_loop(0, n_chunks - 2, far_body, 0)
        sel_chunk(past - chunk, bs_ref[:, 0:chunk])
        sel_chunk(past, bs_ref[:, chunk:2 * chunk])
        o_s = acc_ref[...] / l_ref[...]

        win = win_ref[0]
        kwn = kwn_ref[0]
        padw = jnp.zeros((LANES - SUBLANES, 2 * LANES), _F32)
        kvw = jnp.concatenate([win, kwn, padw], axis=0).astype(_BF)
        sw = _dot_nt(lq, kvw[:, 0:LANES]) + bw_ref[...]
        mw = jnp.max(sw, axis=-1, keepdims=True)
        pw = jnp.exp(sw - mw)
        o_w = _dot(pw.astype(_BF), kvw[:, LANES:2 * LANES]) / jnp.sum(pw, axis=-1, keepdims=True)

        misc = misc_ref[0]
        out_rows = []
        for i in range(n_new):
            sl = slice(i * SUBLANES, (i + 1) * SUBLANES)
            g_row = jnp.broadcast_to(misc[i:i + 1, :], (SUBLANES, LANES))

            def gcol(kind, g_row=g_row):
                pick = lane == L_GATE + kind * NSA_HEADS + rows8
                return jnp.sum(jnp.where(pick, g_row, 0.0), axis=-1, keepdims=True)
            o_blk = gcol(0) * o_c[sl] + gcol(1) * o_s[sl] + gcol(2) * o_w[sl]
            pieces = [jnp.where(lane[0:1] < HEAD_DIM, o_blk[r:r + 1], o_blk[NSA_GROUP + r:NSA_GROUP + r + 1])
                      for r in range(NSA_GROUP)]
            out_rows.append(jnp.concatenate(pieces, axis=1))
        on_ref[0] = _row_select(_iota((SUBLANES, 4 * LANES), 0), out_rows)


def _nsa_sample(page_table, cache, qn, misc, new, win, kwn, cw, bc, bs, bw, past, n_new, n_cmp, ncp, n_slc, pps):
    n_seq, n_pages = page_table.shape
    n_steps = n_pages // pps
    rows = NSA_HEADS * n_new
    nc = past // CMP_STRIDE

    def page_spec(k):
        return pl.BlockSpec((1, PAGE_SIZE, 4 * LANES), lambda s, j, pt, k=k: (pt[s, j * pps + k], 0, 0))
    per_seq = lambda a: pl.BlockSpec((1,) + a.shape[1:], lambda s, j, pt: (s,) + (0,) * (a.ndim - 1))
    const = lambda a: pl.BlockSpec(a.shape, lambda s, j, pt: (0,) * a.ndim)
    consts = [cw["w1k"], cw["w1v"], cw["pe8"], cw["w1raw"], cw["w2p"], bc, bs, bw]
    grid_spec = pltpu.PrefetchScalarGridSpec(
        num_scalar_prefetch=1,
        grid=(n_seq, n_steps),
        in_specs=[page_spec(k) for k in range(pps)] + [per_seq(a) for a in (qn, misc, new, win, kwn)]
        + [const(a) for a in consts],
        out_specs=pl.BlockSpec((1, SUBLANES, 4 * LANES), lambda s, j, pt: (s, 0, 0)),
        scratch_shapes=[pltpu.VMEM((nc, CMP_STRIDE * LANES), _F32), pltpu.VMEM((nc, CMP_STRIDE * LANES), _F32),
                        pltpu.VMEM((past + WINDOW, LANES), _BF), pltpu.VMEM((past + WINDOW, LANES), _BF),
                        pltpu.VMEM((PAGE_SIZE, LANES), _F32), pltpu.VMEM((PAGE_SIZE, LANES), _F32),
                        pltpu.VMEM((rows, 1), _F32), pltpu.VMEM((rows, 1), _F32), pltpu.VMEM((rows, LANES), _F32)],
    )
    return pl.pallas_call(
        functools.partial(_nsa_sample_kernel, pps=pps, past=past, n_new=n_new, n_cmp=n_cmp, ncp=ncp, n_slc=n_slc),
        grid_spec=grid_spec,
        out_shape=jax.ShapeDtypeStruct((n_seq, SUBLANES, 4 * LANES), _F32),
        compiler_params=_cparams(("arbitrary", "arbitrary")),
        name="nsa_sample",
    )(page_table, *([cache] * pps), qn, misc, new, win, kwn, *consts)


def _fox_sample_kernel(pt_ref, *refs, pps, n_new):
    kv_refs = refs[:pps]
    lf_refs = refs[pps:2 * pps]
    (qf_ref, kvn_ref, lfn_ref, of_ref, q_ref, e_ref, carry_ref, m_ref, l_ref, acc_ref) = refs[2 * pps:]
    del pt_ref
    j = pl.program_id(1)
    n_steps = pl.num_programs(1)
    rows = FOX_HEADS * n_new
    width = FOX_HEADS * HEAD_DIM
    srow = _iota((PAGE_SIZE, PAGE_SIZE), 0)
    scol = _iota((PAGE_SIZE, PAGE_SIZE), 1)
    later = jnp.where(srow > scol, 1.0, 0.0).astype(_BF)

    def suffix(lf):
        return _dot3_left(lf, later)

    def _dot3_left(x, b_bf):
        hi, mid, lo = _split3(x)
        return _dot(hi, b_bf) + _dot(mid, b_bf) + _dot(lo, b_bf)

    def attend(k2d, v2d, rt, extra):
        s = _dot_nt(q_ref[...], k2d)
        dec = jnp.concatenate([rt] * n_new, axis=0) - e_ref[...]
        s = s + dec
        if extra is not None:
            s = s + extra
        _softmax_step(s, v2d, m_ref, l_ref, acc_ref)

    @pl.when(j == 0)
    def _():
        _softmax_init(m_ref, l_ref, acc_ref)
        rows8 = _iota((SUBLANES, width), 0)
        lane = _iota((SUBLANES, width), 1)
        q = qf_ref[0].astype(_F32)
        blocks = []
        for i in range(n_new):
            piece = jnp.broadcast_to(q[i:i + 1, :], (SUBLANES, width))
            blocks.append(jnp.where(_div(lane, HEAD_DIM) == rows8, piece, jnp.zeros_like(piece)))
        q_ref[...] = jnp.concatenate(blocks, axis=0).astype(_BF)
        lfn = lfn_ref[0]
        rt = suffix(lfn)
        e_ref[...] = jnp.concatenate([rt[:, i:i + 1] for i in range(n_new)], axis=0)
        carry_ref[...] = jnp.broadcast_to(jnp.sum(lfn, axis=-1, keepdims=True), carry_ref.shape)
        kvn = kvn_ref[0]
        ri = _div(_iota((rows, PAGE_SIZE), 0), SUBLANES)
        ci = _iota((rows, PAGE_SIZE), 1)
        causal = jnp.where(ci <= ri, 0.0, NEG)
        attend(kvn[:, 0:width].astype(_BF), kvn[:, width:2 * width].astype(_BF), rt, causal)

    ks, vs, rts = [], [], []
    carry = carry_ref[...]
    for k in range(pps):
        kv = kv_refs[k][0]
        lf = lf_refs[k][0]
        ks.append(kv[:, 0:width].astype(_BF))
        vs.append(kv[:, width:2 * width].astype(_BF))
        rts.append(suffix(lf) + carry)
        carry = carry + jnp.sum(lf, axis=-1, keepdims=True)
    carry_ref[...] = carry
    attend(jnp.concatenate(ks, axis=0), jnp.concatenate(vs, axis=0), jnp.concatenate(rts, axis=1), None)

    @pl.when(j == n_steps - 1)
    def _():
        o = acc_ref[...] / l_ref[...]
        rows8 = _iota((SUBLANES, width), 0)
        lane = _iota((SUBLANES, width), 1)
        out_rows = []
        for i in range(n_new):
            blk = jnp.where(_div(lane, HEAD_DIM) == rows8, o[i * SUBLANES:(i + 1) * SUBLANES], 0.0)
            out_rows.append(jnp.sum(blk, axis=0, keepdims=True))
        of_ref[0] = _row_select(rows8, out_rows)


def _fox_sample(page_table, cache_kv, cache_lft, qf, kvn, lfn, n_new, pps):
    n_seq, n_pages = page_table.shape
    n_steps = n_pages // pps
    rows = FOX_HEADS * n_new
    width = FOX_HEADS * HEAD_DIM

    def page_idx(s, j, pt, k):
        return pt[s, n_pages - 1 - (j * pps + k)]
    kv_spec = lambda k: pl.BlockSpec((1, PAGE_SIZE, 2 * width), lambda s, j, pt, k=k: (page_idx(s, j, pt, k), 0, 0))
    lf_spec = lambda k: pl.BlockSpec((1, FOX_HEADS, PAGE_SIZE), lambda s, j, pt, k=k: (page_idx(s, j, pt, k), 0, 0))
    per_seq = lambda a: pl.BlockSpec((1,) + a.shape[1:], lambda s, j, pt: (s,) + (0,) * (a.ndim - 1))
    grid_spec = pltpu.PrefetchScalarGridSpec(
        num_scalar_prefetch=1,
        grid=(n_seq, n_steps),
        in_specs=[kv_spec(k) for k in range(pps)] + [lf_spec(k) for k in range(pps)] + [per_seq(a) for a in (qf, kvn, lfn)],
        out_specs=pl.BlockSpec((1, SUBLANES, width), lambda s, j, pt: (s, 0, 0)),
        scratch_shapes=[pltpu.VMEM((rows, width), _BF), pltpu.VMEM((rows, 1), _F32), pltpu.VMEM((SUBLANES, LANES), _F32),
                        pltpu.VMEM((rows, 1), _F32), pltpu.VMEM((rows, 1), _F32), pltpu.VMEM((rows, width), _F32)],
    )
    return pl.pallas_call(
        functools.partial(_fox_sample_kernel, pps=pps, n_new=n_new),
        grid_spec=grid_spec,
        out_shape=jax.ShapeDtypeStruct((n_seq, SUBLANES, width), _F32),
        compiler_params=_cparams(("arbitrary", "arbitrary")),
        name="fox_sample",
    )(page_table, *([cache_kv] * pps), *([cache_lft] * pps), qf, kvn, lfn)


def _nsa_perm():
    idx = np.zeros(NSA_HEADS * HEAD_DIM, np.int32)
    for r in range(NSA_GROUP):
        for g in range(NSA_KV_HEADS):
            for d in range(HEAD_DIM):
                idx[r * LANES + g * HEAD_DIM + d] = (g * NSA_GROUP + r) * HEAD_DIM + d
    return idx


def _layer_weights(l, w_in, b_gate, b_forget, cmp_pe, cmp_w1, cmp_w2, grp_norm_nsa, grp_norm_fox, w_o):
    perm = _nsa_perm()
    w = w_in[l]
    o_qn, o_nsa, o_kw, o_g, o_qf, o_kf, o_f = 0, 512, 1024, 1280, 1304, 1816, 2840
    f_cols = w[:, o_f:o_f + FOX_HEADS]
    misc = jnp.concatenate([w[:, o_g:o_g + 3 * NSA_HEADS], f_cols, f_cols,
                            jnp.zeros((w.shape[0], LANES - 3 * NSA_HEADS - 2 * FOX_HEADS), w.dtype)], axis=1)
    wp = jnp.concatenate([w[:, o_qn:o_nsa][:, perm], w[:, o_nsa:o_kw], w[:, o_kw:o_g], w[:, o_qf:o_kf],
                          w[:, o_kf:o_f], misc], axis=1).astype(_BF)
    bias = jnp.concatenate([b_gate[l].reshape(-1), b_forget[l], b_forget[l],
                            jnp.zeros((LANES - 3 * NSA_HEADS - 2 * FOX_HEADS,), _F32)]).reshape(1, LANES)

    def w1_layout(w1):
        w1 = w1.reshape(2, CMP_STRIDE, HEAD_DIM, CMP_HIDDEN)
        z = jnp.zeros((CMP_STRIDE, HEAD_DIM, CMP_HIDDEN), w1.dtype)
        g0 = jnp.concatenate([w1[0], w1[1], z, z], axis=-1)
        g1 = jnp.concatenate([z, z, w1[0], w1[1]], axis=-1)
        return jnp.concatenate([g0, g1], axis=1).reshape(CMP_STRIDE * LANES, 4 * CMP_HIDDEN).astype(_BF)

    def w2_layout(w2):
        z = jnp.zeros_like(w2)
        return jnp.concatenate([jnp.concatenate([w2, z], axis=1), jnp.concatenate([z, w2], axis=1)], axis=0).astype(_BF)
    cw = dict(
        w1k=w1_layout(cmp_w1[l, 0]), w1v=w1_layout(cmp_w1[l, 1]),
        pe8=jnp.broadcast_to(cmp_pe[l].reshape(2, 1, CMP_BLOCK * HEAD_DIM), (2, SUBLANES, CMP_BLOCK * HEAD_DIM)).astype(_BF),
        w1raw=cmp_w1[l].astype(_BF),
        w2p=jnp.stack([w2_layout(cmp_w2[l, 0]), w2_layout(cmp_w2[l, 1])]),
    )
    gn = grp_norm_nsa[l][perm].reshape(1, -1)
    gf = grp_norm_fox[l].reshape(1, -1)
    wo = jnp.concatenate([w_o[l][:NSA_HEADS * HEAD_DIM][perm], w_o[l][NSA_HEADS * HEAD_DIM:]], axis=0).astype(_BF)
    return wp, bias, cw, gn, gf, wo


def kernel(x_prompt, x_sample, cache_nsa_kv, cache_fox_kv, cache_fox_logf, state_win_kv, page_table, rel_bias,
           norm_mix_pre, norm_mix_post, norm_ffn_pre, norm_ffn_post, w_in, b_gate, b_forget, cmp_pe, cmp_w1, cmp_w2,
           grp_norm_nsa, grp_norm_fox, w_o, w_ffn_gate, w_ffn_up, w_ffn_down):
    depth = w_in.shape[0]
    b, t, d = x_prompt.shape
    n_seq, n_new, _ = x_sample.shape
    n_pages = page_table.shape[1]
    past = n_pages * PAGE_SIZE
    n_pool = cache_nsa_kv.shape[1]
    n_win = state_win_kv.shape[2]
    assert t % TQF == 0 and t % TQ == 0 and t >= WINDOW and WINDOW % TQ == 0 and TQ > T5_FAR
    assert n_new <= SUBLANES and SLC_BLOCK >= n_new and past >= 2 * WINDOW and n_win == WINDOW
    assert (n_seq * n_new) % SUBLANES == 0

    nc_p = t // CMP_STRIDE
    ncp_p = -(-nc_p // LANES) * LANES
    n_slc_p = t // SLC_BLOCK
    assert n_slc_p <= LANES
    nc_s = past // CMP_STRIDE
    ncp_s = -(-nc_s // LANES) * LANES
    n_slc_s = past // SLC_BLOCK + 1
    assert n_slc_s <= 2 * LANES
    pps = math.gcd(n_pages, 8)

    tz, nb = _bias_prompt(rel_bias)
    bc, bs, bw = _bias_sample(rel_bias, past, n_new, nc_s - 1, ncp_s)

    row1 = lambda a: a.reshape(1, -1)
    pad_new = lambda a: jnp.pad(a, ((0, 0), (0, SUBLANES - n_new), (0, 0)))
    rs = n_seq * n_new
    xp, xs = x_prompt, x_sample.reshape(1, rs, d)
    outs_p, outs_s = [], []
    for l in range(depth):
        wp, bias, cw, gn, gf, wo = _layer_weights(l, w_in, b_gate, b_forget, cmp_pe, cmp_w1, cmp_w2,
                                                  grp_norm_nsa, grp_norm_fox, w_o)
        wg, wu, wd = w_ffn_gate[l].astype(_BF), w_ffn_up[l].astype(_BF), w_ffn_down[l].astype(_BF)
        g_pre, g_post = row1(norm_mix_pre[l]), row1(norm_mix_post[l])
        g_fpre, g_fpost = row1(norm_ffn_pre[l]), row1(norm_ffn_post[l])

        qn, nsa, nsab, kw, kwb, qf, fox, foxb, misc, ckt = _proj(xp, g_pre, wp, bias, TR_PROJ, True)
        kcmp, vcmp = _compress_prompt(nsa, cw, nc_p, ncp_p)
        o_n = _nsa_prompt(qn, misc, kcmp, vcmp, nsab, kwb, tz, nb, nc_p - 1, ncp_p, n_slc_p)
        o_f = _fox_prompt(qf, misc, foxb, ckt)
        xp = _post(o_n, o_f, xp, gn, gf, wo, g_post, TR_POST)
        xp = _ffn(xp, g_fpre, wg, wu, wd, g_fpost, TR_FFN)
        outs_p.append((nsa.reshape(b, t, 4, NSA_KV_HEADS, HEAD_DIM), fox.reshape(b, t, 2, FOX_HEADS, HEAD_DIM),
                       misc[:, :, L_LOGF:L_LOGF + FOX_HEADS],
                       kw[:, t - WINDOW:].reshape(b, WINDOW, 2, NSA_KV_HEADS, HEAD_DIM)))

        tr_s = math.gcd(rs, TR_PROJ)
        qn, nsa, nsab, kw, kwb, qf, fox, foxb, misc, _ = _proj(xs, g_pre, wp, bias, tr_s, False)
        per = lambda a: a.reshape(n_seq, n_new, a.shape[-1])
        nsa_s, fox_s, kw_s, misc_s = per(nsa), per(fox), per(kw), per(misc)
        logf_s = misc_s[:, :, L_LOGF:L_LOGF + FOX_HEADS]
        win = state_win_kv[l].reshape(n_seq, n_win, 2 * NSA_KV_HEADS * HEAD_DIM)
        o_n = _nsa_sample(page_table, cache_nsa_kv[l].reshape(n_pool, PAGE_SIZE, 4 * NSA_KV_HEADS * HEAD_DIM),
                          pad_new(per(qn)), pad_new(misc_s), pad_new(nsa_s), win, pad_new(kw_s), cw, bc, bs, bw,
                          past, n_new, nc_s - 1, ncp_s, n_slc_s, pps)
        kvn = jnp.pad(fox_s, ((0, 0), (0, PAGE_SIZE - n_new), (0, 0)))
        lfn = jnp.pad(jnp.swapaxes(logf_s, 1, 2), ((0, 0), (0, 0), (0, PAGE_SIZE - n_new)))
        o_f = _fox_sample(page_table, cache_fox_kv[l].reshape(n_pool, PAGE_SIZE, 2 * FOX_HEADS * HEAD_DIM),
                          jnp.swapaxes(cache_fox_logf[l], 1, 2), pad_new(per(qf)), kvn, lfn, n_new, pps)
        o_n = o_n[:, :n_new].reshape(1, rs, -1)
        o_f = o_f[:, :n_new].reshape(1, rs, -1)
        tr_post = math.gcd(rs, TR_POST)
        xs = _post(o_n, o_f, xs, gn, gf, wo, g_post, tr_post)
        xs = _ffn(xs, g_fpre, wg, wu, wd, g_fpost, math.gcd(rs, TR_FFN))
        win_new = jnp.concatenate([win[:, n_new:], kw_s], axis=1)
        outs_s.append((nsa_s.reshape(n_seq, n_new, 4, NSA_KV_HEADS, HEAD_DIM),
                       fox_s.reshape(n_seq, n_new, 2, FOX_HEADS, HEAD_DIM), logf_s,
                       win_new.reshape(n_seq, n_win, 2, NSA_KV_HEADS, HEAD_DIM)))

    stack = lambda outs, i: jnp.stack([o[i] for o in outs], axis=0)
    return (xp, xs.reshape(n_seq, n_new, d), stack(outs_p, 0), stack(outs_p, 1), stack(outs_p, 2), stack(outs_p, 3),
            stack(outs_s, 0), stack(outs_s, 1), stack(outs_s, 2), stack(outs_s, 3))
```

```python
import functools
import math

import numpy as np
import jax
import jax.numpy as jnp
from jax import lax
from jax.experimental import pallas as pl
from jax.experimental.pallas import tpu as pltpu

HEAD_DIM = 64
NSA_HEADS = 8
FOX_HEADS = 8
NSA_KV_HEADS = 2
NSA_GROUP = NSA_HEADS // NSA_KV_HEADS
CMP_BLOCK = 32
CMP_STRIDE = 16
CMP_HIDDEN = 2 * HEAD_DIM
SLC_BLOCK = 64
N_SELECT = 16
WINDOW = 512
T5_BUCKETS = 32
T5_EXACT = T5_BUCKETS // 2
T5_MAX_DIST = 128
PAGE_SIZE = 128
NORM_EPS = 1e-6
FORCE_SCORE = 1e4
LOG2E = math.log2(math.e)
Q_SCALE = HEAD_DIM ** -0.5 * LOG2E

LANES = 128
SUBLANES = 8
VMEM_LIMIT = 56 * 1024 * 1024

_F32 = jnp.float32
_BF = jnp.bfloat16
NEG = -(2.0 ** 100)
NEG_HALF = -(2.0 ** 99)
M_INIT = -3.0e38
REMOVED = -3.4e38


def _t5_thresholds():
    n = np.arange(1, 4 * T5_MAX_DIST)
    large = T5_EXACT + (np.log(n / T5_EXACT) / math.log(T5_MAX_DIST / T5_EXACT) * (T5_BUCKETS - T5_EXACT)).astype(np.int64)
    return tuple(int(n[np.argmax(large >= k)]) for k in range(T5_EXACT + 1, T5_BUCKETS))


_T5_THR = _t5_thresholds()
T5_FAR = _T5_THR[-1]


def _log2(n):
    assert n & (n - 1) == 0
    return n.bit_length() - 1


TQ = 256
TQF = 512
RB = 256
AHEAD = 2
TR_PROJ = 256
TR_POST = 512
TR_FFN = 256
CPT = TQ // CMP_STRIDE
NEAR_BACK = -(-(T5_FAR + CMP_BLOCK - 1) // CMP_STRIDE) - 1
NEAR_U = CPT + NEAR_BACK

C_QN, C_NSA, C_KW, C_QF, C_FOX, C_MISC = 0, 512, 1024, 1280, 1792, 2816
C_TOT = 2944
L_GATE, L_LOGF, L_CUM = 0, 24, 32
N_PIECE = 3
L_NEG, L_POS = 0, N_PIECE * FOX_HEADS


def _dot(a, b):
    return jnp.dot(a, b, preferred_element_type=_F32)


def _dot_nt(a, b):
    return lax.dot_general(a, b, (((1,), (1,)), ((), ())), preferred_element_type=_F32)


def _split3(x):
    hi = x.astype(_BF)
    r1 = x - hi.astype(_F32)
    mid = r1.astype(_BF)
    lo = (r1 - mid.astype(_F32)).astype(_BF)
    return hi, mid, lo


def _split2(x):
    hi = x.astype(_BF)
    return hi, (x - hi.astype(_F32)).astype(_BF)


def _dot3(a_bf, x):
    hi, mid, lo = _split3(x)
    return _dot(a_bf, hi) + _dot(a_bf, mid) + _dot(a_bf, lo)


def _rms(x, g):
    ms = jnp.mean(x * x, axis=-1, keepdims=True)
    return x * lax.rsqrt(ms + NORM_EPS) * g


def _gelu_tanh(x):
    c = math.sqrt(2.0 / math.pi)
    return x * (0.5 * (1.0 + jnp.tanh(c * (x + 0.044715 * (x * x * x)))))


def _iota(shape, dim):
    return lax.broadcasted_iota(jnp.int32, shape, dim)


def _div(x, n):
    return jnp.right_shift(x, _log2(n))


def _mod(x, n):
    return x & (n - 1)


def _cparams(sem):
    return pltpu.CompilerParams(dimension_semantics=sem, vmem_limit_bytes=VMEM_LIMIT)


def _t5_rel(dist, rel_ref, h):
    d = jnp.minimum(dist, T5_MAX_DIST - 1)
    big = jnp.full(d.shape, T5_EXACT, jnp.int32)
    for thr in _T5_THR:
        big = big + jnp.where(d >= thr, 1, 0)
    bkt = jnp.where(d < T5_EXACT, d, big)
    far = rel_ref[T5_BUCKETS - 1, h]
    val = jnp.zeros(d.shape, _F32)
    for k in range(T5_BUCKETS - 1):
        val = jnp.where(bkt == k, (rel_ref[k, h] - far) * LOG2E, val)
    return val


def _t5_masked(dist, rel_ref, h):
    return jnp.where(dist < 0, NEG, _t5_rel(jnp.maximum(dist, 0), rel_ref, h))


def _bias_prompt_kernel(rel_ref, tz_ref, nbh_ref, nbl_ref):
    def body(h, c):
        i = _iota((TQ, 2 * TQ), 0)
        m = _iota((TQ, 2 * TQ), 1)
        tz_ref[h] = _t5_masked(i + TQ - m, rel_ref, h)
        i = _iota((TQ, LANES), 0)
        u = _iota((TQ, LANES), 1)
        near = _t5_masked(i + (CMP_STRIDE * NEAR_BACK - (CMP_BLOCK - 1)) - CMP_STRIDE * u, rel_ref, h)
        nb = jnp.where(u < NEAR_U, near, jnp.where(u == NEAR_U, NEG, 0.0))
        hi, lo = _split2(nb)
        rows = pl.ds(pl.multiple_of(h * TQ, TQ), TQ)
        nbh_ref[rows, :] = hi
        nbl_ref[rows, :] = lo
        return c
    lax.fori_loop(0, NSA_HEADS, body, 0)


def _bias_prompt(rel_bias):
    return pl.pallas_call(
        _bias_prompt_kernel,
        out_shape=(jax.ShapeDtypeStruct((NSA_HEADS, TQ, 2 * TQ), _F32),
                   jax.ShapeDtypeStruct((NSA_HEADS * TQ, LANES), _BF),
                   jax.ShapeDtypeStruct((NSA_HEADS * TQ, LANES), _BF)),
        in_specs=[pl.BlockSpec(memory_space=pltpu.SMEM)],
        name="t5_bias_prompt",
    )(rel_bias)


def _bias_sample_kernel(rel_ref, bc_ref, bs_ref, bw_ref, *, past, n_new, n_cmp):
    def table(shape, dist_fn, extra_invalid=None):
        out = jnp.zeros(shape, _F32)
        r = _iota(shape, 0)
        c = _iota(shape, 1)
        i = _div(r, NSA_HEADS)
        dist = dist_fn(i, c)
        for h in range(NSA_HEADS):
            v = _t5_masked(dist, rel_ref, h)
            out = jnp.where(_mod(r, NSA_HEADS) == h, v, out)
        if extra_invalid is not None:
            out = jnp.where(extra_invalid(i, c, dist), NEG, out)
        return out

    bc_ref[...] = table(bc_ref.shape, lambda i, c: past + i - (CMP_STRIDE * c + CMP_BLOCK - 1),
                        lambda i, c, d: c >= n_cmp)
    bs_ref[...] = table(bs_ref.shape, lambda i, c: i + WINDOW - c)
    bw_ref[...] = table(bw_ref.shape, lambda i, c: i + WINDOW - c,
                        lambda i, c, d: (d >= WINDOW) | (c >= WINDOW + n_new))


def _bias_sample(rel_bias, past, n_new, n_cmp, ncp):
    rows = NSA_HEADS * n_new
    return pl.pallas_call(
        functools.partial(_bias_sample_kernel, past=past, n_new=n_new, n_cmp=n_cmp),
        out_shape=(jax.ShapeDtypeStruct((rows, ncp), _F32),
                   jax.ShapeDtypeStruct((rows, WINDOW + PAGE_SIZE), _F32),
                   jax.ShapeDtypeStruct((rows, WINDOW + PAGE_SIZE), _F32)),
        in_specs=[pl.BlockSpec(memory_space=pltpu.SMEM)],
        name="t5_bias_sample",
    )(rel_bias)


def _proj_kernel(x_ref, g_ref, w_ref, b_ref, qn_ref, nsa_ref, nsab_ref, kw_ref, kwb_ref, qf_ref,
                 fox_ref, foxb_ref, misc_ref, dec_ref, carry_ref, *, tr, cum):
    x = x_ref[0]
    h = _rms(x, g_ref[...]).astype(_BF)
    qn_ref[0] = (_dot(h, w_ref[:, C_QN:C_NSA]) * Q_SCALE).astype(_BF)
    z = _dot(h, w_ref[:, C_NSA:C_KW])
    nsa_ref[0] = z
    nsab_ref[0] = z.astype(_BF)
    z = _dot(h, w_ref[:, C_KW:C_QF])
    kw_ref[0] = z
    kwb_ref[0] = z.astype(_BF)
    qf_ref[0] = (_dot(h, w_ref[:, C_QF:C_FOX]) * Q_SCALE).astype(_BF)
    z = _dot(h, w_ref[:, C_FOX:C_MISC])
    fox_ref[0] = z
    foxb_ref[0] = z.astype(_BF)
    zm = _dot(h, w_ref[:, C_MISC:C_TOT]) + b_ref[...]
    lane = _iota((tr, LANES), 1)
    sg = jax.nn.sigmoid(zm)
    ls = jnp.minimum(zm, 0.0) - jnp.log1p(jnp.exp(-jnp.abs(zm)))
    if cum:
        @pl.when(pl.program_id(1) == 0)
        def _():
            carry_ref[...] = jnp.zeros(carry_ref.shape, _F32)
        row = _iota((tr, tr), 0)
        col = _iota((tr, tr), 1)
        tri = jnp.where(col <= row, 1.0, 0.0).astype(_BF)
        cs = _dot3(tri, ls) + carry_ref[0:1, :]
        carry_ref[...] = jnp.broadcast_to(cs[tr - 1:tr, :], carry_ref.shape)
        pieces = _split3(cs * LOG2E)
        r = _iota((LANES, LANES), 0) - L_CUM
        c = _iota((LANES, LANES), 1)
        head_row = (r >= 0) & (r < FOX_HEADS)
        dec = jnp.zeros((tr, LANES), _F32)
        for j, piece in enumerate(pieces):
            put = jnp.where(head_row & (c == L_NEG + N_PIECE * r + j), -1.0,
                            jnp.where(head_row & (c == L_POS + N_PIECE * r + j), 1.0, 0.0)).astype(_BF)
            dec = dec + _dot(piece, put)
        dec_ref[0] = dec.astype(_BF)
    else:
        cs = jnp.zeros((tr, LANES), _F32)
        dec_ref[0] = jnp.zeros(dec_ref.shape[1:], _BF)
    misc_ref[0] = jnp.where(lane < L_LOGF, sg, jnp.where(lane < L_CUM, ls, jnp.where(lane < L_CUM + FOX_HEADS, cs, 0.0)))


def _proj(x, g, w, b, tr, cum):
    nb, t, d = x.shape
    grid = (nb, t // tr)
    row = lambda width: pl.BlockSpec((1, tr, width), lambda bi, i: (bi, i, 0))
    const = lambda shape: pl.BlockSpec(shape, lambda bi, i: (0,) * len(shape))
    shp = lambda width, dt: jax.ShapeDtypeStruct((nb, t, width), dt)
    return pl.pallas_call(
        functools.partial(_proj_kernel, tr=tr, cum=cum),
        grid=grid,
        in_specs=[row(d), const((1, d)), const((d, C_TOT)), const((1, LANES))],
        out_specs=(row(512), row(512), row(512), row(256), row(256), row(512), row(1024), row(1024), row(LANES), row(LANES)),
        out_shape=(shp(512, _BF), shp(512, _F32), shp(512, _BF), shp(256, _F32), shp(256, _BF), shp(512, _BF),
                   shp(1024, _F32), shp(1024, _BF), shp(LANES, _F32), shp(LANES, _BF)),
        scratch_shapes=[pltpu.VMEM((SUBLANES, LANES), _F32)],
        compiler_params=_cparams(("arbitrary", "arbitrary")),
        name="in_proj",
    )(x, g, w, b)


def _compress(lhs_bf, w1p, pe8, w1raw, w2p, n_valid):
    nc = lhs_bf.shape[0]
    hcat = _dot(lhs_bf, w1p)
    cst = _dot(pe8, w1raw)[0:1]

    def hidden(g):
        a = hcat[:, g * 256:g * 256 + CMP_HIDDEN]
        b = hcat[:, g * 256 + CMP_HIDDEN:(g + 1) * 256]
        return _gelu_tanh(a + pltpu.roll(b, nc - 1, 0) + cst)

    hh = jnp.concatenate([hidden(0), hidden(1)], axis=1).astype(_BF)
    out = _dot(hh, w2p)
    return jnp.where(_iota(out.shape, 0) < n_valid, out, 0.0)


def _compress_prompt_kernel(xk_ref, xv_ref, w1k_ref, w1v_ref, pe_ref, w1r_ref, w2_ref, kc_ref, vc_ref, *, nc, ncp):
    for idx, (x_ref, w1, out_ref) in enumerate(((xk_ref, w1k_ref, kc_ref), (xv_ref, w1v_ref, vc_ref))):
        pieces = [x_ref[0, pl.ds(p, nc, stride=CMP_STRIDE), :] for p in range(CMP_STRIDE)]
        lhs = jnp.concatenate(pieces, axis=1).astype(_BF)
        out = _compress(lhs, w1[...], pe_ref[idx], w1r_ref[idx], w2_ref[idx], nc - 1)
        if ncp > nc:
            out = jnp.concatenate([out, jnp.zeros((ncp - nc, LANES), _F32)], axis=0)
        out_ref[0] = out.astype(_BF)


def _compress_prompt(nsa_state, cw, nc, ncp):
    b, t, _ = nsa_state.shape
    const = lambda a: pl.BlockSpec(a.shape, lambda bi: (0,) * a.ndim)
    return pl.pallas_call(
        functools.partial(_compress_prompt_kernel, nc=nc, ncp=ncp),
        grid=(b,),
        in_specs=[pl.BlockSpec((1, t, LANES), lambda bi: (bi, 0, 0)), pl.BlockSpec((1, t, LANES), lambda bi: (bi, 0, 1)),
                  const(cw["w1k"]), const(cw["w1v"]), const(cw["pe8"]), const(cw["w1raw"]), const(cw["w2p"])],
        out_specs=(pl.BlockSpec((1, ncp, LANES), lambda bi: (bi, 0, 0)),) * 2,
        out_shape=(jax.ShapeDtypeStruct((b, ncp, LANES), _BF),) * 2,
        compiler_params=_cparams(("arbitrary",)),
        name="compress_prompt",
    )(nsa_state, nsa_state, cw["w1k"], cw["w1v"], cw["pe8"], cw["w1raw"], cw["w2p"])


def _attend_tile(lhs_ref, cols, row_blocks, k_t, v_t, m_ref, l_ref, acc_ref, add_fn=None):
    n = len(row_blocks)
    score = lambda b: _dot_nt(lhs_ref[row_blocks[b], cols], k_t)
    scores = [score(b) for b in range(min(AHEAD, n))]
    for b, rows in enumerate(row_blocks):
        if b + AHEAD < n:
            scores.append(score(b + AHEAD))
        _attend_block(scores[b], v_t, m_ref, l_ref, acc_ref, rows, None if add_fn is None else functools.partial(add_fn, b))
        scores[b] = None


def _attend_block(s, v_t, m_ref, l_ref, acc_ref, rows, add_fn=None):
    n_col = s.shape[1] // LANES
    cols = [s[:, c * LANES:(c + 1) * LANES] for c in range(n_col)]
    if add_fn is not None:
        cols = [add_fn(c, x) for c, x in enumerate(cols)]
    mx = cols[0]
    for x in cols[1:]:
        mx = jnp.maximum(mx, x)
    m_old = m_ref[rows, :]
    m_new = jnp.maximum(m_old, jnp.max(mx, axis=-1, keepdims=True))
    alpha = jnp.exp2(m_old - m_new)
    ps = [jnp.exp2(x - m_new) for x in cols]
    tot = ps[0]
    for x in ps[1:]:
        tot = tot + x
    l_ref[rows, :] = alpha * l_ref[rows, :] + jnp.sum(tot, axis=-1, keepdims=True)
    p = jnp.concatenate([x.astype(_BF) for x in ps], axis=1)
    acc_ref[rows, :] = alpha * acc_ref[rows, :] + _dot(p, v_t)
    m_ref[rows, :] = m_new


def _softmax_init(m_ref, l_ref, acc_ref):
    m_ref[...] = jnp.full(m_ref.shape, M_INIT, _F32)
    l_ref[...] = jnp.zeros(l_ref.shape, _F32)
    acc_ref[...] = jnp.zeros(acc_ref.shape, _F32)


def _online_update(s, pv_fn, m_ref, l_ref, acc_ref):
    m_old = m_ref[...]
    m_new = jnp.maximum(m_old, jnp.max(s, axis=-1, keepdims=True))
    alpha = jnp.exp2(m_old - m_new)
    p = jnp.exp2(s - m_new)
    l_ref[...] = alpha * l_ref[...] + jnp.sum(p, axis=-1, keepdims=True)
    acc_ref[...] = alpha * acc_ref[...] + pv_fn(p.astype(_BF))
    m_ref[...] = m_new


def _masked_softmax_full(s):
    m = jnp.max(s, axis=-1, keepdims=True)
    p = jnp.exp2(s - m)
    l = jnp.sum(p, axis=-1, keepdims=True)
    return jnp.where(m > NEG_HALF, p / l, 0.0)


def _overlap(ncp, n_cmp, n_blk, blk_axis=1):
    shape = (ncp, n_blk) if blk_axis == 1 else (n_blk, ncp)
    c = _iota(shape, 1 - blk_axis)
    j = _iota(shape, blk_axis)
    r = SLC_BLOCK // CMP_STRIDE
    hit = (c >= r * j - (CMP_BLOCK // CMP_STRIDE - 1)) & (c <= r * j + r - 1) & (c < n_cmp)
    return jnp.where(hit, 1.0, 0.0).astype(_BF)


def _select_blocks(imp, t_pos, n_slc, n_top, blk_axis=1):
    j = _iota(imp.shape, blk_axis)
    cur = _div(t_pos, SLC_BLOCK)
    forced = (j == 0) | (j == cur) | (j == cur - 1)
    score = jnp.where(forced, FORCE_SCORE, imp)
    score = jnp.where(j * SLC_BLOCK > t_pos, -FORCE_SCORE, score)
    score = jnp.where(j >= n_slc, M_INIT, score)
    sel = jnp.zeros(imp.shape, jnp.bool_)
    for _ in range(n_top):
        mx = jnp.max(score, axis=blk_axis, keepdims=True)
        idx = jnp.min(jnp.where(score == mx, j, 1 << 20), axis=blk_axis, keepdims=True)
        hit = j == idx
        sel = sel | hit
        score = jnp.where(hit, REMOVED, score)
    return jnp.where(sel, 0.0, NEG)


def _block_onehot(k0, tk, n_lanes):
    s = _iota((tk, n_lanes), 0)
    j = _iota((tk, n_lanes), 1)
    return jnp.where(j == _div(k0 + s, SLC_BLOCK), 1.0, 0.0).astype(_BF)


def _nsa_prompt_kernel(qn_ref, misc_ref, kc_ref, vc_ref, ks_ref, vs_ref, kw_ref, vw_ref, tz_ref, nbh_ref, nbl_ref,
                       on_ref, lhs_ref, oc_ref, imp_ref, ms_ref, ls_ref, as_ref, mw_ref, lw_ref, aw_ref,
                       *, n_cmp, ncp, n_slc):
    qt = pl.program_id(1)
    q0 = qt * TQ
    rows = NSA_HEADS * TQ
    n_rb = TQ // RB
    row_blocks = [slice(b * RB, (b + 1) * RB) for b in range(rows // RB)]
    q_cols = slice(0, LANES)
    lane = _iota((TQ, LANES), 1)
    low = lane < HEAD_DIM

    q = qn_ref[0]
    for g in range(NSA_KV_HEADS):
        for r in range(NSA_GROUP):
            blk = q[:, r * LANES:(r + 1) * LANES]
            blk = jnp.where(low if g == 0 else jnp.logical_not(low), blk, jnp.zeros_like(blk))
            lhs_ref[(g * NSA_GROUP + r) * TQ:(g * NSA_GROUP + r + 1) * TQ, q_cols] = blk

    u = _iota((LANES, ncp), 0)
    c = _iota((LANES, ncp), 1)
    place = ((u < NEAR_U) & (c == CPT * qt - NEAR_BACK + u)) | ((u == NEAR_U) & (c >= CPT * qt + CPT))
    place = jnp.where(place, 1.0, 0.0).astype(_BF)
    kc = kc_ref[0]
    vc = vc_ref[0]
    ovl_t = _overlap(ncp, n_cmp, LANES, blk_axis=0)

    def cmp_scores(b):
        rs = row_blocks[b]
        return _dot_nt(lhs_ref[rs, q_cols], kc) + _dot(nbh_ref[rs, :], place) + _dot(nbl_ref[rs, :], place)

    n_blk = len(row_blocks)
    scores = [cmp_scores(b) for b in range(min(AHEAD, n_blk))]
    group_sum = {}
    for b, rs in enumerate(row_blocks):
        if b + AHEAD < n_blk:
            scores.append(cmp_scores(b + AHEAD))
        pc = _masked_softmax_full(scores[b])
        scores[b] = None
        oc_ref[rs, :] = _dot(pc.astype(_BF), vc)
        head, part = divmod(b, n_rb)
        g, r = divmod(head, NSA_GROUP)
        group_sum[(g, part)] = pc if r == 0 else group_sum[(g, part)] + pc
        if r == NSA_GROUP - 1:
            hi, lo = _split2(group_sum.pop((g, part)))
            imp_ref[:, g * TQ + part * RB:g * TQ + (part + 1) * RB] = _dot_nt(ovl_t, hi) + _dot_nt(ovl_t, lo)

    t_pos = q0 + _mod(_iota((1, NSA_KV_HEADS * TQ), 1), TQ)
    msel = _select_blocks(imp_ref[...], t_pos, n_slc, min(N_SELECT, n_slc), blk_axis=0).T.astype(_BF)
    for g in range(NSA_KV_HEADS):
        for r in range(NSA_GROUP):
            lhs_ref[(g * NSA_GROUP + r) * TQ:(g * NSA_GROUP + r + 1) * TQ, LANES:2 * LANES] = msel[g * TQ:(g + 1) * TQ]

    def near_bias(half):
        def add(b, c, x):
            head, part = divmod(b, n_rb)
            return x + tz_ref[head, part * RB:(part + 1) * RB, half * TQ + c * LANES:half * TQ + (c + 1) * LANES]
        return add

    _softmax_init(ms_ref, ls_ref, as_ref)
    all_cols = slice(0, 2 * LANES)

    def sel_tile(k0, half):
        kaug = jnp.concatenate([ks_ref[0, pl.ds(k0, TQ), :], _block_onehot(k0, TQ, LANES)], axis=1)
        _attend_tile(lhs_ref, all_cols, row_blocks, kaug, vs_ref[0, pl.ds(k0, TQ), :], ms_ref, ls_ref, as_ref,
                     None if half is None else near_bias(half))

    def far_body(kt, carry):
        sel_tile(pl.multiple_of(kt * TQ, TQ), None)
        return carry
    lax.fori_loop(0, jnp.maximum(qt - 1, 0), far_body, 0)

    @pl.when(qt >= 1)
    def _():
        sel_tile(pl.multiple_of(q0 - TQ, TQ), 0)
    sel_tile(pl.multiple_of(q0, TQ), 1)

    _softmax_init(mw_ref, lw_ref, aw_ref)
    n_wt = WINDOW // TQ

    def edge_mask(b, c, x):
        r0 = (b % n_rb) * RB
        if (c + 1) * LANES <= r0:
            return jnp.full(x.shape, NEG, _F32)
        if c * LANES > r0 + RB - 1:
            return x
        i2 = r0 + _iota((RB, LANES), 0)
        c2 = c * LANES + _iota((RB, LANES), 1)
        return jnp.where(c2 > i2, x, NEG)

    for w in range(n_wt + 1):
        def win_tile(w=w):
            k0 = pl.multiple_of(q0 - (n_wt - w) * TQ, TQ)
            add = edge_mask if w == 0 else near_bias(0) if w == n_wt - 1 else near_bias(1) if w == n_wt else None
            _attend_tile(lhs_ref, q_cols, row_blocks, kw_ref[0, pl.ds(k0, TQ), :], vw_ref[0, pl.ds(k0, TQ), :],
                         mw_ref, lw_ref, aw_ref, add)
        if w == n_wt:
            win_tile()
        else:
            pl.when(qt >= n_wt - w)(win_tile)

    gates = misc_ref[0]
    for r in range(NSA_GROUP):
        halves = []
        for g in range(NSA_KV_HEADS):
            h = g * NSA_GROUP + r
            sl = slice(h * TQ, (h + 1) * TQ)
            gate = lambda kind: gates[:, L_GATE + kind * NSA_HEADS + h:L_GATE + kind * NSA_HEADS + h + 1]
            halves.append(gate(0) * oc_ref[sl, :] + gate(1) * (as_ref[sl, :] / ls_ref[sl, :])
                          + gate(2) * (aw_ref[sl, :] / lw_ref[sl, :]))
        on_ref[0, :, r * LANES:(r + 1) * LANES] = jnp.where(low, halves[0], halves[1])


def _nsa_prompt(qn, misc, kcmp, vcmp, nsab, kwb, tz, nbh, nbl, n_cmp, ncp, n_slc):
    b, t, _ = qn.shape
    rows = NSA_HEADS * TQ
    seq = lambda arr, blk: pl.BlockSpec((1, t, LANES), lambda bi, qi, blk=blk: (bi, 0, blk))
    const = lambda a: pl.BlockSpec(a.shape, lambda bi, qi: (0,) * a.ndim)
    stat = pltpu.VMEM((rows, LANES), _F32)
    return pl.pallas_call(
        functools.partial(_nsa_prompt_kernel, n_cmp=n_cmp, ncp=ncp, n_slc=n_slc),
        grid=(b, t // TQ),
        in_specs=[pl.BlockSpec((1, TQ, 512), lambda bi, qi: (bi, qi, 0)),
                  pl.BlockSpec((1, TQ, LANES), lambda bi, qi: (bi, qi, 0)),
                  pl.BlockSpec((1, ncp, LANES), lambda bi, qi: (bi, 0, 0)),
                  pl.BlockSpec((1, ncp, LANES), lambda bi, qi: (bi, 0, 0)),
                  seq(nsab, 2), seq(nsab, 3), seq(kwb, 0), seq(kwb, 1), const(tz), const(nbh), const(nbl)],
        out_specs=pl.BlockSpec((1, TQ, 512), lambda bi, qi: (bi, qi, 0)),
        out_shape=jax.ShapeDtypeStruct((b, t, 512), _F32),
        scratch_shapes=[pltpu.VMEM((rows, 2 * LANES), _BF), stat, pltpu.VMEM((LANES, NSA_KV_HEADS * TQ), _F32),
                        stat, stat, stat, stat, stat, stat],
        compiler_params=_cparams(("arbitrary", "arbitrary")),
        name="nsa_prompt",
    )(qn, misc, kcmp, vcmp, nsab, nsab, kwb, kwb, tz, nbh, nbl)


def _fox_prompt_kernel(qf_ref, dq_ref, kf_ref, vf_ref, dk_ref, of_ref, lhs_ref, m_ref, l_ref, acc_ref):
    p = pl.program_id(1)
    qt = pl.program_id(2)
    q0 = qt * TQF
    lane = _iota((TQF, LANES), 1)
    low = lane < HEAD_DIM
    q = qf_ref[0]
    dq = dq_ref[0]
    zero = jnp.zeros_like(q)
    one = jnp.ones_like(q)
    for a in range(2):
        head = 2 * p + a
        take_neg = (lane >= L_NEG + N_PIECE * head) & (lane < L_NEG + N_PIECE * (head + 1))
        take_pos = (lane >= L_POS + N_PIECE * head) & (lane < L_POS + N_PIECE * (head + 1))
        lhs_ref[a * TQF:(a + 1) * TQF, 0:LANES] = jnp.where(low if a == 0 else jnp.logical_not(low), q, zero)
        lhs_ref[a * TQF:(a + 1) * TQF, LANES:2 * LANES] = jnp.where(take_neg, one, jnp.where(take_pos, dq, zero))
    _softmax_init(m_ref, l_ref, acc_ref)
    n_rb = TQF // RB
    row_blocks = [slice(b * RB, (b + 1) * RB) for b in range(2 * n_rb)]
    all_cols = slice(0, 2 * LANES)

    def causal(b, c, x):
        r0 = (b % n_rb) * RB
        if (c + 1) * LANES <= r0:
            return x
        if c * LANES > r0 + RB - 1:
            return jnp.full(x.shape, NEG, _F32)
        i2 = r0 + _iota((RB, LANES), 0)
        c2 = c * LANES + _iota((RB, LANES), 1)
        return jnp.where(c2 <= i2, x, NEG)

    ones_pos = (jnp.clip(lane - (L_POS - 1), 0, 1) * jnp.clip(2 * L_POS - lane, 0, 1)).astype(_F32).astype(_BF)

    def tile(k0, diag):
        dk = dk_ref[0, pl.ds(k0, TQF), :]
        kaug = jnp.concatenate([kf_ref[0, pl.ds(k0, TQF), :], jnp.where(lane < L_POS, dk, ones_pos)], axis=1)
        _attend_tile(lhs_ref, all_cols, row_blocks, kaug, vf_ref[0, pl.ds(k0, TQF), :], m_ref, l_ref, acc_ref,
                     causal if diag else None)

    def far_body(kt, carry):
        tile(pl.multiple_of(kt * TQF, TQF), False)
        return carry
    lax.fori_loop(0, qt, far_body, 0)
    tile(pl.multiple_of(q0, TQF), True)
    o = acc_ref[...] / l_ref[...]
    of_ref[0] = jnp.where(low, o[0:TQF], o[TQF:2 * TQF])


def _fox_prompt(qf, dec, foxb):
    b, t, _ = qf.shape
    n_pair = FOX_HEADS // 2
    stat = pltpu.VMEM((2 * TQF, LANES), _F32)
    return pl.pallas_call(
        _fox_prompt_kernel,
        grid=(b, n_pair, t // TQF),
        in_specs=[pl.BlockSpec((1, TQF, LANES), lambda bi, p, qi: (bi, qi, p)),
                  pl.BlockSpec((1, TQF, LANES), lambda bi, p, qi: (bi, qi, 0)),
                  pl.BlockSpec((1, t, LANES), lambda bi, p, qi: (bi, 0, p)),
                  pl.BlockSpec((1, t, LANES), lambda bi, p, qi: (bi, 0, n_pair + p)),
                  pl.BlockSpec((1, t, LANES), lambda bi, p, qi: (bi, 0, 0))],
        out_specs=pl.BlockSpec((1, TQF, LANES), lambda bi, p, qi: (bi, qi, p)),
        out_shape=jax.ShapeDtypeStruct((b, t, 512), _F32),
        scratch_shapes=[pltpu.VMEM((2 * TQF, 2 * LANES), _BF), stat, stat, stat],
        compiler_params=_cparams(("arbitrary", "arbitrary", "arbitrary")),
        name="fox_prompt",
    )(qf, dec, foxb, foxb, dec)


def _post_kernel(on_ref, of_ref, x_ref, gn_ref, gf_ref, wo_ref, gp_ref, y_ref):
    half = on_ref.shape[-1]
    a = _rms(on_ref[0], gn_ref[...]).astype(_BF)
    f = _rms(of_ref[0], gf_ref[...]).astype(_BF)
    mixed = _dot(a, wo_ref[0:half, :]) + _dot(f, wo_ref[half:2 * half, :])
    y_ref[0] = x_ref[0] + _rms(mixed, gp_ref[...])


def _post(o_n, o_f, x, gn, gf, wo, gp, tr):
    nb, t, d = x.shape
    half = o_n.shape[-1]
    row = lambda width: pl.BlockSpec((1, tr, width), lambda bi, i: (bi, i, 0))
    const = lambda shape: pl.BlockSpec(shape, lambda bi, i: (0,) * len(shape))
    return pl.pallas_call(
        _post_kernel,
        grid=(nb, t // tr),
        in_specs=[row(half), row(half), row(d), const((1, half)), const((1, half)), const((2 * half, d)), const((1, d))],
        out_specs=row(d),
        out_shape=jax.ShapeDtypeStruct((nb, t, d), _F32),
        compiler_params=_cparams(("arbitrary", "arbitrary")),
        name="out_proj",
    )(o_n, o_f, x, gn, gf, wo, gp)


def _ffn_kernel(x_ref, gpre_ref, wg_ref, wu_ref, wd_ref, gpost_ref, y_ref):
    x = x_ref[0]
    h = _rms(x, gpre_ref[...]).astype(_BF)
    act = (jax.nn.silu(_dot(h, wg_ref[...])) * _dot(h, wu_ref[...])).astype(_BF)
    y_ref[0] = x + _rms(_dot(act, wd_ref[...]), gpost_ref[...])


def _ffn(x, gpre, wg, wu, wd, gpost, tr):
    nb, t, d = x.shape
    dff = wg.shape[1]
    row = pl.BlockSpec((1, tr, d), lambda bi, i: (bi, i, 0))
    const = lambda shape: pl.BlockSpec(shape, lambda bi, i: (0,) * len(shape))
    return pl.pallas_call(
        _ffn_kernel,
        grid=(nb, t // tr),
        in_specs=[row, const((1, d)), const((d, dff)), const((d, dff)), const((dff, d)), const((1, d))],
        out_specs=row,
        out_shape=jax.ShapeDtypeStruct((nb, t, d), _F32),
        compiler_params=_cparams(("arbitrary", "arbitrary")),
        name="ffn",
    )(x, gpre, wg, wu, wd, gpost)


def _row_select(rows8, pieces):
    out = jnp.zeros((SUBLANES, pieces[0].shape[-1]), _F32)
    for i, piece in enumerate(pieces):
        out = jnp.where(rows8 == i, jnp.broadcast_to(piece, out.shape), out)
    return out


def _pad_rows(a, n):
    return jnp.concatenate([a, jnp.zeros((n - a.shape[0],) + a.shape[1:], a.dtype)], axis=0)


def _nsa_sample_kernel(pt_ref, *refs, pps, past, n_new, n_cmp, ncp, n_slc):
    page_refs = refs[:pps]
    (qn_ref, misc_ref, new_ref, win_ref, kwn_ref, w1k_ref, w1v_ref, pe_ref, w1r_ref, w2_ref,
     bc_ref, bs_ref, bw_ref, on_ref, lk_ref, lv_ref, kst_ref, vst_ref, m_ref, l_ref, acc_ref) = refs[pps:]
    del pt_ref
    j = pl.program_id(1)
    n_steps = pl.num_programs(1)
    cpp = PAGE_SIZE // CMP_STRIDE
    ppc = WINDOW // PAGE_SIZE
    n_pages = past // PAGE_SIZE

    rr = _iota((PAGE_SIZE, PAGE_SIZE), 0)
    tok = _iota((PAGE_SIZE, PAGE_SIZE), 1)
    regroup = jnp.where(tok == CMP_STRIDE * _mod(rr, cpp) + _div(rr, cpp), 1.0, 0.0).astype(_BF)

    for k in range(pps):
        pg = j * pps + k
        ref = page_refs[k]
        kst_ref[pg] = ref[0, 0, 2].astype(_BF)
        vst_ref[pg] = ref[0, 0, 3].astype(_BF)
        c0 = pl.multiple_of(pg * cpp, cpp)
        for src, dst in ((0, lk_ref), (1, lv_ref)):
            x = _dot_nt(regroup, ref[0, 0, src].astype(_BF))
            for p in range(CMP_STRIDE):
                dst[pl.ds(c0, cpp), p * LANES:(p + 1) * LANES] = x[p * cpp:(p + 1) * cpp, :]

    @pl.when(j == n_steps - 1)
    def _():
        nc = past // CMP_STRIDE
        new = new_ref[0]
        ks_new = _pad_rows(new[:, 2 * LANES:3 * LANES], PAGE_SIZE).astype(_BF)
        vs_new = _pad_rows(new[:, 3 * LANES:4 * LANES], PAGE_SIZE).astype(_BF)

        def padc(a):
            return a if ncp == nc else _pad_rows(a, ncp)
        kc = padc(_compress(lk_ref[...].astype(_BF), w1k_ref[...], pe_ref[0], w1r_ref[0], w2_ref[0], n_cmp)).astype(_BF)
        vc = padc(_compress(lv_ref[...].astype(_BF), w1v_ref[...], pe_ref[1], w1r_ref[1], w2_ref[1], n_cmp)).astype(_BF)

        rows8 = _iota((SUBLANES, LANES), 0)
        lane = _iota((SUBLANES, LANES), 1)
        grp_low = rows8 < NSA_GROUP
        q = qn_ref[0].astype(_F32)
        blocks = []
        for i in range(n_new):
            blk = jnp.zeros((SUBLANES, LANES), _F32)
            for r in range(NSA_GROUP):
                piece = jnp.broadcast_to(q[i:i + 1, r * LANES:(r + 1) * LANES], (SUBLANES, LANES))
                blk = jnp.where(_mod(rows8, NSA_GROUP) == r, piece, blk)
            on_group = jnp.logical_not(jnp.logical_xor(lane < HEAD_DIM, grp_low))
            blocks.append(jnp.where(on_group, blk, jnp.zeros_like(blk)))
        lq = jnp.concatenate(blocks, axis=0).astype(_BF)

        pc = _masked_softmax_full(_dot_nt(lq, kc) + bc_ref[...])
        o_c = _dot(pc.astype(_BF), vc)

        nl = 2 * LANES
        ovl = _overlap(ncp, n_cmp, nl)
        sums = []
        for i in range(n_new):
            blk = pc[i * SUBLANES:(i + 1) * SUBLANES]
            rr8 = _iota(blk.shape, 0)
            for g in range(NSA_KV_HEADS):
                in_group = (rr8 < NSA_GROUP) if g == 0 else (rr8 >= NSA_GROUP)
                sums.append(jnp.sum(jnp.where(in_group, blk, 0.0), axis=0, keepdims=True))
        hi, lo = _split2(_row_select(_iota((SUBLANES, ncp), 0), sums))
        imp = _dot(hi, ovl) + _dot(lo, ovl)
        t_pos = past + _div(_iota((SUBLANES, 1), 0), NSA_KV_HEADS)
        msel = _select_blocks(imp, t_pos, n_slc, min(N_SELECT, n_slc))
        rows8n = _iota((SUBLANES, nl), 0)
        mrows = []
        for i in range(n_new):
            m0 = jnp.broadcast_to(msel[2 * i:2 * i + 1], (SUBLANES, nl))
            m1 = jnp.broadcast_to(msel[2 * i + 1:2 * i + 2], (SUBLANES, nl))
            mrows.append(jnp.where(rows8n < NSA_GROUP, m0, m1))
        mrows = jnp.concatenate(mrows, axis=0).astype(_BF)

        def block_mask(k0, n_keys):
            s_i = _iota((nl, n_keys), 1)
            j_i = _iota((nl, n_keys), 0)
            return _dot(mrows, jnp.where(j_i == _div(k0 + s_i, SLC_BLOCK), 1.0, 0.0).astype(_BF))

        _softmax_init(m_ref, l_ref, acc_ref)

        def past_chunk(pg0, bias):
            kt = jnp.concatenate([kst_ref[pg0 + u] for u in range(ppc)], axis=1)
            vt = jnp.concatenate([vst_ref[pg0 + u] for u in range(ppc)], axis=1)
            s = _dot(lq, kt) + block_mask(pg0 * PAGE_SIZE, ppc * PAGE_SIZE)
            if bias is not None:
                s = s + bias
            _online_update(s, lambda p: _dot_nt(p, vt), m_ref, l_ref, acc_ref)

        def far_body(ci, carry):
            past_chunk(ci * ppc, None)
            return carry
        lax.fori_loop(0, n_pages // ppc - 1, far_body, 0)
        past_chunk(n_pages - ppc, bs_ref[:, 0:WINDOW])
        s_new = _dot_nt(lq, ks_new) + block_mask(past, PAGE_SIZE) + bs_ref[:, WINDOW:WINDOW + PAGE_SIZE]
        _online_update(s_new, lambda p: _dot(p, vs_new), m_ref, l_ref, acc_ref)
        o_s = acc_ref[...] / l_ref[...]

        kwn = kwn_ref[0]
        kw_new = _pad_rows(kwn[:, 0:LANES], PAGE_SIZE).astype(_BF)
        vw_new = _pad_rows(kwn[:, LANES:2 * LANES], PAGE_SIZE).astype(_BF)
        sw = jnp.concatenate([_dot(lq, win_ref[0, 0, 0].astype(_BF)), _dot_nt(lq, kw_new)], axis=1) + bw_ref[...]
        mw = jnp.max(sw, axis=-1, keepdims=True)
        pw = jnp.exp2(sw - mw)
        pwb = pw.astype(_BF)
        o_w = (_dot_nt(pwb[:, 0:WINDOW], win_ref[0, 0, 1].astype(_BF)) + _dot(pwb[:, WINDOW:], vw_new)) \
            / jnp.sum(pw, axis=-1, keepdims=True)

        misc = misc_ref[0]
        out_rows = []
        for i in range(n_new):
            sl = slice(i * SUBLANES, (i + 1) * SUBLANES)
            g_row = jnp.broadcast_to(misc[i:i + 1, :], (SUBLANES, LANES))

            def gcol(kind, g_row=g_row):
                pick = lane == L_GATE + kind * NSA_HEADS + rows8
                return jnp.sum(jnp.where(pick, g_row, 0.0), axis=-1, keepdims=True)
            o_blk = gcol(0) * o_c[sl] + gcol(1) * o_s[sl] + gcol(2) * o_w[sl]
            pieces = [jnp.where(lane[0:1] < HEAD_DIM, o_blk[r:r + 1], o_blk[NSA_GROUP + r:NSA_GROUP + r + 1])
                      for r in range(NSA_GROUP)]
            out_rows.append(jnp.concatenate(pieces, axis=1))
        on_ref[0] = _row_select(_iota((SUBLANES, 4 * LANES), 0), out_rows)


def _nsa_sample(layer, page_table, cache_t, qn, misc, new, win_t, kwn, cw, bc, bs, bw, past, n_new, n_cmp, ncp, n_slc, pps):
    n_seq, n_pages = page_table.shape
    n_steps = n_pages // pps
    rows = NSA_HEADS * n_new
    nc = past // CMP_STRIDE

    def page_spec(k):
        return pl.BlockSpec((1, 1, 4, LANES, PAGE_SIZE), lambda s, j, pt, k=k: (layer, pt[s, j * pps + k], 0, 0, 0))
    per_seq = lambda a: pl.BlockSpec((1,) + a.shape[1:], lambda s, j, pt: (s,) + (0,) * (a.ndim - 1))
    const = lambda a: pl.BlockSpec(a.shape, lambda s, j, pt: (0,) * a.ndim)
    consts = [cw["w1k"], cw["w1v"], cw["pe8"], cw["w1raw"], cw["w2p"], bc, bs, bw]
    grid_spec = pltpu.PrefetchScalarGridSpec(
        num_scalar_prefetch=1,
        grid=(n_seq, n_steps),
        in_specs=[page_spec(k) for k in range(pps)] + [per_seq(qn), per_seq(misc), per_seq(new),
                                                        pl.BlockSpec((1, 1) + win_t.shape[2:], lambda s, j, pt: (layer, s, 0, 0, 0)),
                                                        per_seq(kwn)]
        + [const(a) for a in consts],
        out_specs=pl.BlockSpec((1, SUBLANES, 4 * LANES), lambda s, j, pt: (s, 0, 0)),
        scratch_shapes=[pltpu.VMEM((nc, CMP_STRIDE * LANES), _F32), pltpu.VMEM((nc, CMP_STRIDE * LANES), _F32),
                        pltpu.VMEM((n_pages, LANES, PAGE_SIZE), _BF), pltpu.VMEM((n_pages, LANES, PAGE_SIZE), _BF),
                        pltpu.VMEM((rows, 1), _F32), pltpu.VMEM((rows, 1), _F32), pltpu.VMEM((rows, LANES), _F32)],
    )
    return pl.pallas_call(
        functools.partial(_nsa_sample_kernel, pps=pps, past=past, n_new=n_new, n_cmp=n_cmp, ncp=ncp, n_slc=n_slc),
        grid_spec=grid_spec,
        out_shape=jax.ShapeDtypeStruct((n_seq, SUBLANES, 4 * LANES), _F32),
        compiler_params=_cparams(("arbitrary", "arbitrary")),
        name="nsa_sample",
    )(page_table, *([cache_t] * pps), qn, misc, new, win_t, kwn, *consts)


def _fox_sample_kernel(pt_ref, *refs, pps, n_new):
    kv_refs = refs[:pps]
    lf_refs = refs[pps:2 * pps]
    (qf_ref, kvn_ref, lfn_ref, of_ref, q_ref, e_ref, carry_ref, m_ref, l_ref, acc_ref) = refs[2 * pps:]
    del pt_ref
    j = pl.program_id(1)
    n_steps = pl.num_programs(1)
    rows = FOX_HEADS * n_new
    width = FOX_HEADS * HEAD_DIM
    srow = _iota((PAGE_SIZE, PAGE_SIZE), 0)
    scol = _iota((PAGE_SIZE, PAGE_SIZE), 1)
    later = jnp.where(srow > scol, 1.0, 0.0).astype(_BF)

    def suffix(lf):
        hi, mid, lo = _split3(lf)
        return _dot(hi, later) + _dot(mid, later) + _dot(lo, later)

    def decay(rt):
        return (jnp.concatenate([rt] * n_new, axis=0) - e_ref[...]) * LOG2E

    @pl.when(j == 0)
    def _():
        _softmax_init(m_ref, l_ref, acc_ref)
        rows8 = _iota((SUBLANES, width), 0)
        lane = _iota((SUBLANES, width), 1)
        q = qf_ref[0].astype(_F32)
        blocks = []
        for i in range(n_new):
            piece = jnp.broadcast_to(q[i:i + 1, :], (SUBLANES, width))
            blocks.append(jnp.where(_div(lane, HEAD_DIM) == rows8, piece, jnp.zeros_like(piece)))
        q_ref[...] = jnp.concatenate(blocks, axis=0).astype(_BF)
        lfn = lfn_ref[0]
        rt = suffix(lfn)
        e_ref[...] = jnp.concatenate([rt[:, i:i + 1] for i in range(n_new)], axis=0)
        carry_ref[...] = jnp.broadcast_to(jnp.sum(lfn, axis=-1, keepdims=True), carry_ref.shape)
        kvn = kvn_ref[0]
        ri = _div(_iota((rows, PAGE_SIZE), 0), SUBLANES)
        ci = _iota((rows, PAGE_SIZE), 1)
        s = _dot_nt(q_ref[...], kvn[:, 0:width].astype(_BF)) + decay(rt) + jnp.where(ci <= ri, 0.0, NEG)
        v_new = kvn[:, width:2 * width].astype(_BF)
        _online_update(s, lambda p: _dot(p, v_new), m_ref, l_ref, acc_ref)

    kts, vts, rts = [], [], []
    carry = carry_ref[...]
    for k in range(pps):
        lf = lf_refs[k][0, 0]
        kts.append(kv_refs[k][0, 0, 0].astype(_BF))
        vts.append(kv_refs[k][0, 0, 1].astype(_BF))
        rts.append(suffix(lf) + carry)
        carry = carry + jnp.sum(lf, axis=-1, keepdims=True)
    carry_ref[...] = carry
    vt = jnp.concatenate(vts, axis=1)
    s = _dot(q_ref[...], jnp.concatenate(kts, axis=1)) + decay(jnp.concatenate(rts, axis=1))
    _online_update(s, lambda p: _dot_nt(p, vt), m_ref, l_ref, acc_ref)

    @pl.when(j == n_steps - 1)
    def _():
        o = acc_ref[...] / l_ref[...]
        rows8 = _iota((SUBLANES, width), 0)
        lane = _iota((SUBLANES, width), 1)
        out_rows = []
        for i in range(n_new):
            blk = jnp.where(_div(lane, HEAD_DIM) == rows8, o[i * SUBLANES:(i + 1) * SUBLANES], 0.0)
            out_rows.append(jnp.sum(blk, axis=0, keepdims=True))
        of_ref[0] = _row_select(rows8, out_rows)


def _fox_sample(layer, page_table, cache_t, cache_lft, qf, kvn, lfn, n_new, pps):
    n_seq, n_pages = page_table.shape
    n_steps = n_pages // pps
    rows = FOX_HEADS * n_new
    width = FOX_HEADS * HEAD_DIM

    def page_idx(s, j, pt, k):
        return pt[s, n_pages - 1 - (j * pps + k)]
    kv_spec = lambda k: pl.BlockSpec((1, 1, 2, width, PAGE_SIZE),
                                     lambda s, j, pt, k=k: (layer, page_idx(s, j, pt, k), 0, 0, 0))
    lf_spec = lambda k: pl.BlockSpec((1, 1, FOX_HEADS, PAGE_SIZE),
                                     lambda s, j, pt, k=k: (layer, page_idx(s, j, pt, k), 0, 0))
    per_seq = lambda a: pl.BlockSpec((1,) + a.shape[1:], lambda s, j, pt: (s,) + (0,) * (a.ndim - 1))
    grid_spec = pltpu.PrefetchScalarGridSpec(
        num_scalar_prefetch=1,
        grid=(n_seq, n_steps),
        in_specs=[kv_spec(k) for k in range(pps)] + [lf_spec(k) for k in range(pps)] + [per_seq(a) for a in (qf, kvn, lfn)],
        out_specs=pl.BlockSpec((1, SUBLANES, width), lambda s, j, pt: (s, 0, 0)),
        scratch_shapes=[pltpu.VMEM((rows, width), _BF), pltpu.VMEM((rows, 1), _F32), pltpu.VMEM((SUBLANES, LANES), _F32),
                        pltpu.VMEM((rows, 1), _F32), pltpu.VMEM((rows, 1), _F32), pltpu.VMEM((rows, width), _F32)],
    )
    return pl.pallas_call(
        functools.partial(_fox_sample_kernel, pps=pps, n_new=n_new),
        grid_spec=grid_spec,
        out_shape=jax.ShapeDtypeStruct((n_seq, SUBLANES, width), _F32),
        compiler_params=_cparams(("arbitrary", "arbitrary")),
        name="fox_sample",
    )(page_table, *([cache_t] * pps), *([cache_lft] * pps), qf, kvn, lfn)


def _nsa_perm():
    idx = np.zeros(NSA_HEADS * HEAD_DIM, np.int32)
    for r in range(NSA_GROUP):
        for g in range(NSA_KV_HEADS):
            for d in range(HEAD_DIM):
                idx[r * LANES + g * HEAD_DIM + d] = (g * NSA_GROUP + r) * HEAD_DIM + d
    return idx


def _layer_weights(l, w_in, b_gate, b_forget, cmp_pe, cmp_w1, cmp_w2, grp_norm_nsa, grp_norm_fox, w_o):
    perm = _nsa_perm()
    w = w_in[l]
    o_qn, o_nsa, o_kw, o_g, o_qf, o_kf, o_f = 0, 512, 1024, 1280, 1304, 1816, 2840
    f_cols = w[:, o_f:o_f + FOX_HEADS]
    misc = jnp.concatenate([w[:, o_g:o_g + 3 * NSA_HEADS], f_cols, f_cols,
                            jnp.zeros((w.shape[0], LANES - 3 * NSA_HEADS - 2 * FOX_HEADS), w.dtype)], axis=1)
    wp = jnp.concatenate([w[:, o_qn:o_nsa][:, perm], w[:, o_nsa:o_kw], w[:, o_kw:o_g], w[:, o_qf:o_kf],
                          w[:, o_kf:o_f], misc], axis=1).astype(_BF)
    bias = jnp.concatenate([b_gate[l].reshape(-1), b_forget[l], b_forget[l],
                            jnp.zeros((LANES - 3 * NSA_HEADS - 2 * FOX_HEADS,), _F32)]).reshape(1, LANES)

    def w1_layout(w1):
        w1 = w1.reshape(2, CMP_STRIDE, HEAD_DIM, CMP_HIDDEN)
        z = jnp.zeros((CMP_STRIDE, HEAD_DIM, CMP_HIDDEN), w1.dtype)
        g0 = jnp.concatenate([w1[0], w1[1], z, z], axis=-1)
        g1 = jnp.concatenate([z, z, w1[0], w1[1]], axis=-1)
        return jnp.concatenate([g0, g1], axis=1).reshape(CMP_STRIDE * LANES, 4 * CMP_HIDDEN).astype(_BF)

    def w2_layout(w2):
        z = jnp.zeros_like(w2)
        return jnp.concatenate([jnp.concatenate([w2, z], axis=1), jnp.concatenate([z, w2], axis=1)], axis=0).astype(_BF)
    cw = dict(
        w1k=w1_layout(cmp_w1[l, 0]), w1v=w1_layout(cmp_w1[l, 1]),
        pe8=jnp.broadcast_to(cmp_pe[l].reshape(2, 1, CMP_BLOCK * HEAD_DIM), (2, SUBLANES, CMP_BLOCK * HEAD_DIM)).astype(_BF),
        w1raw=cmp_w1[l].astype(_BF),
        w2p=jnp.stack([w2_layout(cmp_w2[l, 0]), w2_layout(cmp_w2[l, 1])]),
    )
    gn = grp_norm_nsa[l][perm].reshape(1, -1)
    gf = grp_norm_fox[l].reshape(1, -1)
    wo = jnp.concatenate([w_o[l][:NSA_HEADS * HEAD_DIM][perm], w_o[l][NSA_HEADS * HEAD_DIM:]], axis=0).astype(_BF)
    return wp, bias, cw, gn, gf, wo


def kernel(x_prompt, x_sample, cache_nsa_kv, cache_fox_kv, cache_fox_logf, state_win_kv, page_table, rel_bias,
           norm_mix_pre, norm_mix_post, norm_ffn_pre, norm_ffn_post, w_in, b_gate, b_forget, cmp_pe, cmp_w1, cmp_w2,
           grp_norm_nsa, grp_norm_fox, w_o, w_ffn_gate, w_ffn_up, w_ffn_down):
    depth = w_in.shape[0]
    b, t, d = x_prompt.shape
    n_seq, n_new, _ = x_sample.shape
    n_pages = page_table.shape[1]
    past = n_pages * PAGE_SIZE
    n_pool = cache_nsa_kv.shape[1]
    n_win = state_win_kv.shape[2]
    assert t % TQF == 0 and t % TQ == 0 and t >= WINDOW and WINDOW % TQ == 0 and WINDOW // TQ >= 2 and TQ > T5_FAR
    assert TQ % RB == 0 and TQF % RB == 0
    assert n_new <= SUBLANES and SLC_BLOCK >= n_new and n_win == WINDOW
    assert n_pages % (WINDOW // PAGE_SIZE) == 0 and past >= 2 * WINDOW
    assert (n_seq * n_new) % SUBLANES == 0

    nc_p = t // CMP_STRIDE
    ncp_p = -(-nc_p // LANES) * LANES
    n_slc_p = t // SLC_BLOCK
    assert n_slc_p <= LANES
    nc_s = past // CMP_STRIDE
    ncp_s = -(-nc_s // LANES) * LANES
    n_slc_s = past // SLC_BLOCK + 1
    assert n_slc_s <= 2 * LANES
    pps = math.gcd(n_pages, 8)

    tz, nbh, nbl = _bias_prompt(rel_bias)
    bc, bs, bw = _bias_sample(rel_bias, past, n_new, nc_s - 1, ncp_s)

    nsa_t = jnp.transpose(cache_nsa_kv, (0, 1, 3, 4, 5, 2)).reshape(depth, n_pool, 4, NSA_KV_HEADS * HEAD_DIM, PAGE_SIZE)
    fox_t = jnp.transpose(cache_fox_kv, (0, 1, 3, 4, 5, 2)).reshape(depth, n_pool, 2, FOX_HEADS * HEAD_DIM, PAGE_SIZE)
    logf_t = jnp.transpose(cache_fox_logf, (0, 1, 3, 2))
    win_t = jnp.transpose(state_win_kv, (0, 1, 3, 4, 5, 2)).reshape(depth, n_seq, 2, NSA_KV_HEADS * HEAD_DIM, n_win)

    row1 = lambda a: a.reshape(1, -1)
    pad_new = lambda a: jnp.pad(a, ((0, 0), (0, SUBLANES - n_new), (0, 0)))
    rs = n_seq * n_new
    xp, xs = x_prompt, x_sample.reshape(1, rs, d)
    outs_p, outs_s = [], []
    for l in range(depth):
        wp, bias, cw, gn, gf, wo = _layer_weights(l, w_in, b_gate, b_forget, cmp_pe, cmp_w1, cmp_w2,
                                                  grp_norm_nsa, grp_norm_fox, w_o)
        wg, wu, wd = w_ffn_gate[l].astype(_BF), w_ffn_up[l].astype(_BF), w_ffn_down[l].astype(_BF)
        g_pre, g_post = row1(norm_mix_pre[l]), row1(norm_mix_post[l])
        g_fpre, g_fpost = row1(norm_ffn_pre[l]), row1(norm_ffn_post[l])

        qn, nsa, nsab, kw, kwb, qf, fox, foxb, misc, dec = _proj(xp, g_pre, wp, bias, TR_PROJ, True)
        kcmp, vcmp = _compress_prompt(nsa, cw, nc_p, ncp_p)
        o_n = _nsa_prompt(qn, misc, kcmp, vcmp, nsab, kwb, tz, nbh, nbl, nc_p - 1, ncp_p, n_slc_p)
        o_f = _fox_prompt(qf, dec, foxb)
        xp = _post(o_n, o_f, xp, gn, gf, wo, g_post, TR_POST)
        xp = _ffn(xp, g_fpre, wg, wu, wd, g_fpost, TR_FFN)
        outs_p.append((nsa.reshape(b, t, 4, NSA_KV_HEADS, HEAD_DIM), fox.reshape(b, t, 2, FOX_HEADS, HEAD_DIM),
                       misc[:, :, L_LOGF:L_LOGF + FOX_HEADS],
                       kw[:, t - WINDOW:].reshape(b, WINDOW, 2, NSA_KV_HEADS, HEAD_DIM)))

        tr_s = math.gcd(rs, TR_PROJ)
        qn, nsa, nsab, kw, kwb, qf, fox, foxb, misc, _ = _proj(xs, g_pre, wp, bias, tr_s, False)
        per = lambda a: a.reshape(n_seq, n_new, a.shape[-1])
        nsa_s, fox_s, kw_s, misc_s = per(nsa), per(fox), per(kw), per(misc)
        logf_s = misc_s[:, :, L_LOGF:L_LOGF + FOX_HEADS]
        o_n = _nsa_sample(l, page_table, nsa_t, pad_new(per(qn)), pad_new(misc_s), pad_new(nsa_s), win_t,
                          pad_new(kw_s), cw, bc, bs, bw, past, n_new, nc_s - 1, ncp_s, n_slc_s, pps)
        kvn = jnp.pad(fox_s, ((0, 0), (0, PAGE_SIZE - n_new), (0, 0)))
        lfn = jnp.pad(jnp.swapaxes(logf_s, 1, 2), ((0, 0), (0, 0), (0, PAGE_SIZE - n_new)))
        o_f = _fox_sample(l, page_table, fox_t, logf_t, pad_new(per(qf)), kvn, lfn, n_new, pps)
        o_n = o_n[:, :n_new].reshape(1, rs, -1)
        o_f = o_f[:, :n_new].reshape(1, rs, -1)
        xs = _post(o_n, o_f, xs, gn, gf, wo, g_post, math.gcd(rs, TR_POST))
        xs = _ffn(xs, g_fpre, wg, wu, wd, g_fpost, math.gcd(rs, TR_FFN))
        win_new = jnp.concatenate([state_win_kv[l][:, n_new:], kw_s.reshape(n_seq, n_new, 2, NSA_KV_HEADS, HEAD_DIM)], axis=1)
        outs_s.append((nsa_s.reshape(n_seq, n_new, 4, NSA_KV_HEADS, HEAD_DIM),
                       fox_s.reshape(n_seq, n_new, 2, FOX_HEADS, HEAD_DIM), logf_s, win_new))

    stack = lambda outs, i: jnp.stack([o[i] for o in outs], axis=0)
    return (xp, xs.reshape(n_seq, n_new, d), stack(outs_p, 0), stack(outs_p, 1), stack(outs_p, 2), stack(outs_p, 3),
            stack(outs_s, 0), stack(outs_s, 1), stack(outs_s, 2), stack(outs_s, 3))
```

```python
import functools
import math

import numpy as np
import jax
import jax.numpy as jnp
from jax import lax
from jax.experimental import pallas as pl
from jax.experimental.pallas import tpu as pltpu

HEAD_DIM = 64
NSA_HEADS = 8
FOX_HEADS = 8
NSA_KV_HEADS = 2
NSA_GROUP = NSA_HEADS // NSA_KV_HEADS
CMP_BLOCK = 32
CMP_STRIDE = 16
CMP_HIDDEN = 2 * HEAD_DIM
SLC_BLOCK = 64
N_SELECT = 16
WINDOW = 512
T5_BUCKETS = 32
T5_EXACT = T5_BUCKETS // 2
T5_MAX_DIST = 128
PAGE_SIZE = 128
NORM_EPS = 1e-6
FORCE_SCORE = 1e4
LOG2E = math.log2(math.e)
Q_SCALE = HEAD_DIM ** -0.5 * LOG2E

LANES = 128
SUBLANES = 8
VMEM_LIMIT = 56 * 1024 * 1024

_F32 = jnp.float32
_BF = jnp.bfloat16
NEG = -(2.0 ** 100)
NEG_HALF = -(2.0 ** 99)
M_INIT = -3.0e38
REMOVED = -3.4e38


def _t5_thresholds():
    n = np.arange(1, 4 * T5_MAX_DIST)
    large = T5_EXACT + (np.log(n / T5_EXACT) / math.log(T5_MAX_DIST / T5_EXACT) * (T5_BUCKETS - T5_EXACT)).astype(np.int64)
    return tuple(int(n[np.argmax(large >= k)]) for k in range(T5_EXACT + 1, T5_BUCKETS))


_T5_THR = _t5_thresholds()
T5_FAR = _T5_THR[-1]


def _log2(n):
    assert n & (n - 1) == 0
    return n.bit_length() - 1


TQ = 512
TQF = 2048
TK = 512
CB = 256
AHEAD = 3
MV = LANES + SUBLANES
TR_PROJ = 256
TR_POST = 512
TR_FFN = 256
CPT = TQ // CMP_STRIDE
NEAR_BACK = -(-(T5_FAR + CMP_BLOCK - 1) // CMP_STRIDE) - 1
NEAR_U = CPT + NEAR_BACK

C_QN, C_NSA, C_KW, C_QF, C_FOX, C_MISC = 0, 512, 1024, 1280, 1792, 2816
C_TOT = 2944
L_GATE, L_LOGF, L_CUM = 0, 24, 32
N_PIECE = 3
L_NEG, L_POS = 0, N_PIECE * FOX_HEADS


def _dot(a, b):
    return jnp.dot(a, b, preferred_element_type=_F32)


def _dot_nt(a, b):
    return lax.dot_general(a, b, (((1,), (1,)), ((), ())), preferred_element_type=_F32)


def _split3(x):
    hi = x.astype(_BF)
    r1 = x - hi.astype(_F32)
    mid = r1.astype(_BF)
    lo = (r1 - mid.astype(_F32)).astype(_BF)
    return hi, mid, lo


def _split2(x):
    hi = x.astype(_BF)
    return hi, (x - hi.astype(_F32)).astype(_BF)


def _dot3(a_bf, x):
    hi, mid, lo = _split3(x)
    return _dot(a_bf, hi) + _dot(a_bf, mid) + _dot(a_bf, lo)


def _rms(x, g):
    ms = jnp.mean(x * x, axis=-1, keepdims=True)
    return x * lax.rsqrt(ms + NORM_EPS) * g


def _gelu_tanh(x):
    c = math.sqrt(2.0 / math.pi)
    return x * (0.5 * (1.0 + jnp.tanh(c * (x + 0.044715 * (x * x * x)))))


def _iota(shape, dim):
    return lax.broadcasted_iota(jnp.int32, shape, dim)


def _div(x, n):
    return jnp.right_shift(x, _log2(n))


def _mod(x, n):
    return x & (n - 1)


def _cparams(sem):
    return pltpu.CompilerParams(dimension_semantics=sem, vmem_limit_bytes=VMEM_LIMIT)


def _t5_rel(dist, rel_ref, h):
    d = jnp.minimum(dist, T5_MAX_DIST - 1)
    big = jnp.full(d.shape, T5_EXACT, jnp.int32)
    for thr in _T5_THR:
        big = big + jnp.where(d >= thr, 1, 0)
    bkt = jnp.where(d < T5_EXACT, d, big)
    far = rel_ref[T5_BUCKETS - 1, h]
    val = jnp.zeros(d.shape, _F32)
    for k in range(T5_BUCKETS - 1):
        val = jnp.where(bkt == k, (rel_ref[k, h] - far) * LOG2E, val)
    return val


def _t5_masked(dist, rel_ref, h):
    return jnp.where(dist < 0, NEG, _t5_rel(jnp.maximum(dist, 0), rel_ref, h))


def _bias_prompt_kernel(rel_ref, dt_ref, nbh_ref, nbl_ref):
    def body(h, carry):
        c = _iota((LANES, LANES), 0)
        i = _iota((LANES, LANES), 1)
        dt_ref[h, 0] = _t5_masked(i - c, rel_ref, h)
        dt_ref[h, 1] = _t5_masked(i - c + LANES, rel_ref, h)
        u = _iota((LANES, TQ), 0)
        i = _iota((LANES, TQ), 1)
        near = _t5_masked(i + (CMP_STRIDE * NEAR_BACK - (CMP_BLOCK - 1)) - CMP_STRIDE * u, rel_ref, h)
        nb = jnp.where(u < NEAR_U, near, jnp.where(u == NEAR_U, NEG, 0.0))
        hi, lo = _split2(nb)
        cols = pl.ds(pl.multiple_of(h * TQ, TQ), TQ)
        nbh_ref[:, cols] = hi
        nbl_ref[:, cols] = lo
        return carry
    lax.fori_loop(0, NSA_HEADS, body, 0)


def _bias_prompt(rel_bias):
    return pl.pallas_call(
        _bias_prompt_kernel,
        out_shape=(jax.ShapeDtypeStruct((NSA_HEADS, 2, LANES, LANES), _F32),
                   jax.ShapeDtypeStruct((LANES, NSA_HEADS * TQ), _BF),
                   jax.ShapeDtypeStruct((LANES, NSA_HEADS * TQ), _BF)),
        in_specs=[pl.BlockSpec(memory_space=pltpu.SMEM)],
        name="t5_bias_prompt",
    )(rel_bias)


def _bias_sample_kernel(rel_ref, bc_ref, bs_ref, bw_ref, *, past, n_new, n_cmp):
    def table(shape, dist_fn, extra_invalid=None):
        out = jnp.zeros(shape, _F32)
        r = _iota(shape, 0)
        c = _iota(shape, 1)
        i = _div(r, NSA_HEADS)
        dist = dist_fn(i, c)
        for h in range(NSA_HEADS):
            v = _t5_masked(dist, rel_ref, h)
            out = jnp.where(_mod(r, NSA_HEADS) == h, v, out)
        if extra_invalid is not None:
            out = jnp.where(extra_invalid(i, c, dist), NEG, out)
        return out

    bc_ref[...] = table(bc_ref.shape, lambda i, c: past + i - (CMP_STRIDE * c + CMP_BLOCK - 1),
                        lambda i, c, d: c >= n_cmp)
    bs_ref[...] = table(bs_ref.shape, lambda i, c: i + WINDOW - c)
    bw_ref[...] = table(bw_ref.shape, lambda i, c: i + WINDOW - c,
                        lambda i, c, d: (d >= WINDOW) | (c >= WINDOW + n_new))


def _bias_sample(rel_bias, past, n_new, n_cmp, ncp):
    rows = NSA_HEADS * n_new
    return pl.pallas_call(
        functools.partial(_bias_sample_kernel, past=past, n_new=n_new, n_cmp=n_cmp),
        out_shape=(jax.ShapeDtypeStruct((rows, ncp), _F32),
                   jax.ShapeDtypeStruct((rows, WINDOW + PAGE_SIZE), _F32),
                   jax.ShapeDtypeStruct((rows, WINDOW + PAGE_SIZE), _F32)),
        in_specs=[pl.BlockSpec(memory_space=pltpu.SMEM)],
        name="t5_bias_sample",
    )(rel_bias)


def _proj_kernel(x_ref, g_ref, w_ref, b_ref, *refs, tr, prompt):
    if prompt:
        (qn_ref, nsa_ref, nsab_ref, kw_ref, kwb_ref, qf_ref, fox_ref, foxb_ref, misc_ref, dec_ref,
         qnt_ref, qft_ref, dect_ref, misct_ref, vst_ref, vwt_ref, vft_ref, carry_ref) = refs
    else:
        qn_ref, nsa_ref, nsab_ref, kw_ref, kwb_ref, qf_ref, fox_ref, foxb_ref, misc_ref = refs
    x = x_ref[0]
    h = _rms(x, g_ref[...]).astype(_BF)
    qn = _dot(h, w_ref[:, C_QN:C_NSA]) * Q_SCALE
    qn_ref[0] = qn.astype(_BF)
    z_nsa = _dot(h, w_ref[:, C_NSA:C_KW])
    nsa_ref[0] = z_nsa
    nsab_ref[0] = z_nsa.astype(_BF)
    z_kw = _dot(h, w_ref[:, C_KW:C_QF])
    kw_ref[0] = z_kw
    kwb_ref[0] = z_kw.astype(_BF)
    qf = _dot(h, w_ref[:, C_QF:C_FOX]) * Q_SCALE
    qf_ref[0] = qf.astype(_BF)
    z_fox = _dot(h, w_ref[:, C_FOX:C_MISC])
    fox_ref[0] = z_fox
    foxb_ref[0] = z_fox.astype(_BF)
    zm = _dot(h, w_ref[:, C_MISC:C_TOT]) + b_ref[...]
    lane = _iota((tr, LANES), 1)
    sg = jax.nn.sigmoid(zm)
    ls = jnp.minimum(zm, 0.0) - jnp.log1p(jnp.exp(-jnp.abs(zm)))
    if not prompt:
        misc_ref[0] = jnp.where(lane < L_LOGF, sg, jnp.where(lane < L_CUM, ls, 0.0))
        return

    @pl.when(pl.program_id(1) == 0)
    def _():
        carry_ref[...] = jnp.zeros(carry_ref.shape, _F32)
    row = _iota((tr, tr), 0)
    col = _iota((tr, tr), 1)
    tri = jnp.where(col <= row, 1.0, 0.0).astype(_BF)
    cs = _dot3(tri, ls) + carry_ref[0:1, :]
    carry_ref[...] = jnp.broadcast_to(cs[tr - 1:tr, :], carry_ref.shape)
    pieces = _split3(cs * LOG2E)
    r = _iota((LANES, LANES), 0) - L_CUM
    c = _iota((LANES, LANES), 1)
    head_row = (r >= 0) & (r < FOX_HEADS)
    dec = jnp.zeros((tr, LANES), _F32)
    for j, piece in enumerate(pieces):
        put = jnp.where(head_row & (c == L_NEG + N_PIECE * r + j), -1.0,
                        jnp.where(head_row & (c == L_POS + N_PIECE * r + j), 1.0, 0.0)).astype(_BF)
        dec = dec + _dot(piece, put)
    dec_ref[0] = dec.astype(_BF)
    misc = jnp.where(lane < L_LOGF, sg, jnp.where(lane < L_CUM, ls, jnp.where(lane < L_CUM + FOX_HEADS, cs, 0.0)))
    misc_ref[0] = misc
    qnt_ref[0] = qn.T.astype(_BF)
    qft_ref[0] = qf.T.astype(_BF)
    dect_ref[0] = dec.T.astype(_BF)
    misct_ref[0] = misc.T
    vst_ref[0, 0] = z_nsa[:, 3 * LANES:4 * LANES].T.astype(_BF)
    vwt_ref[0, 0] = z_kw[:, LANES:2 * LANES].T.astype(_BF)
    vft_ref[0, 0] = z_fox[:, FOX_HEADS * HEAD_DIM:].T.astype(_BF)


def _proj(x, g, w, b, tr, prompt):
    nb, t, d = x.shape
    grid = (nb, t // tr)
    row = lambda width: pl.BlockSpec((1, tr, width), lambda bi, i: (bi, i, 0))
    const = lambda shape: pl.BlockSpec(shape, lambda bi, i: (0,) * len(shape))
    shp = lambda width, dt: jax.ShapeDtypeStruct((nb, t, width), dt)
    out_specs = [row(512), row(512), row(512), row(256), row(256), row(512), row(1024), row(1024), row(LANES)]
    out_shape = [shp(512, _BF), shp(512, _F32), shp(512, _BF), shp(256, _F32), shp(256, _BF), shp(512, _BF),
                 shp(1024, _F32), shp(1024, _BF), shp(LANES, _F32)]
    scratch = []
    if prompt:
        assert TK % tr == 0 and t % TK == 0
        per = TK // tr
        colm = lambda rows: pl.BlockSpec((1, rows, tr), lambda bi, i: (bi, 0, i))
        tile = lambda rows: pl.BlockSpec((1, 1, rows, tr), lambda bi, i: (bi, i // per, 0, i % per))
        tshp = lambda rows, dt: jax.ShapeDtypeStruct((nb, rows, t), dt)
        t4 = lambda rows: jax.ShapeDtypeStruct((nb, t // TK, rows, TK), _BF)
        out_specs += [row(LANES), colm(512), colm(512), colm(LANES), colm(LANES), tile(LANES), tile(LANES), tile(512)]
        out_shape += [shp(LANES, _BF), tshp(512, _BF), tshp(512, _BF), tshp(LANES, _BF), tshp(LANES, _F32),
                      t4(LANES), t4(LANES), t4(512)]
        scratch = [pltpu.VMEM((SUBLANES, LANES), _F32)]
    return pl.pallas_call(
        functools.partial(_proj_kernel, tr=tr, prompt=prompt),
        grid=grid,
        in_specs=[row(d), const((1, d)), const((d, C_TOT)), const((1, LANES))],
        out_specs=tuple(out_specs),
        out_shape=tuple(out_shape),
        scratch_shapes=scratch,
        compiler_params=_cparams(("arbitrary", "arbitrary")),
        name="in_proj",
    )(x, g, w, b)


def _compress(lhs_bf, w1p, pe8, w1raw, w2p, n_valid):
    nc = lhs_bf.shape[0]
    hcat = _dot(lhs_bf, w1p)
    cst = _dot(pe8, w1raw)[0:1]

    def hidden(g):
        a = hcat[:, g * 256:g * 256 + CMP_HIDDEN]
        b = hcat[:, g * 256 + CMP_HIDDEN:(g + 1) * 256]
        return _gelu_tanh(a + pltpu.roll(b, nc - 1, 0) + cst)

    hh = jnp.concatenate([hidden(0), hidden(1)], axis=1).astype(_BF)
    out = _dot(hh, w2p)
    return jnp.where(_iota(out.shape, 0) < n_valid, out, 0.0)


def _compress_prompt_kernel(xk_ref, xv_ref, w1k_ref, w1v_ref, pe_ref, w1r_ref, w2_ref, kc_ref, vc_ref, *, nc, ncp):
    for idx, (x_ref, w1, out_ref) in enumerate(((xk_ref, w1k_ref, kc_ref), (xv_ref, w1v_ref, vc_ref))):
        pieces = [x_ref[0, pl.ds(p, nc, stride=CMP_STRIDE), :] for p in range(CMP_STRIDE)]
        lhs = jnp.concatenate(pieces, axis=1).astype(_BF)
        out = _compress(lhs, w1[...], pe_ref[idx], w1r_ref[idx], w2_ref[idx], nc - 1)
        if ncp > nc:
            out = jnp.concatenate([out, jnp.zeros((ncp - nc, LANES), _F32)], axis=0)
        out_ref[0] = (out.T if idx == 1 else out).astype(_BF)


def _compress_prompt(nsa_state, cw, nc, ncp):
    b, t, _ = nsa_state.shape
    const = lambda a: pl.BlockSpec(a.shape, lambda bi: (0,) * a.ndim)
    return pl.pallas_call(
        functools.partial(_compress_prompt_kernel, nc=nc, ncp=ncp),
        grid=(b,),
        in_specs=[pl.BlockSpec((1, t, LANES), lambda bi: (bi, 0, 0)), pl.BlockSpec((1, t, LANES), lambda bi: (bi, 0, 1)),
                  const(cw["w1k"]), const(cw["w1v"]), const(cw["pe8"]), const(cw["w1raw"]), const(cw["w2p"])],
        out_specs=(pl.BlockSpec((1, ncp, LANES), lambda bi: (bi, 0, 0)), pl.BlockSpec((1, LANES, ncp), lambda bi: (bi, 0, 0))),
        out_shape=(jax.ShapeDtypeStruct((b, ncp, LANES), _BF), jax.ShapeDtypeStruct((b, LANES, ncp), _BF)),
        compiler_params=_cparams(("arbitrary",)),
        name="compress_prompt",
    )(nsa_state, nsa_state, cw["w1k"], cw["w1v"], cw["pe8"], cw["w1raw"], cw["w2p"])


def _attend_tile_t(lhst_ref, k_t, v_t, m_ref, acc_ref, blocks, add_fn=None, feat=slice(None)):
    n = len(blocks)
    cols = lambda b: slice(b * CB, (b + 1) * CB)
    score = lambda b: _dot(k_t, lhst_ref[feat, cols(b)])
    scores = [score(b) for b in blocks[:AHEAD]]
    for i, b in enumerate(blocks):
        if i + AHEAD < n:
            scores.append(score(blocks[i + AHEAD]))
        s = scores[i]
        scores[i] = None
        if add_fn is not None:
            s = add_fn(b, s)
        m_old = m_ref[0:1, cols(b)]
        m_new = jnp.maximum(m_old, jnp.max(s, axis=0, keepdims=True))
        alpha = jnp.exp2(m_old - m_new)
        p = jnp.exp2(s - m_new).astype(_BF)
        acc_ref[:, cols(b)] = alpha * acc_ref[:, cols(b)] + _dot(v_t, p)
        m_ref[0:1, cols(b)] = m_new


def _attend_init_t(m_ref, acc_ref):
    m_ref[...] = jnp.full(m_ref.shape, M_INIT, _F32)
    acc_ref[...] = jnp.zeros(acc_ref.shape, _F32)


def _edit_blocks(s, fn):
    rows = []
    for a in range(s.shape[0] // LANES):
        pieces = []
        for q in range(s.shape[1] // LANES):
            piece = s[a * LANES:(a + 1) * LANES, q * LANES:(q + 1) * LANES]
            new = fn(a, q, piece)
            pieces.append(piece if new is None else new)
        rows.append(jnp.concatenate(pieces, axis=1))
    return jnp.concatenate(rows, axis=0)


def _with_ones(v_t):
    return jnp.concatenate([v_t, jnp.ones((MV - v_t.shape[0], v_t.shape[1]), _BF)], axis=0)


def _softmax_init(m_ref, l_ref, acc_ref):
    m_ref[...] = jnp.full(m_ref.shape, M_INIT, _F32)
    l_ref[...] = jnp.zeros(l_ref.shape, _F32)
    acc_ref[...] = jnp.zeros(acc_ref.shape, _F32)


def _online_update(s, pv_fn, m_ref, l_ref, acc_ref):
    m_old = m_ref[...]
    m_new = jnp.maximum(m_old, jnp.max(s, axis=-1, keepdims=True))
    alpha = jnp.exp2(m_old - m_new)
    p = jnp.exp2(s - m_new)
    l_ref[...] = alpha * l_ref[...] + jnp.sum(p, axis=-1, keepdims=True)
    acc_ref[...] = alpha * acc_ref[...] + pv_fn(p.astype(_BF))
    m_ref[...] = m_new


def _masked_softmax_full(s):
    m = jnp.max(s, axis=-1, keepdims=True)
    p = jnp.exp2(s - m)
    l = jnp.sum(p, axis=-1, keepdims=True)
    return jnp.where(m > NEG_HALF, p / l, 0.0)


def _overlap(ncp, n_cmp, n_blk, blk_axis=1):
    shape = (ncp, n_blk) if blk_axis == 1 else (n_blk, ncp)
    c = _iota(shape, 1 - blk_axis)
    j = _iota(shape, blk_axis)
    r = SLC_BLOCK // CMP_STRIDE
    hit = (c >= r * j - (CMP_BLOCK // CMP_STRIDE - 1)) & (c <= r * j + r - 1) & (c < n_cmp)
    return jnp.where(hit, 1.0, 0.0).astype(_BF)


def _select_blocks(imp, t_pos, n_slc, n_top, blk_axis=1):
    j = _iota(imp.shape, blk_axis)
    cur = _div(t_pos, SLC_BLOCK)
    forced = (j == 0) | (j == cur) | (j == cur - 1)
    score = jnp.where(forced, FORCE_SCORE, imp)
    score = jnp.where(j * SLC_BLOCK > t_pos, -FORCE_SCORE, score)
    score = jnp.where(j >= n_slc, M_INIT, score)
    sel = jnp.zeros(imp.shape, jnp.bool_)
    for _ in range(n_top):
        mx = jnp.max(score, axis=blk_axis, keepdims=True)
        idx = jnp.min(jnp.where(score == mx, j, 1 << 20), axis=blk_axis, keepdims=True)
        hit = j == idx
        sel = sel | hit
        score = jnp.where(hit, REMOVED, score)
    return jnp.where(sel, 0.0, NEG)


def _block_onehot(k0, tk, n_lanes):
    s = _iota((tk, n_lanes), 0)
    j = _iota((tk, n_lanes), 1)
    return jnp.where(j == _div(k0 + s, SLC_BLOCK), 1.0, 0.0).astype(_BF)


def _nsa_prompt_kernel(qnt_ref, misct_ref, kc_ref, vct_ref, ks_ref, vst_ref, kw_ref, vwt_ref, dt_ref, nbh_ref, nbl_ref,
                       on_ref, lhst_ref, oct_ref, imp_ref, ms_ref, as_ref, mw_ref, aw_ref, *, n_cmp, ncp, n_slc):
    qt = pl.program_id(1)
    q0 = qt * TQ
    n_col = NSA_HEADS * TQ
    n_cb = TQ // CB
    blocks = list(range(n_col // CB))
    cols = lambda b: slice(b * CB, (b + 1) * CB)
    head_cols = lambda h: slice(h * TQ, (h + 1) * TQ)
    low = _iota((LANES, TQ), 0) < HEAD_DIM

    for g in range(NSA_KV_HEADS):
        for r in range(NSA_GROUP):
            blk = qnt_ref[0, r * LANES:(r + 1) * LANES, :]
            lhst_ref[0:LANES, head_cols(g * NSA_GROUP + r)] = jnp.where(low if g == 0 else jnp.logical_not(low),
                                                                       blk, jnp.zeros_like(blk))

    c = _iota((ncp, LANES), 0)
    u = _iota((ncp, LANES), 1)
    place = ((u < NEAR_U) & (c == CPT * qt - NEAR_BACK + u)) | ((u == NEAR_U) & (c >= CPT * qt + CPT))
    place = jnp.where(place, 1.0, 0.0).astype(_BF)
    kc = kc_ref[0]
    vct = vct_ref[0]
    ovl_t = _overlap(ncp, n_cmp, LANES, blk_axis=0)

    def cmp_scores(b):
        return _dot(kc, lhst_ref[0:LANES, cols(b)]) + _dot(place, nbh_ref[:, cols(b)]) + _dot(place, nbl_ref[:, cols(b)])

    scores = [cmp_scores(b) for b in blocks[:AHEAD]]
    group_sum = {}
    for b in blocks:
        if b + AHEAD < len(blocks):
            scores.append(cmp_scores(b + AHEAD))
        s = scores[b]
        scores[b] = None
        m = jnp.max(s, axis=0, keepdims=True)
        p = jnp.exp2(s - m)
        pc = jnp.where(m > NEG_HALF, p / jnp.sum(p, axis=0, keepdims=True), 0.0)
        oct_ref[:, cols(b)] = _dot(vct, pc.astype(_BF))
        head, part = divmod(b, n_cb)
        g, r = divmod(head, NSA_GROUP)
        group_sum[(g, part)] = pc if r == 0 else group_sum[(g, part)] + pc
        if r == NSA_GROUP - 1:
            hi, lo = _split2(group_sum.pop((g, part)))
            imp_ref[:, g * TQ + part * CB:g * TQ + (part + 1) * CB] = _dot(ovl_t, hi) + _dot(ovl_t, lo)

    t_pos = q0 + _mod(_iota((1, NSA_KV_HEADS * TQ), 1), TQ)
    msel = _select_blocks(imp_ref[...], t_pos, n_slc, min(N_SELECT, n_slc), blk_axis=0).astype(_BF)
    for g in range(NSA_KV_HEADS):
        for r in range(NSA_GROUP):
            lhst_ref[LANES:2 * LANES, head_cols(g * NSA_GROUP + r)] = msel[:, g * TQ:(g + 1) * TQ]

    c2 = _iota((LANES, LANES), 0)
    i2 = _iota((LANES, LANES), 1)
    n_kb = TK // LANES

    def near_add(kind):
        def add(b, s):
            head, part = divmod(b, n_cb)

            def piece_fn(a, q, x):
                rel = part * (CB // LANES) + q - a + (0 if kind == 'diag' else n_kb)
                if kind == 'diag' and rel < 0:
                    return jnp.full(x.shape, NEG, _F32)
                if kind == 'wprev' and rel > n_kb:
                    return jnp.full(x.shape, NEG, _F32)
                if kind == 'wprev' and rel == n_kb:
                    return jnp.where(c2 > i2, x, NEG)
                if rel == 0:
                    return x + dt_ref[head, 0]
                if rel == 1:
                    return x + dt_ref[head, 1]
                return None
            return _edit_blocks(s, piece_fn)
        return add

    _attend_init_t(ms_ref, as_ref)

    def sel_tile(kt, kind):
        k0 = pl.multiple_of(kt * TK, TK)
        kaug = jnp.concatenate([ks_ref[0, pl.ds(k0, TK), :], _block_onehot(k0, TK, LANES)], axis=1)
        _attend_tile_t(lhst_ref, kaug, _with_ones(vst_ref[0, kt]), ms_ref, as_ref, blocks,
                       None if kind is None else near_add(kind))

    def far_body(kt, carry):
        sel_tile(kt, None)
        return carry
    lax.fori_loop(0, jnp.maximum(qt - 1, 0), far_body, 0)

    @pl.when(qt >= 1)
    def _():
        sel_tile(qt - 1, 'prev')
    sel_tile(qt, 'diag')

    _attend_init_t(mw_ref, aw_ref)
    qfeat = slice(0, LANES)

    def win_tile(kt, kind):
        k0 = pl.multiple_of(kt * TK, TK)
        _attend_tile_t(lhst_ref, kw_ref[0, pl.ds(k0, TK), :], _with_ones(vwt_ref[0, kt]), mw_ref, aw_ref,
                       blocks, near_add(kind), feat=qfeat)

    @pl.when(qt >= 1)
    def _():
        win_tile(qt - 1, 'wprev')
    win_tile(qt, 'diag')

    gates = misct_ref[0]
    for r in range(NSA_GROUP):
        parts = []
        for g in range(NSA_KV_HEADS):
            h = g * NSA_GROUP + r
            hc = head_cols(h)
            feat = slice(g * HEAD_DIM, (g + 1) * HEAD_DIM)
            gate = lambda kind: gates[L_GATE + kind * NSA_HEADS + h:L_GATE + kind * NSA_HEADS + h + 1, :]
            parts.append(gate(0) * oct_ref[feat, hc] + gate(1) * (as_ref[feat, hc] / as_ref[LANES:LANES + 1, hc])
                         + gate(2) * (aw_ref[feat, hc] / aw_ref[LANES:LANES + 1, hc]))
        on_ref[0, :, r * LANES:(r + 1) * LANES] = jnp.concatenate(parts, axis=0).T


def _nsa_prompt(qnt, misct, kcmp, vcmpt, nsab, vst, kwb, vwt, dt, nbh, nbl, n_cmp, ncp, n_slc):
    b, _, t = qnt.shape
    n_col = NSA_HEADS * TQ
    seq = lambda blk: pl.BlockSpec((1, t, LANES), lambda bi, qi, blk=blk: (bi, 0, blk))
    tiles = pl.BlockSpec((1, t // TK, LANES, TK), lambda bi, qi: (bi, 0, 0, 0))
    const = lambda a: pl.BlockSpec(a.shape, lambda bi, qi: (0,) * a.ndim)
    return pl.pallas_call(
        functools.partial(_nsa_prompt_kernel, n_cmp=n_cmp, ncp=ncp, n_slc=n_slc),
        grid=(b, t // TQ),
        in_specs=[pl.BlockSpec((1, 512, TQ), lambda bi, qi: (bi, 0, qi)),
                  pl.BlockSpec((1, LANES, TQ), lambda bi, qi: (bi, 0, qi)),
                  pl.BlockSpec((1, ncp, LANES), lambda bi, qi: (bi, 0, 0)),
                  pl.BlockSpec((1, LANES, ncp), lambda bi, qi: (bi, 0, 0)),
                  seq(2), tiles, seq(0), tiles, const(dt), const(nbh), const(nbl)],
        out_specs=pl.BlockSpec((1, TQ, 512), lambda bi, qi: (bi, qi, 0)),
        out_shape=jax.ShapeDtypeStruct((b, t, 512), _F32),
        scratch_shapes=[pltpu.VMEM((2 * LANES, n_col), _BF), pltpu.VMEM((LANES, n_col), _F32),
                        pltpu.VMEM((LANES, NSA_KV_HEADS * TQ), _F32),
                        pltpu.VMEM((SUBLANES, n_col), _F32), pltpu.VMEM((MV, n_col), _F32),
                        pltpu.VMEM((SUBLANES, n_col), _F32), pltpu.VMEM((MV, n_col), _F32)],
        compiler_params=_cparams(("arbitrary", "arbitrary")),
        name="nsa_prompt",
    )(qnt, misct, kcmp, vcmpt, nsab, vst, kwb, vwt, dt, nbh, nbl)


def _fox_prompt_kernel(qft_ref, dqt_ref, kf_ref, dk_ref, vft_ref, of_ref, lhst_ref, m_ref, acc_ref, *, tqf):
    p = pl.program_id(1)
    qt = pl.program_id(2)
    row = _iota((LANES, tqf), 0)
    low = row < HEAD_DIM
    qt_ = qft_ref[0]
    dq = dqt_ref[0]
    zero = jnp.zeros_like(qt_)
    one = jnp.ones_like(qt_)
    for a in range(2):
        head = 2 * p + a
        take_neg = (row >= L_NEG + N_PIECE * head) & (row < L_NEG + N_PIECE * (head + 1))
        take_pos = (row >= L_POS + N_PIECE * head) & (row < L_POS + N_PIECE * (head + 1))
        lhst_ref[0:LANES, a * tqf:(a + 1) * tqf] = jnp.where(low if a == 0 else jnp.logical_not(low), qt_, zero)
        lhst_ref[LANES:2 * LANES, a * tqf:(a + 1) * tqf] = jnp.where(take_neg, one, jnp.where(take_pos, dq, zero))
    _attend_init_t(m_ref, acc_ref)
    n_blk = 2 * tqf // CB
    per_q = tqf // TK
    lane = _iota((TK, LANES), 1)
    ones_pos = (jnp.clip(lane - (L_POS - 1), 0, 1) * jnp.clip(2 * L_POS - lane, 0, 1)).astype(_F32).astype(_BF)

    def tile(kt, blocks, add_fn):
        k0 = pl.multiple_of(kt * TK, TK)
        kaug = jnp.concatenate([kf_ref[0, pl.ds(k0, TK), :], jnp.where(lane < L_POS, dk_ref[0, pl.ds(k0, TK), :], ones_pos)],
                               axis=1)
        _attend_tile_t(lhst_ref, kaug, _with_ones(vft_ref[0, kt]), m_ref, acc_ref, blocks, add_fn)

    def far_body(kt, carry):
        tile(kt, list(range(n_blk)), None)
        return carry
    lax.fori_loop(0, qt * per_q, far_body, 0)

    for j in range(per_q):
        q_lo = lambda b: (b * CB) % tqf
        blocks = [b for b in range(n_blk) if q_lo(b) + CB > j * TK]

        def causal(b, s, j=j):
            if j * TK + TK - 1 <= q_lo(b):
                return s
            kk = j * TK + _iota((TK, CB), 0)
            qq = q_lo(b) + _iota((TK, CB), 1)
            return jnp.where(kk <= qq, s, NEG)
        tile(qt * per_q + j, blocks, causal)

    acc = acc_ref[...]
    o = [acc[0:LANES, a * tqf:(a + 1) * tqf] / acc[LANES:LANES + 1, a * tqf:(a + 1) * tqf] for a in range(2)]
    of_ref[0] = jnp.where(low, o[0], o[1]).T


def _fox_prompt(qft, dect, foxb, dec, vft):
    b, _, t = qft.shape
    tqf = min(TQF, t)
    n_pair = FOX_HEADS // 2
    return pl.pallas_call(
        functools.partial(_fox_prompt_kernel, tqf=tqf),
        grid=(b, n_pair, t // tqf),
        in_specs=[pl.BlockSpec((1, LANES, tqf), lambda bi, p, qi: (bi, p, qi)),
                  pl.BlockSpec((1, LANES, tqf), lambda bi, p, qi: (bi, 0, qi)),
                  pl.BlockSpec((1, t, LANES), lambda bi, p, qi: (bi, 0, p)),
                  pl.BlockSpec((1, t, LANES), lambda bi, p, qi: (bi, 0, 0)),
                  pl.BlockSpec((1, t // TK, LANES, TK), lambda bi, p, qi: (bi, 0, p, 0))],
        out_specs=pl.BlockSpec((1, tqf, LANES), lambda bi, p, qi: (bi, qi, p)),
        out_shape=jax.ShapeDtypeStruct((b, t, 512), _F32),
        scratch_shapes=[pltpu.VMEM((2 * LANES, 2 * tqf), _BF), pltpu.VMEM((SUBLANES, 2 * tqf), _F32),
                        pltpu.VMEM((MV, 2 * tqf), _F32)],
        compiler_params=_cparams(("arbitrary", "arbitrary", "arbitrary")),
        name="fox_prompt",
    )(qft, dect, foxb, dec, vft)


def _post_kernel(on_ref, of_ref, x_ref, gn_ref, gf_ref, wo_ref, gp_ref, y_ref):
    half = on_ref.shape[-1]
    a = _rms(on_ref[0], gn_ref[...]).astype(_BF)
    f = _rms(of_ref[0], gf_ref[...]).astype(_BF)
    mixed = _dot(a, wo_ref[0:half, :]) + _dot(f, wo_ref[half:2 * half, :])
    y_ref[0] = x_ref[0] + _rms(mixed, gp_ref[...])


def _post(o_n, o_f, x, gn, gf, wo, gp, tr):
    nb, t, d = x.shape
    half = o_n.shape[-1]
    row = lambda width: pl.BlockSpec((1, tr, width), lambda bi, i: (bi, i, 0))
    const = lambda shape: pl.BlockSpec(shape, lambda bi, i: (0,) * len(shape))
    return pl.pallas_call(
        _post_kernel,
        grid=(nb, t // tr),
        in_specs=[row(half), row(half), row(d), const((1, half)), const((1, half)), const((2 * half, d)), const((1, d))],
        out_specs=row(d),
        out_shape=jax.ShapeDtypeStruct((nb, t, d), _F32),
        compiler_params=_cparams(("arbitrary", "arbitrary")),
        name="out_proj",
    )(o_n, o_f, x, gn, gf, wo, gp)


def _ffn_kernel(x_ref, gpre_ref, wg_ref, wu_ref, wd_ref, gpost_ref, y_ref):
    x = x_ref[0]
    h = _rms(x, gpre_ref[...]).astype(_BF)
    act = (jax.nn.silu(_dot(h, wg_ref[...])) * _dot(h, wu_ref[...])).astype(_BF)
    y_ref[0] = x + _rms(_dot(act, wd_ref[...]), gpost_ref[...])


def _ffn(x, gpre, wg, wu, wd, gpost, tr):
    nb, t, d = x.shape
    dff = wg.shape[1]
    row = pl.BlockSpec((1, tr, d), lambda bi, i: (bi, i, 0))
    const = lambda shape: pl.BlockSpec(shape, lambda bi, i: (0,) * len(shape))
    return pl.pallas_call(
        _ffn_kernel,
        grid=(nb, t // tr),
        in_specs=[row, const((1, d)), const((d, dff)), const((d, dff)), const((dff, d)), const((1, d))],
        out_specs=row,
        out_shape=jax.ShapeDtypeStruct((nb, t, d), _F32),
        compiler_params=_cparams(("arbitrary", "arbitrary")),
        name="ffn",
    )(x, gpre, wg, wu, wd, gpost)


def _row_select(rows8, pieces):
    out = jnp.zeros((SUBLANES, pieces[0].shape[-1]), _F32)
    for i, piece in enumerate(pieces):
        out = jnp.where(rows8 == i, jnp.broadcast_to(piece, out.shape), out)
    return out


def _pad_rows(a, n):
    return jnp.concatenate([a, jnp.zeros((n - a.shape[0],) + a.shape[1:], a.dtype)], axis=0)


def _nsa_sample_kernel(pt_ref, *refs, pps, past, n_new, n_cmp, ncp, n_slc):
    page_refs = refs[:pps]
    (qn_ref, misc_ref, new_ref, win_ref, kwn_ref, w1k_ref, w1v_ref, pe_ref, w1r_ref, w2_ref,
     bc_ref, bs_ref, bw_ref, on_ref, lk_ref, lv_ref, kst_ref, vst_ref) = refs[pps:]
    del pt_ref
    j = pl.program_id(1)
    n_steps = pl.num_programs(1)
    cpp = PAGE_SIZE // CMP_STRIDE
    ppc = WINDOW // PAGE_SIZE
    n_pages = past // PAGE_SIZE

    rr = _iota((PAGE_SIZE, PAGE_SIZE), 0)
    tok = _iota((PAGE_SIZE, PAGE_SIZE), 1)
    regroup = jnp.where(tok == CMP_STRIDE * _mod(rr, cpp) + _div(rr, cpp), 1.0, 0.0).astype(_BF)

    for k in range(pps):
        pg = j * pps + k
        ref = page_refs[k]
        kst_ref[pg] = ref[0, 0, 2].astype(_BF)
        vst_ref[pg] = ref[0, 0, 3].astype(_BF)
        c0 = pl.multiple_of(pg * cpp, cpp)
        for src, dst in ((0, lk_ref), (1, lv_ref)):
            x = _dot_nt(regroup, ref[0, 0, src].astype(_BF))
            for p in range(CMP_STRIDE):
                dst[pl.ds(c0, cpp), p * LANES:(p + 1) * LANES] = x[p * cpp:(p + 1) * cpp, :]

    @pl.when(j == n_steps - 1)
    def _():
        nc = past // CMP_STRIDE
        new = new_ref[0]
        ks_new = _pad_rows(new[:, 2 * LANES:3 * LANES], PAGE_SIZE).astype(_BF)
        vs_new = _pad_rows(new[:, 3 * LANES:4 * LANES], PAGE_SIZE).astype(_BF)

        def padc(a):
            return a if ncp == nc else _pad_rows(a, ncp)
        kc = padc(_compress(lk_ref[...].astype(_BF), w1k_ref[...], pe_ref[0], w1r_ref[0], w2_ref[0], n_cmp)).astype(_BF)
        vc = padc(_compress(lv_ref[...].astype(_BF), w1v_ref[...], pe_ref[1], w1r_ref[1], w2_ref[1], n_cmp)).astype(_BF)

        rows8 = _iota((SUBLANES, LANES), 0)
        lane = _iota((SUBLANES, LANES), 1)
        grp_low = rows8 < NSA_GROUP
        q = qn_ref[0].astype(_F32)
        blocks = []
        for i in range(n_new):
            blk = jnp.zeros((SUBLANES, LANES), _F32)
            for r in range(NSA_GROUP):
                piece = jnp.broadcast_to(q[i:i + 1, r * LANES:(r + 1) * LANES], (SUBLANES, LANES))
                blk = jnp.where(_mod(rows8, NSA_GROUP) == r, piece, blk)
            on_group = jnp.logical_not(jnp.logical_xor(lane < HEAD_DIM, grp_low))
            blocks.append(jnp.where(on_group, blk, jnp.zeros_like(blk)))
        lq = jnp.concatenate(blocks, axis=0).astype(_BF)

        pc = _masked_softmax_full(_dot_nt(lq, kc) + bc_ref[...])
        o_c = _dot(pc.astype(_BF), vc)

        nl = 2 * LANES
        ovl = _overlap(ncp, n_cmp, nl)
        sums = []
        for i in range(n_new):
            blk = pc[i * SUBLANES:(i + 1) * SUBLANES]
            rr8 = _iota(blk.shape, 0)
            for g in range(NSA_KV_HEADS):
                in_group = (rr8 < NSA_GROUP) if g == 0 else (rr8 >= NSA_GROUP)
                sums.append(jnp.sum(jnp.where(in_group, blk, 0.0), axis=0, keepdims=True))
        hi, lo = _split2(_row_select(_iota((SUBLANES, ncp), 0), sums))
        imp = _dot(hi, ovl) + _dot(lo, ovl)
        t_pos = past + _div(_iota((SUBLANES, 1), 0), NSA_KV_HEADS)
        msel = _select_blocks(imp, t_pos, n_slc, min(N_SELECT, n_slc))
        rows8n = _iota((SUBLANES, nl), 0)
        mrows = []
        for i in range(n_new):
            m0 = jnp.broadcast_to(msel[2 * i:2 * i + 1], (SUBLANES, nl))
            m1 = jnp.broadcast_to(msel[2 * i + 1:2 * i + 2], (SUBLANES, nl))
            mrows.append(jnp.where(rows8n < NSA_GROUP, m0, m1))
        mrows = jnp.concatenate(mrows, axis=0).astype(_BF)

        def block_mask(k0, n_keys):
            s_i = _iota((nl, n_keys), 1)
            j_i = _iota((nl, n_keys), 0)
            return _dot(mrows, jnp.where(j_i == _div(k0 + s_i, SLC_BLOCK), 1.0, 0.0).astype(_BF))

        n_chunk = n_pages // ppc
        kts = lambda ci: jnp.concatenate([kst_ref[ci * ppc + u] for u in range(ppc)], axis=1)
        vts = lambda ci: jnp.concatenate([vst_ref[ci * ppc + u] for u in range(ppc)], axis=1)
        parts = [_dot(lq, kts(ci)) + block_mask(ci * ppc * PAGE_SIZE, ppc * PAGE_SIZE) for ci in range(n_chunk)]
        parts[-1] = parts[-1] + bs_ref[:, 0:WINDOW]
        parts.append(_dot_nt(lq, ks_new) + block_mask(past, PAGE_SIZE) + bs_ref[:, WINDOW:WINDOW + PAGE_SIZE])
        m_s = jnp.max(parts[0], axis=-1, keepdims=True)
        for x in parts[1:]:
            m_s = jnp.maximum(m_s, jnp.max(x, axis=-1, keepdims=True))
        l_s = jnp.zeros_like(m_s)
        acc = jnp.zeros((lq.shape[0], LANES), _F32)
        for ci, x in enumerate(parts):
            pr = jnp.exp2(x - m_s)
            l_s = l_s + jnp.sum(pr, axis=-1, keepdims=True)
            acc = acc + (_dot_nt(pr.astype(_BF), vts(ci)) if ci < n_chunk else _dot(pr.astype(_BF), vs_new))
        o_s = acc / l_s

        kwn = kwn_ref[0]
        kw_new = _pad_rows(kwn[:, 0:LANES], PAGE_SIZE).astype(_BF)
        vw_new = _pad_rows(kwn[:, LANES:2 * LANES], PAGE_SIZE).astype(_BF)
        sw = jnp.concatenate([_dot(lq, win_ref[0, 0, 0].astype(_BF)), _dot_nt(lq, kw_new)], axis=1) + bw_ref[...]
        mw = jnp.max(sw, axis=-1, keepdims=True)
        pw = jnp.exp2(sw - mw)
        pwb = pw.astype(_BF)
        o_w = (_dot_nt(pwb[:, 0:WINDOW], win_ref[0, 0, 1].astype(_BF)) + _dot(pwb[:, WINDOW:], vw_new)) \
            / jnp.sum(pw, axis=-1, keepdims=True)

        misc = misc_ref[0]
        out_rows = []
        for i in range(n_new):
            sl = slice(i * SUBLANES, (i + 1) * SUBLANES)
            g_row = jnp.broadcast_to(misc[i:i + 1, :], (SUBLANES, LANES))

            def gcol(kind, g_row=g_row):
                pick = lane == L_GATE + kind * NSA_HEADS + rows8
                return jnp.sum(jnp.where(pick, g_row, 0.0), axis=-1, keepdims=True)
            o_blk = gcol(0) * o_c[sl] + gcol(1) * o_s[sl] + gcol(2) * o_w[sl]
            pieces = [jnp.where(lane[0:1] < HEAD_DIM, o_blk[r:r + 1], o_blk[NSA_GROUP + r:NSA_GROUP + r + 1])
                      for r in range(NSA_GROUP)]
            out_rows.append(jnp.concatenate(pieces, axis=1))
        on_ref[0] = _row_select(_iota((SUBLANES, 4 * LANES), 0), out_rows)


def _nsa_sample(layer, page_table, cache_t, qn, misc, new, win_t, kwn, cw, bc, bs, bw, past, n_new, n_cmp, ncp, n_slc, pps):
    n_seq, n_pages = page_table.shape
    n_steps = n_pages // pps
    nc = past // CMP_STRIDE

    def page_spec(k):
        return pl.BlockSpec((1, 1, 4, LANES, PAGE_SIZE), lambda s, j, pt, k=k: (layer, pt[s, j * pps + k], 0, 0, 0))
    per_seq = lambda a: pl.BlockSpec((1,) + a.shape[1:], lambda s, j, pt: (s,) + (0,) * (a.ndim - 1))
    const = lambda a: pl.BlockSpec(a.shape, lambda s, j, pt: (0,) * a.ndim)
    consts = [cw["w1k"], cw["w1v"], cw["pe8"], cw["w1raw"], cw["w2p"], bc, bs, bw]
    grid_spec = pltpu.PrefetchScalarGridSpec(
        num_scalar_prefetch=1,
        grid=(n_seq, n_steps),
        in_specs=[page_spec(k) for k in range(pps)] + [per_seq(qn), per_seq(misc), per_seq(new),
                                                        pl.BlockSpec((1, 1) + win_t.shape[2:], lambda s, j, pt: (layer, s, 0, 0, 0)),
                                                        per_seq(kwn)]
        + [const(a) for a in consts],
        out_specs=pl.BlockSpec((1, SUBLANES, 4 * LANES), lambda s, j, pt: (s, 0, 0)),
        scratch_shapes=[pltpu.VMEM((nc, CMP_STRIDE * LANES), _F32), pltpu.VMEM((nc, CMP_STRIDE * LANES), _F32),
                        pltpu.VMEM((n_pages, LANES, PAGE_SIZE), _BF), pltpu.VMEM((n_pages, LANES, PAGE_SIZE), _BF)],
    )
    return pl.pallas_call(
        functools.partial(_nsa_sample_kernel, pps=pps, past=past, n_new=n_new, n_cmp=n_cmp, ncp=ncp, n_slc=n_slc),
        grid_spec=grid_spec,
        out_shape=jax.ShapeDtypeStruct((n_seq, SUBLANES, 4 * LANES), _F32),
        compiler_params=_cparams(("arbitrary", "arbitrary")),
        name="nsa_sample",
    )(page_table, *([cache_t] * pps), qn, misc, new, win_t, kwn, *consts)


def _fox_sample_kernel(pt_ref, *refs, pps, n_new):
    kv_refs = refs[:pps]
    lf_refs = refs[pps:2 * pps]
    (qf_ref, kvn_ref, lfn_ref, of_ref, q_ref, e_ref, carry_ref, m_ref, l_ref, acc_ref) = refs[2 * pps:]
    del pt_ref
    j = pl.program_id(1)
    n_steps = pl.num_programs(1)
    rows = FOX_HEADS * n_new
    width = FOX_HEADS * HEAD_DIM
    srow = _iota((PAGE_SIZE, PAGE_SIZE), 0)
    scol = _iota((PAGE_SIZE, PAGE_SIZE), 1)
    later = jnp.where(srow > scol, 1.0, 0.0).astype(_BF)

    def suffix(lf):
        hi, mid, lo = _split3(lf)
        return _dot(hi, later) + _dot(mid, later) + _dot(lo, later)

    def decay(rt):
        return (jnp.concatenate([rt] * n_new, axis=0) - e_ref[...]) * LOG2E

    @pl.when(j == 0)
    def _():
        _softmax_init(m_ref, l_ref, acc_ref)
        rows8 = _iota((SUBLANES, width), 0)
        lane = _iota((SUBLANES, width), 1)
        q = qf_ref[0].astype(_F32)
        blocks = []
        for i in range(n_new):
            piece = jnp.broadcast_to(q[i:i + 1, :], (SUBLANES, width))
            blocks.append(jnp.where(_div(lane, HEAD_DIM) == rows8, piece, jnp.zeros_like(piece)))
        q_ref[...] = jnp.concatenate(blocks, axis=0).astype(_BF)
        lfn = lfn_ref[0]
        rt = suffix(lfn)
        e_ref[...] = jnp.concatenate([rt[:, i:i + 1] for i in range(n_new)], axis=0)
        carry_ref[...] = jnp.broadcast_to(jnp.sum(lfn, axis=-1, keepdims=True), carry_ref.shape)
        kvn = kvn_ref[0]
        ri = _div(_iota((rows, PAGE_SIZE), 0), SUBLANES)
        ci = _iota((rows, PAGE_SIZE), 1)
        s = _dot_nt(q_ref[...], kvn[:, 0:width].astype(_BF)) + decay(rt) + jnp.where(ci <= ri, 0.0, NEG)
        v_new = kvn[:, width:2 * width].astype(_BF)
        _online_update(s, lambda p: _dot(p, v_new), m_ref, l_ref, acc_ref)

    kts, vts, rts = [], [], []
    carry = carry_ref[...]
    for k in range(pps):
        lf = lf_refs[k][0, 0]
        kts.append(kv_refs[k][0, 0, 0].astype(_BF))
        vts.append(kv_refs[k][0, 0, 1].astype(_BF))
        rts.append(suffix(lf) + carry)
        carry = carry + jnp.sum(lf, axis=-1, keepdims=True)
    carry_ref[...] = carry
    vt = jnp.concatenate(vts, axis=1)
    s = _dot(q_ref[...], jnp.concatenate(kts, axis=1)) + decay(jnp.concatenate(rts, axis=1))
    _online_update(s, lambda p: _dot_nt(p, vt), m_ref, l_ref, acc_ref)

    @pl.when(j == n_steps - 1)
    def _():
        o = acc_ref[...] / l_ref[...]
        rows8 = _iota((SUBLANES, width), 0)
        lane = _iota((SUBLANES, width), 1)
        out_rows = []
        for i in range(n_new):
            blk = jnp.where(_div(lane, HEAD_DIM) == rows8, o[i * SUBLANES:(i + 1) * SUBLANES], 0.0)
            out_rows.append(jnp.sum(blk, axis=0, keepdims=True))
        of_ref[0] = _row_select(rows8, out_rows)


def _fox_sample(layer, page_table, cache_t, cache_lft, qf, kvn, lfn, n_new, pps):
    n_seq, n_pages = page_table.shape
    n_steps = n_pages // pps
    rows = FOX_HEADS * n_new
    width = FOX_HEADS * HEAD_DIM

    def page_idx(s, j, pt, k):
        return pt[s, n_pages - 1 - (j * pps + k)]
    kv_spec = lambda k: pl.BlockSpec((1, 1, 2, width, PAGE_SIZE),
                                     lambda s, j, pt, k=k: (layer, page_idx(s, j, pt, k), 0, 0, 0))
    lf_spec = lambda k: pl.BlockSpec((1, 1, FOX_HEADS, PAGE_SIZE),
                                     lambda s, j, pt, k=k: (layer, page_idx(s, j, pt, k), 0, 0))
    per_seq = lambda a: pl.BlockSpec((1,) + a.shape[1:], lambda s, j, pt: (s,) + (0,) * (a.ndim - 1))
    grid_spec = pltpu.PrefetchScalarGridSpec(
        num_scalar_prefetch=1,
        grid=(n_seq, n_steps),
        in_specs=[kv_spec(k) for k in range(pps)] + [lf_spec(k) for k in range(pps)] + [per_seq(a) for a in (qf, kvn, lfn)],
        out_specs=pl.BlockSpec((1, SUBLANES, width), lambda s, j, pt: (s, 0, 0)),
        scratch_shapes=[pltpu.VMEM((rows, width), _BF), pltpu.VMEM((rows, 1), _F32), pltpu.VMEM((SUBLANES, LANES), _F32),
                        pltpu.VMEM((rows, 1), _F32), pltpu.VMEM((rows, 1), _F32), pltpu.VMEM((rows, width), _F32)],
    )
    return pl.pallas_call(
        functools.partial(_fox_sample_kernel, pps=pps, n_new=n_new),
        grid_spec=grid_spec,
        out_shape=jax.ShapeDtypeStruct((n_seq, SUBLANES, width), _F32),
        compiler_params=_cparams(("arbitrary", "arbitrary")),
        name="fox_sample",
    )(page_table, *([cache_t] * pps), *([cache_lft] * pps), qf, kvn, lfn)


def _nsa_perm():
    idx = np.zeros(NSA_HEADS * HEAD_DIM, np.int32)
    for r in range(NSA_GROUP):
        for g in range(NSA_KV_HEADS):
            for d in range(HEAD_DIM):
                idx[r * LANES + g * HEAD_DIM + d] = (g * NSA_GROUP + r) * HEAD_DIM + d
    return idx


def _layer_weights(l, w_in, b_gate, b_forget, cmp_pe, cmp_w1, cmp_w2, grp_norm_nsa, grp_norm_fox, w_o):
    perm = _nsa_perm()
    w = w_in[l]
    o_qn, o_nsa, o_kw, o_g, o_qf, o_kf, o_f = 0, 512, 1024, 1280, 1304, 1816, 2840
    f_cols = w[:, o_f:o_f + FOX_HEADS]
    misc = jnp.concatenate([w[:, o_g:o_g + 3 * NSA_HEADS], f_cols, f_cols,
                            jnp.zeros((w.shape[0], LANES - 3 * NSA_HEADS - 2 * FOX_HEADS), w.dtype)], axis=1)
    wp = jnp.concatenate([w[:, o_qn:o_nsa][:, perm], w[:, o_nsa:o_kw], w[:, o_kw:o_g], w[:, o_qf:o_kf],
                          w[:, o_kf:o_f], misc], axis=1).astype(_BF)
    bias = jnp.concatenate([b_gate[l].reshape(-1), b_forget[l], b_forget[l],
                            jnp.zeros((LANES - 3 * NSA_HEADS - 2 * FOX_HEADS,), _F32)]).reshape(1, LANES)

    def w1_layout(w1):
        w1 = w1.reshape(2, CMP_STRIDE, HEAD_DIM, CMP_HIDDEN)
        z = jnp.zeros((CMP_STRIDE, HEAD_DIM, CMP_HIDDEN), w1.dtype)
        g0 = jnp.concatenate([w1[0], w1[1], z, z], axis=-1)
        g1 = jnp.concatenate([z, z, w1[0], w1[1]], axis=-1)
        return jnp.concatenate([g0, g1], axis=1).reshape(CMP_STRIDE * LANES, 4 * CMP_HIDDEN).astype(_BF)

    def w2_layout(w2):
        z = jnp.zeros_like(w2)
        return jnp.concatenate([jnp.concatenate([w2, z], axis=1), jnp.concatenate([z, w2], axis=1)], axis=0).astype(_BF)
    cw = dict(
        w1k=w1_layout(cmp_w1[l, 0]), w1v=w1_layout(cmp_w1[l, 1]),
        pe8=jnp.broadcast_to(cmp_pe[l].reshape(2, 1, CMP_BLOCK * HEAD_DIM), (2, SUBLANES, CMP_BLOCK * HEAD_DIM)).astype(_BF),
        w1raw=cmp_w1[l].astype(_BF),
        w2p=jnp.stack([w2_layout(cmp_w2[l, 0]), w2_layout(cmp_w2[l, 1])]),
    )
    gn = grp_norm_nsa[l][perm].reshape(1, -1)
    gf = grp_norm_fox[l].reshape(1, -1)
    wo = jnp.concatenate([w_o[l][:NSA_HEADS * HEAD_DIM][perm], w_o[l][NSA_HEADS * HEAD_DIM:]], axis=0).astype(_BF)
    return wp, bias, cw, gn, gf, wo


def kernel(x_prompt, x_sample, cache_nsa_kv, cache_fox_kv, cache_fox_logf, state_win_kv, page_table, rel_bias,
           norm_mix_pre, norm_mix_post, norm_ffn_pre, norm_ffn_post, w_in, b_gate, b_forget, cmp_pe, cmp_w1, cmp_w2,
           grp_norm_nsa, grp_norm_fox, w_o, w_ffn_gate, w_ffn_up, w_ffn_down):
    depth = w_in.shape[0]
    b, t, d = x_prompt.shape
    n_seq, n_new, _ = x_sample.shape
    n_pages = page_table.shape[1]
    past = n_pages * PAGE_SIZE
    n_pool = cache_nsa_kv.shape[1]
    n_win = state_win_kv.shape[2]
    assert t % min(TQF, t) == 0 and min(TQF, t) % TK == 0 and t % TQ == 0 and TQ == TK == WINDOW and TQ % CB == 0
    assert CB % LANES == 0 and LANES > T5_FAR
    assert n_new <= SUBLANES and SLC_BLOCK >= n_new and n_win == WINDOW
    assert n_pages % (WINDOW // PAGE_SIZE) == 0 and past >= 2 * WINDOW
    assert (n_seq * n_new) % SUBLANES == 0

    nc_p = t // CMP_STRIDE
    ncp_p = -(-nc_p // LANES) * LANES
    n_slc_p = t // SLC_BLOCK
    assert n_slc_p <= LANES
    nc_s = past // CMP_STRIDE
    ncp_s = -(-nc_s // LANES) * LANES
    n_slc_s = past // SLC_BLOCK + 1
    assert n_slc_s <= 2 * LANES
    pps = math.gcd(n_pages, 8)
    pps_fox = math.gcd(n_pages, 16)

    dt, nbh, nbl = _bias_prompt(rel_bias)
    bc, bs, bw = _bias_sample(rel_bias, past, n_new, nc_s - 1, ncp_s)

    nsa_t = jnp.transpose(cache_nsa_kv, (0, 1, 3, 4, 5, 2)).reshape(depth, n_pool, 4, NSA_KV_HEADS * HEAD_DIM, PAGE_SIZE)
    fox_t = jnp.transpose(cache_fox_kv, (0, 1, 3, 4, 5, 2)).reshape(depth, n_pool, 2, FOX_HEADS * HEAD_DIM, PAGE_SIZE)
    logf_t = jnp.transpose(cache_fox_logf, (0, 1, 3, 2))
    win_t = jnp.transpose(state_win_kv, (0, 1, 3, 4, 5, 2)).reshape(depth, n_seq, 2, NSA_KV_HEADS * HEAD_DIM, n_win)

    row1 = lambda a: a.reshape(1, -1)
    pad_new = lambda a: jnp.pad(a, ((0, 0), (0, SUBLANES - n_new), (0, 0)))
    rs = n_seq * n_new
    xp, xs = x_prompt, x_sample.reshape(1, rs, d)
    outs_p, outs_s = [], []
    for l in range(depth):
        wp, bias, cw, gn, gf, wo = _layer_weights(l, w_in, b_gate, b_forget, cmp_pe, cmp_w1, cmp_w2,
                                                  grp_norm_nsa, grp_norm_fox, w_o)
        wg, wu, wd = w_ffn_gate[l].astype(_BF), w_ffn_up[l].astype(_BF), w_ffn_down[l].astype(_BF)
        g_pre, g_post = row1(norm_mix_pre[l]), row1(norm_mix_post[l])
        g_fpre, g_fpost = row1(norm_ffn_pre[l]), row1(norm_ffn_post[l])

        (_, nsa, nsab, kw, kwb, _, fox, foxb, misc, dec,
         qnt, qft, dect, misct, vst, vwt, vft) = _proj(xp, g_pre, wp, bias, TR_PROJ, True)
        kcmp, vcmpt = _compress_prompt(nsa, cw, nc_p, ncp_p)
        o_n = _nsa_prompt(qnt, misct, kcmp, vcmpt, nsab, vst, kwb, vwt, dt, nbh, nbl, nc_p - 1, ncp_p, n_slc_p)
        o_f = _fox_prompt(qft, dect, foxb, dec, vft)
        xp = _post(o_n, o_f, xp, gn, gf, wo, g_post, TR_POST)
        xp = _ffn(xp, g_fpre, wg, wu, wd, g_fpost, TR_FFN)
        outs_p.append((nsa.reshape(b, t, 4, NSA_KV_HEADS, HEAD_DIM), fox.reshape(b, t, 2, FOX_HEADS, HEAD_DIM),
                       misc[:, :, L_LOGF:L_LOGF + FOX_HEADS],
                       kw[:, t - WINDOW:].reshape(b, WINDOW, 2, NSA_KV_HEADS, HEAD_DIM)))

        tr_s = math.gcd(rs, TR_PROJ)
        qn, nsa, nsab, kw, kwb, qf, fox, foxb, misc = _proj(xs, g_pre, wp, bias, tr_s, False)
        per = lambda a: a.reshape(n_seq, n_new, a.shape[-1])
        nsa_s, fox_s, kw_s, misc_s = per(nsa), per(fox), per(kw), per(misc)
        logf_s = misc_s[:, :, L_LOGF:L_LOGF + FOX_HEADS]
        o_n = _nsa_sample(l, page_table, nsa_t, pad_new(per(qn)), pad_new(misc_s), pad_new(nsa_s), win_t,
                          pad_new(kw_s), cw, bc, bs, bw, past, n_new, nc_s - 1, ncp_s, n_slc_s, pps)
        kvn = jnp.pad(fox_s, ((0, 0), (0, PAGE_SIZE - n_new), (0, 0)))
        lfn = jnp.pad(jnp.swapaxes(logf_s, 1, 2), ((0, 0), (0, 0), (0, PAGE_SIZE - n_new)))
        o_f = _fox_sample(l, page_table, fox_t, logf_t, pad_new(per(qf)), kvn, lfn, n_new, pps_fox)
        o_n = o_n[:, :n_new].reshape(1, rs, -1)
        o_f = o_f[:, :n_new].reshape(1, rs, -1)
        xs = _post(o_n, o_f, xs, gn, gf, wo, g_post, math.gcd(rs, TR_POST))
        xs = _ffn(xs, g_fpre, wg, wu, wd, g_fpost, math.gcd(rs, TR_FFN))
        win_new = jnp.concatenate([state_win_kv[l][:, n_new:], kw_s.reshape(n_seq, n_new, 2, NSA_KV_HEADS, HEAD_DIM)], axis=1)
        outs_s.append((nsa_s.reshape(n_seq, n_new, 4, NSA_KV_HEADS, HEAD_DIM),
                       fox_s.reshape(n_seq, n_new, 2, FOX_HEADS, HEAD_DIM), logf_s, win_new))

    stack = lambda outs, i: jnp.stack([o[i] for o in outs], axis=0)
    return (xp, xs.reshape(n_seq, n_new, d), stack(outs_p, 0), stack(outs_p, 1), stack(outs_p, 2), stack(outs_p, 3),
            stack(outs_s, 0), stack(outs_s, 1), stack(outs_s, 2), stack(outs_s, 3))
```

```python
import functools
import math

import numpy as np
import jax
import jax.numpy as jnp
from jax import lax
from jax.experimental import pallas as pl
from jax.experimental.pallas import tpu as pltpu

HEAD_DIM = 64
NSA_HEADS = 8
FOX_HEADS = 8
NSA_KV_HEADS = 2
NSA_GROUP = NSA_HEADS // NSA_KV_HEADS
CMP_BLOCK = 32
CMP_STRIDE = 16
CMP_HIDDEN = 2 * HEAD_DIM
SLC_BLOCK = 64
N_SELECT = 16
WINDOW = 512
T5_BUCKETS = 32
T5_EXACT = T5_BUCKETS // 2
T5_MAX_DIST = 128
PAGE_SIZE = 128
NORM_EPS = 1e-6
FORCE_SCORE = 1e4
LOG2E = math.log2(math.e)
Q_SCALE = HEAD_DIM ** -0.5 * LOG2E

LANES = 128
SUBLANES = 8
VMEM_LIMIT = 56 * 1024 * 1024

_F32 = jnp.float32
_BF = jnp.bfloat16
NEG = -(2.0 ** 100)
NEG_HALF = -(2.0 ** 99)
M_INIT = -3.0e38
REMOVED = -3.4e38


def _t5_thresholds():
    n = np.arange(1, 4 * T5_MAX_DIST)
    large = T5_EXACT + (np.log(n / T5_EXACT) / math.log(T5_MAX_DIST / T5_EXACT) * (T5_BUCKETS - T5_EXACT)).astype(np.int64)
    return tuple(int(n[np.argmax(large >= k)]) for k in range(T5_EXACT + 1, T5_BUCKETS))


_T5_THR = _t5_thresholds()
T5_FAR = _T5_THR[-1]


def _log2(n):
    assert n & (n - 1) == 0
    return n.bit_length() - 1


TQ = 512
TQF = 2048
TK = 512
CB = 256
AHEAD = 3
MV = HEAD_DIM + SUBLANES
TR_PROJ = 256
TR_POST = 512
TR_FFN = 256
CPT = TQ // CMP_STRIDE
NEAR_BACK = -(-(T5_FAR + CMP_BLOCK - 1) // CMP_STRIDE) - 1
NEAR_U = CPT + NEAR_BACK

C_QN, C_NSA, C_KW, C_QF, C_FOX, C_MISC = 0, 512, 1024, 1280, 1792, 2816
C_TOT = 2944
L_GATE, L_LOGF, L_CUM = 0, 24, 32
N_PIECE = 3
L_NEG, L_POS = 0, N_PIECE * FOX_HEADS


def _dot(a, b):
    return jnp.dot(a, b, preferred_element_type=_F32)


def _dot_nt(a, b):
    return lax.dot_general(a, b, (((1,), (1,)), ((), ())), preferred_element_type=_F32)


def _split3(x):
    hi = x.astype(_BF)
    r1 = x - hi.astype(_F32)
    mid = r1.astype(_BF)
    lo = (r1 - mid.astype(_F32)).astype(_BF)
    return hi, mid, lo


def _split2(x):
    hi = x.astype(_BF)
    return hi, (x - hi.astype(_F32)).astype(_BF)


def _dot3(a_bf, x):
    hi, mid, lo = _split3(x)
    return _dot(a_bf, hi) + _dot(a_bf, mid) + _dot(a_bf, lo)


def _rms(x, g):
    ms = jnp.mean(x * x, axis=-1, keepdims=True)
    return x * lax.rsqrt(ms + NORM_EPS) * g


def _gelu_tanh(x):
    c = math.sqrt(2.0 / math.pi)
    return x * (0.5 * (1.0 + jnp.tanh(c * (x + 0.044715 * (x * x * x)))))


def _iota(shape, dim):
    return lax.broadcasted_iota(jnp.int32, shape, dim)


def _div(x, n):
    return jnp.right_shift(x, _log2(n))


def _mod(x, n):
    return x & (n - 1)


def _cparams(sem):
    return pltpu.CompilerParams(dimension_semantics=sem, vmem_limit_bytes=VMEM_LIMIT)


def _t5_rel(dist, rel_ref, h):
    d = jnp.minimum(dist, T5_MAX_DIST - 1)
    big = jnp.full(d.shape, T5_EXACT, jnp.int32)
    for thr in _T5_THR:
        big = big + jnp.where(d >= thr, 1, 0)
    bkt = jnp.where(d < T5_EXACT, d, big)
    far = rel_ref[T5_BUCKETS - 1, h]
    val = jnp.zeros(d.shape, _F32)
    for k in range(T5_BUCKETS - 1):
        val = jnp.where(bkt == k, (rel_ref[k, h] - far) * LOG2E, val)
    return val


def _t5_masked(dist, rel_ref, h):
    return jnp.where(dist < 0, NEG, _t5_rel(jnp.maximum(dist, 0), rel_ref, h))


def _bias_prompt_kernel(rel_ref, dt_ref, nbh_ref, nbl_ref):
    def body(h, carry):
        c = _iota((LANES, LANES), 0)
        i = _iota((LANES, LANES), 1)
        dt_ref[h, 0] = _t5_masked(i - c, rel_ref, h)
        dt_ref[h, 1] = _t5_masked(i - c + LANES, rel_ref, h)
        u = _iota((LANES, TQ), 0)
        i = _iota((LANES, TQ), 1)
        near = _t5_masked(i + (CMP_STRIDE * NEAR_BACK - (CMP_BLOCK - 1)) - CMP_STRIDE * u, rel_ref, h)
        nb = jnp.where(u < NEAR_U, near, jnp.where(u == NEAR_U, NEG, 0.0))
        hi, lo = _split2(nb)
        cols = pl.ds(pl.multiple_of(h * TQ, TQ), TQ)
        nbh_ref[:, cols] = hi
        nbl_ref[:, cols] = lo
        return carry
    lax.fori_loop(0, NSA_HEADS, body, 0)


def _bias_prompt(rel_bias):
    return pl.pallas_call(
        _bias_prompt_kernel,
        out_shape=(jax.ShapeDtypeStruct((NSA_HEADS, 2, LANES, LANES), _F32),
                   jax.ShapeDtypeStruct((LANES, NSA_HEADS * TQ), _BF),
                   jax.ShapeDtypeStruct((LANES, NSA_HEADS * TQ), _BF)),
        in_specs=[pl.BlockSpec(memory_space=pltpu.SMEM)],
        name="t5_bias_prompt",
    )(rel_bias)


def _bias_sample_kernel(rel_ref, bc_ref, bs_ref, bw_ref, *, past, n_new, n_cmp):
    def table(shape, dist_fn, extra_invalid=None):
        out = jnp.zeros(shape, _F32)
        r = _iota(shape, 0)
        c = _iota(shape, 1)
        i = _div(r, NSA_HEADS)
        dist = dist_fn(i, c)
        for h in range(NSA_HEADS):
            v = _t5_masked(dist, rel_ref, h)
            out = jnp.where(_mod(r, NSA_HEADS) == h, v, out)
        if extra_invalid is not None:
            out = jnp.where(extra_invalid(i, c, dist), NEG, out)
        return out

    bc_ref[...] = table(bc_ref.shape, lambda i, c: past + i - (CMP_STRIDE * c + CMP_BLOCK - 1),
                        lambda i, c, d: c >= n_cmp)
    bs_ref[...] = table(bs_ref.shape, lambda i, c: i + WINDOW - c)
    bw_ref[...] = table(bw_ref.shape, lambda i, c: i + WINDOW - c,
                        lambda i, c, d: (d >= WINDOW) | (c >= WINDOW + n_new))


def _bias_sample(rel_bias, past, n_new, n_cmp, ncp):
    rows = NSA_HEADS * n_new
    return pl.pallas_call(
        functools.partial(_bias_sample_kernel, past=past, n_new=n_new, n_cmp=n_cmp),
        out_shape=(jax.ShapeDtypeStruct((rows, ncp), _F32),
                   jax.ShapeDtypeStruct((rows, WINDOW + PAGE_SIZE), _F32),
                   jax.ShapeDtypeStruct((rows, WINDOW + PAGE_SIZE), _F32)),
        in_specs=[pl.BlockSpec(memory_space=pltpu.SMEM)],
        name="t5_bias_sample",
    )(rel_bias)


def _proj_kernel(x_ref, g_ref, w_ref, b_ref, *refs, tr, prompt):
    if prompt:
        (qn_ref, nsa_ref, nsab_ref, kw_ref, kwb_ref, qf_ref, fox_ref, foxb_ref, misc_ref, dec_ref,
         qnt_ref, qft_ref, dect_ref, misct_ref, vst_ref, vwt_ref, vft_ref, carry_ref) = refs
    else:
        qn_ref, nsa_ref, nsab_ref, kw_ref, kwb_ref, qf_ref, fox_ref, foxb_ref, misc_ref = refs
    x = x_ref[0]
    h = _rms(x, g_ref[...]).astype(_BF)
    qn = _dot(h, w_ref[:, C_QN:C_NSA]) * Q_SCALE
    qn_ref[0] = qn.astype(_BF)
    z_nsa = _dot(h, w_ref[:, C_NSA:C_KW])
    nsa_ref[0] = z_nsa
    nsab_ref[0] = z_nsa.astype(_BF)
    z_kw = _dot(h, w_ref[:, C_KW:C_QF])
    kw_ref[0] = z_kw
    kwb_ref[0] = z_kw.astype(_BF)
    qf = _dot(h, w_ref[:, C_QF:C_FOX]) * Q_SCALE
    qf_ref[0] = qf.astype(_BF)
    z_fox = _dot(h, w_ref[:, C_FOX:C_MISC])
    fox_ref[0] = z_fox
    foxb_ref[0] = z_fox.astype(_BF)
    zm = _dot(h, w_ref[:, C_MISC:C_TOT]) + b_ref[...]
    lane = _iota((tr, LANES), 1)
    sg = jax.nn.sigmoid(zm)
    ls = jnp.minimum(zm, 0.0) - jnp.log1p(jnp.exp(-jnp.abs(zm)))
    if not prompt:
        misc_ref[0] = jnp.where(lane < L_LOGF, sg, jnp.where(lane < L_CUM, ls, 0.0))
        return

    @pl.when(pl.program_id(1) == 0)
    def _():
        carry_ref[...] = jnp.zeros(carry_ref.shape, _F32)
    row = _iota((tr, tr), 0)
    col = _iota((tr, tr), 1)
    tri = jnp.where(col <= row, 1.0, 0.0).astype(_BF)
    cs = _dot3(tri, ls) + carry_ref[0:1, :]
    carry_ref[...] = jnp.broadcast_to(cs[tr - 1:tr, :], carry_ref.shape)
    pieces = _split3(cs * LOG2E)
    r = _iota((LANES, LANES), 0) - L_CUM
    c = _iota((LANES, LANES), 1)
    head_row = (r >= 0) & (r < FOX_HEADS)
    dec = jnp.zeros((tr, LANES), _F32)
    for j, piece in enumerate(pieces):
        put = jnp.where(head_row & (c == L_NEG + N_PIECE * r + j), -1.0,
                        jnp.where(head_row & (c == L_POS + N_PIECE * r + j), 1.0, 0.0)).astype(_BF)
        dec = dec + _dot(piece, put)
    dec_ref[0] = dec.astype(_BF)
    misc = jnp.where(lane < L_LOGF, sg, jnp.where(lane < L_CUM, ls, jnp.where(lane < L_CUM + FOX_HEADS, cs, 0.0)))
    misc_ref[0] = misc
    qnt_ref[0] = qn.T.astype(_BF)
    qft_ref[0] = qf.T.astype(_BF)
    dect_ref[0] = dec.T.astype(_BF)
    misct_ref[0] = misc.T
    vst_ref[0, 0] = z_nsa[:, 3 * LANES:4 * LANES].T.astype(_BF)
    vwt_ref[0, 0] = z_kw[:, LANES:2 * LANES].T.astype(_BF)
    vft_ref[0, 0] = z_fox[:, FOX_HEADS * HEAD_DIM:].T.astype(_BF)


def _proj(x, g, w, b, tr, prompt):
    nb, t, d = x.shape
    grid = (nb, t // tr)
    row = lambda width: pl.BlockSpec((1, tr, width), lambda bi, i: (bi, i, 0))
    const = lambda shape: pl.BlockSpec(shape, lambda bi, i: (0,) * len(shape))
    shp = lambda width, dt: jax.ShapeDtypeStruct((nb, t, width), dt)
    out_specs = [row(512), row(512), row(512), row(256), row(256), row(512), row(1024), row(1024), row(LANES)]
    out_shape = [shp(512, _BF), shp(512, _F32), shp(512, _BF), shp(256, _F32), shp(256, _BF), shp(512, _BF),
                 shp(1024, _F32), shp(1024, _BF), shp(LANES, _F32)]
    scratch = []
    if prompt:
        assert TK % tr == 0 and t % TK == 0
        per = TK // tr
        colm = lambda rows: pl.BlockSpec((1, rows, tr), lambda bi, i: (bi, 0, i))
        tile = lambda rows: pl.BlockSpec((1, 1, rows, tr), lambda bi, i: (bi, i // per, 0, i % per))
        tshp = lambda rows, dt: jax.ShapeDtypeStruct((nb, rows, t), dt)
        t4 = lambda rows: jax.ShapeDtypeStruct((nb, t // TK, rows, TK), _BF)
        out_specs += [row(LANES), colm(512), colm(512), colm(LANES), colm(LANES), tile(LANES), tile(LANES), tile(512)]
        out_shape += [shp(LANES, _BF), tshp(512, _BF), tshp(512, _BF), tshp(LANES, _BF), tshp(LANES, _F32),
                      t4(LANES), t4(LANES), t4(512)]
        scratch = [pltpu.VMEM((SUBLANES, LANES), _F32)]
    return pl.pallas_call(
        functools.partial(_proj_kernel, tr=tr, prompt=prompt),
        grid=grid,
        in_specs=[row(d), const((1, d)), const((d, C_TOT)), const((1, LANES))],
        out_specs=tuple(out_specs),
        out_shape=tuple(out_shape),
        scratch_shapes=scratch,
        compiler_params=_cparams(("arbitrary", "arbitrary")),
        name="in_proj",
    )(x, g, w, b)


def _compress(lhs_bf, w1p, pe8, w1raw, w2p, n_valid):
    nc = lhs_bf.shape[0]
    hcat = _dot(lhs_bf, w1p)
    cst = _dot(pe8, w1raw)[0:1]

    def hidden(g):
        a = hcat[:, g * 256:g * 256 + CMP_HIDDEN]
        b = hcat[:, g * 256 + CMP_HIDDEN:(g + 1) * 256]
        return _gelu_tanh(a + pltpu.roll(b, nc - 1, 0) + cst)

    hh = jnp.concatenate([hidden(0), hidden(1)], axis=1).astype(_BF)
    out = _dot(hh, w2p)
    return jnp.where(_iota(out.shape, 0) < n_valid, out, 0.0)


def _compress_prompt_kernel(xk_ref, xv_ref, w1k_ref, w1v_ref, pe_ref, w1r_ref, w2_ref, kc_ref, vc_ref, *, nc, ncp):
    for idx, (x_ref, w1, out_ref) in enumerate(((xk_ref, w1k_ref, kc_ref), (xv_ref, w1v_ref, vc_ref))):
        pieces = [x_ref[0, pl.ds(p, nc, stride=CMP_STRIDE), :] for p in range(CMP_STRIDE)]
        lhs = jnp.concatenate(pieces, axis=1).astype(_BF)
        out = _compress(lhs, w1[...], pe_ref[idx], w1r_ref[idx], w2_ref[idx], nc - 1)
        if ncp > nc:
            out = jnp.concatenate([out, jnp.zeros((ncp - nc, LANES), _F32)], axis=0)
        out_ref[0] = (out.T if idx == 1 else out).astype(_BF)


def _compress_prompt(nsa_state, cw, nc, ncp):
    b, t, _ = nsa_state.shape
    const = lambda a: pl.BlockSpec(a.shape, lambda bi: (0,) * a.ndim)
    return pl.pallas_call(
        functools.partial(_compress_prompt_kernel, nc=nc, ncp=ncp),
        grid=(b,),
        in_specs=[pl.BlockSpec((1, t, LANES), lambda bi: (bi, 0, 0)), pl.BlockSpec((1, t, LANES), lambda bi: (bi, 0, 1)),
                  const(cw["w1k"]), const(cw["w1v"]), const(cw["pe8"]), const(cw["w1raw"]), const(cw["w2p"])],
        out_specs=(pl.BlockSpec((1, ncp, LANES), lambda bi: (bi, 0, 0)), pl.BlockSpec((1, LANES, ncp), lambda bi: (bi, 0, 0))),
        out_shape=(jax.ShapeDtypeStruct((b, ncp, LANES), _BF), jax.ShapeDtypeStruct((b, LANES, ncp), _BF)),
        compiler_params=_cparams(("arbitrary",)),
        name="compress_prompt",
    )(nsa_state, nsa_state, cw["w1k"], cw["w1v"], cw["pe8"], cw["w1raw"], cw["w2p"])


def _attend_tile_t(lhst_ref, k_t, v_of, m_ref, acc_ref, blocks, add_fn=None, feat=slice(None), keys_of=None):
    n = len(blocks)
    cols = lambda b: slice(b * CB, (b + 1) * CB)
    keys = (lambda b: slice(None)) if keys_of is None else keys_of
    score = lambda b: _dot(k_t[keys(b), :], lhst_ref[feat, cols(b)])
    scores = [score(b) for b in blocks[:AHEAD]]
    for i, b in enumerate(blocks):
        if i + AHEAD < n:
            scores.append(score(blocks[i + AHEAD]))
        s = scores[i]
        scores[i] = None
        if add_fn is not None:
            s = add_fn(b, s)
        m_old = m_ref[0:1, cols(b)]
        m_new = jnp.maximum(m_old, jnp.max(s, axis=0, keepdims=True))
        alpha = jnp.exp2(m_old - m_new)
        p = jnp.exp2(s - m_new).astype(_BF)
        acc_ref[:, cols(b)] = alpha * acc_ref[:, cols(b)] + _dot(v_of(b)[:, keys(b)], p)
        m_ref[0:1, cols(b)] = m_new


def _attend_init_t(m_ref, acc_ref):
    m_ref[...] = jnp.full(m_ref.shape, M_INIT, _F32)
    acc_ref[...] = jnp.zeros(acc_ref.shape, _F32)


def _edit_blocks(s, fn, a0=0):
    rows = []
    for a in range(s.shape[0] // LANES):
        pieces = []
        for q in range(s.shape[1] // LANES):
            piece = s[a * LANES:(a + 1) * LANES, q * LANES:(q + 1) * LANES]
            new = fn(a0 + a, q, piece)
            pieces.append(piece if new is None else new)
        rows.append(jnp.concatenate(pieces, axis=1))
    return jnp.concatenate(rows, axis=0)


def _head_values(v_t):
    ones = jnp.ones((MV - HEAD_DIM, v_t.shape[1]), _BF)
    return [jnp.concatenate([v_t[a * HEAD_DIM:(a + 1) * HEAD_DIM], ones], axis=0) for a in range(2)]


def _softmax_init(m_ref, l_ref, acc_ref):
    m_ref[...] = jnp.full(m_ref.shape, M_INIT, _F32)
    l_ref[...] = jnp.zeros(l_ref.shape, _F32)
    acc_ref[...] = jnp.zeros(acc_ref.shape, _F32)


def _online_update(s, pv_fn, m_ref, l_ref, acc_ref):
    m_old = m_ref[...]
    m_new = jnp.maximum(m_old, jnp.max(s, axis=-1, keepdims=True))
    alpha = jnp.exp2(m_old - m_new)
    p = jnp.exp2(s - m_new)
    l_ref[...] = alpha * l_ref[...] + jnp.sum(p, axis=-1, keepdims=True)
    acc_ref[...] = alpha * acc_ref[...] + pv_fn(p.astype(_BF))
    m_ref[...] = m_new


def _masked_softmax_full(s):
    m = jnp.max(s, axis=-1, keepdims=True)
    p = jnp.exp2(s - m)
    l = jnp.sum(p, axis=-1, keepdims=True)
    return jnp.where(m > NEG_HALF, p / l, 0.0)


def _overlap(ncp, n_cmp, n_blk, blk_axis=1):
    shape = (ncp, n_blk) if blk_axis == 1 else (n_blk, ncp)
    c = _iota(shape, 1 - blk_axis)
    j = _iota(shape, blk_axis)
    r = SLC_BLOCK // CMP_STRIDE
    hit = (c >= r * j - (CMP_BLOCK // CMP_STRIDE - 1)) & (c <= r * j + r - 1) & (c < n_cmp)
    return jnp.where(hit, 1.0, 0.0).astype(_BF)


def _select_blocks(imp, t_pos, n_slc, n_top, blk_axis=1):
    j = _iota(imp.shape, blk_axis)
    cur = _div(t_pos, SLC_BLOCK)
    forced = (j == 0) | (j == cur) | (j == cur - 1)
    score = jnp.where(forced, FORCE_SCORE, imp)
    score = jnp.where(j * SLC_BLOCK > t_pos, -FORCE_SCORE, score)
    score = jnp.where(j >= n_slc, M_INIT, score)
    sel = jnp.zeros(imp.shape, jnp.bool_)
    for _ in range(n_top):
        mx = jnp.max(score, axis=blk_axis, keepdims=True)
        idx = jnp.min(jnp.where(score == mx, j, 1 << 20), axis=blk_axis, keepdims=True)
        hit = j == idx
        sel = sel | hit
        score = jnp.where(hit, REMOVED, score)
    return jnp.where(sel, 0.0, NEG)


def _block_onehot(k0, tk, n_lanes):
    s = _iota((tk, n_lanes), 0)
    j = _iota((tk, n_lanes), 1)
    return jnp.where(j == _div(k0 + s, SLC_BLOCK), 1.0, 0.0).astype(_BF)


def _nsa_prompt_kernel(qnt_ref, misct_ref, kc_ref, vct_ref, ks_ref, vst_ref, kw_ref, vwt_ref, dt_ref, nbh_ref, nbl_ref,
                       on_ref, lhst_ref, oct_ref, imp_ref, ms_ref, as_ref, mw_ref, aw_ref, *, n_cmp, ncp, n_slc):
    qt = pl.program_id(1)
    q0 = qt * TQ
    n_col = NSA_HEADS * TQ
    n_cb = TQ // CB
    blocks = list(range(n_col // CB))
    cols = lambda b: slice(b * CB, (b + 1) * CB)
    head_cols = lambda h: slice(h * TQ, (h + 1) * TQ)
    low = _iota((LANES, TQ), 0) < HEAD_DIM

    for g in range(NSA_KV_HEADS):
        for r in range(NSA_GROUP):
            blk = qnt_ref[0, r * LANES:(r + 1) * LANES, :]
            lhst_ref[0:LANES, head_cols(g * NSA_GROUP + r)] = jnp.where(low if g == 0 else jnp.logical_not(low),
                                                                       blk, jnp.zeros_like(blk))

    c = _iota((ncp, LANES), 0)
    u = _iota((ncp, LANES), 1)
    place = ((u < NEAR_U) & (c == CPT * qt - NEAR_BACK + u)) | ((u == NEAR_U) & (c >= CPT * qt + CPT))
    place = jnp.where(place, 1.0, 0.0).astype(_BF)
    kc = kc_ref[0]
    vct = vct_ref[0]
    ovl_t = _overlap(ncp, n_cmp, LANES, blk_axis=0)

    kc_aug = jnp.concatenate([kc, place, place], axis=1)

    def cmp_scores(b):
        rhs = jnp.concatenate([lhst_ref[0:LANES, cols(b)], nbh_ref[:, cols(b)], nbl_ref[:, cols(b)]], axis=0)
        return _dot(kc_aug, rhs)

    scores = [cmp_scores(b) for b in blocks[:AHEAD]]
    group_sum = {}
    for b in blocks:
        if b + AHEAD < len(blocks):
            scores.append(cmp_scores(b + AHEAD))
        s = scores[b]
        scores[b] = None
        m = jnp.max(s, axis=0, keepdims=True)
        p = jnp.exp2(s - m)
        pc = jnp.where(m > NEG_HALF, p / jnp.sum(p, axis=0, keepdims=True), 0.0)
        oct_ref[:, cols(b)] = _dot(vct, pc.astype(_BF))
        head, part = divmod(b, n_cb)
        g, r = divmod(head, NSA_GROUP)
        group_sum[(g, part)] = pc if r == 0 else group_sum[(g, part)] + pc
        if r == NSA_GROUP - 1:
            hi, lo = _split2(group_sum.pop((g, part)))
            imp_ref[:, g * TQ + part * CB:g * TQ + (part + 1) * CB] = _dot(ovl_t, hi) + _dot(ovl_t, lo)

    t_pos = q0 + _mod(_iota((1, NSA_KV_HEADS * TQ), 1), TQ)
    msel = _select_blocks(imp_ref[...], t_pos, n_slc, min(N_SELECT, n_slc), blk_axis=0).astype(_BF)
    for g in range(NSA_KV_HEADS):
        for r in range(NSA_GROUP):
            lhst_ref[LANES:2 * LANES, head_cols(g * NSA_GROUP + r)] = msel[:, g * TQ:(g + 1) * TQ]

    c2 = _iota((LANES, LANES), 0)
    i2 = _iota((LANES, LANES), 1)
    n_kb = TK // LANES

    def near_keys(kind):
        def key_blocks(b):
            part = b % n_cb
            q_lo, q_hi = part * (CB // LANES), (part + 1) * (CB // LANES) - 1
            if kind == 'diag':
                return 0, min(n_kb, q_hi + 1)
            if kind == 'wprev':
                return q_lo, n_kb
            return 0, n_kb
        return key_blocks

    def near_add(kind):
        def add(b, s):
            head, part = divmod(b, n_cb)
            a0 = near_keys(kind)(b)[0]

            def piece_fn(a, q, x):
                rel = part * (CB // LANES) + q - a + (0 if kind == 'diag' else n_kb)
                if kind == 'diag' and rel < 0:
                    return jnp.full(x.shape, NEG, _F32)
                if kind == 'wprev' and rel > n_kb:
                    return jnp.full(x.shape, NEG, _F32)
                if kind == 'wprev' and rel == n_kb:
                    return jnp.where(c2 > i2, x, NEG)
                if rel == 0:
                    return x + dt_ref[head, 0]
                if rel == 1:
                    return x + dt_ref[head, 1]
                return None
            return _edit_blocks(s, piece_fn, a0)
        return add

    def key_slice(kind):
        def keys_of(b):
            lo, hi = near_keys(kind)(b)
            return slice(lo * LANES, hi * LANES)
        return keys_of

    _attend_init_t(ms_ref, as_ref)

    def sel_tile(kt, kind):
        k0 = pl.multiple_of(kt * TK, TK)
        kaug = jnp.concatenate([ks_ref[0, pl.ds(k0, TK), :], _block_onehot(k0, TK, LANES)], axis=1)
        v_g = _head_values(vst_ref[0, kt])
        _attend_tile_t(lhst_ref, kaug, lambda b: v_g[b // (NSA_GROUP * n_cb)], ms_ref, as_ref, blocks,
                       None if kind is None else near_add(kind), keys_of=None if kind is None else key_slice(kind))

    def far_body(kt, carry):
        sel_tile(kt, None)
        return carry
    lax.fori_loop(0, jnp.maximum(qt - 1, 0), far_body, 0)

    @pl.when(qt >= 1)
    def _():
        sel_tile(qt - 1, 'prev')
    sel_tile(qt, 'diag')

    _attend_init_t(mw_ref, aw_ref)
    qfeat = slice(0, LANES)

    def win_tile(kt, kind):
        k0 = pl.multiple_of(kt * TK, TK)
        v_g = _head_values(vwt_ref[0, kt])
        _attend_tile_t(lhst_ref, kw_ref[0, pl.ds(k0, TK), :], lambda b: v_g[b // (NSA_GROUP * n_cb)], mw_ref, aw_ref,
                       blocks, near_add(kind), feat=qfeat, keys_of=key_slice(kind))

    @pl.when(qt >= 1)
    def _():
        win_tile(qt - 1, 'wprev')
    win_tile(qt, 'diag')

    gates = misct_ref[0]
    for r in range(NSA_GROUP):
        parts = []
        for g in range(NSA_KV_HEADS):
            h = g * NSA_GROUP + r
            hc = head_cols(h)
            feat = slice(g * HEAD_DIM, (g + 1) * HEAD_DIM)
            own = slice(0, HEAD_DIM)
            den = slice(HEAD_DIM, HEAD_DIM + 1)
            gate = lambda kind: gates[L_GATE + kind * NSA_HEADS + h:L_GATE + kind * NSA_HEADS + h + 1, :]
            parts.append(gate(0) * oct_ref[feat, hc] + gate(1) * (as_ref[own, hc] / as_ref[den, hc])
                         + gate(2) * (aw_ref[own, hc] / aw_ref[den, hc]))
        on_ref[0, :, r * LANES:(r + 1) * LANES] = jnp.concatenate(parts, axis=0).T


def _nsa_prompt(qnt, misct, kcmp, vcmpt, nsab, vst, kwb, vwt, dt, nbh, nbl, n_cmp, ncp, n_slc):
    b, _, t = qnt.shape
    n_col = NSA_HEADS * TQ
    seq = lambda blk: pl.BlockSpec((1, t, LANES), lambda bi, qi, blk=blk: (bi, 0, blk))
    tiles = pl.BlockSpec((1, t // TK, LANES, TK), lambda bi, qi: (bi, 0, 0, 0))
    const = lambda a: pl.BlockSpec(a.shape, lambda bi, qi: (0,) * a.ndim)
    return pl.pallas_call(
        functools.partial(_nsa_prompt_kernel, n_cmp=n_cmp, ncp=ncp, n_slc=n_slc),
        grid=(b, t // TQ),
        in_specs=[pl.BlockSpec((1, 512, TQ), lambda bi, qi: (bi, 0, qi)),
                  pl.BlockSpec((1, LANES, TQ), lambda bi, qi: (bi, 0, qi)),
                  pl.BlockSpec((1, ncp, LANES), lambda bi, qi: (bi, 0, 0)),
                  pl.BlockSpec((1, LANES, ncp), lambda bi, qi: (bi, 0, 0)),
                  seq(2), tiles, seq(0), tiles, const(dt), const(nbh), const(nbl)],
        out_specs=pl.BlockSpec((1, TQ, 512), lambda bi, qi: (bi, qi, 0)),
        out_shape=jax.ShapeDtypeStruct((b, t, 512), _F32),
        scratch_shapes=[pltpu.VMEM((2 * LANES, n_col), _BF), pltpu.VMEM((LANES, n_col), _F32),
                        pltpu.VMEM((LANES, NSA_KV_HEADS * TQ), _F32),
                        pltpu.VMEM((SUBLANES, n_col), _F32), pltpu.VMEM((MV, n_col), _F32),
                        pltpu.VMEM((SUBLANES, n_col), _F32), pltpu.VMEM((MV, n_col), _F32)],
        compiler_params=_cparams(("arbitrary", "arbitrary")),
        name="nsa_prompt",
    )(qnt, misct, kcmp, vcmpt, nsab, vst, kwb, vwt, dt, nbh, nbl)


def _fox_prompt_kernel(qft_ref, dqt_ref, kf_ref, dk_ref, vft_ref, of_ref, lhst_ref, m_ref, acc_ref, *, tqf):
    p = pl.program_id(1)
    qt = pl.program_id(2)
    row = _iota((LANES, tqf), 0)
    low = row < HEAD_DIM
    qt_ = qft_ref[0]
    dq = dqt_ref[0]
    zero = jnp.zeros_like(qt_)
    one = jnp.ones_like(qt_)
    for a in range(2):
        head = 2 * p + a
        take_neg = (row >= L_NEG + N_PIECE * head) & (row < L_NEG + N_PIECE * (head + 1))
        take_pos = (row >= L_POS + N_PIECE * head) & (row < L_POS + N_PIECE * (head + 1))
        lhst_ref[0:LANES, a * tqf:(a + 1) * tqf] = jnp.where(low if a == 0 else jnp.logical_not(low), qt_, zero)
        lhst_ref[LANES:2 * LANES, a * tqf:(a + 1) * tqf] = jnp.where(take_neg, one, jnp.where(take_pos, dq, zero))
    _attend_init_t(m_ref, acc_ref)
    n_blk = 2 * tqf // CB
    per_q = tqf // TK
    lane = _iota((TK, LANES), 1)
    ones_pos = (jnp.clip(lane - (L_POS - 1), 0, 1) * jnp.clip(2 * L_POS - lane, 0, 1)).astype(_F32).astype(_BF)

    def tile(kt, blocks, add_fn):
        k0 = pl.multiple_of(kt * TK, TK)
        kaug = jnp.concatenate([kf_ref[0, pl.ds(k0, TK), :], jnp.where(lane < L_POS, dk_ref[0, pl.ds(k0, TK), :], ones_pos)],
                               axis=1)
        v_a = _head_values(vft_ref[0, kt])
        _attend_tile_t(lhst_ref, kaug, lambda b: v_a[b * CB // tqf], m_ref, acc_ref, blocks, add_fn)

    def far_body(kt, carry):
        tile(kt, list(range(n_blk)), None)
        return carry
    lax.fori_loop(0, qt * per_q, far_body, 0)

    for j in range(per_q):
        q_lo = lambda b: (b * CB) % tqf
        blocks = [b for b in range(n_blk) if q_lo(b) + CB > j * TK]

        def causal(b, s, j=j):
            if j * TK + TK - 1 <= q_lo(b):
                return s
            kk = j * TK + _iota((TK, CB), 0)
            qq = q_lo(b) + _iota((TK, CB), 1)
            return jnp.where(kk <= qq, s, NEG)
        tile(qt * per_q + j, blocks, causal)

    acc = acc_ref[...]
    o = [acc[0:HEAD_DIM, a * tqf:(a + 1) * tqf] / acc[HEAD_DIM:HEAD_DIM + 1, a * tqf:(a + 1) * tqf] for a in range(2)]
    of_ref[0] = jnp.concatenate(o, axis=0).T


def _fox_prompt(qft, dect, foxb, dec, vft):
    b, _, t = qft.shape
    tqf = min(TQF, t)
    n_pair = FOX_HEADS // 2
    return pl.pallas_call(
        functools.partial(_fox_prompt_kernel, tqf=tqf),
        grid=(b, n_pair, t // tqf),
        in_specs=[pl.BlockSpec((1, LANES, tqf), lambda bi, p, qi: (bi, p, qi)),
                  pl.BlockSpec((1, LANES, tqf), lambda bi, p, qi: (bi, 0, qi)),
                  pl.BlockSpec((1, t, LANES), lambda bi, p, qi: (bi, 0, p)),
                  pl.BlockSpec((1, t, LANES), lambda bi, p, qi: (bi, 0, 0)),
                  pl.BlockSpec((1, t // TK, LANES, TK), lambda bi, p, qi: (bi, 0, p, 0))],
        out_specs=pl.BlockSpec((1, tqf, LANES), lambda bi, p, qi: (bi, qi, p)),
        out_shape=jax.ShapeDtypeStruct((b, t, 512), _F32),
        scratch_shapes=[pltpu.VMEM((2 * LANES, 2 * tqf), _BF), pltpu.VMEM((SUBLANES, 2 * tqf), _F32),
                        pltpu.VMEM((MV, 2 * tqf), _F32)],
        compiler_params=_cparams(("arbitrary", "arbitrary", "arbitrary")),
        name="fox_prompt",
    )(qft, dect, foxb, dec, vft)


def _post_kernel(on_ref, of_ref, x_ref, gn_ref, gf_ref, wo_ref, gp_ref, y_ref):
    half = on_ref.shape[-1]
    a = _rms(on_ref[0], gn_ref[...]).astype(_BF)
    f = _rms(of_ref[0], gf_ref[...]).astype(_BF)
    mixed = _dot(a, wo_ref[0:half, :]) + _dot(f, wo_ref[half:2 * half, :])
    y_ref[0] = x_ref[0] + _rms(mixed, gp_ref[...])


def _post(o_n, o_f, x, gn, gf, wo, gp, tr):
    nb, t, d = x.shape
    half = o_n.shape[-1]
    row = lambda width: pl.BlockSpec((1, tr, width), lambda bi, i: (bi, i, 0))
    const = lambda shape: pl.BlockSpec(shape, lambda bi, i: (0,) * len(shape))
    return pl.pallas_call(
        _post_kernel,
        grid=(nb, t // tr),
        in_specs=[row(half), row(half), row(d), const((1, half)), const((1, half)), const((2 * half, d)), const((1, d))],
        out_specs=row(d),
        out_shape=jax.ShapeDtypeStruct((nb, t, d), _F32),
        compiler_params=_cparams(("arbitrary", "arbitrary")),
        name="out_proj",
    )(o_n, o_f, x, gn, gf, wo, gp)


def _ffn_kernel(x_ref, gpre_ref, wg_ref, wu_ref, wd_ref, gpost_ref, y_ref):
    x = x_ref[0]
    h = _rms(x, gpre_ref[...]).astype(_BF)
    act = (jax.nn.silu(_dot(h, wg_ref[...])) * _dot(h, wu_ref[...])).astype(_BF)
    y_ref[0] = x + _rms(_dot(act, wd_ref[...]), gpost_ref[...])


def _ffn(x, gpre, wg, wu, wd, gpost, tr):
    nb, t, d = x.shape
    dff = wg.shape[1]
    row = pl.BlockSpec((1, tr, d), lambda bi, i: (bi, i, 0))
    const = lambda shape: pl.BlockSpec(shape, lambda bi, i: (0,) * len(shape))
    return pl.pallas_call(
        _ffn_kernel,
        grid=(nb, t // tr),
        in_specs=[row, const((1, d)), const((d, dff)), const((d, dff)), const((dff, d)), const((1, d))],
        out_specs=row,
        out_shape=jax.ShapeDtypeStruct((nb, t, d), _F32),
        compiler_params=_cparams(("arbitrary", "arbitrary")),
        name="ffn",
    )(x, gpre, wg, wu, wd, gpost)


def _row_select(rows8, pieces):
    out = jnp.zeros((SUBLANES, pieces[0].shape[-1]), _F32)
    for i, piece in enumerate(pieces):
        out = jnp.where(rows8 == i, jnp.broadcast_to(piece, out.shape), out)
    return out


def _pad_rows(a, n):
    return jnp.concatenate([a, jnp.zeros((n - a.shape[0],) + a.shape[1:], a.dtype)], axis=0)


def _nsa_sample_kernel(pt_ref, *refs, pps, past, n_new, n_cmp, ncp, n_slc):
    page_refs = refs[:pps]
    (qn_ref, misc_ref, new_ref, win_ref, kwn_ref, w1k_ref, w1v_ref, pe_ref, w1r_ref, w2_ref,
     bc_ref, bs_ref, bw_ref, on_ref, lk_ref, lv_ref, kst_ref, vst_ref) = refs[pps:]
    del pt_ref
    j = pl.program_id(1)
    n_steps = pl.num_programs(1)
    cpp = PAGE_SIZE // CMP_STRIDE
    ppc = WINDOW // PAGE_SIZE
    n_pages = past // PAGE_SIZE

    rr = _iota((PAGE_SIZE, PAGE_SIZE), 0)
    tok = _iota((PAGE_SIZE, PAGE_SIZE), 1)
    regroup = jnp.where(tok == CMP_STRIDE * _mod(rr, cpp) + _div(rr, cpp), 1.0, 0.0).astype(_BF)

    for k in range(pps):
        pg = j * pps + k
        ref = page_refs[k]
        kst_ref[pg] = ref[0, 0, 2].astype(_BF)
        vst_ref[pg] = ref[0, 0, 3].astype(_BF)
        c0 = pl.multiple_of(pg * cpp, cpp)
        for src, dst in ((0, lk_ref), (1, lv_ref)):
            x = _dot_nt(regroup, ref[0, 0, src].astype(_BF))
            for p in range(CMP_STRIDE):
                dst[pl.ds(c0, cpp), p * LANES:(p + 1) * LANES] = x[p * cpp:(p + 1) * cpp, :]

    @pl.when(j == n_steps - 1)
    def _():
        nc = past // CMP_STRIDE
        new = new_ref[0]
        ks_new = _pad_rows(new[:, 2 * LANES:3 * LANES], PAGE_SIZE).astype(_BF)
        vs_new = _pad_rows(new[:, 3 * LANES:4 * LANES], PAGE_SIZE).astype(_BF)

        def padc(a):
            return a if ncp == nc else _pad_rows(a, ncp)
        kc = padc(_compress(lk_ref[...].astype(_BF), w1k_ref[...], pe_ref[0], w1r_ref[0], w2_ref[0], n_cmp)).astype(_BF)
        vc = padc(_compress(lv_ref[...].astype(_BF), w1v_ref[...], pe_ref[1], w1r_ref[1], w2_ref[1], n_cmp)).astype(_BF)

        rows8 = _iota((SUBLANES, LANES), 0)
        lane = _iota((SUBLANES, LANES), 1)
        grp_low = rows8 < NSA_GROUP
        q = qn_ref[0].astype(_F32)
        blocks = []
        for i in range(n_new):
            blk = jnp.zeros((SUBLANES, LANES), _F32)
            for r in range(NSA_GROUP):
                piece = jnp.broadcast_to(q[i:i + 1, r * LANES:(r + 1) * LANES], (SUBLANES, LANES))
                blk = jnp.where(_mod(rows8, NSA_GROUP) == r, piece, blk)
            on_group = jnp.logical_not(jnp.logical_xor(lane < HEAD_DIM, grp_low))
            blocks.append(jnp.where(on_group, blk, jnp.zeros_like(blk)))
        lq = jnp.concatenate(blocks, axis=0).astype(_BF)

        pc = _masked_softmax_full(_dot_nt(lq, kc) + bc_ref[...])
        o_c = _dot(pc.astype(_BF), vc)

        nl = 2 * LANES
        ovl_t = _overlap(ncp, n_cmp, nl, blk_axis=0)
        sums = []
        for i in range(n_new):
            blk = pc[i * SUBLANES:(i + 1) * SUBLANES]
            rr8 = _iota(blk.shape, 0)
            for g in range(NSA_KV_HEADS):
                in_group = (rr8 < NSA_GROUP) if g == 0 else (rr8 >= NSA_GROUP)
                sums.append(jnp.sum(jnp.where(in_group, blk, 0.0), axis=0, keepdims=True))
        hi, lo = _split2(_pad_rows(_row_select(_iota((SUBLANES, ncp), 0), sums), LANES))
        imp_t = _dot_nt(ovl_t, hi) + _dot_nt(ovl_t, lo)
        t_pos = past + _div(_iota((1, LANES), 1), NSA_KV_HEADS)
        msel = _select_blocks(imp_t, t_pos, n_slc, min(N_SELECT, n_slc), blk_axis=0).T[0:SUBLANES]
        rows8n = _iota((SUBLANES, nl), 0)
        mrows = []
        for i in range(n_new):
            m0 = jnp.broadcast_to(msel[2 * i:2 * i + 1], (SUBLANES, nl))
            m1 = jnp.broadcast_to(msel[2 * i + 1:2 * i + 2], (SUBLANES, nl))
            mrows.append(jnp.where(rows8n < NSA_GROUP, m0, m1))
        mrows = jnp.concatenate(mrows, axis=0).astype(_BF)

        def block_mask(k0, n_keys):
            s_i = _iota((nl, n_keys), 1)
            j_i = _iota((nl, n_keys), 0)
            return _dot(mrows, jnp.where(j_i == _div(k0 + s_i, SLC_BLOCK), 1.0, 0.0).astype(_BF))

        n_chunk = n_pages // ppc
        kts = lambda ci: jnp.concatenate([kst_ref[ci * ppc + u] for u in range(ppc)], axis=1)
        vts = lambda ci: jnp.concatenate([vst_ref[ci * ppc + u] for u in range(ppc)], axis=1)
        parts = [_dot(lq, kts(ci)) + block_mask(ci * ppc * PAGE_SIZE, ppc * PAGE_SIZE) for ci in range(n_chunk)]
        parts[-1] = parts[-1] + bs_ref[:, 0:WINDOW]
        parts.append(_dot_nt(lq, ks_new) + block_mask(past, PAGE_SIZE) + bs_ref[:, WINDOW:WINDOW + PAGE_SIZE])
        m_s = jnp.max(parts[0], axis=-1, keepdims=True)
        for x in parts[1:]:
            m_s = jnp.maximum(m_s, jnp.max(x, axis=-1, keepdims=True))
        l_s = jnp.zeros_like(m_s)
        acc = jnp.zeros((lq.shape[0], LANES), _F32)
        for ci, x in enumerate(parts):
            pr = jnp.exp2(x - m_s)
            l_s = l_s + jnp.sum(pr, axis=-1, keepdims=True)
            acc = acc + (_dot_nt(pr.astype(_BF), vts(ci)) if ci < n_chunk else _dot(pr.astype(_BF), vs_new))
        o_s = acc / l_s

        kwn = kwn_ref[0]
        kw_new = _pad_rows(kwn[:, 0:LANES], PAGE_SIZE).astype(_BF)
        vw_new = _pad_rows(kwn[:, LANES:2 * LANES], PAGE_SIZE).astype(_BF)
        sw = jnp.concatenate([_dot(lq, win_ref[0, 0, 0].astype(_BF)), _dot_nt(lq, kw_new)], axis=1) + bw_ref[...]
        mw = jnp.max(sw, axis=-1, keepdims=True)
        pw = jnp.exp2(sw - mw)
        pwb = pw.astype(_BF)
        o_w = (_dot_nt(pwb[:, 0:WINDOW], win_ref[0, 0, 1].astype(_BF)) + _dot(pwb[:, WINDOW:], vw_new)) \
            / jnp.sum(pw, axis=-1, keepdims=True)

        misc = misc_ref[0]
        out_rows = []
        for i in range(n_new):
            sl = slice(i * SUBLANES, (i + 1) * SUBLANES)
            g_row = jnp.broadcast_to(misc[i:i + 1, :], (SUBLANES, LANES))

            def gcol(kind, g_row=g_row):
                pick = lane == L_GATE + kind * NSA_HEADS + rows8
                return jnp.sum(jnp.where(pick, g_row, 0.0), axis=-1, keepdims=True)
            o_blk = gcol(0) * o_c[sl] + gcol(1) * o_s[sl] + gcol(2) * o_w[sl]
            pieces = [jnp.where(lane[0:1] < HEAD_DIM, o_blk[r:r + 1], o_blk[NSA_GROUP + r:NSA_GROUP + r + 1])
                      for r in range(NSA_GROUP)]
            out_rows.append(jnp.concatenate(pieces, axis=1))
        on_ref[0] = _row_select(_iota((SUBLANES, 4 * LANES), 0), out_rows)


def _nsa_sample(layer, page_table, cache_t, qn, misc, new, win_t, kwn, cw, bc, bs, bw, past, n_new, n_cmp, ncp, n_slc, pps):
    n_seq, n_pages = page_table.shape
    n_steps = n_pages // pps
    nc = past // CMP_STRIDE

    def page_spec(k):
        return pl.BlockSpec((1, 1, 4, LANES, PAGE_SIZE), lambda s, j, pt, k=k: (layer, pt[s, j * pps + k], 0, 0, 0))
    per_seq = lambda a: pl.BlockSpec((1,) + a.shape[1:], lambda s, j, pt: (s,) + (0,) * (a.ndim - 1))
    const = lambda a: pl.BlockSpec(a.shape, lambda s, j, pt: (0,) * a.ndim)
    consts = [cw["w1k"], cw["w1v"], cw["pe8"], cw["w1raw"], cw["w2p"], bc, bs, bw]
    grid_spec = pltpu.PrefetchScalarGridSpec(
        num_scalar_prefetch=1,
        grid=(n_seq, n_steps),
        in_specs=[page_spec(k) for k in range(pps)] + [per_seq(qn), per_seq(misc), per_seq(new),
                                                        pl.BlockSpec((1, 1) + win_t.shape[2:], lambda s, j, pt: (layer, s, 0, 0, 0)),
                                                        per_seq(kwn)]
        + [const(a) for a in consts],
        out_specs=pl.BlockSpec((1, SUBLANES, 4 * LANES), lambda s, j, pt: (s, 0, 0)),
        scratch_shapes=[pltpu.VMEM((nc, CMP_STRIDE * LANES), _F32), pltpu.VMEM((nc, CMP_STRIDE * LANES), _F32),
                        pltpu.VMEM((n_pages, LANES, PAGE_SIZE), _BF), pltpu.VMEM((n_pages, LANES, PAGE_SIZE), _BF)],
    )
    return pl.pallas_call(
        functools.partial(_nsa_sample_kernel, pps=pps, past=past, n_new=n_new, n_cmp=n_cmp, ncp=ncp, n_slc=n_slc),
        grid_spec=grid_spec,
        out_shape=jax.ShapeDtypeStruct((n_seq, SUBLANES, 4 * LANES), _F32),
        compiler_params=_cparams(("arbitrary", "arbitrary")),
        name="nsa_sample",
    )(page_table, *([cache_t] * pps), qn, misc, new, win_t, kwn, *consts)


def _fox_sample_kernel(pt_ref, *refs, pps, n_new):
    kv_refs = refs[:pps]
    lf_refs = refs[pps:2 * pps]
    (qf_ref, kvn_ref, lfn_ref, of_ref, q_ref, e_ref, carry_ref, m_ref, l_ref, acc_ref) = refs[2 * pps:]
    del pt_ref
    j = pl.program_id(1)
    n_steps = pl.num_programs(1)
    rows = FOX_HEADS * n_new
    width = FOX_HEADS * HEAD_DIM
    srow = _iota((PAGE_SIZE, PAGE_SIZE), 0)
    scol = _iota((PAGE_SIZE, PAGE_SIZE), 1)
    later = jnp.where(srow > scol, 1.0, 0.0).astype(_BF)

    def suffix(lf):
        hi, mid, lo = _split3(lf)
        return _dot(hi, later) + _dot(mid, later) + _dot(lo, later)

    def decay(rt):
        return (jnp.concatenate([rt] * n_new, axis=0) - e_ref[...]) * LOG2E

    @pl.when(j == 0)
    def _():
        _softmax_init(m_ref, l_ref, acc_ref)
        rows8 = _iota((SUBLANES, width), 0)
        lane = _iota((SUBLANES, width), 1)
        q = qf_ref[0].astype(_F32)
        blocks = []
        for i in range(n_new):
            piece = jnp.broadcast_to(q[i:i + 1, :], (SUBLANES, width))
            blocks.append(jnp.where(_div(lane, HEAD_DIM) == rows8, piece, jnp.zeros_like(piece)))
        q_ref[...] = jnp.concatenate(blocks, axis=0).astype(_BF)
        lfn = lfn_ref[0]
        rt = suffix(lfn)
        e_ref[...] = jnp.concatenate([rt[:, i:i + 1] for i in range(n_new)], axis=0)
        carry_ref[...] = jnp.broadcast_to(jnp.sum(lfn, axis=-1, keepdims=True), carry_ref.shape)
        kvn = kvn_ref[0]
        ri = _div(_iota((rows, PAGE_SIZE), 0), SUBLANES)
        ci = _iota((rows, PAGE_SIZE), 1)
        s = _dot_nt(q_ref[...], kvn[:, 0:width].astype(_BF)) + decay(rt) + jnp.where(ci <= ri, 0.0, NEG)
        v_new = kvn[:, width:2 * width].astype(_BF)
        _online_update(s, lambda p: _dot(p, v_new), m_ref, l_ref, acc_ref)

    kts, vts, rts = [], [], []
    carry = carry_ref[...]
    for k in range(pps):
        lf = lf_refs[k][0, 0]
        kts.append(kv_refs[k][0, 0, 0].astype(_BF))
        vts.append(kv_refs[k][0, 0, 1].astype(_BF))
        rts.append(suffix(lf) + carry)
        carry = carry + jnp.sum(lf, axis=-1, keepdims=True)
    carry_ref[...] = carry
    vt = jnp.concatenate(vts, axis=1)
    s = _dot(q_ref[...], jnp.concatenate(kts, axis=1)) + decay(jnp.concatenate(rts, axis=1))
    _online_update(s, lambda p: _dot_nt(p, vt), m_ref, l_ref, acc_ref)

    @pl.when(j == n_steps - 1)
    def _():
        o = acc_ref[...] / l_ref[...]
        rows8 = _iota((SUBLANES, width), 0)
        lane = _iota((SUBLANES, width), 1)
        out_rows = []
        for i in range(n_new):
            blk = jnp.where(_div(lane, HEAD_DIM) == rows8, o[i * SUBLANES:(i + 1) * SUBLANES], 0.0)
            out_rows.append(jnp.sum(blk, axis=0, keepdims=True))
        of_ref[0] = _row_select(rows8, out_rows)


def _fox_sample(layer, page_table, cache_t, cache_lft, qf, kvn, lfn, n_new, pps):
    n_seq, n_pages = page_table.shape
    n_steps = n_pages // pps
    rows = FOX_HEADS * n_new
    width = FOX_HEADS * HEAD_DIM

    def page_idx(s, j, pt, k):
        return pt[s, n_pages - 1 - (j * pps + k)]
    kv_spec = lambda k: pl.BlockSpec((1, 1, 2, width, PAGE_SIZE),
                                     lambda s, j, pt, k=k: (layer, page_idx(s, j, pt, k), 0, 0, 0))
    lf_spec = lambda k: pl.BlockSpec((1, 1, FOX_HEADS, PAGE_SIZE),
                                     lambda s, j, pt, k=k: (layer, page_idx(s, j, pt, k), 0, 0))
    per_seq = lambda a: pl.BlockSpec((1,) + a.shape[1:], lambda s, j, pt: (s,) + (0,) * (a.ndim - 1))
    grid_spec = pltpu.PrefetchScalarGridSpec(
        num_scalar_prefetch=1,
        grid=(n_seq, n_steps),
        in_specs=[kv_spec(k) for k in range(pps)] + [lf_spec(k) for k in range(pps)] + [per_seq(a) for a in (qf, kvn, lfn)],
        out_specs=pl.BlockSpec((1, SUBLANES, width), lambda s, j, pt: (s, 0, 0)),
        scratch_shapes=[pltpu.VMEM((rows, width), _BF), pltpu.VMEM((rows, 1), _F32), pltpu.VMEM((SUBLANES, LANES), _F32),
                        pltpu.VMEM((rows, 1), _F32), pltpu.VMEM((rows, 1), _F32), pltpu.VMEM((rows, width), _F32)],
    )
    return pl.pallas_call(
        functools.partial(_fox_sample_kernel, pps=pps, n_new=n_new),
        grid_spec=grid_spec,
        out_shape=jax.ShapeDtypeStruct((n_seq, SUBLANES, width), _F32),
        compiler_params=_cparams(("arbitrary", "arbitrary")),
        name="fox_sample",
    )(page_table, *([cache_t] * pps), *([cache_lft] * pps), qf, kvn, lfn)


def _nsa_perm():
    idx = np.zeros(NSA_HEADS * HEAD_DIM, np.int32)
    for r in range(NSA_GROUP):
        for g in range(NSA_KV_HEADS):
            for d in range(HEAD_DIM):
                idx[r * LANES + g * HEAD_DIM + d] = (g * NSA_GROUP + r) * HEAD_DIM + d
    return idx


def _layer_weights(l, w_in, b_gate, b_forget, cmp_pe, cmp_w1, cmp_w2, grp_norm_nsa, grp_norm_fox, w_o):
    perm = _nsa_perm()
    w = w_in[l]
    o_qn, o_nsa, o_kw, o_g, o_qf, o_kf, o_f = 0, 512, 1024, 1280, 1304, 1816, 2840
    f_cols = w[:, o_f:o_f + FOX_HEADS]
    misc = jnp.concatenate([w[:, o_g:o_g + 3 * NSA_HEADS], f_cols, f_cols,
                            jnp.zeros((w.shape[0], LANES - 3 * NSA_HEADS - 2 * FOX_HEADS), w.dtype)], axis=1)
    wp = jnp.concatenate([w[:, o_qn:o_nsa][:, perm], w[:, o_nsa:o_kw], w[:, o_kw:o_g], w[:, o_qf:o_kf],
                          w[:, o_kf:o_f], misc], axis=1).astype(_BF)
    bias = jnp.concatenate([b_gate[l].reshape(-1), b_forget[l], b_forget[l],
                            jnp.zeros((LANES - 3 * NSA_HEADS - 2 * FOX_HEADS,), _F32)]).reshape(1, LANES)

    def w1_layout(w1):
        w1 = w1.reshape(2, CMP_STRIDE, HEAD_DIM, CMP_HIDDEN)
        z = jnp.zeros((CMP_STRIDE, HEAD_DIM, CMP_HIDDEN), w1.dtype)
        g0 = jnp.concatenate([w1[0], w1[1], z, z], axis=-1)
        g1 = jnp.concatenate([z, z, w1[0], w1[1]], axis=-1)
        return jnp.concatenate([g0, g1], axis=1).reshape(CMP_STRIDE * LANES, 4 * CMP_HIDDEN).astype(_BF)

    def w2_layout(w2):
        z = jnp.zeros_like(w2)
        return jnp.concatenate([jnp.concatenate([w2, z], axis=1), jnp.concatenate([z, w2], axis=1)], axis=0).astype(_BF)
    cw = dict(
        w1k=w1_layout(cmp_w1[l, 0]), w1v=w1_layout(cmp_w1[l, 1]),
        pe8=jnp.broadcast_to(cmp_pe[l].reshape(2, 1, CMP_BLOCK * HEAD_DIM), (2, SUBLANES, CMP_BLOCK * HEAD_DIM)).astype(_BF),
        w1raw=cmp_w1[l].astype(_BF),
        w2p=jnp.stack([w2_layout(cmp_w2[l, 0]), w2_layout(cmp_w2[l, 1])]),
    )
    gn = grp_norm_nsa[l][perm].reshape(1, -1)
    gf = grp_norm_fox[l].reshape(1, -1)
    wo = jnp.concatenate([w_o[l][:NSA_HEADS * HEAD_DIM][perm], w_o[l][NSA_HEADS * HEAD_DIM:]], axis=0).astype(_BF)
    return wp, bias, cw, gn, gf, wo


def kernel(x_prompt, x_sample, cache_nsa_kv, cache_fox_kv, cache_fox_logf, state_win_kv, page_table, rel_bias,
           norm_mix_pre, norm_mix_post, norm_ffn_pre, norm_ffn_post, w_in, b_gate, b_forget, cmp_pe, cmp_w1, cmp_w2,
           grp_norm_nsa, grp_norm_fox, w_o, w_ffn_gate, w_ffn_up, w_ffn_down):
    depth = w_in.shape[0]
    b, t, d = x_prompt.shape
    n_seq, n_new, _ = x_sample.shape
    n_pages = page_table.shape[1]
    past = n_pages * PAGE_SIZE
    n_pool = cache_nsa_kv.shape[1]
    n_win = state_win_kv.shape[2]
    assert t % min(TQF, t) == 0 and min(TQF, t) % TK == 0 and t % TQ == 0 and TQ == TK == WINDOW and TQ % CB == 0
    assert CB % LANES == 0 and LANES > T5_FAR
    assert n_new <= SUBLANES and SLC_BLOCK >= n_new and n_win == WINDOW
    assert n_pages % (WINDOW // PAGE_SIZE) == 0 and past >= 2 * WINDOW
    assert (n_seq * n_new) % SUBLANES == 0

    nc_p = t // CMP_STRIDE
    ncp_p = -(-nc_p // LANES) * LANES
    n_slc_p = t // SLC_BLOCK
    assert n_slc_p <= LANES
    nc_s = past // CMP_STRIDE
    ncp_s = -(-nc_s // LANES) * LANES
    n_slc_s = past // SLC_BLOCK + 1
    assert n_slc_s <= 2 * LANES
    pps = math.gcd(n_pages, 8)
    pps_fox = math.gcd(n_pages, 16)

    dt, nbh, nbl = _bias_prompt(rel_bias)
    bc, bs, bw = _bias_sample(rel_bias, past, n_new, nc_s - 1, ncp_s)

    nsa_t = jnp.transpose(cache_nsa_kv, (0, 1, 3, 4, 5, 2)).reshape(depth, n_pool, 4, NSA_KV_HEADS * HEAD_DIM, PAGE_SIZE)
    fox_t = jnp.transpose(cache_fox_kv, (0, 1, 3, 4, 5, 2)).reshape(depth, n_pool, 2, FOX_HEADS * HEAD_DIM, PAGE_SIZE)
    logf_t = jnp.transpose(cache_fox_logf, (0, 1, 3, 2))
    win_t = jnp.transpose(state_win_kv, (0, 1, 3, 4, 5, 2)).reshape(depth, n_seq, 2, NSA_KV_HEADS * HEAD_DIM, n_win)

    row1 = lambda a: a.reshape(1, -1)
    pad_new = lambda a: jnp.pad(a, ((0, 0), (0, SUBLANES - n_new), (0, 0)))
    rs = n_seq * n_new
    xp, xs = x_prompt, x_sample.reshape(1, rs, d)
    outs_p, outs_s = [], []
    for l in range(depth):
        wp, bias, cw, gn, gf, wo = _layer_weights(l, w_in, b_gate, b_forget, cmp_pe, cmp_w1, cmp_w2,
                                                  grp_norm_nsa, grp_norm_fox, w_o)
        wg, wu, wd = w_ffn_gate[l].astype(_BF), w_ffn_up[l].astype(_BF), w_ffn_down[l].astype(_BF)
        g_pre, g_post = row1(norm_mix_pre[l]), row1(norm_mix_post[l])
        g_fpre, g_fpost = row1(norm_ffn_pre[l]), row1(norm_ffn_post[l])

        (_, nsa, nsab, kw, kwb, _, fox, foxb, misc, dec,
         qnt, qft, dect, misct, vst, vwt, vft) = _proj(xp, g_pre, wp, bias, TR_PROJ, True)
        kcmp, vcmpt = _compress_prompt(nsa, cw, nc_p, ncp_p)
        o_n = _nsa_prompt(qnt, misct, kcmp, vcmpt, nsab, vst, kwb, vwt, dt, nbh, nbl, nc_p - 1, ncp_p, n_slc_p)
        o_f = _fox_prompt(qft, dect, foxb, dec, vft)
        xp = _post(o_n, o_f, xp, gn, gf, wo, g_post, TR_POST)
        xp = _ffn(xp, g_fpre, wg, wu, wd, g_fpost, TR_FFN)
        outs_p.append((nsa.reshape(b, t, 4, NSA_KV_HEADS, HEAD_DIM), fox.reshape(b, t, 2, FOX_HEADS, HEAD_DIM),
                       misc[:, :, L_LOGF:L_LOGF + FOX_HEADS],
                       kw[:, t - WINDOW:].reshape(b, WINDOW, 2, NSA_KV_HEADS, HEAD_DIM)))

        tr_s = math.gcd(rs, TR_PROJ)
        qn, nsa, nsab, kw, kwb, qf, fox, foxb, misc = _proj(xs, g_pre, wp, bias, tr_s, False)
        per = lambda a: a.reshape(n_seq, n_new, a.shape[-1])
        nsa_s, fox_s, kw_s, misc_s = per(nsa), per(fox), per(kw), per(misc)
        logf_s = misc_s[:, :, L_LOGF:L_LOGF + FOX_HEADS]
        o_n = _nsa_sample(l, page_table, nsa_t, pad_new(per(qn)), pad_new(misc_s), pad_new(nsa_s), win_t,
                          pad_new(kw_s), cw, bc, bs, bw, past, n_new, nc_s - 1, ncp_s, n_slc_s, pps)
        kvn = jnp.pad(fox_s, ((0, 0), (0, PAGE_SIZE - n_new), (0, 0)))
        lfn = jnp.pad(jnp.swapaxes(logf_s, 1, 2), ((0, 0), (0, 0), (0, PAGE_SIZE - n_new)))
        o_f = _fox_sample(l, page_table, fox_t, logf_t, pad_new(per(qf)), kvn, lfn, n_new, pps_fox)
        o_n = o_n[:, :n_new].reshape(1, rs, -1)
        o_f = o_f[:, :n_new].reshape(1, rs, -1)
        xs = _post(o_n, o_f, xs, gn, gf, wo, g_post, math.gcd(rs, TR_POST))
        xs = _ffn(xs, g_fpre, wg, wu, wd, g_fpost, math.gcd(rs, TR_FFN))
        win_new = jnp.concatenate([state_win_kv[l][:, n_new:], kw_s.reshape(n_seq, n_new, 2, NSA_KV_HEADS, HEAD_DIM)], axis=1)
        outs_s.append((nsa_s.reshape(n_seq, n_new, 4, NSA_KV_HEADS, HEAD_DIM),
                       fox_s.reshape(n_seq, n_new, 2, FOX_HEADS, HEAD_DIM), logf_s, win_new))

    stack = lambda outs, i: jnp.stack([o[i] for o in outs], axis=0)
    return (xp, xs.reshape(n_seq, n_new, d), stack(outs_p, 0), stack(outs_p, 1), stack(outs_p, 2), stack(outs_p, 3),
            stack(outs_s, 0), stack(outs_s, 1), stack(outs_s, 2), stack(outs_s, 3))
```

```python
import functools
import math

import numpy as np
import jax
import jax.numpy as jnp
from jax import lax
from jax.experimental import pallas as pl
from jax.experimental.pallas import tpu as pltpu

HEAD_DIM = 64
NSA_HEADS = 8
FOX_HEADS = 8
NSA_KV_HEADS = 2
NSA_GROUP = NSA_HEADS // NSA_KV_HEADS
CMP_BLOCK = 32
CMP_STRIDE = 16
CMP_HIDDEN = 2 * HEAD_DIM
SLC_BLOCK = 64
N_SELECT = 16
WINDOW = 512
T5_BUCKETS = 32
T5_EXACT = T5_BUCKETS // 2
T5_MAX_DIST = 128
PAGE_SIZE = 128
NORM_EPS = 1e-6
FORCE_SCORE = 1e4
LOG2E = math.log2(math.e)
Q_SCALE = HEAD_DIM ** -0.5 * LOG2E

LANES = 128
SUBLANES = 8
VMEM_LIMIT = 56 * 1024 * 1024

_F32 = jnp.float32
_BF = jnp.bfloat16
NEG = -(2.0 ** 100)
NEG_HALF = -(2.0 ** 99)
M_INIT = -3.0e38
REMOVED = -3.4e38


def _t5_thresholds():
    n = np.arange(1, 4 * T5_MAX_DIST)
    large = T5_EXACT + (np.log(n / T5_EXACT) / math.log(T5_MAX_DIST / T5_EXACT) * (T5_BUCKETS - T5_EXACT)).astype(np.int64)
    return tuple(int(n[np.argmax(large >= k)]) for k in range(T5_EXACT + 1, T5_BUCKETS))


_T5_THR = _t5_thresholds()
T5_FAR = _T5_THR[-1]


def _log2(n):
    assert n & (n - 1) == 0
    return n.bit_length() - 1


TQ = 512
TQF = 2048
TK = 512
CB = 256
AHEAD = 3
MV = HEAD_DIM + SUBLANES
TR_PROJ = 256
TR_POST = 512
TR_FFN = 256
CPT = TQ // CMP_STRIDE
NEAR_BACK = -(-(T5_FAR + CMP_BLOCK - 1) // CMP_STRIDE) - 1
NEAR_U = CPT + NEAR_BACK

C_QN, C_NSA, C_KW, C_QF, C_FOX, C_MISC = 0, 512, 1024, 1280, 1792, 2816
C_TOT = 2944
L_GATE, L_LOGF, L_CUM = 0, 24, 32
N_PIECE = 3
L_NEG, L_POS = 0, N_PIECE * FOX_HEADS


def _dot(a, b):
    return jnp.dot(a, b, preferred_element_type=_F32)


def _dot_nt(a, b):
    return lax.dot_general(a, b, (((1,), (1,)), ((), ())), preferred_element_type=_F32)


def _split3(x):
    hi = x.astype(_BF)
    r1 = x - hi.astype(_F32)
    mid = r1.astype(_BF)
    lo = (r1 - mid.astype(_F32)).astype(_BF)
    return hi, mid, lo


def _split2(x):
    hi = x.astype(_BF)
    return hi, (x - hi.astype(_F32)).astype(_BF)


def _dot3(a_bf, x):
    hi, mid, lo = _split3(x)
    return _dot(a_bf, hi) + _dot(a_bf, mid) + _dot(a_bf, lo)


def _rms(x, g):
    ms = jnp.mean(x * x, axis=-1, keepdims=True)
    return x * lax.rsqrt(ms + NORM_EPS) * g


def _gelu_tanh(x):
    c = math.sqrt(2.0 / math.pi)
    return x * (0.5 * (1.0 + jnp.tanh(c * (x + 0.044715 * (x * x * x)))))


def _iota(shape, dim):
    return lax.broadcasted_iota(jnp.int32, shape, dim)


def _div(x, n):
    return jnp.right_shift(x, _log2(n))


def _mod(x, n):
    return x & (n - 1)


def _cparams(sem):
    return pltpu.CompilerParams(dimension_semantics=sem, vmem_limit_bytes=VMEM_LIMIT)


def _t5_rel(dist, rel_ref, h):
    d = jnp.minimum(dist, T5_MAX_DIST - 1)
    big = jnp.full(d.shape, T5_EXACT, jnp.int32)
    for thr in _T5_THR:
        big = big + jnp.where(d >= thr, 1, 0)
    bkt = jnp.where(d < T5_EXACT, d, big)
    far = rel_ref[T5_BUCKETS - 1, h]
    val = jnp.zeros(d.shape, _F32)
    for k in range(T5_BUCKETS - 1):
        val = jnp.where(bkt == k, (rel_ref[k, h] - far) * LOG2E, val)
    return val


def _t5_masked(dist, rel_ref, h):
    return jnp.where(dist < 0, NEG, _t5_rel(jnp.maximum(dist, 0), rel_ref, h))


def _bias_prompt_kernel(rel_ref, dt_ref, nbh_ref, nbl_ref):
    def body(h, carry):
        c = _iota((LANES, LANES), 0)
        i = _iota((LANES, LANES), 1)
        dt_ref[h, 0] = _t5_masked(i - c, rel_ref, h)
        dt_ref[h, 1] = _t5_masked(i - c + LANES, rel_ref, h)
        u = _iota((LANES, TQ), 0)
        i = _iota((LANES, TQ), 1)
        near = _t5_masked(i + (CMP_STRIDE * NEAR_BACK - (CMP_BLOCK - 1)) - CMP_STRIDE * u, rel_ref, h)
        nb = jnp.where(u < NEAR_U, near, jnp.where(u == NEAR_U, NEG, 0.0))
        hi, lo = _split2(nb)
        cols = pl.ds(pl.multiple_of(h * TQ, TQ), TQ)
        nbh_ref[:, cols] = hi
        nbl_ref[:, cols] = lo
        return carry
    lax.fori_loop(0, NSA_HEADS, body, 0)


def _bias_prompt(rel_bias):
    return pl.pallas_call(
        _bias_prompt_kernel,
        out_shape=(jax.ShapeDtypeStruct((NSA_HEADS, 2, LANES, LANES), _F32),
                   jax.ShapeDtypeStruct((LANES, NSA_HEADS * TQ), _BF),
                   jax.ShapeDtypeStruct((LANES, NSA_HEADS * TQ), _BF)),
        in_specs=[pl.BlockSpec(memory_space=pltpu.SMEM)],
        name="t5_bias_prompt",
    )(rel_bias)


def _bias_sample_kernel(rel_ref, bc_ref, bs_ref, bw_ref, *, past, n_new, n_cmp):
    def table(shape, dist_fn, extra_invalid=None):
        out = jnp.zeros(shape, _F32)
        r = _iota(shape, 0)
        c = _iota(shape, 1)
        i = _div(r, NSA_HEADS)
        dist = dist_fn(i, c)
        for h in range(NSA_HEADS):
            v = _t5_masked(dist, rel_ref, h)
            out = jnp.where(_mod(r, NSA_HEADS) == h, v, out)
        if extra_invalid is not None:
            out = jnp.where(extra_invalid(i, c, dist), NEG, out)
        return out

    bc_ref[...] = table(bc_ref.shape, lambda i, c: past + i - (CMP_STRIDE * c + CMP_BLOCK - 1),
                        lambda i, c, d: c >= n_cmp)
    bs_ref[...] = table(bs_ref.shape, lambda i, c: i + WINDOW - c)
    bw_ref[...] = table(bw_ref.shape, lambda i, c: i + WINDOW - c,
                        lambda i, c, d: (d >= WINDOW) | (c >= WINDOW + n_new))


def _bias_sample(rel_bias, past, n_new, n_cmp, ncp):
    rows = NSA_HEADS * n_new
    return pl.pallas_call(
        functools.partial(_bias_sample_kernel, past=past, n_new=n_new, n_cmp=n_cmp),
        out_shape=(jax.ShapeDtypeStruct((rows, ncp), _F32),
                   jax.ShapeDtypeStruct((rows, WINDOW + PAGE_SIZE), _F32),
                   jax.ShapeDtypeStruct((rows, WINDOW + PAGE_SIZE), _F32)),
        in_specs=[pl.BlockSpec(memory_space=pltpu.SMEM)],
        name="t5_bias_sample",
    )(rel_bias)


def _proj_kernel(x_ref, g_ref, w_ref, b_ref, *refs, tr, prompt):
    if prompt:
        (qn_ref, nsa_ref, nsab_ref, kw_ref, kwb_ref, qf_ref, fox_ref, foxb_ref, misc_ref, dec_ref,
         qnt_ref, qft_ref, dect_ref, misct_ref, vst_ref, vwt_ref, vft_ref, carry_ref) = refs
    else:
        qn_ref, nsa_ref, nsab_ref, kw_ref, kwb_ref, qf_ref, fox_ref, foxb_ref, misc_ref = refs
    x = x_ref[0]
    h = _rms(x, g_ref[...]).astype(_BF)
    qn = _dot(h, w_ref[:, C_QN:C_NSA]) * Q_SCALE
    qn_ref[0] = qn.astype(_BF)
    z_nsa = _dot(h, w_ref[:, C_NSA:C_KW])
    nsa_ref[0] = z_nsa
    nsab_ref[0] = z_nsa.astype(_BF)
    z_kw = _dot(h, w_ref[:, C_KW:C_QF])
    kw_ref[0] = z_kw
    kwb_ref[0] = z_kw.astype(_BF)
    qf = _dot(h, w_ref[:, C_QF:C_FOX]) * Q_SCALE
    qf_ref[0] = qf.astype(_BF)
    z_fox = _dot(h, w_ref[:, C_FOX:C_MISC])
    fox_ref[0] = z_fox
    foxb_ref[0] = z_fox.astype(_BF)
    zm = _dot(h, w_ref[:, C_MISC:C_TOT]) + b_ref[...]
    lane = _iota((tr, LANES), 1)
    sg = jax.nn.sigmoid(zm)
    ls = jnp.minimum(zm, 0.0) - jnp.log1p(jnp.exp(-jnp.abs(zm)))
    if not prompt:
        misc_ref[0] = jnp.where(lane < L_LOGF, sg, jnp.where(lane < L_CUM, ls, 0.0))
        return

    @pl.when(pl.program_id(1) == 0)
    def _():
        carry_ref[...] = jnp.zeros(carry_ref.shape, _F32)
    row = _iota((tr, tr), 0)
    col = _iota((tr, tr), 1)
    tri = jnp.where(col <= row, 1.0, 0.0).astype(_BF)
    cs = _dot3(tri, ls) + carry_ref[0:1, :]
    carry_ref[...] = jnp.broadcast_to(cs[tr - 1:tr, :], carry_ref.shape)
    pieces = _split3(cs * LOG2E)
    r = _iota((LANES, LANES), 0) - L_CUM
    c = _iota((LANES, LANES), 1)
    head_row = (r >= 0) & (r < FOX_HEADS)
    dec = jnp.zeros((tr, LANES), _F32)
    for j, piece in enumerate(pieces):
        put = jnp.where(head_row & (c == L_NEG + N_PIECE * r + j), -1.0,
                        jnp.where(head_row & (c == L_POS + N_PIECE * r + j), 1.0, 0.0)).astype(_BF)
        dec = dec + _dot(piece, put)
    dec_ref[0] = dec.astype(_BF)
    misc = jnp.where(lane < L_LOGF, sg, jnp.where(lane < L_CUM, ls, jnp.where(lane < L_CUM + FOX_HEADS, cs, 0.0)))
    misc_ref[0] = misc
    qnt_ref[0] = qn.T.astype(_BF)
    qft_ref[0] = qf.T.astype(_BF)
    dect_ref[0] = dec.T.astype(_BF)
    misct_ref[0] = misc.T
    vst_ref[0, 0] = z_nsa[:, 3 * LANES:4 * LANES].T.astype(_BF)
    vwt_ref[0, 0] = z_kw[:, LANES:2 * LANES].T.astype(_BF)
    vft_ref[0, 0] = z_fox[:, FOX_HEADS * HEAD_DIM:].T.astype(_BF)


def _proj(x, g, w, b, tr, prompt):
    nb, t, d = x.shape
    grid = (nb, t // tr)
    row = lambda width: pl.BlockSpec((1, tr, width), lambda bi, i: (bi, i, 0))
    const = lambda shape: pl.BlockSpec(shape, lambda bi, i: (0,) * len(shape))
    shp = lambda width, dt: jax.ShapeDtypeStruct((nb, t, width), dt)
    out_specs = [row(512), row(512), row(512), row(256), row(256), row(512), row(1024), row(1024), row(LANES)]
    out_shape = [shp(512, _BF), shp(512, _F32), shp(512, _BF), shp(256, _F32), shp(256, _BF), shp(512, _BF),
                 shp(1024, _F32), shp(1024, _BF), shp(LANES, _F32)]
    scratch = []
    if prompt:
        assert TK % tr == 0 and t % TK == 0
        per = TK // tr
        colm = lambda rows: pl.BlockSpec((1, rows, tr), lambda bi, i: (bi, 0, i))
        tile = lambda rows: pl.BlockSpec((1, 1, rows, tr), lambda bi, i: (bi, i // per, 0, i % per))
        tshp = lambda rows, dt: jax.ShapeDtypeStruct((nb, rows, t), dt)
        t4 = lambda rows: jax.ShapeDtypeStruct((nb, t // TK, rows, TK), _BF)
        out_specs += [row(LANES), colm(512), colm(512), colm(LANES), colm(LANES), tile(LANES), tile(LANES), tile(512)]
        out_shape += [shp(LANES, _BF), tshp(512, _BF), tshp(512, _BF), tshp(LANES, _BF), tshp(LANES, _F32),
                      t4(LANES), t4(LANES), t4(512)]
        scratch = [pltpu.VMEM((SUBLANES, LANES), _F32)]
    return pl.pallas_call(
        functools.partial(_proj_kernel, tr=tr, prompt=prompt),
        grid=grid,
        in_specs=[row(d), const((1, d)), const((d, C_TOT)), const((1, LANES))],
        out_specs=tuple(out_specs),
        out_shape=tuple(out_shape),
        scratch_shapes=scratch,
        compiler_params=_cparams(("arbitrary", "arbitrary")),
        name="in_proj",
    )(x, g, w, b)


def _compress(lhs_bf, w1p, pe8, w1raw, w2p, n_valid):
    nc = lhs_bf.shape[0]
    hcat = _dot(lhs_bf, w1p)
    cst = _dot(pe8, w1raw)[0:1]

    def hidden(g):
        a = hcat[:, g * 256:g * 256 + CMP_HIDDEN]
        b = hcat[:, g * 256 + CMP_HIDDEN:(g + 1) * 256]
        return _gelu_tanh(a + pltpu.roll(b, nc - 1, 0) + cst)

    hh = jnp.concatenate([hidden(0), hidden(1)], axis=1).astype(_BF)
    out = _dot(hh, w2p)
    return jnp.where(_iota(out.shape, 0) < n_valid, out, 0.0)


def _compress_prompt_kernel(xk_ref, xv_ref, w1k_ref, w1v_ref, pe_ref, w1r_ref, w2_ref, kc_ref, vc_ref, *, nc, ncp):
    for idx, (x_ref, w1, out_ref) in enumerate(((xk_ref, w1k_ref, kc_ref), (xv_ref, w1v_ref, vc_ref))):
        pieces = [x_ref[0, pl.ds(p, nc, stride=CMP_STRIDE), :] for p in range(CMP_STRIDE)]
        lhs = jnp.concatenate(pieces, axis=1).astype(_BF)
        out = _compress(lhs, w1[...], pe_ref[idx], w1r_ref[idx], w2_ref[idx], nc - 1)
        if ncp > nc:
            out = jnp.concatenate([out, jnp.zeros((ncp - nc, LANES), _F32)], axis=0)
        out_ref[0] = (out.T if idx == 1 else out).astype(_BF)


def _compress_prompt(nsa_state, cw, nc, ncp):
    b, t, _ = nsa_state.shape
    const = lambda a: pl.BlockSpec(a.shape, lambda bi: (0,) * a.ndim)
    return pl.pallas_call(
        functools.partial(_compress_prompt_kernel, nc=nc, ncp=ncp),
        grid=(b,),
        in_specs=[pl.BlockSpec((1, t, LANES), lambda bi: (bi, 0, 0)), pl.BlockSpec((1, t, LANES), lambda bi: (bi, 0, 1)),
                  const(cw["w1k"]), const(cw["w1v"]), const(cw["pe8"]), const(cw["w1raw"]), const(cw["w2p"])],
        out_specs=(pl.BlockSpec((1, ncp, LANES), lambda bi: (bi, 0, 0)), pl.BlockSpec((1, LANES, ncp), lambda bi: (bi, 0, 0))),
        out_shape=(jax.ShapeDtypeStruct((b, ncp, LANES), _BF), jax.ShapeDtypeStruct((b, LANES, ncp), _BF)),
        compiler_params=_cparams(("arbitrary",)),
        name="compress_prompt",
    )(nsa_state, nsa_state, cw["w1k"], cw["w1v"], cw["pe8"], cw["w1raw"], cw["w2p"])


def _attend_tile_t(lhst_ref, k_t, v_of, m_ref, acc_ref, blocks, add_fn=None, feat=slice(None), keys_of=None):
    n = len(blocks)
    cols = lambda b: slice(b * CB, (b + 1) * CB)
    keys = (lambda b: slice(None)) if keys_of is None else keys_of
    score = lambda b: _dot(k_t[keys(b), :], lhst_ref[feat, cols(b)])
    scores = [score(b) for b in blocks[:AHEAD]]
    for i, b in enumerate(blocks):
        if i + AHEAD < n:
            scores.append(score(blocks[i + AHEAD]))
        s = scores[i]
        scores[i] = None
        if add_fn is not None:
            s = add_fn(b, s)
        m_old = m_ref[0:1, cols(b)]
        m_new = jnp.maximum(m_old, jnp.max(s, axis=0, keepdims=True))
        alpha = jnp.exp2(m_old - m_new)
        p = jnp.exp2(s - m_new).astype(_BF)
        acc_ref[:, cols(b)] = alpha * acc_ref[:, cols(b)] + _dot(v_of(b)[:, keys(b)], p)
        m_ref[0:1, cols(b)] = m_new


def _attend_init_t(m_ref, acc_ref):
    m_ref[...] = jnp.full(m_ref.shape, M_INIT, _F32)
    acc_ref[...] = jnp.zeros(acc_ref.shape, _F32)


def _edit_blocks(s, fn, a0=0):
    rows = []
    for a in range(s.shape[0] // LANES):
        pieces = []
        for q in range(s.shape[1] // LANES):
            piece = s[a * LANES:(a + 1) * LANES, q * LANES:(q + 1) * LANES]
            new = fn(a0 + a, q, piece)
            pieces.append(piece if new is None else new)
        rows.append(jnp.concatenate(pieces, axis=1))
    return jnp.concatenate(rows, axis=0)


def _head_values(v_t):
    ones = jnp.ones((MV - HEAD_DIM, v_t.shape[1]), _BF)
    return [jnp.concatenate([v_t[a * HEAD_DIM:(a + 1) * HEAD_DIM], ones], axis=0) for a in range(2)]


def _softmax_init(m_ref, l_ref, acc_ref):
    m_ref[...] = jnp.full(m_ref.shape, M_INIT, _F32)
    l_ref[...] = jnp.zeros(l_ref.shape, _F32)
    acc_ref[...] = jnp.zeros(acc_ref.shape, _F32)


def _online_update(s, pv_fn, m_ref, l_ref, acc_ref):
    m_old = m_ref[...]
    m_new = jnp.maximum(m_old, jnp.max(s, axis=-1, keepdims=True))
    alpha = jnp.exp2(m_old - m_new)
    p = jnp.exp2(s - m_new)
    l_ref[...] = alpha * l_ref[...] + jnp.sum(p, axis=-1, keepdims=True)
    acc_ref[...] = alpha * acc_ref[...] + pv_fn(p.astype(_BF))
    m_ref[...] = m_new


def _masked_softmax_full(s):
    m = jnp.max(s, axis=-1, keepdims=True)
    p = jnp.exp2(s - m)
    l = jnp.sum(p, axis=-1, keepdims=True)
    return jnp.where(m > NEG_HALF, p / l, 0.0)


def _overlap(ncp, n_cmp, n_blk, blk_axis=1):
    shape = (ncp, n_blk) if blk_axis == 1 else (n_blk, ncp)
    c = _iota(shape, 1 - blk_axis)
    j = _iota(shape, blk_axis)
    r = SLC_BLOCK // CMP_STRIDE
    hit = (c >= r * j - (CMP_BLOCK // CMP_STRIDE - 1)) & (c <= r * j + r - 1) & (c < n_cmp)
    return jnp.where(hit, 1.0, 0.0).astype(_BF)


def _select_blocks(imp, t_pos, n_slc, n_top, blk_axis=1):
    j = _iota(imp.shape, blk_axis)
    cur = _div(t_pos, SLC_BLOCK)
    forced = (j == 0) | (j == cur) | (j == cur - 1)
    score = jnp.where(forced, FORCE_SCORE, imp)
    score = jnp.where(j * SLC_BLOCK > t_pos, -FORCE_SCORE, score)
    score = jnp.where(j >= n_slc, M_INIT, score)
    sel = jnp.zeros(imp.shape, jnp.bool_)
    for _ in range(n_top):
        mx = jnp.max(score, axis=blk_axis, keepdims=True)
        idx = jnp.min(jnp.where(score == mx, j, 1 << 20), axis=blk_axis, keepdims=True)
        hit = j == idx
        sel = sel | hit
        score = jnp.where(hit, REMOVED, score)
    return jnp.where(sel, 0.0, NEG)


def _block_onehot(k0, tk, n_lanes):
    s = _iota((tk, n_lanes), 0)
    j = _iota((tk, n_lanes), 1)
    return jnp.where(j == _div(k0 + s, SLC_BLOCK), 1.0, 0.0).astype(_BF)


def _nsa_prompt_kernel(qnt_ref, misct_ref, kc_ref, vct_ref, ks_ref, vst_ref, kw_ref, vwt_ref, dt_ref, nbh_ref, nbl_ref,
                       on_ref, lhst_ref, oct_ref, imp_ref, ms_ref, as_ref, mw_ref, aw_ref, *, n_cmp, ncp, n_slc):
    qt = pl.program_id(1)
    q0 = qt * TQ
    n_col = NSA_HEADS * TQ
    n_cb = TQ // CB
    blocks = list(range(n_col // CB))
    cols = lambda b: slice(b * CB, (b + 1) * CB)
    head_cols = lambda h: slice(h * TQ, (h + 1) * TQ)
    low = _iota((LANES, TQ), 0) < HEAD_DIM

    for g in range(NSA_KV_HEADS):
        for r in range(NSA_GROUP):
            blk = qnt_ref[0, r * LANES:(r + 1) * LANES, :]
            lhst_ref[0:LANES, head_cols(g * NSA_GROUP + r)] = jnp.where(low if g == 0 else jnp.logical_not(low),
                                                                       blk, jnp.zeros_like(blk))

    c = _iota((ncp, LANES), 0)
    u = _iota((ncp, LANES), 1)
    place = ((u < NEAR_U) & (c == CPT * qt - NEAR_BACK + u)) | ((u == NEAR_U) & (c >= CPT * qt + CPT))
    place = jnp.where(place, 1.0, 0.0).astype(_BF)
    kc = kc_ref[0]
    vct = vct_ref[0]
    ovl_t = _overlap(ncp, n_cmp, LANES, blk_axis=0)

    kc_aug = jnp.concatenate([kc, place, place], axis=1)

    def cmp_scores(b):
        rhs = jnp.concatenate([lhst_ref[0:LANES, cols(b)], nbh_ref[:, cols(b)], nbl_ref[:, cols(b)]], axis=0)
        return _dot(kc_aug, rhs)

    scores = [cmp_scores(b) for b in blocks[:AHEAD]]
    group_sum = {}
    for b in blocks:
        if b + AHEAD < len(blocks):
            scores.append(cmp_scores(b + AHEAD))
        s = scores[b]
        scores[b] = None
        m = jnp.max(s, axis=0, keepdims=True)
        p = jnp.exp2(s - m)
        pc = jnp.where(m > NEG_HALF, p / jnp.sum(p, axis=0, keepdims=True), 0.0)
        oct_ref[:, cols(b)] = _dot(vct, pc.astype(_BF))
        head, part = divmod(b, n_cb)
        g, r = divmod(head, NSA_GROUP)
        group_sum[(g, part)] = pc if r == 0 else group_sum[(g, part)] + pc
        if r == NSA_GROUP - 1:
            hi, lo = _split2(group_sum.pop((g, part)))
            imp_ref[:, g * TQ + part * CB:g * TQ + (part + 1) * CB] = _dot(ovl_t, hi) + _dot(ovl_t, lo)

    t_pos = q0 + _mod(_iota((1, NSA_KV_HEADS * TQ), 1), TQ)
    msel = _select_blocks(imp_ref[...], t_pos, n_slc, min(N_SELECT, n_slc), blk_axis=0).astype(_BF)
    for g in range(NSA_KV_HEADS):
        for r in range(NSA_GROUP):
            lhst_ref[LANES:2 * LANES, head_cols(g * NSA_GROUP + r)] = msel[:, g * TQ:(g + 1) * TQ]

    c2 = _iota((LANES, LANES), 0)
    i2 = _iota((LANES, LANES), 1)
    n_kb = TK // LANES

    def near_keys(kind):
        def key_blocks(b):
            part = b % n_cb
            q_lo, q_hi = part * (CB // LANES), (part + 1) * (CB // LANES) - 1
            if kind == 'diag':
                return 0, min(n_kb, q_hi + 1)
            if kind == 'wprev':
                return q_lo, n_kb
            return 0, n_kb
        return key_blocks

    def near_add(kind):
        def add(b, s):
            head, part = divmod(b, n_cb)
            a0 = near_keys(kind)(b)[0]

            def piece_fn(a, q, x):
                rel = part * (CB // LANES) + q - a + (0 if kind == 'diag' else n_kb)
                if kind == 'diag' and rel < 0:
                    return jnp.full(x.shape, NEG, _F32)
                if kind == 'wprev' and rel > n_kb:
                    return jnp.full(x.shape, NEG, _F32)
                if kind == 'wprev' and rel == n_kb:
                    return jnp.where(c2 > i2, x, NEG)
                if rel == 0:
                    return x + dt_ref[head, 0]
                if rel == 1:
                    return x + dt_ref[head, 1]
                return None
            return _edit_blocks(s, piece_fn, a0)
        return add

    def key_slice(kind):
        def keys_of(b):
            lo, hi = near_keys(kind)(b)
            return slice(lo * LANES, hi * LANES)
        return keys_of

    _attend_init_t(ms_ref, as_ref)

    def sel_tile(kt, kind):
        k0 = pl.multiple_of(kt * TK, TK)
        kaug = jnp.concatenate([ks_ref[0, pl.ds(k0, TK), :], _block_onehot(k0, TK, LANES)], axis=1)
        v_g = _head_values(vst_ref[0, kt])
        _attend_tile_t(lhst_ref, kaug, lambda b: v_g[b // (NSA_GROUP * n_cb)], ms_ref, as_ref, blocks,
                       None if kind is None else near_add(kind), keys_of=None if kind is None else key_slice(kind))

    def far_body(kt, carry):
        sel_tile(kt, None)
        return carry
    lax.fori_loop(0, jnp.maximum(qt - 1, 0), far_body, 0)

    @pl.when(qt >= 1)
    def _():
        sel_tile(qt - 1, 'prev')
    sel_tile(qt, 'diag')

    _attend_init_t(mw_ref, aw_ref)
    qfeat = slice(0, LANES)

    def win_tile(kt, kind):
        k0 = pl.multiple_of(kt * TK, TK)
        v_g = _head_values(vwt_ref[0, kt])
        _attend_tile_t(lhst_ref, kw_ref[0, pl.ds(k0, TK), :], lambda b: v_g[b // (NSA_GROUP * n_cb)], mw_ref, aw_ref,
                       blocks, near_add(kind), feat=qfeat, keys_of=key_slice(kind))

    @pl.when(qt >= 1)
    def _():
        win_tile(qt - 1, 'wprev')
    win_tile(qt, 'diag')

    gates = misct_ref[0]
    for r in range(NSA_GROUP):
        parts = []
        for g in range(NSA_KV_HEADS):
            h = g * NSA_GROUP + r
            hc = head_cols(h)
            feat = slice(g * HEAD_DIM, (g + 1) * HEAD_DIM)
            own = slice(0, HEAD_DIM)
            den = slice(HEAD_DIM, HEAD_DIM + 1)
            gate = lambda kind: gates[L_GATE + kind * NSA_HEADS + h:L_GATE + kind * NSA_HEADS + h + 1, :]
            parts.append(gate(0) * oct_ref[feat, hc] + gate(1) * (as_ref[own, hc] / as_ref[den, hc])
                         + gate(2) * (aw_ref[own, hc] / aw_ref[den, hc]))
        on_ref[0, :, r * LANES:(r + 1) * LANES] = jnp.concatenate(parts, axis=0).T


def _nsa_prompt(qnt, misct, kcmp, vcmpt, nsab, vst, kwb, vwt, dt, nbh, nbl, n_cmp, ncp, n_slc):
    b, _, t = qnt.shape
    n_col = NSA_HEADS * TQ
    seq = lambda blk: pl.BlockSpec((1, t, LANES), lambda bi, qi, blk=blk: (bi, 0, blk))
    tiles = pl.BlockSpec((1, t // TK, LANES, TK), lambda bi, qi: (bi, 0, 0, 0))
    const = lambda a: pl.BlockSpec(a.shape, lambda bi, qi: (0,) * a.ndim)
    return pl.pallas_call(
        functools.partial(_nsa_prompt_kernel, n_cmp=n_cmp, ncp=ncp, n_slc=n_slc),
        grid=(b, t // TQ),
        in_specs=[pl.BlockSpec((1, 512, TQ), lambda bi, qi: (bi, 0, qi)),
                  pl.BlockSpec((1, LANES, TQ), lambda bi, qi: (bi, 0, qi)),
                  pl.BlockSpec((1, ncp, LANES), lambda bi, qi: (bi, 0, 0)),
                  pl.BlockSpec((1, LANES, ncp), lambda bi, qi: (bi, 0, 0)),
                  seq(2), tiles, seq(0), tiles, const(dt), const(nbh), const(nbl)],
        out_specs=pl.BlockSpec((1, TQ, 512), lambda bi, qi: (bi, qi, 0)),
        out_shape=jax.ShapeDtypeStruct((b, t, 512), _F32),
        scratch_shapes=[pltpu.VMEM((2 * LANES, n_col), _BF), pltpu.VMEM((LANES, n_col), _F32),
                        pltpu.VMEM((LANES, NSA_KV_HEADS * TQ), _F32),
                        pltpu.VMEM((SUBLANES, n_col), _F32), pltpu.VMEM((MV, n_col), _F32),
                        pltpu.VMEM((SUBLANES, n_col), _F32), pltpu.VMEM((MV, n_col), _F32)],
        compiler_params=_cparams(("arbitrary", "arbitrary")),
        name="nsa_prompt",
    )(qnt, misct, kcmp, vcmpt, nsab, vst, kwb, vwt, dt, nbh, nbl)


def _fox_prompt_kernel(qft_ref, dqt_ref, kf_ref, dk_ref, vft_ref, of_ref, lhst_ref, m_ref, acc_ref, *, tqf):
    p = pl.program_id(1)
    qt = pl.program_id(2)
    row = _iota((LANES, tqf), 0)
    low = row < HEAD_DIM
    qt_ = qft_ref[0]
    dq = dqt_ref[0]
    zero = jnp.zeros_like(qt_)
    one = jnp.ones_like(qt_)
    for a in range(2):
        head = 2 * p + a
        take_neg = (row >= L_NEG + N_PIECE * head) & (row < L_NEG + N_PIECE * (head + 1))
        take_pos = (row >= L_POS + N_PIECE * head) & (row < L_POS + N_PIECE * (head + 1))
        lhst_ref[0:LANES, a * tqf:(a + 1) * tqf] = jnp.where(low if a == 0 else jnp.logical_not(low), qt_, zero)
        lhst_ref[LANES:2 * LANES, a * tqf:(a + 1) * tqf] = jnp.where(take_neg, one, jnp.where(take_pos, dq, zero))
    _attend_init_t(m_ref, acc_ref)
    n_blk = 2 * tqf // CB
    per_q = tqf // TK
    lane = _iota((TK, LANES), 1)
    ones_pos = (jnp.clip(lane - (L_POS - 1), 0, 1) * jnp.clip(2 * L_POS - lane, 0, 1)).astype(_F32).astype(_BF)

    def tile(kt, blocks, add_fn):
        k0 = pl.multiple_of(kt * TK, TK)
        kaug = jnp.concatenate([kf_ref[0, pl.ds(k0, TK), :], jnp.where(lane < L_POS, dk_ref[0, pl.ds(k0, TK), :], ones_pos)],
                               axis=1)
        v_a = _head_values(vft_ref[0, kt])
        _attend_tile_t(lhst_ref, kaug, lambda b: v_a[b * CB // tqf], m_ref, acc_ref, blocks, add_fn)

    def far_body(kt, carry):
        tile(kt, list(range(n_blk)), None)
        return carry
    lax.fori_loop(0, qt * per_q, far_body, 0)

    for j in range(per_q):
        q_lo = lambda b: (b * CB) % tqf
        blocks = [b for b in range(n_blk) if q_lo(b) + CB > j * TK]

        def causal(b, s, j=j):
            if j * TK + TK - 1 <= q_lo(b):
                return s
            kk = j * TK + _iota((TK, CB), 0)
            qq = q_lo(b) + _iota((TK, CB), 1)
            return jnp.where(kk <= qq, s, NEG)
        tile(qt * per_q + j, blocks, causal)

    acc = acc_ref[...]
    o = [acc[0:HEAD_DIM, a * tqf:(a + 1) * tqf] / acc[HEAD_DIM:HEAD_DIM + 1, a * tqf:(a + 1) * tqf] for a in range(2)]
    of_ref[0] = jnp.concatenate(o, axis=0).T


def _fox_prompt(qft, dect, foxb, dec, vft):
    b, _, t = qft.shape
    tqf = min(TQF, t)
    n_pair = FOX_HEADS // 2
    return pl.pallas_call(
        functools.partial(_fox_prompt_kernel, tqf=tqf),
        grid=(b, n_pair, t // tqf),
        in_specs=[pl.BlockSpec((1, LANES, tqf), lambda bi, p, qi: (bi, p, qi)),
                  pl.BlockSpec((1, LANES, tqf), lambda bi, p, qi: (bi, 0, qi)),
                  pl.BlockSpec((1, t, LANES), lambda bi, p, qi: (bi, 0, p)),
                  pl.BlockSpec((1, t, LANES), lambda bi, p, qi: (bi, 0, 0)),
                  pl.BlockSpec((1, t // TK, LANES, TK), lambda bi, p, qi: (bi, 0, p, 0))],
        out_specs=pl.BlockSpec((1, tqf, LANES), lambda bi, p, qi: (bi, qi, p)),
        out_shape=jax.ShapeDtypeStruct((b, t, 512), _F32),
        scratch_shapes=[pltpu.VMEM((2 * LANES, 2 * tqf), _BF), pltpu.VMEM((SUBLANES, 2 * tqf), _F32),
                        pltpu.VMEM((MV, 2 * tqf), _F32)],
        compiler_params=_cparams(("arbitrary", "arbitrary", "arbitrary")),
        name="fox_prompt",
    )(qft, dect, foxb, dec, vft)


def _post_kernel(on_ref, of_ref, x_ref, gn_ref, gf_ref, wo_ref, gp_ref, y_ref):
    half = on_ref.shape[-1]
    a = _rms(on_ref[0], gn_ref[...]).astype(_BF)
    f = _rms(of_ref[0], gf_ref[...]).astype(_BF)
    mixed = _dot(a, wo_ref[0:half, :]) + _dot(f, wo_ref[half:2 * half, :])
    y_ref[0] = x_ref[0] + _rms(mixed, gp_ref[...])


def _post(o_n, o_f, x, gn, gf, wo, gp, tr):
    nb, t, d = x.shape
    half = o_n.shape[-1]
    row = lambda width: pl.BlockSpec((1, tr, width), lambda bi, i: (bi, i, 0))
    const = lambda shape: pl.BlockSpec(shape, lambda bi, i: (0,) * len(shape))
    return pl.pallas_call(
        _post_kernel,
        grid=(nb, t // tr),
        in_specs=[row(half), row(half), row(d), const((1, half)), const((1, half)), const((2 * half, d)), const((1, d))],
        out_specs=row(d),
        out_shape=jax.ShapeDtypeStruct((nb, t, d), _F32),
        compiler_params=_cparams(("arbitrary", "arbitrary")),
        name="out_proj",
    )(o_n, o_f, x, gn, gf, wo, gp)


def _ffn_kernel(x_ref, gpre_ref, wg_ref, wu_ref, wd_ref, gpost_ref, y_ref):
    x = x_ref[0]
    h = _rms(x, gpre_ref[...]).astype(_BF)
    act = (jax.nn.silu(_dot(h, wg_ref[...])) * _dot(h, wu_ref[...])).astype(_BF)
    y_ref[0] = x + _rms(_dot(act, wd_ref[...]), gpost_ref[...])


def _ffn(x, gpre, wg, wu, wd, gpost, tr):
    nb, t, d = x.shape
    dff = wg.shape[1]
    row = pl.BlockSpec((1, tr, d), lambda bi, i: (bi, i, 0))
    const = lambda shape: pl.BlockSpec(shape, lambda bi, i: (0,) * len(shape))
    return pl.pallas_call(
        _ffn_kernel,
        grid=(nb, t // tr),
        in_specs=[row, const((1, d)), const((d, dff)), const((d, dff)), const((dff, d)), const((1, d))],
        out_specs=row,
        out_shape=jax.ShapeDtypeStruct((nb, t, d), _F32),
        compiler_params=_cparams(("arbitrary", "arbitrary")),
        name="ffn",
    )(x, gpre, wg, wu, wd, gpost)


def _row_select(rows8, pieces):
    out = jnp.zeros((SUBLANES, pieces[0].shape[-1]), _F32)
    for i, piece in enumerate(pieces):
        out = jnp.where(rows8 == i, jnp.broadcast_to(piece, out.shape), out)
    return out


def _pad_rows(a, n):
    return jnp.concatenate([a, jnp.zeros((n - a.shape[0],) + a.shape[1:], a.dtype)], axis=0)


def _nsa_sample_kernel(pt_ref, *refs, pps, past, n_new, n_cmp, ncp, n_slc):
    page_refs = refs[:pps]
    (qn_ref, misc_ref, new_ref, win_ref, kwn_ref, w1k_ref, w1v_ref, pe_ref, w1r_ref, w2_ref,
     bc_ref, bs_ref, bw_ref, kwt_ref, on_ref, wout_ref, lk_ref, lv_ref, kst_ref, vst_ref) = refs[pps:]
    del pt_ref
    j = pl.program_id(1)
    n_steps = pl.num_programs(1)
    cpp = PAGE_SIZE // CMP_STRIDE
    ppc = WINDOW // PAGE_SIZE
    n_pages = past // PAGE_SIZE

    rr = _iota((PAGE_SIZE, PAGE_SIZE), 0)
    tok = _iota((PAGE_SIZE, PAGE_SIZE), 1)
    regroup = jnp.where(tok == CMP_STRIDE * _mod(rr, cpp) + _div(rr, cpp), 1.0, 0.0).astype(_BF)

    for k in range(pps):
        pg = j * pps + k
        ref = page_refs[k]
        kst_ref[pg] = ref[0, 0, 2].astype(_BF)
        vst_ref[pg] = ref[0, 0, 3].astype(_BF)
        c0 = pl.multiple_of(pg * cpp, cpp)
        for src, dst in ((0, lk_ref), (1, lv_ref)):
            x = _dot_nt(regroup, ref[0, 0, src].astype(_BF))
            for p in range(CMP_STRIDE):
                dst[pl.ds(c0, cpp), p * LANES:(p + 1) * LANES] = x[p * cpp:(p + 1) * cpp, :]

    @pl.when(j == n_steps - 1)
    def _():
        nc = past // CMP_STRIDE
        new = new_ref[0]
        ks_new = _pad_rows(new[:, 2 * LANES:3 * LANES], PAGE_SIZE).astype(_BF)
        vs_new = _pad_rows(new[:, 3 * LANES:4 * LANES], PAGE_SIZE).astype(_BF)

        def padc(a):
            return a if ncp == nc else _pad_rows(a, ncp)
        kc = padc(_compress(lk_ref[...].astype(_BF), w1k_ref[...], pe_ref[0], w1r_ref[0], w2_ref[0], n_cmp)).astype(_BF)
        vc = padc(_compress(lv_ref[...].astype(_BF), w1v_ref[...], pe_ref[1], w1r_ref[1], w2_ref[1], n_cmp)).astype(_BF)

        rows8 = _iota((SUBLANES, LANES), 0)
        lane = _iota((SUBLANES, LANES), 1)
        grp_low = rows8 < NSA_GROUP
        q = qn_ref[0].astype(_F32)
        blocks = []
        for i in range(n_new):
            blk = jnp.zeros((SUBLANES, LANES), _F32)
            for r in range(NSA_GROUP):
                piece = jnp.broadcast_to(q[i:i + 1, r * LANES:(r + 1) * LANES], (SUBLANES, LANES))
                blk = jnp.where(_mod(rows8, NSA_GROUP) == r, piece, blk)
            on_group = jnp.logical_not(jnp.logical_xor(lane < HEAD_DIM, grp_low))
            blocks.append(jnp.where(on_group, blk, jnp.zeros_like(blk)))
        lq = jnp.concatenate(blocks, axis=0).astype(_BF)

        pc = _masked_softmax_full(_dot_nt(lq, kc) + bc_ref[...])
        o_c = _dot(pc.astype(_BF), vc)

        nl = 2 * LANES
        ovl_t = _overlap(ncp, n_cmp, nl, blk_axis=0)
        sums = []
        for i in range(n_new):
            blk = pc[i * SUBLANES:(i + 1) * SUBLANES]
            rr8 = _iota(blk.shape, 0)
            for g in range(NSA_KV_HEADS):
                in_group = (rr8 < NSA_GROUP) if g == 0 else (rr8 >= NSA_GROUP)
                sums.append(jnp.sum(jnp.where(in_group, blk, 0.0), axis=0, keepdims=True))
        hi, lo = _split2(_pad_rows(_row_select(_iota((SUBLANES, ncp), 0), sums), LANES))
        imp_t = _dot_nt(ovl_t, hi) + _dot_nt(ovl_t, lo)
        t_pos = past + _div(_iota((1, LANES), 1), NSA_KV_HEADS)
        msel = _select_blocks(imp_t, t_pos, n_slc, min(N_SELECT, n_slc), blk_axis=0).T[0:SUBLANES]
        rows8n = _iota((SUBLANES, nl), 0)
        mrows = []
        for i in range(n_new):
            m0 = jnp.broadcast_to(msel[2 * i:2 * i + 1], (SUBLANES, nl))
            m1 = jnp.broadcast_to(msel[2 * i + 1:2 * i + 2], (SUBLANES, nl))
            mrows.append(jnp.where(rows8n < NSA_GROUP, m0, m1))
        mrows = jnp.concatenate(mrows, axis=0).astype(_BF)

        def block_mask(k0, n_keys):
            s_i = _iota((nl, n_keys), 1)
            j_i = _iota((nl, n_keys), 0)
            return _dot(mrows, jnp.where(j_i == _div(k0 + s_i, SLC_BLOCK), 1.0, 0.0).astype(_BF))

        n_chunk = n_pages // ppc
        kts = lambda ci: jnp.concatenate([kst_ref[ci * ppc + u] for u in range(ppc)], axis=1)
        vts = lambda ci: jnp.concatenate([vst_ref[ci * ppc + u] for u in range(ppc)], axis=1)
        parts = [_dot(lq, kts(ci)) + block_mask(ci * ppc * PAGE_SIZE, ppc * PAGE_SIZE) for ci in range(n_chunk)]
        parts[-1] = parts[-1] + bs_ref[:, 0:WINDOW]
        parts.append(_dot_nt(lq, ks_new) + block_mask(past, PAGE_SIZE) + bs_ref[:, WINDOW:WINDOW + PAGE_SIZE])
        m_s = jnp.max(parts[0], axis=-1, keepdims=True)
        for x in parts[1:]:
            m_s = jnp.maximum(m_s, jnp.max(x, axis=-1, keepdims=True))
        l_s = jnp.zeros_like(m_s)
        acc = jnp.zeros((lq.shape[0], LANES), _F32)
        for ci, x in enumerate(parts):
            pr = jnp.exp2(x - m_s)
            l_s = l_s + jnp.sum(pr, axis=-1, keepdims=True)
            acc = acc + (_dot_nt(pr.astype(_BF), vts(ci)) if ci < n_chunk else _dot(pr.astype(_BF), vs_new))
        o_s = acc / l_s

        kwn = kwn_ref[0]
        kw_new = _pad_rows(kwn[:, 0:LANES], PAGE_SIZE).astype(_BF)
        vw_new = _pad_rows(kwn[:, LANES:2 * LANES], PAGE_SIZE).astype(_BF)
        sw = jnp.concatenate([_dot(lq, win_ref[0, 0, 0].astype(_BF)), _dot_nt(lq, kw_new)], axis=1) + bw_ref[...]
        mw = jnp.max(sw, axis=-1, keepdims=True)
        pw = jnp.exp2(sw - mw)
        pwb = pw.astype(_BF)
        o_w = (_dot_nt(pwb[:, 0:WINDOW], win_ref[0, 0, 1].astype(_BF)) + _dot(pwb[:, WINDOW:], vw_new)) \
            / jnp.sum(pw, axis=-1, keepdims=True)

        lane_w = _iota((LANES, WINDOW), 1)
        for kv in range(2):
            buf = pltpu.roll(win_ref[0, 0, kv], WINDOW - n_new, 1)
            for i in range(n_new):
                col = jnp.broadcast_to(kwt_ref[0, kv * LANES:(kv + 1) * LANES, i:i + 1], (LANES, WINDOW))
                buf = jnp.where(lane_w == WINDOW - n_new + i, col, buf)
            wout_ref[0, kv] = buf

        misc = misc_ref[0]
        out_rows = []
        for i in range(n_new):
            sl = slice(i * SUBLANES, (i + 1) * SUBLANES)
            g_row = jnp.broadcast_to(misc[i:i + 1, :], (SUBLANES, LANES))

            def gcol(kind, g_row=g_row):
                pick = lane == L_GATE + kind * NSA_HEADS + rows8
                return jnp.sum(jnp.where(pick, g_row, 0.0), axis=-1, keepdims=True)
            o_blk = gcol(0) * o_c[sl] + gcol(1) * o_s[sl] + gcol(2) * o_w[sl]
            pieces = [jnp.where(lane[0:1] < HEAD_DIM, o_blk[r:r + 1], o_blk[NSA_GROUP + r:NSA_GROUP + r + 1])
                      for r in range(NSA_GROUP)]
            out_rows.append(jnp.concatenate(pieces, axis=1))
        on_ref[0] = _row_select(_iota((SUBLANES, 4 * LANES), 0), out_rows)


def _nsa_sample(layer, page_table, cache_t, qn, misc, new, win_t, kwn, kwn_t, cw, bc, bs, bw, past, n_new, n_cmp, ncp, n_slc, pps):
    n_seq, n_pages = page_table.shape
    n_steps = n_pages // pps
    nc = past // CMP_STRIDE

    def page_spec(k):
        return pl.BlockSpec((1, 1, 4, LANES, PAGE_SIZE), lambda s, j, pt, k=k: (layer, pt[s, j * pps + k], 0, 0, 0))
    per_seq = lambda a: pl.BlockSpec((1,) + a.shape[1:], lambda s, j, pt: (s,) + (0,) * (a.ndim - 1))
    const = lambda a: pl.BlockSpec(a.shape, lambda s, j, pt: (0,) * a.ndim)
    consts = [cw["w1k"], cw["w1v"], cw["pe8"], cw["w1raw"], cw["w2p"], bc, bs, bw]
    grid_spec = pltpu.PrefetchScalarGridSpec(
        num_scalar_prefetch=1,
        grid=(n_seq, n_steps),
        in_specs=[page_spec(k) for k in range(pps)] + [per_seq(qn), per_seq(misc), per_seq(new),
                                                        pl.BlockSpec((1, 1) + win_t.shape[2:], lambda s, j, pt: (layer, s, 0, 0, 0)),
                                                        per_seq(kwn)]
        + [const(a) for a in consts] + [per_seq(kwn_t)],
        out_specs=(pl.BlockSpec((1, SUBLANES, 4 * LANES), lambda s, j, pt: (s, 0, 0)),
                   pl.BlockSpec((1,) + win_t.shape[2:], lambda s, j, pt: (s, 0, 0, 0))),
        scratch_shapes=[pltpu.VMEM((nc, CMP_STRIDE * LANES), _F32), pltpu.VMEM((nc, CMP_STRIDE * LANES), _F32),
                        pltpu.VMEM((n_pages, LANES, PAGE_SIZE), _BF), pltpu.VMEM((n_pages, LANES, PAGE_SIZE), _BF)],
    )
    return pl.pallas_call(
        functools.partial(_nsa_sample_kernel, pps=pps, past=past, n_new=n_new, n_cmp=n_cmp, ncp=ncp, n_slc=n_slc),
        grid_spec=grid_spec,
        out_shape=(jax.ShapeDtypeStruct((n_seq, SUBLANES, 4 * LANES), _F32),
                   jax.ShapeDtypeStruct((n_seq,) + win_t.shape[2:], _F32)),
        compiler_params=_cparams(("arbitrary", "arbitrary")),
        name="nsa_sample",
    )(page_table, *([cache_t] * pps), qn, misc, new, win_t, kwn, *consts, kwn_t)


def _fox_sample_kernel(pt_ref, *refs, pps, n_new):
    kv_refs = refs[:pps]
    lf_refs = refs[pps:2 * pps]
    (qf_ref, kvn_ref, lfn_ref, of_ref, q_ref, e_ref, carry_ref, m_ref, l_ref, acc_ref) = refs[2 * pps:]
    del pt_ref
    j = pl.program_id(1)
    n_steps = pl.num_programs(1)
    rows = FOX_HEADS * n_new
    width = FOX_HEADS * HEAD_DIM
    srow = _iota((PAGE_SIZE, PAGE_SIZE), 0)
    scol = _iota((PAGE_SIZE, PAGE_SIZE), 1)
    later = jnp.where(srow > scol, 1.0, 0.0).astype(_BF)

    def suffix(lf):
        hi, mid, lo = _split3(lf)
        return _dot(hi, later) + _dot(mid, later) + _dot(lo, later)

    def decay(rt):
        return (jnp.concatenate([rt] * n_new, axis=0) - e_ref[...]) * LOG2E

    @pl.when(j == 0)
    def _():
        _softmax_init(m_ref, l_ref, acc_ref)
        rows8 = _iota((SUBLANES, width), 0)
        lane = _iota((SUBLANES, width), 1)
        q = qf_ref[0].astype(_F32)
        blocks = []
        for i in range(n_new):
            piece = jnp.broadcast_to(q[i:i + 1, :], (SUBLANES, width))
            blocks.append(jnp.where(_div(lane, HEAD_DIM) == rows8, piece, jnp.zeros_like(piece)))
        q_ref[...] = jnp.concatenate(blocks, axis=0).astype(_BF)
        lfn = lfn_ref[0]
        rt = suffix(lfn)
        e_ref[...] = jnp.concatenate([rt[:, i:i + 1] for i in range(n_new)], axis=0)
        carry_ref[...] = jnp.broadcast_to(jnp.sum(lfn, axis=-1, keepdims=True), carry_ref.shape)
        kvn = _pad_rows(kvn_ref[0], PAGE_SIZE)
        ri = _div(_iota((rows, PAGE_SIZE), 0), SUBLANES)
        ci = _iota((rows, PAGE_SIZE), 1)
        s = _dot_nt(q_ref[...], kvn[:, 0:width].astype(_BF)) + decay(rt) + jnp.where(ci <= ri, 0.0, NEG)
        v_new = kvn[:, width:2 * width].astype(_BF)
        _online_update(s, lambda p: _dot(p, v_new), m_ref, l_ref, acc_ref)

    kts, vts, rts = [], [], []
    carry = carry_ref[...]
    for k in range(pps):
        lf = lf_refs[k][0, 0]
        kts.append(kv_refs[k][0, 0, 0].astype(_BF))
        vts.append(kv_refs[k][0, 0, 1].astype(_BF))
        rts.append(suffix(lf) + carry)
        carry = carry + jnp.sum(lf, axis=-1, keepdims=True)
    carry_ref[...] = carry
    vt = jnp.concatenate(vts, axis=1)
    s = _dot(q_ref[...], jnp.concatenate(kts, axis=1)) + decay(jnp.concatenate(rts, axis=1))
    _online_update(s, lambda p: _dot_nt(p, vt), m_ref, l_ref, acc_ref)

    @pl.when(j == n_steps - 1)
    def _():
        o = acc_ref[...] / l_ref[...]
        rows8 = _iota((SUBLANES, width), 0)
        lane = _iota((SUBLANES, width), 1)
        out_rows = []
        for i in range(n_new):
            blk = jnp.where(_div(lane, HEAD_DIM) == rows8, o[i * SUBLANES:(i + 1) * SUBLANES], 0.0)
            out_rows.append(jnp.sum(blk, axis=0, keepdims=True))
        of_ref[0] = _row_select(rows8, out_rows)


def _fox_sample(layer, page_table, cache_t, cache_lft, qf, kvn, lfn, n_new, pps):
    n_seq, n_pages = page_table.shape
    n_steps = n_pages // pps
    rows = FOX_HEADS * n_new
    width = FOX_HEADS * HEAD_DIM

    def page_idx(s, j, pt, k):
        return pt[s, n_pages - 1 - (j * pps + k)]
    kv_spec = lambda k: pl.BlockSpec((1, 1, 2, width, PAGE_SIZE),
                                     lambda s, j, pt, k=k: (layer, page_idx(s, j, pt, k), 0, 0, 0))
    lf_spec = lambda k: pl.BlockSpec((1, 1, FOX_HEADS, PAGE_SIZE),
                                     lambda s, j, pt, k=k: (layer, page_idx(s, j, pt, k), 0, 0))
    per_seq = lambda a: pl.BlockSpec((1,) + a.shape[1:], lambda s, j, pt: (s,) + (0,) * (a.ndim - 1))
    grid_spec = pltpu.PrefetchScalarGridSpec(
        num_scalar_prefetch=1,
        grid=(n_seq, n_steps),
        in_specs=[kv_spec(k) for k in range(pps)] + [lf_spec(k) for k in range(pps)] + [per_seq(a) for a in (qf, kvn, lfn)],
        out_specs=pl.BlockSpec((1, SUBLANES, width), lambda s, j, pt: (s, 0, 0)),
        scratch_shapes=[pltpu.VMEM((rows, width), _BF), pltpu.VMEM((rows, 1), _F32), pltpu.VMEM((SUBLANES, LANES), _F32),
                        pltpu.VMEM((rows, 1), _F32), pltpu.VMEM((rows, 1), _F32), pltpu.VMEM((rows, width), _F32)],
    )
    return pl.pallas_call(
        functools.partial(_fox_sample_kernel, pps=pps, n_new=n_new),
        grid_spec=grid_spec,
        out_shape=jax.ShapeDtypeStruct((n_seq, SUBLANES, width), _F32),
        compiler_params=_cparams(("arbitrary", "arbitrary")),
        name="fox_sample",
    )(page_table, *([cache_t] * pps), *([cache_lft] * pps), qf, kvn, lfn)


def _nsa_perm():
    idx = np.zeros(NSA_HEADS * HEAD_DIM, np.int32)
    for r in range(NSA_GROUP):
        for g in range(NSA_KV_HEADS):
            for d in range(HEAD_DIM):
                idx[r * LANES + g * HEAD_DIM + d] = (g * NSA_GROUP + r) * HEAD_DIM + d
    return idx


def _layer_weights(l, w_in, b_gate, b_forget, cmp_pe, cmp_w1, cmp_w2, grp_norm_nsa, grp_norm_fox, w_o):
    perm = _nsa_perm()
    w = w_in[l]
    o_qn, o_nsa, o_kw, o_g, o_qf, o_kf, o_f = 0, 512, 1024, 1280, 1304, 1816, 2840
    f_cols = w[:, o_f:o_f + FOX_HEADS]
    misc = jnp.concatenate([w[:, o_g:o_g + 3 * NSA_HEADS], f_cols, f_cols,
                            jnp.zeros((w.shape[0], LANES - 3 * NSA_HEADS - 2 * FOX_HEADS), w.dtype)], axis=1)
    wp = jnp.concatenate([w[:, o_qn:o_nsa][:, perm], w[:, o_nsa:o_kw], w[:, o_kw:o_g], w[:, o_qf:o_kf],
                          w[:, o_kf:o_f], misc], axis=1).astype(_BF)
    bias = jnp.concatenate([b_gate[l].reshape(-1), b_forget[l], b_forget[l],
                            jnp.zeros((LANES - 3 * NSA_HEADS - 2 * FOX_HEADS,), _F32)]).reshape(1, LANES)

    def w1_layout(w1):
        w1 = w1.reshape(2, CMP_STRIDE, HEAD_DIM, CMP_HIDDEN)
        z = jnp.zeros((CMP_STRIDE, HEAD_DIM, CMP_HIDDEN), w1.dtype)
        g0 = jnp.concatenate([w1[0], w1[1], z, z], axis=-1)
        g1 = jnp.concatenate([z, z, w1[0], w1[1]], axis=-1)
        return jnp.concatenate([g0, g1], axis=1).reshape(CMP_STRIDE * LANES, 4 * CMP_HIDDEN).astype(_BF)

    def w2_layout(w2):
        z = jnp.zeros_like(w2)
        return jnp.concatenate([jnp.concatenate([w2, z], axis=1), jnp.concatenate([z, w2], axis=1)], axis=0).astype(_BF)
    cw = dict(
        w1k=w1_layout(cmp_w1[l, 0]), w1v=w1_layout(cmp_w1[l, 1]),
        pe8=jnp.broadcast_to(cmp_pe[l].reshape(2, 1, CMP_BLOCK * HEAD_DIM), (2, SUBLANES, CMP_BLOCK * HEAD_DIM)).astype(_BF),
        w1raw=cmp_w1[l].astype(_BF),
        w2p=jnp.stack([w2_layout(cmp_w2[l, 0]), w2_layout(cmp_w2[l, 1])]),
    )
    gn = grp_norm_nsa[l][perm].reshape(1, -1)
    gf = grp_norm_fox[l].reshape(1, -1)
    wo = jnp.concatenate([w_o[l][:NSA_HEADS * HEAD_DIM][perm], w_o[l][NSA_HEADS * HEAD_DIM:]], axis=0).astype(_BF)
    return wp, bias, cw, gn, gf, wo


def kernel(x_prompt, x_sample, cache_nsa_kv, cache_fox_kv, cache_fox_logf, state_win_kv, page_table, rel_bias,
           norm_mix_pre, norm_mix_post, norm_ffn_pre, norm_ffn_post, w_in, b_gate, b_forget, cmp_pe, cmp_w1, cmp_w2,
           grp_norm_nsa, grp_norm_fox, w_o, w_ffn_gate, w_ffn_up, w_ffn_down):
    depth = w_in.shape[0]
    b, t, d = x_prompt.shape
    n_seq, n_new, _ = x_sample.shape
    n_pages = page_table.shape[1]
    past = n_pages * PAGE_SIZE
    n_pool = cache_nsa_kv.shape[1]
    n_win = state_win_kv.shape[2]
    assert t % min(TQF, t) == 0 and min(TQF, t) % TK == 0 and t % TQ == 0 and TQ == TK == WINDOW and TQ % CB == 0
    assert CB % LANES == 0 and LANES > T5_FAR
    assert n_new <= SUBLANES and SLC_BLOCK >= n_new and n_win == WINDOW
    assert n_pages % (WINDOW // PAGE_SIZE) == 0 and past >= 2 * WINDOW
    assert (n_seq * n_new) % SUBLANES == 0

    nc_p = t // CMP_STRIDE
    ncp_p = -(-nc_p // LANES) * LANES
    n_slc_p = t // SLC_BLOCK
    assert n_slc_p <= LANES
    nc_s = past // CMP_STRIDE
    ncp_s = -(-nc_s // LANES) * LANES
    n_slc_s = past // SLC_BLOCK + 1
    assert n_slc_s <= 2 * LANES
    pps = math.gcd(n_pages, 32)
    pps_fox = math.gcd(n_pages, 32)

    dt, nbh, nbl = _bias_prompt(rel_bias)
    bc, bs, bw = _bias_sample(rel_bias, past, n_new, nc_s - 1, ncp_s)

    nsa_t = jnp.transpose(cache_nsa_kv, (0, 1, 3, 4, 5, 2)).reshape(depth, n_pool, 4, NSA_KV_HEADS * HEAD_DIM, PAGE_SIZE)
    fox_t = jnp.transpose(cache_fox_kv, (0, 1, 3, 4, 5, 2)).reshape(depth, n_pool, 2, FOX_HEADS * HEAD_DIM, PAGE_SIZE)
    logf_t = jnp.transpose(cache_fox_logf, (0, 1, 3, 2))
    win_t = jnp.transpose(state_win_kv, (0, 1, 3, 4, 5, 2)).reshape(depth, n_seq, 2, NSA_KV_HEADS * HEAD_DIM, n_win)

    row1 = lambda a: a.reshape(1, -1)
    pad_new = lambda a: jnp.pad(a, ((0, 0), (0, SUBLANES - n_new), (0, 0)))
    rs = n_seq * n_new
    xp, xs = x_prompt, x_sample.reshape(1, rs, d)
    outs_p, outs_s = [], []
    for l in range(depth):
        wp, bias, cw, gn, gf, wo = _layer_weights(l, w_in, b_gate, b_forget, cmp_pe, cmp_w1, cmp_w2,
                                                  grp_norm_nsa, grp_norm_fox, w_o)
        wg, wu, wd = w_ffn_gate[l].astype(_BF), w_ffn_up[l].astype(_BF), w_ffn_down[l].astype(_BF)
        g_pre, g_post = row1(norm_mix_pre[l]), row1(norm_mix_post[l])
        g_fpre, g_fpost = row1(norm_ffn_pre[l]), row1(norm_ffn_post[l])

        (_, nsa, nsab, kw, kwb, _, fox, foxb, misc, dec,
         qnt, qft, dect, misct, vst, vwt, vft) = _proj(xp, g_pre, wp, bias, TR_PROJ, True)
        kcmp, vcmpt = _compress_prompt(nsa, cw, nc_p, ncp_p)
        o_n = _nsa_prompt(qnt, misct, kcmp, vcmpt, nsab, vst, kwb, vwt, dt, nbh, nbl, nc_p - 1, ncp_p, n_slc_p)
        o_f = _fox_prompt(qft, dect, foxb, dec, vft)
        xp = _post(o_n, o_f, xp, gn, gf, wo, g_post, TR_POST)
        xp = _ffn(xp, g_fpre, wg, wu, wd, g_fpost, TR_FFN)
        outs_p.append((nsa.reshape(b, t, 4, NSA_KV_HEADS, HEAD_DIM), fox.reshape(b, t, 2, FOX_HEADS, HEAD_DIM),
                       misc[:, :, L_LOGF:L_LOGF + FOX_HEADS],
                       kw[:, t - WINDOW:].reshape(b, WINDOW, 2, NSA_KV_HEADS, HEAD_DIM)))

        tr_s = math.gcd(rs, TR_PROJ)
        qn, nsa, nsab, kw, kwb, qf, fox, foxb, misc = _proj(xs, g_pre, wp, bias, tr_s, False)
        per = lambda a: a.reshape(n_seq, n_new, a.shape[-1])
        nsa_s, fox_s, kw_s, misc_s = per(nsa), per(fox), per(kw), per(misc)
        logf_s = misc_s[:, :, L_LOGF:L_LOGF + FOX_HEADS]
        kw_st = jnp.pad(jnp.swapaxes(kw_s, 1, 2), ((0, 0), (0, 0), (0, LANES - n_new)))
        o_n, win_new = _nsa_sample(l, page_table, nsa_t, pad_new(per(qn)), pad_new(misc_s), pad_new(nsa_s), win_t,
                                   pad_new(kw_s), kw_st, cw, bc, bs, bw, past, n_new, nc_s - 1, ncp_s, n_slc_s, pps)
        kvn = pad_new(fox_s)
        lfn = jnp.pad(jnp.swapaxes(logf_s, 1, 2), ((0, 0), (0, 0), (0, PAGE_SIZE - n_new)))
        o_f = _fox_sample(l, page_table, fox_t, logf_t, pad_new(per(qf)), kvn, lfn, n_new, pps_fox)
        o_n = o_n[:, :n_new].reshape(1, rs, -1)
        o_f = o_f[:, :n_new].reshape(1, rs, -1)
        xs = _post(o_n, o_f, xs, gn, gf, wo, g_post, math.gcd(rs, TR_POST))
        xs = _ffn(xs, g_fpre, wg, wu, wd, g_fpost, math.gcd(rs, TR_FFN))
        win_new = jnp.transpose(win_new.reshape(n_seq, 2, NSA_KV_HEADS, HEAD_DIM, n_win), (0, 4, 1, 2, 3))
        outs_s.append((nsa_s.reshape(n_seq, n_new, 4, NSA_KV_HEADS, HEAD_DIM),
                       fox_s.reshape(n_seq, n_new, 2, FOX_HEADS, HEAD_DIM), logf_s, win_new))

    stack = lambda outs, i: jnp.stack([o[i] for o in outs], axis=0)
    return (xp, xs.reshape(n_seq, n_new, d), stack(outs_p, 0), stack(outs_p, 1), stack(outs_p, 2), stack(outs_p, 3),
            stack(outs_s, 0), stack(outs_s, 1), stack(outs_s, 2), stack(outs_s, 3))
```

```python
import functools
import math

import numpy as np
import jax
import jax.numpy as jnp
from jax import lax
from jax.experimental import pallas as pl
from jax.experimental.pallas import tpu as pltpu

HEAD_DIM = 64
NSA_HEADS = 8
FOX_HEADS = 8
NSA_KV_HEADS = 2
NSA_GROUP = NSA_HEADS // NSA_KV_HEADS
CMP_BLOCK = 32
CMP_STRIDE = 16
CMP_HIDDEN = 2 * HEAD_DIM
SLC_BLOCK = 64
N_SELECT = 16
WINDOW = 512
T5_BUCKETS = 32
T5_EXACT = T5_BUCKETS // 2
T5_MAX_DIST = 128
PAGE_SIZE = 128
NORM_EPS = 1e-6
FORCE_SCORE = 1e4
LOG2E = math.log2(math.e)
Q_SCALE = HEAD_DIM ** -0.5 * LOG2E

LANES = 128
SUBLANES = 8
VMEM_LIMIT = 56 * 1024 * 1024

_F32 = jnp.float32
_BF = jnp.bfloat16
NEG = -(2.0 ** 100)
NEG_HALF = -(2.0 ** 99)
M_INIT = -3.0e38
REMOVED = -3.4e38


def _t5_thresholds():
    n = np.arange(1, 4 * T5_MAX_DIST)
    large = T5_EXACT + (np.log(n / T5_EXACT) / math.log(T5_MAX_DIST / T5_EXACT) * (T5_BUCKETS - T5_EXACT)).astype(np.int64)
    return tuple(int(n[np.argmax(large >= k)]) for k in range(T5_EXACT + 1, T5_BUCKETS))


_T5_THR = _t5_thresholds()
T5_FAR = _T5_THR[-1]


def _log2(n):
    assert n & (n - 1) == 0
    return n.bit_length() - 1


TQ = 512
TQF = 4096
TK = 512
CB = 256
AHEAD = 3
MV = HEAD_DIM + SUBLANES
TR_PROJ = 256
TR_POST = 512
TR_FFN = 256
CPT = TQ // CMP_STRIDE
NEAR_BACK = -(-(T5_FAR + CMP_BLOCK - 1) // CMP_STRIDE) - 1
NEAR_U = CPT + NEAR_BACK

C_QN, C_NSA, C_KW, C_QF, C_FOX, C_MISC = 0, 512, 1024, 1280, 1792, 2816
C_TOT = 2944
L_GATE, L_LOGF, L_CUM = 0, 24, 32
N_PIECE = 3
L_NEG, L_POS = 0, N_PIECE * FOX_HEADS


def _dot(a, b):
    return jnp.dot(a, b, preferred_element_type=_F32)


def _dot_nt(a, b):
    return lax.dot_general(a, b, (((1,), (1,)), ((), ())), preferred_element_type=_F32)


def _split3(x):
    hi = x.astype(_BF)
    r1 = x - hi.astype(_F32)
    mid = r1.astype(_BF)
    lo = (r1 - mid.astype(_F32)).astype(_BF)
    return hi, mid, lo


def _split2(x):
    hi = x.astype(_BF)
    return hi, (x - hi.astype(_F32)).astype(_BF)


def _dot3(a_bf, x):
    hi, mid, lo = _split3(x)
    return _dot(a_bf, hi) + _dot(a_bf, mid) + _dot(a_bf, lo)


def _rms(x, g):
    ms = jnp.mean(x * x, axis=-1, keepdims=True)
    return x * lax.rsqrt(ms + NORM_EPS) * g


def _gelu_tanh(x):
    c = math.sqrt(2.0 / math.pi)
    return x * (0.5 * (1.0 + jnp.tanh(c * (x + 0.044715 * (x * x * x)))))


def _iota(shape, dim):
    return lax.broadcasted_iota(jnp.int32, shape, dim)


def _div(x, n):
    return jnp.right_shift(x, _log2(n))


def _mod(x, n):
    return x & (n - 1)


def _cparams(sem):
    return pltpu.CompilerParams(dimension_semantics=sem, vmem_limit_bytes=VMEM_LIMIT)


def _t5_rel(dist, rel_ref, h):
    d = jnp.minimum(dist, T5_MAX_DIST - 1)
    big = jnp.full(d.shape, T5_EXACT, jnp.int32)
    for thr in _T5_THR:
        big = big + jnp.where(d >= thr, 1, 0)
    bkt = jnp.where(d < T5_EXACT, d, big)
    far = rel_ref[T5_BUCKETS - 1, h]
    val = jnp.zeros(d.shape, _F32)
    for k in range(T5_BUCKETS - 1):
        val = jnp.where(bkt == k, (rel_ref[k, h] - far) * LOG2E, val)
    return val


def _t5_masked(dist, rel_ref, h):
    return jnp.where(dist < 0, NEG, _t5_rel(jnp.maximum(dist, 0), rel_ref, h))


def _bias_prompt_kernel(rel_ref, dt_ref, nbh_ref, nbl_ref):
    def body(h, carry):
        c = _iota((LANES, LANES), 0)
        i = _iota((LANES, LANES), 1)
        dt_ref[h, 0] = _t5_masked(i - c, rel_ref, h)
        dt_ref[h, 1] = _t5_masked(i - c + LANES, rel_ref, h)
        u = _iota((LANES, TQ), 0)
        i = _iota((LANES, TQ), 1)
        near = _t5_masked(i + (CMP_STRIDE * NEAR_BACK - (CMP_BLOCK - 1)) - CMP_STRIDE * u, rel_ref, h)
        nb = jnp.where(u < NEAR_U, near, jnp.where(u == NEAR_U, NEG, 0.0))
        hi, lo = _split2(nb)
        cols = pl.ds(pl.multiple_of(h * TQ, TQ), TQ)
        nbh_ref[:, cols] = hi
        nbl_ref[:, cols] = lo
        return carry
    lax.fori_loop(0, NSA_HEADS, body, 0)


def _bias_prompt(rel_bias):
    return pl.pallas_call(
        _bias_prompt_kernel,
        out_shape=(jax.ShapeDtypeStruct((NSA_HEADS, 2, LANES, LANES), _F32),
                   jax.ShapeDtypeStruct((LANES, NSA_HEADS * TQ), _BF),
                   jax.ShapeDtypeStruct((LANES, NSA_HEADS * TQ), _BF)),
        in_specs=[pl.BlockSpec(memory_space=pltpu.SMEM)],
        name="t5_bias_prompt",
    )(rel_bias)


def _bias_sample_kernel(rel_ref, bc_ref, bs_ref, bw_ref, *, past, n_new, n_cmp):
    def table(shape, dist_fn, extra_invalid=None):
        out = jnp.zeros(shape, _F32)
        r = _iota(shape, 0)
        c = _iota(shape, 1)
        i = _div(r, NSA_HEADS)
        dist = dist_fn(i, c)
        for h in range(NSA_HEADS):
            v = _t5_masked(dist, rel_ref, h)
            out = jnp.where(_mod(r, NSA_HEADS) == h, v, out)
        if extra_invalid is not None:
            out = jnp.where(extra_invalid(i, c, dist), NEG, out)
        return out

    bc_ref[...] = table(bc_ref.shape, lambda i, c: past + i - (CMP_STRIDE * c + CMP_BLOCK - 1),
                        lambda i, c, d: c >= n_cmp)
    bs_ref[...] = table(bs_ref.shape, lambda i, c: i + WINDOW - c)
    bw_ref[...] = table(bw_ref.shape, lambda i, c: i + WINDOW - c,
                        lambda i, c, d: (d >= WINDOW) | (c >= WINDOW + n_new))


def _bias_sample(rel_bias, past, n_new, n_cmp, ncp):
    rows = NSA_HEADS * n_new
    return pl.pallas_call(
        functools.partial(_bias_sample_kernel, past=past, n_new=n_new, n_cmp=n_cmp),
        out_shape=(jax.ShapeDtypeStruct((rows, ncp), _F32),
                   jax.ShapeDtypeStruct((rows, WINDOW + PAGE_SIZE), _F32),
                   jax.ShapeDtypeStruct((rows, WINDOW + PAGE_SIZE), _F32)),
        in_specs=[pl.BlockSpec(memory_space=pltpu.SMEM)],
        name="t5_bias_sample",
    )(rel_bias)


def _proj_kernel(x_ref, g_ref, w_ref, b_ref, *refs, tr, prompt):
    if prompt:
        (qn_ref, nsa_ref, nsab_ref, kw_ref, kwb_ref, qf_ref, fox_ref, foxb_ref, misc_ref, dec_ref,
         qnt_ref, qft_ref, dect_ref, misct_ref, vst_ref, vwt_ref, vft_ref, carry_ref) = refs
    else:
        qn_ref, nsa_ref, nsab_ref, kw_ref, kwb_ref, qf_ref, fox_ref, foxb_ref, misc_ref = refs
    x = x_ref[0]
    h = _rms(x, g_ref[...]).astype(_BF)
    qn = _dot(h, w_ref[:, C_QN:C_NSA]) * Q_SCALE
    qn_ref[0] = qn.astype(_BF)
    z_nsa = _dot(h, w_ref[:, C_NSA:C_KW])
    nsa_ref[0] = z_nsa
    nsab_ref[0] = z_nsa.astype(_BF)
    z_kw = _dot(h, w_ref[:, C_KW:C_QF])
    kw_ref[0] = z_kw
    kwb_ref[0] = z_kw.astype(_BF)
    qf = _dot(h, w_ref[:, C_QF:C_FOX]) * Q_SCALE
    qf_ref[0] = qf.astype(_BF)
    z_fox = _dot(h, w_ref[:, C_FOX:C_MISC])
    fox_ref[0] = z_fox
    foxb_ref[0] = z_fox.astype(_BF)
    zm = _dot(h, w_ref[:, C_MISC:C_TOT]) + b_ref[...]
    lane = _iota((tr, LANES), 1)
    sg = jax.nn.sigmoid(zm)
    ls = jnp.minimum(zm, 0.0) - jnp.log1p(jnp.exp(-jnp.abs(zm)))
    if not prompt:
        misc_ref[0] = jnp.where(lane < L_LOGF, sg, jnp.where(lane < L_CUM, ls, 0.0))
        return

    @pl.when(pl.program_id(1) == 0)
    def _():
        carry_ref[...] = jnp.zeros(carry_ref.shape, _F32)
    row = _iota((tr, tr), 0)
    col = _iota((tr, tr), 1)
    tri = jnp.where(col <= row, 1.0, 0.0).astype(_BF)
    cs = _dot3(tri, ls) + carry_ref[0:1, :]
    carry_ref[...] = jnp.broadcast_to(cs[tr - 1:tr, :], carry_ref.shape)
    pieces = _split3(cs * LOG2E)
    r = _iota((LANES, LANES), 0) - L_CUM
    c = _iota((LANES, LANES), 1)
    head_row = (r >= 0) & (r < FOX_HEADS)
    dec = jnp.zeros((tr, LANES), _F32)
    for j, piece in enumerate(pieces):
        put = jnp.where(head_row & (c == L_NEG + N_PIECE * r + j), -1.0,
                        jnp.where(head_row & (c == L_POS + N_PIECE * r + j), 1.0, 0.0)).astype(_BF)
        dec = dec + _dot(piece, put)
    dec_ref[0] = dec.astype(_BF)
    misc = jnp.where(lane < L_LOGF, sg, jnp.where(lane < L_CUM, ls, jnp.where(lane < L_CUM + FOX_HEADS, cs, 0.0)))
    misc_ref[0] = misc
    qnt_ref[0] = qn.T.astype(_BF)
    qft_ref[0] = qf.T.astype(_BF)
    dect_ref[0] = dec.T.astype(_BF)
    misct_ref[0] = misc.T
    vst_ref[0, 0] = z_nsa[:, 3 * LANES:4 * LANES].T.astype(_BF)
    vwt_ref[0, 0] = z_kw[:, LANES:2 * LANES].T.astype(_BF)
    vft_ref[0, 0] = z_fox[:, FOX_HEADS * HEAD_DIM:].T.astype(_BF)


def _proj(x, g, w, b, tr, prompt):
    nb, t, d = x.shape
    grid = (nb, t // tr)
    row = lambda width: pl.BlockSpec((1, tr, width), lambda bi, i: (bi, i, 0))
    const = lambda shape: pl.BlockSpec(shape, lambda bi, i: (0,) * len(shape))
    shp = lambda width, dt: jax.ShapeDtypeStruct((nb, t, width), dt)
    out_specs = [row(512), row(512), row(512), row(256), row(256), row(512), row(1024), row(1024), row(LANES)]
    out_shape = [shp(512, _BF), shp(512, _F32), shp(512, _BF), shp(256, _F32), shp(256, _BF), shp(512, _BF),
                 shp(1024, _F32), shp(1024, _BF), shp(LANES, _F32)]
    scratch = []
    if prompt:
        assert TK % tr == 0 and t % TK == 0
        per = TK // tr
        colm = lambda rows: pl.BlockSpec((1, rows, tr), lambda bi, i: (bi, 0, i))
        tile = lambda rows: pl.BlockSpec((1, 1, rows, tr), lambda bi, i: (bi, i // per, 0, i % per))
        tshp = lambda rows, dt: jax.ShapeDtypeStruct((nb, rows, t), dt)
        t4 = lambda rows: jax.ShapeDtypeStruct((nb, t // TK, rows, TK), _BF)
        out_specs += [row(LANES), colm(512), colm(512), colm(LANES), colm(LANES), tile(LANES), tile(LANES), tile(512)]
        out_shape += [shp(LANES, _BF), tshp(512, _BF), tshp(512, _BF), tshp(LANES, _BF), tshp(LANES, _F32),
                      t4(LANES), t4(LANES), t4(512)]
        scratch = [pltpu.VMEM((SUBLANES, LANES), _F32)]
    return pl.pallas_call(
        functools.partial(_proj_kernel, tr=tr, prompt=prompt),
        grid=grid,
        in_specs=[row(d), const((1, d)), const((d, C_TOT)), const((1, LANES))],
        out_specs=tuple(out_specs),
        out_shape=tuple(out_shape),
        scratch_shapes=scratch,
        compiler_params=_cparams(("arbitrary", "arbitrary")),
        name="in_proj",
    )(x, g, w, b)


def _compress(lhs_bf, w1p, pe8, w1raw, w2p, n_valid):
    nc = lhs_bf.shape[0]
    hcat = _dot(lhs_bf, w1p)
    cst = _dot(pe8, w1raw)[0:1]

    def hidden(g):
        a = hcat[:, g * 256:g * 256 + CMP_HIDDEN]
        b = hcat[:, g * 256 + CMP_HIDDEN:(g + 1) * 256]
        return _gelu_tanh(a + pltpu.roll(b, nc - 1, 0) + cst)

    hh = jnp.concatenate([hidden(0), hidden(1)], axis=1).astype(_BF)
    out = _dot(hh, w2p)
    return jnp.where(_iota(out.shape, 0) < n_valid, out, 0.0)


def _compress_prompt_kernel(xk_ref, xv_ref, w1k_ref, w1v_ref, pe_ref, w1r_ref, w2_ref, kc_ref, vc_ref, *, nc, ncp):
    for idx, (x_ref, w1, out_ref) in enumerate(((xk_ref, w1k_ref, kc_ref), (xv_ref, w1v_ref, vc_ref))):
        pieces = [x_ref[0, pl.ds(p, nc, stride=CMP_STRIDE), :] for p in range(CMP_STRIDE)]
        lhs = jnp.concatenate(pieces, axis=1).astype(_BF)
        out = _compress(lhs, w1[...], pe_ref[idx], w1r_ref[idx], w2_ref[idx], nc - 1)
        if ncp > nc:
            out = jnp.concatenate([out, jnp.zeros((ncp - nc, LANES), _F32)], axis=0)
        out_ref[0] = (out.T if idx == 1 else out).astype(_BF)


def _compress_prompt(nsa_state, cw, nc, ncp):
    b, t, _ = nsa_state.shape
    const = lambda a: pl.BlockSpec(a.shape, lambda bi: (0,) * a.ndim)
    return pl.pallas_call(
        functools.partial(_compress_prompt_kernel, nc=nc, ncp=ncp),
        grid=(b,),
        in_specs=[pl.BlockSpec((1, t, LANES), lambda bi: (bi, 0, 0)), pl.BlockSpec((1, t, LANES), lambda bi: (bi, 0, 1)),
                  const(cw["w1k"]), const(cw["w1v"]), const(cw["pe8"]), const(cw["w1raw"]), const(cw["w2p"])],
        out_specs=(pl.BlockSpec((1, ncp, LANES), lambda bi: (bi, 0, 0)), pl.BlockSpec((1, LANES, ncp), lambda bi: (bi, 0, 0))),
        out_shape=(jax.ShapeDtypeStruct((b, ncp, LANES), _BF), jax.ShapeDtypeStruct((b, LANES, ncp), _BF)),
        compiler_params=_cparams(("arbitrary",)),
        name="compress_prompt",
    )(nsa_state, nsa_state, cw["w1k"], cw["w1v"], cw["pe8"], cw["w1raw"], cw["w2p"])


def _attend_tile_t(lhst_ref, k_t, v_of, m_ref, acc_ref, blocks, add_fn=None, feat=slice(None), keys_of=None):
    n = len(blocks)
    cols = lambda b: slice(b * CB, (b + 1) * CB)
    keys = (lambda b: slice(None)) if keys_of is None else keys_of
    score = lambda b: _dot(k_t[keys(b), :], lhst_ref[feat, cols(b)])
    scores = [score(b) for b in blocks[:AHEAD]]
    for i, b in enumerate(blocks):
        if i + AHEAD < n:
            scores.append(score(blocks[i + AHEAD]))
        s = scores[i]
        scores[i] = None
        if add_fn is not None:
            s = add_fn(b, s)
        m_old = m_ref[0:1, cols(b)]
        m_new = jnp.maximum(m_old, jnp.max(s, axis=0, keepdims=True))
        alpha = jnp.exp2(m_old - m_new)
        p = jnp.exp2(s - m_new).astype(_BF)
        acc_ref[:, cols(b)] = alpha * acc_ref[:, cols(b)] + _dot(v_of(b)[:, keys(b)], p)
        m_ref[0:1, cols(b)] = m_new


def _attend_init_t(m_ref, acc_ref):
    m_ref[...] = jnp.full(m_ref.shape, M_INIT, _F32)
    acc_ref[...] = jnp.zeros(acc_ref.shape, _F32)


def _edit_blocks(s, fn, a0=0):
    rows = []
    for a in range(s.shape[0] // LANES):
        pieces = []
        for q in range(s.shape[1] // LANES):
            piece = s[a * LANES:(a + 1) * LANES, q * LANES:(q + 1) * LANES]
            new = fn(a0 + a, q, piece)
            pieces.append(piece if new is None else new)
        rows.append(jnp.concatenate(pieces, axis=1))
    return jnp.concatenate(rows, axis=0)


def _head_values(v_t):
    ones = jnp.ones((MV - HEAD_DIM, v_t.shape[1]), _BF)
    return [jnp.concatenate([v_t[a * HEAD_DIM:(a + 1) * HEAD_DIM], ones], axis=0) for a in range(2)]


def _softmax_init(m_ref, l_ref, acc_ref):
    m_ref[...] = jnp.full(m_ref.shape, M_INIT, _F32)
    l_ref[...] = jnp.zeros(l_ref.shape, _F32)
    acc_ref[...] = jnp.zeros(acc_ref.shape, _F32)


def _online_update(s, pv_fn, m_ref, l_ref, acc_ref):
    m_old = m_ref[...]
    m_new = jnp.maximum(m_old, jnp.max(s, axis=-1, keepdims=True))
    alpha = jnp.exp2(m_old - m_new)
    p = jnp.exp2(s - m_new)
    l_ref[...] = alpha * l_ref[...] + jnp.sum(p, axis=-1, keepdims=True)
    acc_ref[...] = alpha * acc_ref[...] + pv_fn(p.astype(_BF))
    m_ref[...] = m_new


def _masked_softmax_full(s):
    m = jnp.max(s, axis=-1, keepdims=True)
    p = jnp.exp2(s - m)
    l = jnp.sum(p, axis=-1, keepdims=True)
    return jnp.where(m > NEG_HALF, p / l, 0.0)


def _overlap(ncp, n_cmp, n_blk, blk_axis=1):
    shape = (ncp, n_blk) if blk_axis == 1 else (n_blk, ncp)
    c = _iota(shape, 1 - blk_axis)
    j = _iota(shape, blk_axis)
    r = SLC_BLOCK // CMP_STRIDE
    hit = (c >= r * j - (CMP_BLOCK // CMP_STRIDE - 1)) & (c <= r * j + r - 1) & (c < n_cmp)
    return jnp.where(hit, 1.0, 0.0).astype(_BF)


def _select_blocks(imp, t_pos, n_slc, n_top, blk_axis=1):
    j = _iota(imp.shape, blk_axis)
    cur = _div(t_pos, SLC_BLOCK)
    forced = (j == 0) | (j == cur) | (j == cur - 1)
    score = jnp.where(forced, FORCE_SCORE, imp)
    score = jnp.where(j * SLC_BLOCK > t_pos, -FORCE_SCORE, score)
    score = jnp.where(j >= n_slc, M_INIT, score)
    sel = jnp.zeros(imp.shape, jnp.bool_)
    for _ in range(n_top):
        mx = jnp.max(score, axis=blk_axis, keepdims=True)
        idx = jnp.min(jnp.where(score == mx, j, 1 << 20), axis=blk_axis, keepdims=True)
        hit = j == idx
        sel = sel | hit
        score = jnp.where(hit, REMOVED, score)
    return jnp.where(sel, 0.0, NEG)


def _block_onehot(k0, tk, n_lanes):
    s = _iota((tk, n_lanes), 0)
    j = _iota((tk, n_lanes), 1)
    return jnp.where(j == _div(k0 + s, SLC_BLOCK), 1.0, 0.0).astype(_BF)


def _nsa_prompt_kernel(qnt_ref, misct_ref, kc_ref, vct_ref, ks_ref, vst_ref, kw_ref, vwt_ref, dt_ref, nbh_ref, nbl_ref,
                       on_ref, lhst_ref, oct_ref, imp_ref, ms_ref, as_ref, mw_ref, aw_ref, *, n_cmp, ncp, n_slc):
    qt = pl.program_id(1)
    q0 = qt * TQ
    n_col = NSA_HEADS * TQ
    n_cb = TQ // CB
    blocks = list(range(n_col // CB))
    cols = lambda b: slice(b * CB, (b + 1) * CB)
    head_cols = lambda h: slice(h * TQ, (h + 1) * TQ)
    low = _iota((LANES, TQ), 0) < HEAD_DIM

    for g in range(NSA_KV_HEADS):
        for r in range(NSA_GROUP):
            blk = qnt_ref[0, r * LANES:(r + 1) * LANES, :]
            lhst_ref[0:LANES, head_cols(g * NSA_GROUP + r)] = jnp.where(low if g == 0 else jnp.logical_not(low),
                                                                       blk, jnp.zeros_like(blk))

    c = _iota((ncp, LANES), 0)
    u = _iota((ncp, LANES), 1)
    place = ((u < NEAR_U) & (c == CPT * qt - NEAR_BACK + u)) | ((u == NEAR_U) & (c >= CPT * qt + CPT))
    place = jnp.where(place, 1.0, 0.0).astype(_BF)
    kc = kc_ref[0]
    vct = vct_ref[0]
    ovl_t = _overlap(ncp, n_cmp, LANES, blk_axis=0)

    kc_aug = jnp.concatenate([kc, place, place], axis=1)

    def cmp_scores(b):
        rhs = jnp.concatenate([lhst_ref[0:LANES, cols(b)], nbh_ref[:, cols(b)], nbl_ref[:, cols(b)]], axis=0)
        return _dot(kc_aug, rhs)

    scores = [cmp_scores(b) for b in blocks[:AHEAD]]
    group_sum = {}
    for b in blocks:
        if b + AHEAD < len(blocks):
            scores.append(cmp_scores(b + AHEAD))
        s = scores[b]
        scores[b] = None
        m = jnp.max(s, axis=0, keepdims=True)
        p = jnp.exp2(s - m)
        pc = jnp.where(m > NEG_HALF, p / jnp.sum(p, axis=0, keepdims=True), 0.0)
        oct_ref[:, cols(b)] = _dot(vct, pc.astype(_BF))
        head, part = divmod(b, n_cb)
        g, r = divmod(head, NSA_GROUP)
        group_sum[(g, part)] = pc if r == 0 else group_sum[(g, part)] + pc
        if r == NSA_GROUP - 1:
            hi, lo = _split2(group_sum.pop((g, part)))
            imp_ref[:, g * TQ + part * CB:g * TQ + (part + 1) * CB] = _dot(ovl_t, hi) + _dot(ovl_t, lo)

    t_pos = q0 + _mod(_iota((1, NSA_KV_HEADS * TQ), 1), TQ)
    msel = _select_blocks(imp_ref[...], t_pos, n_slc, min(N_SELECT, n_slc), blk_axis=0).astype(_BF)
    for g in range(NSA_KV_HEADS):
        for r in range(NSA_GROUP):
            lhst_ref[LANES:2 * LANES, head_cols(g * NSA_GROUP + r)] = msel[:, g * TQ:(g + 1) * TQ]

    c2 = _iota((LANES, LANES), 0)
    i2 = _iota((LANES, LANES), 1)
    n_kb = TK // LANES

    def near_keys(kind):
        def key_blocks(b):
            part = b % n_cb
            q_lo, q_hi = part * (CB // LANES), (part + 1) * (CB // LANES) - 1
            if kind == 'diag':
                return 0, min(n_kb, q_hi + 1)
            if kind == 'wprev':
                return q_lo, n_kb
            return 0, n_kb
        return key_blocks

    def near_add(kind):
        def add(b, s):
            head, part = divmod(b, n_cb)
            a0 = near_keys(kind)(b)[0]

            def piece_fn(a, q, x):
                rel = part * (CB // LANES) + q - a + (0 if kind == 'diag' else n_kb)
                if kind == 'diag' and rel < 0:
                    return jnp.full(x.shape, NEG, _F32)
                if kind == 'wprev' and rel > n_kb:
                    return jnp.full(x.shape, NEG, _F32)
                if kind == 'wprev' and rel == n_kb:
                    return jnp.where(c2 > i2, x, NEG)
                if rel == 0:
                    return x + dt_ref[head, 0]
                if rel == 1:
                    return x + dt_ref[head, 1]
                return None
            return _edit_blocks(s, piece_fn, a0)
        return add

    def key_slice(kind):
        def keys_of(b):
            lo, hi = near_keys(kind)(b)
            return slice(lo * LANES, hi * LANES)
        return keys_of

    _attend_init_t(ms_ref, as_ref)

    def sel_tile(kt, kind):
        k0 = pl.multiple_of(kt * TK, TK)
        kaug = jnp.concatenate([ks_ref[0, pl.ds(k0, TK), :], _block_onehot(k0, TK, LANES)], axis=1)
        v_g = _head_values(vst_ref[0, kt])
        _attend_tile_t(lhst_ref, kaug, lambda b: v_g[b // (NSA_GROUP * n_cb)], ms_ref, as_ref, blocks,
                       None if kind is None else near_add(kind), keys_of=None if kind is None else key_slice(kind))

    def far_body(kt, carry):
        sel_tile(kt, None)
        return carry
    lax.fori_loop(0, jnp.maximum(qt - 1, 0), far_body, 0)

    @pl.when(qt >= 1)
    def _():
        sel_tile(qt - 1, 'prev')
    sel_tile(qt, 'diag')

    _attend_init_t(mw_ref, aw_ref)
    qfeat = slice(0, LANES)

    def win_tile(kt, kind):
        k0 = pl.multiple_of(kt * TK, TK)
        v_g = _head_values(vwt_ref[0, kt])
        _attend_tile_t(lhst_ref, kw_ref[0, pl.ds(k0, TK), :], lambda b: v_g[b // (NSA_GROUP * n_cb)], mw_ref, aw_ref,
                       blocks, near_add(kind), feat=qfeat, keys_of=key_slice(kind))

    @pl.when(qt >= 1)
    def _():
        win_tile(qt - 1, 'wprev')
    win_tile(qt, 'diag')

    gates = misct_ref[0]
    for r in range(NSA_GROUP):
        parts = []
        for g in range(NSA_KV_HEADS):
            h = g * NSA_GROUP + r
            hc = head_cols(h)
            feat = slice(g * HEAD_DIM, (g + 1) * HEAD_DIM)
            own = slice(0, HEAD_DIM)
            den = slice(HEAD_DIM, HEAD_DIM + 1)
            gate = lambda kind: gates[L_GATE + kind * NSA_HEADS + h:L_GATE + kind * NSA_HEADS + h + 1, :]
            parts.append(gate(0) * oct_ref[feat, hc] + gate(1) * (as_ref[own, hc] / as_ref[den, hc])
                         + gate(2) * (aw_ref[own, hc] / aw_ref[den, hc]))
        on_ref[0, :, r * LANES:(r + 1) * LANES] = jnp.concatenate(parts, axis=0).T


def _nsa_prompt(qnt, misct, kcmp, vcmpt, nsab, vst, kwb, vwt, dt, nbh, nbl, n_cmp, ncp, n_slc):
    b, _, t = qnt.shape
    n_col = NSA_HEADS * TQ
    seq = lambda blk: pl.BlockSpec((1, t, LANES), lambda bi, qi, blk=blk: (bi, 0, blk))
    tiles = pl.BlockSpec((1, t // TK, LANES, TK), lambda bi, qi: (bi, 0, 0, 0))
    const = lambda a: pl.BlockSpec(a.shape, lambda bi, qi: (0,) * a.ndim)
    return pl.pallas_call(
        functools.partial(_nsa_prompt_kernel, n_cmp=n_cmp, ncp=ncp, n_slc=n_slc),
        grid=(b, t // TQ),
        in_specs=[pl.BlockSpec((1, 512, TQ), lambda bi, qi: (bi, 0, qi)),
                  pl.BlockSpec((1, LANES, TQ), lambda bi, qi: (bi, 0, qi)),
                  pl.BlockSpec((1, ncp, LANES), lambda bi, qi: (bi, 0, 0)),
                  pl.BlockSpec((1, LANES, ncp), lambda bi, qi: (bi, 0, 0)),
                  seq(2), tiles, seq(0), tiles, const(dt), const(nbh), const(nbl)],
        out_specs=pl.BlockSpec((1, TQ, 512), lambda bi, qi: (bi, qi, 0)),
        out_shape=jax.ShapeDtypeStruct((b, t, 512), _F32),
        scratch_shapes=[pltpu.VMEM((2 * LANES, n_col), _BF), pltpu.VMEM((LANES, n_col), _F32),
                        pltpu.VMEM((LANES, NSA_KV_HEADS * TQ), _F32),
                        pltpu.VMEM((SUBLANES, n_col), _F32), pltpu.VMEM((MV, n_col), _F32),
                        pltpu.VMEM((SUBLANES, n_col), _F32), pltpu.VMEM((MV, n_col), _F32)],
        compiler_params=_cparams(("arbitrary", "arbitrary")),
        name="nsa_prompt",
    )(qnt, misct, kcmp, vcmpt, nsab, vst, kwb, vwt, dt, nbh, nbl)


def _fox_prompt_kernel(qft_ref, dqt_ref, kf_ref, dk_ref, vft_ref, of_ref, lhst_ref, m_ref, acc_ref, *, tqf):
    p = pl.program_id(1)
    qt = pl.program_id(2)
    row = _iota((LANES, tqf), 0)
    low = row < HEAD_DIM
    qt_ = qft_ref[0]
    dq = dqt_ref[0]
    zero = jnp.zeros_like(qt_)
    one = jnp.ones_like(qt_)
    for a in range(2):
        head = 2 * p + a
        take_neg = (row >= L_NEG + N_PIECE * head) & (row < L_NEG + N_PIECE * (head + 1))
        take_pos = (row >= L_POS + N_PIECE * head) & (row < L_POS + N_PIECE * (head + 1))
        lhst_ref[0:LANES, a * tqf:(a + 1) * tqf] = jnp.where(low if a == 0 else jnp.logical_not(low), qt_, zero)
        lhst_ref[LANES:2 * LANES, a * tqf:(a + 1) * tqf] = jnp.where(take_neg, one, jnp.where(take_pos, dq, zero))
    _attend_init_t(m_ref, acc_ref)
    n_blk = 2 * tqf // CB
    per_q = tqf // TK
    lane = _iota((TK, LANES), 1)
    ones_pos = (jnp.clip(lane - (L_POS - 1), 0, 1) * jnp.clip(2 * L_POS - lane, 0, 1)).astype(_F32).astype(_BF)

    def tile(kt, blocks, add_fn):
        k0 = pl.multiple_of(kt * TK, TK)
        kaug = jnp.concatenate([kf_ref[0, pl.ds(k0, TK), :], jnp.where(lane < L_POS, dk_ref[0, pl.ds(k0, TK), :], ones_pos)],
                               axis=1)
        v_a = _head_values(vft_ref[0, kt])
        _attend_tile_t(lhst_ref, kaug, lambda b: v_a[b * CB // tqf], m_ref, acc_ref, blocks, add_fn)

    def far_body(kt, carry):
        tile(kt, list(range(n_blk)), None)
        return carry
    lax.fori_loop(0, qt * per_q, far_body, 0)

    for j in range(per_q):
        q_lo = lambda b: (b * CB) % tqf
        blocks = [b for b in range(n_blk) if q_lo(b) + CB > j * TK]

        def causal(b, s, j=j):
            if j * TK + TK - 1 <= q_lo(b):
                return s
            kk = j * TK + _iota((TK, CB), 0)
            qq = q_lo(b) + _iota((TK, CB), 1)
            return jnp.where(kk <= qq, s, NEG)
        tile(qt * per_q + j, blocks, causal)

    acc = acc_ref[...]
    o = [acc[0:HEAD_DIM, a * tqf:(a + 1) * tqf] / acc[HEAD_DIM:HEAD_DIM + 1, a * tqf:(a + 1) * tqf] for a in range(2)]
    of_ref[0] = jnp.concatenate(o, axis=0).T


def _fox_prompt(qft, dect, foxb, dec, vft):
    b, _, t = qft.shape
    tqf = min(TQF, t)
    n_pair = FOX_HEADS // 2
    return pl.pallas_call(
        functools.partial(_fox_prompt_kernel, tqf=tqf),
        grid=(b, n_pair, t // tqf),
        in_specs=[pl.BlockSpec((1, LANES, tqf), lambda bi, p, qi: (bi, p, qi)),
                  pl.BlockSpec((1, LANES, tqf), lambda bi, p, qi: (bi, 0, qi)),
                  pl.BlockSpec((1, t, LANES), lambda bi, p, qi: (bi, 0, p)),
                  pl.BlockSpec((1, t, LANES), lambda bi, p, qi: (bi, 0, 0)),
                  pl.BlockSpec((1, t // TK, LANES, TK), lambda bi, p, qi: (bi, 0, p, 0))],
        out_specs=pl.BlockSpec((1, tqf, LANES), lambda bi, p, qi: (bi, qi, p)),
        out_shape=jax.ShapeDtypeStruct((b, t, 512), _F32),
        scratch_shapes=[pltpu.VMEM((2 * LANES, 2 * tqf), _BF), pltpu.VMEM((SUBLANES, 2 * tqf), _F32),
                        pltpu.VMEM((MV, 2 * tqf), _F32)],
        compiler_params=_cparams(("arbitrary", "arbitrary", "arbitrary")),
        name="fox_prompt",
    )(qft, dect, foxb, dec, vft)


def _post_kernel(on_ref, of_ref, x_ref, gn_ref, gf_ref, wo_ref, gp_ref, y_ref):
    half = on_ref.shape[-1]
    a = _rms(on_ref[0], gn_ref[...]).astype(_BF)
    f = _rms(of_ref[0], gf_ref[...]).astype(_BF)
    mixed = _dot(a, wo_ref[0:half, :]) + _dot(f, wo_ref[half:2 * half, :])
    y_ref[0] = x_ref[0] + _rms(mixed, gp_ref[...])


def _post(o_n, o_f, x, gn, gf, wo, gp, tr):
    nb, t, d = x.shape
    half = o_n.shape[-1]
    row = lambda width: pl.BlockSpec((1, tr, width), lambda bi, i: (bi, i, 0))
    const = lambda shape: pl.BlockSpec(shape, lambda bi, i: (0,) * len(shape))
    return pl.pallas_call(
        _post_kernel,
        grid=(nb, t // tr),
        in_specs=[row(half), row(half), row(d), const((1, half)), const((1, half)), const((2 * half, d)), const((1, d))],
        out_specs=row(d),
        out_shape=jax.ShapeDtypeStruct((nb, t, d), _F32),
        compiler_params=_cparams(("arbitrary", "arbitrary")),
        name="out_proj",
    )(o_n, o_f, x, gn, gf, wo, gp)


def _ffn_kernel(x_ref, gpre_ref, wg_ref, wu_ref, wd_ref, gpost_ref, y_ref):
    x = x_ref[0]
    h = _rms(x, gpre_ref[...]).astype(_BF)
    act = (jax.nn.silu(_dot(h, wg_ref[...])) * _dot(h, wu_ref[...])).astype(_BF)
    y_ref[0] = x + _rms(_dot(act, wd_ref[...]), gpost_ref[...])


def _ffn(x, gpre, wg, wu, wd, gpost, tr):
    nb, t, d = x.shape
    dff = wg.shape[1]
    row = pl.BlockSpec((1, tr, d), lambda bi, i: (bi, i, 0))
    const = lambda shape: pl.BlockSpec(shape, lambda bi, i: (0,) * len(shape))
    return pl.pallas_call(
        _ffn_kernel,
        grid=(nb, t // tr),
        in_specs=[row, const((1, d)), const((d, dff)), const((d, dff)), const((dff, d)), const((1, d))],
        out_specs=row,
        out_shape=jax.ShapeDtypeStruct((nb, t, d), _F32),
        compiler_params=_cparams(("arbitrary", "arbitrary")),
        name="ffn",
    )(x, gpre, wg, wu, wd, gpost)


def _row_select(rows8, pieces):
    out = jnp.zeros((SUBLANES, pieces[0].shape[-1]), _F32)
    for i, piece in enumerate(pieces):
        out = jnp.where(rows8 == i, jnp.broadcast_to(piece, out.shape), out)
    return out


def _pad_rows(a, n):
    return jnp.concatenate([a, jnp.zeros((n - a.shape[0],) + a.shape[1:], a.dtype)], axis=0)


def _nsa_sample_kernel(pt_ref, *refs, pps, past, n_new, n_cmp, ncp, n_slc):
    page_refs = refs[:pps]
    (qn_ref, misc_ref, new_ref, win_ref, kwn_ref, w1k_ref, w1v_ref, pe_ref, w1r_ref, w2_ref,
     bc_ref, bs_ref, bw_ref, kwt_ref, on_ref, wout_ref, lk_ref, lv_ref, kst_ref, vst_ref) = refs[pps:]
    del pt_ref
    j = pl.program_id(1)
    n_steps = pl.num_programs(1)
    cpp = PAGE_SIZE // CMP_STRIDE
    ppc = WINDOW // PAGE_SIZE
    n_pages = past // PAGE_SIZE

    rr = _iota((PAGE_SIZE, PAGE_SIZE), 0)
    tok = _iota((PAGE_SIZE, PAGE_SIZE), 1)
    regroup = jnp.where(tok == CMP_STRIDE * _mod(rr, cpp) + _div(rr, cpp), 1.0, 0.0).astype(_BF)

    for k in range(pps):
        pg = j * pps + k
        ref = page_refs[k]
        kst_ref[pg] = ref[0, 0, 2].astype(_BF)
        vst_ref[pg] = ref[0, 0, 3].astype(_BF)
        c0 = pl.multiple_of(pg * cpp, cpp)
        for src, dst in ((0, lk_ref), (1, lv_ref)):
            x = _dot_nt(regroup, ref[0, 0, src].astype(_BF))
            for p in range(CMP_STRIDE):
                dst[pl.ds(c0, cpp), p * LANES:(p + 1) * LANES] = x[p * cpp:(p + 1) * cpp, :]

    @pl.when(j == n_steps - 1)
    def _():
        nc = past // CMP_STRIDE
        new = new_ref[0]
        ks_new = _pad_rows(new[:, 2 * LANES:3 * LANES], PAGE_SIZE).astype(_BF)
        vs_new = _pad_rows(new[:, 3 * LANES:4 * LANES], PAGE_SIZE).astype(_BF)

        def padc(a):
            return a if ncp == nc else _pad_rows(a, ncp)
        kc = padc(_compress(lk_ref[...].astype(_BF), w1k_ref[...], pe_ref[0], w1r_ref[0], w2_ref[0], n_cmp)).astype(_BF)
        vc = padc(_compress(lv_ref[...].astype(_BF), w1v_ref[...], pe_ref[1], w1r_ref[1], w2_ref[1], n_cmp)).astype(_BF)

        rows8 = _iota((SUBLANES, LANES), 0)
        lane = _iota((SUBLANES, LANES), 1)
        grp_low = rows8 < NSA_GROUP
        q = qn_ref[0].astype(_F32)
        blocks = []
        for i in range(n_new):
            blk = jnp.zeros((SUBLANES, LANES), _F32)
            for r in range(NSA_GROUP):
                piece = jnp.broadcast_to(q[i:i + 1, r * LANES:(r + 1) * LANES], (SUBLANES, LANES))
                blk = jnp.where(_mod(rows8, NSA_GROUP) == r, piece, blk)
            on_group = jnp.logical_not(jnp.logical_xor(lane < HEAD_DIM, grp_low))
            blocks.append(jnp.where(on_group, blk, jnp.zeros_like(blk)))
        lq = jnp.concatenate(blocks, axis=0).astype(_BF)

        pc = _masked_softmax_full(_dot_nt(lq, kc) + bc_ref[...])
        o_c = _dot(pc.astype(_BF), vc)

        nl = 2 * LANES
        ovl_t = _overlap(ncp, n_cmp, nl, blk_axis=0)
        sums = []
        for i in range(n_new):
            blk = pc[i * SUBLANES:(i + 1) * SUBLANES]
            rr8 = _iota(blk.shape, 0)
            for g in range(NSA_KV_HEADS):
                in_group = (rr8 < NSA_GROUP) if g == 0 else (rr8 >= NSA_GROUP)
                sums.append(jnp.sum(jnp.where(in_group, blk, 0.0), axis=0, keepdims=True))
        hi, lo = _split2(_pad_rows(_row_select(_iota((SUBLANES, ncp), 0), sums), LANES))
        imp_t = _dot_nt(ovl_t, hi) + _dot_nt(ovl_t, lo)
        t_pos = past + _div(_iota((1, LANES), 1), NSA_KV_HEADS)
        msel = _select_blocks(imp_t, t_pos, n_slc, min(N_SELECT, n_slc), blk_axis=0).T[0:SUBLANES]
        rows8n = _iota((SUBLANES, nl), 0)
        mrows = []
        for i in range(n_new):
            m0 = jnp.broadcast_to(msel[2 * i:2 * i + 1], (SUBLANES, nl))
            m1 = jnp.broadcast_to(msel[2 * i + 1:2 * i + 2], (SUBLANES, nl))
            mrows.append(jnp.where(rows8n < NSA_GROUP, m0, m1))
        mrows = jnp.concatenate(mrows, axis=0).astype(_BF)

        def block_mask(k0, n_keys):
            s_i = _iota((nl, n_keys), 1)
            j_i = _iota((nl, n_keys), 0)
            return _dot(mrows, jnp.where(j_i == _div(k0 + s_i, SLC_BLOCK), 1.0, 0.0).astype(_BF))

        n_chunk = n_pages // ppc
        kts = lambda ci: jnp.concatenate([kst_ref[ci * ppc + u] for u in range(ppc)], axis=1)
        vts = lambda ci: jnp.concatenate([vst_ref[ci * ppc + u] for u in range(ppc)], axis=1)
        parts = [_dot(lq, kts(ci)) + block_mask(ci * ppc * PAGE_SIZE, ppc * PAGE_SIZE) for ci in range(n_chunk)]
        parts[-1] = parts[-1] + bs_ref[:, 0:WINDOW]
        parts.append(_dot_nt(lq, ks_new) + block_mask(past, PAGE_SIZE) + bs_ref[:, WINDOW:WINDOW + PAGE_SIZE])
        m_s = jnp.max(parts[0], axis=-1, keepdims=True)
        for x in parts[1:]:
            m_s = jnp.maximum(m_s, jnp.max(x, axis=-1, keepdims=True))
        l_s = jnp.zeros_like(m_s)
        acc = jnp.zeros((lq.shape[0], LANES), _F32)
        for ci, x in enumerate(parts):
            pr = jnp.exp2(x - m_s)
            l_s = l_s + jnp.sum(pr, axis=-1, keepdims=True)
            acc = acc + (_dot_nt(pr.astype(_BF), vts(ci)) if ci < n_chunk else _dot(pr.astype(_BF), vs_new))
        o_s = acc / l_s

        kwn = kwn_ref[0]
        kw_new = _pad_rows(kwn[:, 0:LANES], PAGE_SIZE).astype(_BF)
        vw_new = _pad_rows(kwn[:, LANES:2 * LANES], PAGE_SIZE).astype(_BF)
        sw = jnp.concatenate([_dot(lq, win_ref[0, 0, 0].astype(_BF)), _dot_nt(lq, kw_new)], axis=1) + bw_ref[...]
        mw = jnp.max(sw, axis=-1, keepdims=True)
        pw = jnp.exp2(sw - mw)
        pwb = pw.astype(_BF)
        o_w = (_dot_nt(pwb[:, 0:WINDOW], win_ref[0, 0, 1].astype(_BF)) + _dot(pwb[:, WINDOW:], vw_new)) \
            / jnp.sum(pw, axis=-1, keepdims=True)

        lane_w = _iota((LANES, WINDOW), 1)
        for kv in range(2):
            buf = pltpu.roll(win_ref[0, 0, kv], WINDOW - n_new, 1)
            for i in range(n_new):
                col = jnp.broadcast_to(kwt_ref[0, kv * LANES:(kv + 1) * LANES, i:i + 1], (LANES, WINDOW))
                buf = jnp.where(lane_w == WINDOW - n_new + i, col, buf)
            wout_ref[0, kv] = buf

        misc = misc_ref[0]
        out_rows = []
        for i in range(n_new):
            sl = slice(i * SUBLANES, (i + 1) * SUBLANES)
            g_row = jnp.broadcast_to(misc[i:i + 1, :], (SUBLANES, LANES))

            def gcol(kind, g_row=g_row):
                pick = lane == L_GATE + kind * NSA_HEADS + rows8
                return jnp.sum(jnp.where(pick, g_row, 0.0), axis=-1, keepdims=True)
            o_blk = gcol(0) * o_c[sl] + gcol(1) * o_s[sl] + gcol(2) * o_w[sl]
            pieces = [jnp.where(lane[0:1] < HEAD_DIM, o_blk[r:r + 1], o_blk[NSA_GROUP + r:NSA_GROUP + r + 1])
                      for r in range(NSA_GROUP)]
            out_rows.append(jnp.concatenate(pieces, axis=1))
        on_ref[0] = _row_select(_iota((SUBLANES, 4 * LANES), 0), out_rows)


def _nsa_sample(layer, page_table, cache_t, qn, misc, new, win_t, kwn, kwn_t, cw, bc, bs, bw, past, n_new, n_cmp, ncp, n_slc, pps):
    n_seq, n_pages = page_table.shape
    n_steps = n_pages // pps
    nc = past // CMP_STRIDE

    def page_spec(k):
        return pl.BlockSpec((1, 1, 4, LANES, PAGE_SIZE), lambda s, j, pt, k=k: (layer, pt[s, j * pps + k], 0, 0, 0))
    per_seq = lambda a: pl.BlockSpec((1,) + a.shape[1:], lambda s, j, pt: (s,) + (0,) * (a.ndim - 1))
    const = lambda a: pl.BlockSpec(a.shape, lambda s, j, pt: (0,) * a.ndim)
    consts = [cw["w1k"], cw["w1v"], cw["pe8"], cw["w1raw"], cw["w2p"], bc, bs, bw]
    grid_spec = pltpu.PrefetchScalarGridSpec(
        num_scalar_prefetch=1,
        grid=(n_seq, n_steps),
        in_specs=[page_spec(k) for k in range(pps)] + [per_seq(qn), per_seq(misc), per_seq(new),
                                                        pl.BlockSpec((1, 1) + win_t.shape[2:], lambda s, j, pt: (layer, s, 0, 0, 0)),
                                                        per_seq(kwn)]
        + [const(a) for a in consts] + [per_seq(kwn_t)],
        out_specs=(pl.BlockSpec((1, SUBLANES, 4 * LANES), lambda s, j, pt: (s, 0, 0)),
                   pl.BlockSpec((1,) + win_t.shape[2:], lambda s, j, pt: (s, 0, 0, 0))),
        scratch_shapes=[pltpu.VMEM((nc, CMP_STRIDE * LANES), _F32), pltpu.VMEM((nc, CMP_STRIDE * LANES), _F32),
                        pltpu.VMEM((n_pages, LANES, PAGE_SIZE), _BF), pltpu.VMEM((n_pages, LANES, PAGE_SIZE), _BF)],
    )
    return pl.pallas_call(
        functools.partial(_nsa_sample_kernel, pps=pps, past=past, n_new=n_new, n_cmp=n_cmp, ncp=ncp, n_slc=n_slc),
        grid_spec=grid_spec,
        out_shape=(jax.ShapeDtypeStruct((n_seq, SUBLANES, 4 * LANES), _F32),
                   jax.ShapeDtypeStruct((n_seq,) + win_t.shape[2:], _F32)),
        compiler_params=_cparams(("arbitrary", "arbitrary")),
        name="nsa_sample",
    )(page_table, *([cache_t] * pps), qn, misc, new, win_t, kwn, *consts, kwn_t)


def _fox_sample_kernel(pt_ref, *refs, pps, n_new):
    kv_refs = refs[:pps]
    lf_refs = refs[pps:2 * pps]
    (qf_ref, kvn_ref, lfn_ref, of_ref, q_ref, e_ref, carry_ref, m_ref, l_ref, acc_ref) = refs[2 * pps:]
    del pt_ref
    j = pl.program_id(1)
    n_steps = pl.num_programs(1)
    rows = FOX_HEADS * n_new
    width = FOX_HEADS * HEAD_DIM
    srow = _iota((PAGE_SIZE, PAGE_SIZE), 0)
    scol = _iota((PAGE_SIZE, PAGE_SIZE), 1)
    later = jnp.where(srow > scol, 1.0, 0.0).astype(_BF)

    def suffix(lf):
        hi, mid, lo = _split3(lf)
        return _dot(hi, later) + _dot(mid, later) + _dot(lo, later)

    def decay(rt):
        return (jnp.concatenate([rt] * n_new, axis=0) - e_ref[...]) * LOG2E

    @pl.when(j == 0)
    def _():
        _softmax_init(m_ref, l_ref, acc_ref)
        rows8 = _iota((SUBLANES, width), 0)
        lane = _iota((SUBLANES, width), 1)
        q = qf_ref[0].astype(_F32)
        blocks = []
        for i in range(n_new):
            piece = jnp.broadcast_to(q[i:i + 1, :], (SUBLANES, width))
            blocks.append(jnp.where(_div(lane, HEAD_DIM) == rows8, piece, jnp.zeros_like(piece)))
        q_ref[...] = jnp.concatenate(blocks, axis=0).astype(_BF)
        lfn = lfn_ref[0]
        rt = suffix(lfn)
        e_ref[...] = jnp.concatenate([rt[:, i:i + 1] for i in range(n_new)], axis=0)
        carry_ref[...] = jnp.broadcast_to(jnp.sum(lfn, axis=-1, keepdims=True), carry_ref.shape)
        kvn = _pad_rows(kvn_ref[0], PAGE_SIZE)
        ri = _div(_iota((rows, PAGE_SIZE), 0), SUBLANES)
        ci = _iota((rows, PAGE_SIZE), 1)
        s = _dot_nt(q_ref[...], kvn[:, 0:width].astype(_BF)) + decay(rt) + jnp.where(ci <= ri, 0.0, NEG)
        v_new = kvn[:, width:2 * width].astype(_BF)
        _online_update(s, lambda p: _dot(p, v_new), m_ref, l_ref, acc_ref)

    kts, vts, rts = [], [], []
    carry = carry_ref[...]
    for k in range(pps):
        lf = lf_refs[k][0, 0]
        kts.append(kv_refs[k][0, 0, 0].astype(_BF))
        vts.append(kv_refs[k][0, 0, 1].astype(_BF))
        rts.append(suffix(lf) + carry)
        carry = carry + jnp.sum(lf, axis=-1, keepdims=True)
    carry_ref[...] = carry
    vt = jnp.concatenate(vts, axis=1)
    s = _dot(q_ref[...], jnp.concatenate(kts, axis=1)) + decay(jnp.concatenate(rts, axis=1))
    _online_update(s, lambda p: _dot_nt(p, vt), m_ref, l_ref, acc_ref)

    @pl.when(j == n_steps - 1)
    def _():
        o = acc_ref[...] / l_ref[...]
        rows8 = _iota((SUBLANES, width), 0)
        lane = _iota((SUBLANES, width), 1)
        out_rows = []
        for i in range(n_new):
            blk = jnp.where(_div(lane, HEAD_DIM) == rows8, o[i * SUBLANES:(i + 1) * SUBLANES], 0.0)
            out_rows.append(jnp.sum(blk, axis=0, keepdims=True))
        of_ref[0] = _row_select(rows8, out_rows)


def _fox_sample(layer, page_table, cache_t, cache_lft, qf, kvn, lfn, n_new, pps):
    n_seq, n_pages = page_table.shape
    n_steps = n_pages // pps
    rows = FOX_HEADS * n_new
    width = FOX_HEADS * HEAD_DIM

    def page_idx(s, j, pt, k):
        return pt[s, n_pages - 1 - (j * pps + k)]
    kv_spec = lambda k: pl.BlockSpec((1, 1, 2, width, PAGE_SIZE),
                                     lambda s, j, pt, k=k: (layer, page_idx(s, j, pt, k), 0, 0, 0))
    lf_spec = lambda k: pl.BlockSpec((1, 1, FOX_HEADS, PAGE_SIZE),
                                     lambda s, j, pt, k=k: (layer, page_idx(s, j, pt, k), 0, 0))
    per_seq = lambda a: pl.BlockSpec((1,) + a.shape[1:], lambda s, j, pt: (s,) + (0,) * (a.ndim - 1))
    grid_spec = pltpu.PrefetchScalarGridSpec(
        num_scalar_prefetch=1,
        grid=(n_seq, n_steps),
        in_specs=[kv_spec(k) for k in range(pps)] + [lf_spec(k) for k in range(pps)] + [per_seq(a) for a in (qf, kvn, lfn)],
        out_specs=pl.BlockSpec((1, SUBLANES, width), lambda s, j, pt: (s, 0, 0)),
        scratch_shapes=[pltpu.VMEM((rows, width), _BF), pltpu.VMEM((rows, 1), _F32), pltpu.VMEM((SUBLANES, LANES), _F32),
                        pltpu.VMEM((rows, 1), _F32), pltpu.VMEM((rows, 1), _F32), pltpu.VMEM((rows, width), _F32)],
    )
    return pl.pallas_call(
        functools.partial(_fox_sample_kernel, pps=pps, n_new=n_new),
        grid_spec=grid_spec,
        out_shape=jax.ShapeDtypeStruct((n_seq, SUBLANES, width), _F32),
        compiler_params=_cparams(("arbitrary", "arbitrary")),
        name="fox_sample",
    )(page_table, *([cache_t] * pps), *([cache_lft] * pps), qf, kvn, lfn)


def _nsa_perm():
    idx = np.zeros(NSA_HEADS * HEAD_DIM, np.int32)
    for r in range(NSA_GROUP):
        for g in range(NSA_KV_HEADS):
            for d in range(HEAD_DIM):
                idx[r * LANES + g * HEAD_DIM + d] = (g * NSA_GROUP + r) * HEAD_DIM + d
    return idx


def _layer_weights(l, w_in, b_gate, b_forget, cmp_pe, cmp_w1, cmp_w2, grp_norm_nsa, grp_norm_fox, w_o):
    perm = _nsa_perm()
    w = w_in[l]
    o_qn, o_nsa, o_kw, o_g, o_qf, o_kf, o_f = 0, 512, 1024, 1280, 1304, 1816, 2840
    f_cols = w[:, o_f:o_f + FOX_HEADS]
    misc = jnp.concatenate([w[:, o_g:o_g + 3 * NSA_HEADS], f_cols, f_cols,
                            jnp.zeros((w.shape[0], LANES - 3 * NSA_HEADS - 2 * FOX_HEADS), w.dtype)], axis=1)
    wp = jnp.concatenate([w[:, o_qn:o_nsa][:, perm], w[:, o_nsa:o_kw], w[:, o_kw:o_g], w[:, o_qf:o_kf],
                          w[:, o_kf:o_f], misc], axis=1).astype(_BF)
    bias = jnp.concatenate([b_gate[l].reshape(-1), b_forget[l], b_forget[l],
                            jnp.zeros((LANES - 3 * NSA_HEADS - 2 * FOX_HEADS,), _F32)]).reshape(1, LANES)

    def w1_layout(w1):
        w1 = w1.reshape(2, CMP_STRIDE, HEAD_DIM, CMP_HIDDEN)
        z = jnp.zeros((CMP_STRIDE, HEAD_DIM, CMP_HIDDEN), w1.dtype)
        g0 = jnp.concatenate([w1[0], w1[1], z, z], axis=-1)
        g1 = jnp.concatenate([z, z, w1[0], w1[1]], axis=-1)
        return jnp.concatenate([g0, g1], axis=1).reshape(CMP_STRIDE * LANES, 4 * CMP_HIDDEN).astype(_BF)

    def w2_layout(w2):
        z = jnp.zeros_like(w2)
        return jnp.concatenate([jnp.concatenate([w2, z], axis=1), jnp.concatenate([z, w2], axis=1)], axis=0).astype(_BF)
    cw = dict(
        w1k=w1_layout(cmp_w1[l, 0]), w1v=w1_layout(cmp_w1[l, 1]),
        pe8=jnp.broadcast_to(cmp_pe[l].reshape(2, 1, CMP_BLOCK * HEAD_DIM), (2, SUBLANES, CMP_BLOCK * HEAD_DIM)).astype(_BF),
        w1raw=cmp_w1[l].astype(_BF),
        w2p=jnp.stack([w2_layout(cmp_w2[l, 0]), w2_layout(cmp_w2[l, 1])]),
    )
    gn = grp_norm_nsa[l][perm].reshape(1, -1)
    gf = grp_norm_fox[l].reshape(1, -1)
    wo = jnp.concatenate([w_o[l][:NSA_HEADS * HEAD_DIM][perm], w_o[l][NSA_HEADS * HEAD_DIM:]], axis=0).astype(_BF)
    return wp, bias, cw, gn, gf, wo


def kernel(x_prompt, x_sample, cache_nsa_kv, cache_fox_kv, cache_fox_logf, state_win_kv, page_table, rel_bias,
           norm_mix_pre, norm_mix_post, norm_ffn_pre, norm_ffn_post, w_in, b_gate, b_forget, cmp_pe, cmp_w1, cmp_w2,
           grp_norm_nsa, grp_norm_fox, w_o, w_ffn_gate, w_ffn_up, w_ffn_down):
    depth = w_in.shape[0]
    b, t, d = x_prompt.shape
    n_seq, n_new, _ = x_sample.shape
    n_pages = page_table.shape[1]
    past = n_pages * PAGE_SIZE
    n_pool = cache_nsa_kv.shape[1]
    n_win = state_win_kv.shape[2]
    assert t % min(TQF, t) == 0 and min(TQF, t) % TK == 0 and t % TQ == 0 and TQ == TK == WINDOW and TQ % CB == 0
    assert CB % LANES == 0 and LANES > T5_FAR
    assert n_new <= SUBLANES and SLC_BLOCK >= n_new and n_win == WINDOW
    assert n_pages % (WINDOW // PAGE_SIZE) == 0 and past >= 2 * WINDOW
    assert (n_seq * n_new) % SUBLANES == 0

    nc_p = t // CMP_STRIDE
    ncp_p = -(-nc_p // LANES) * LANES
    n_slc_p = t // SLC_BLOCK
    assert n_slc_p <= LANES
    nc_s = past // CMP_STRIDE
    ncp_s = -(-nc_s // LANES) * LANES
    n_slc_s = past // SLC_BLOCK + 1
    assert n_slc_s <= 2 * LANES
    pps = math.gcd(n_pages, 32)
    pps_fox = math.gcd(n_pages, 32)

    dt, nbh, nbl = _bias_prompt(rel_bias)
    bc, bs, bw = _bias_sample(rel_bias, past, n_new, nc_s - 1, ncp_s)

    nsa_t = jnp.transpose(cache_nsa_kv, (0, 1, 3, 4, 5, 2)).reshape(depth, n_pool, 4, NSA_KV_HEADS * HEAD_DIM, PAGE_SIZE)
    fox_t = jnp.transpose(cache_fox_kv, (0, 1, 3, 4, 5, 2)).reshape(depth, n_pool, 2, FOX_HEADS * HEAD_DIM, PAGE_SIZE)
    logf_t = jnp.transpose(cache_fox_logf, (0, 1, 3, 2))
    win_t = jnp.transpose(state_win_kv, (0, 1, 3, 4, 5, 2)).reshape(depth, n_seq, 2, NSA_KV_HEADS * HEAD_DIM, n_win)

    row1 = lambda a: a.reshape(1, -1)
    pad_new = lambda a: jnp.pad(a, ((0, 0), (0, SUBLANES - n_new), (0, 0)))
    rs = n_seq * n_new
    xp, xs = x_prompt, x_sample.reshape(1, rs, d)
    outs_p, outs_s = [], []
    for l in range(depth):
        wp, bias, cw, gn, gf, wo = _layer_weights(l, w_in, b_gate, b_forget, cmp_pe, cmp_w1, cmp_w2,
                                                  grp_norm_nsa, grp_norm_fox, w_o)
        wg, wu, wd = w_ffn_gate[l].astype(_BF), w_ffn_up[l].astype(_BF), w_ffn_down[l].astype(_BF)
        g_pre, g_post = row1(norm_mix_pre[l]), row1(norm_mix_post[l])
        g_fpre, g_fpost = row1(norm_ffn_pre[l]), row1(norm_ffn_post[l])

        (_, nsa, nsab, kw, kwb, _, fox, foxb, misc, dec,
         qnt, qft, dect, misct, vst, vwt, vft) = _proj(xp, g_pre, wp, bias, TR_PROJ, True)
        kcmp, vcmpt = _compress_prompt(nsa, cw, nc_p, ncp_p)
        o_n = _nsa_prompt(qnt, misct, kcmp, vcmpt, nsab, vst, kwb, vwt, dt, nbh, nbl, nc_p - 1, ncp_p, n_slc_p)
        o_f = _fox_prompt(qft, dect, foxb, dec, vft)
        xp = _post(o_n, o_f, xp, gn, gf, wo, g_post, TR_POST)
        xp = _ffn(xp, g_fpre, wg, wu, wd, g_fpost, TR_FFN)
        outs_p.append((nsa.reshape(b, t, 4, NSA_KV_HEADS, HEAD_DIM), fox.reshape(b, t, 2, FOX_HEADS, HEAD_DIM),
                       misc[:, :, L_LOGF:L_LOGF + FOX_HEADS],
                       kw[:, t - WINDOW:].reshape(b, WINDOW, 2, NSA_KV_HEADS, HEAD_DIM)))

        tr_s = math.gcd(rs, TR_PROJ)
        qn, nsa, nsab, kw, kwb, qf, fox, foxb, misc = _proj(xs, g_pre, wp, bias, tr_s, False)
        per = lambda a: a.reshape(n_seq, n_new, a.shape[-1])
        nsa_s, fox_s, kw_s, misc_s = per(nsa), per(fox), per(kw), per(misc)
        logf_s = misc_s[:, :, L_LOGF:L_LOGF + FOX_HEADS]
        kw_st = jnp.pad(jnp.swapaxes(kw_s, 1, 2), ((0, 0), (0, 0), (0, LANES - n_new)))
        o_n, win_new = _nsa_sample(l, page_table, nsa_t, pad_new(per(qn)), pad_new(misc_s), pad_new(nsa_s), win_t,
                                   pad_new(kw_s), kw_st, cw, bc, bs, bw, past, n_new, nc_s - 1, ncp_s, n_slc_s, pps)
        kvn = pad_new(fox_s)
        lfn = jnp.pad(jnp.swapaxes(logf_s, 1, 2), ((0, 0), (0, 0), (0, PAGE_SIZE - n_new)))
        o_f = _fox_sample(l, page_table, fox_t, logf_t, pad_new(per(qf)), kvn, lfn, n_new, pps_fox)
        o_n = o_n[:, :n_new].reshape(1, rs, -1)
        o_f = o_f[:, :n_new].reshape(1, rs, -1)
        xs = _post(o_n, o_f, xs, gn, gf, wo, g_post, math.gcd(rs, TR_POST))
        xs = _ffn(xs, g_fpre, wg, wu, wd, g_fpost, math.gcd(rs, TR_FFN))
        win_new = jnp.transpose(win_new.reshape(n_seq, 2, NSA_KV_HEADS, HEAD_DIM, n_win), (0, 4, 1, 2, 3))
        outs_s.append((nsa_s.reshape(n_seq, n_new, 4, NSA_KV_HEADS, HEAD_DIM),
                       fox_s.reshape(n_seq, n_new, 2, FOX_HEADS, HEAD_DIM), logf_s, win_new))

    stack = lambda outs, i: jnp.stack([o[i] for o in outs], axis=0)
    return (xp, xs.reshape(n_seq, n_new, d), stack(outs_p, 0), stack(outs_p, 1), stack(outs_p, 2), stack(outs_p, 3),
            stack(outs_s, 0), stack(outs_s, 1), stack(outs_s, 2), stack(outs_s, 3))
```

```python
import functools
import math

import numpy as np
import jax
import jax.numpy as jnp
from jax import lax
from jax.experimental import pallas as pl
from jax.experimental.pallas import tpu as pltpu

HEAD_DIM = 64
NSA_HEADS = 8
FOX_HEADS = 8
NSA_KV_HEADS = 2
NSA_GROUP = NSA_HEADS // NSA_KV_HEADS
CMP_BLOCK = 32
CMP_STRIDE = 16
CMP_HIDDEN = 2 * HEAD_DIM
SLC_BLOCK = 64
N_SELECT = 16
WINDOW = 512
T5_BUCKETS = 32
T5_EXACT = T5_BUCKETS // 2
T5_MAX_DIST = 128
PAGE_SIZE = 128
NORM_EPS = 1e-6
FORCE_SCORE = 1e4
LOG2E = math.log2(math.e)
Q_SCALE = HEAD_DIM ** -0.5 * LOG2E

LANES = 128
SUBLANES = 8
VMEM_LIMIT = 56 * 1024 * 1024

_F32 = jnp.float32
_BF = jnp.bfloat16
NEG = -(2.0 ** 100)
NEG_HALF = -(2.0 ** 99)
M_INIT = -3.0e38
REMOVED = -3.4e38


def _t5_thresholds():
    n = np.arange(1, 4 * T5_MAX_DIST)
    large = T5_EXACT + (np.log(n / T5_EXACT) / math.log(T5_MAX_DIST / T5_EXACT) * (T5_BUCKETS - T5_EXACT)).astype(np.int64)
    return tuple(int(n[np.argmax(large >= k)]) for k in range(T5_EXACT + 1, T5_BUCKETS))


_T5_THR = _t5_thresholds()
T5_FAR = _T5_THR[-1]


def _log2(n):
    assert n & (n - 1) == 0
    return n.bit_length() - 1


TQ = 512
TQF = 4096
TK = 512
CB = 256
AHEAD = 3
MV = HEAD_DIM + SUBLANES
TR_PROJ = 256
TR_POST = 512
TR_FFN = 256
CPT = TQ // CMP_STRIDE
NEAR_BACK = -(-(T5_FAR + CMP_BLOCK - 1) // CMP_STRIDE) - 1
NEAR_U = CPT + NEAR_BACK

C_QN, C_NSA, C_KW, C_QF, C_FOX, C_MISC = 0, 512, 1024, 1280, 1792, 2816
C_TOT = 2944
L_GATE, L_LOGF, L_CUM = 0, 24, 32
N_PIECE = 3
L_NEG, L_POS = 0, N_PIECE * FOX_HEADS


def _dot(a, b):
    return jnp.dot(a, b, preferred_element_type=_F32)


def _dot_nt(a, b):
    return lax.dot_general(a, b, (((1,), (1,)), ((), ())), preferred_element_type=_F32)


def _split3(x):
    hi = x.astype(_BF)
    r1 = x - hi.astype(_F32)
    mid = r1.astype(_BF)
    lo = (r1 - mid.astype(_F32)).astype(_BF)
    return hi, mid, lo


def _split2(x):
    hi = x.astype(_BF)
    return hi, (x - hi.astype(_F32)).astype(_BF)


def _dot3(a_bf, x):
    hi, mid, lo = _split3(x)
    return _dot(a_bf, hi) + _dot(a_bf, mid) + _dot(a_bf, lo)


def _rms(x, g):
    ms = jnp.mean(x * x, axis=-1, keepdims=True)
    return x * lax.rsqrt(ms + NORM_EPS) * g


def _gelu_tanh(x):
    c = math.sqrt(2.0 / math.pi)
    return x * (0.5 * (1.0 + jnp.tanh(c * (x + 0.044715 * (x * x * x)))))


def _iota(shape, dim):
    return lax.broadcasted_iota(jnp.int32, shape, dim)


def _div(x, n):
    return jnp.right_shift(x, _log2(n))


def _mod(x, n):
    return x & (n - 1)


def _cparams(sem):
    return pltpu.CompilerParams(dimension_semantics=sem, vmem_limit_bytes=VMEM_LIMIT)


def _t5_rel(dist, rel_ref, h):
    d = jnp.minimum(dist, T5_MAX_DIST - 1)
    big = jnp.full(d.shape, T5_EXACT, jnp.int32)
    for thr in _T5_THR:
        big = big + jnp.where(d >= thr, 1, 0)
    bkt = jnp.where(d < T5_EXACT, d, big)
    far = rel_ref[T5_BUCKETS - 1, h]
    val = jnp.zeros(d.shape, _F32)
    for k in range(T5_BUCKETS - 1):
        val = jnp.where(bkt == k, (rel_ref[k, h] - far) * LOG2E, val)
    return val


def _t5_masked(dist, rel_ref, h):
    return jnp.where(dist < 0, NEG, _t5_rel(jnp.maximum(dist, 0), rel_ref, h))


def _bias_prompt_kernel(rel_ref, dt_ref, nbh_ref, nbl_ref):
    def body(h, carry):
        c = _iota((LANES, LANES), 0)
        i = _iota((LANES, LANES), 1)
        dt_ref[h, 0] = _t5_masked(i - c, rel_ref, h)
        dt_ref[h, 1] = _t5_masked(i - c + LANES, rel_ref, h)
        u = _iota((LANES, TQ), 0)
        i = _iota((LANES, TQ), 1)
        near = _t5_masked(i + (CMP_STRIDE * NEAR_BACK - (CMP_BLOCK - 1)) - CMP_STRIDE * u, rel_ref, h)
        nb = jnp.where(u < NEAR_U, near, jnp.where(u == NEAR_U, NEG, 0.0))
        hi, lo = _split2(nb)
        cols = pl.ds(pl.multiple_of(h * TQ, TQ), TQ)
        nbh_ref[:, cols] = hi
        nbl_ref[:, cols] = lo
        return carry
    lax.fori_loop(0, NSA_HEADS, body, 0)


def _bias_prompt(rel_bias):
    return pl.pallas_call(
        _bias_prompt_kernel,
        out_shape=(jax.ShapeDtypeStruct((NSA_HEADS, 2, LANES, LANES), _F32),
                   jax.ShapeDtypeStruct((LANES, NSA_HEADS * TQ), _BF),
                   jax.ShapeDtypeStruct((LANES, NSA_HEADS * TQ), _BF)),
        in_specs=[pl.BlockSpec(memory_space=pltpu.SMEM)],
        name="t5_bias_prompt",
    )(rel_bias)


def _bias_sample_kernel(rel_ref, bc_ref, bs_ref, bw_ref, *, past, n_new, n_cmp):
    def table(shape, dist_fn, extra_invalid=None):
        out = jnp.zeros(shape, _F32)
        r = _iota(shape, 0)
        c = _iota(shape, 1)
        i = _div(r, NSA_HEADS)
        dist = dist_fn(i, c)
        for h in range(NSA_HEADS):
            v = _t5_masked(dist, rel_ref, h)
            out = jnp.where(_mod(r, NSA_HEADS) == h, v, out)
        if extra_invalid is not None:
            out = jnp.where(extra_invalid(i, c, dist), NEG, out)
        return out

    bc_ref[...] = table(bc_ref.shape, lambda i, c: past + i - (CMP_STRIDE * c + CMP_BLOCK - 1),
                        lambda i, c, d: c >= n_cmp)
    bs_ref[...] = table(bs_ref.shape, lambda i, c: i + WINDOW - c)
    bw_ref[...] = table(bw_ref.shape, lambda i, c: i + WINDOW - c,
                        lambda i, c, d: (d >= WINDOW) | (c >= WINDOW + n_new))


def _bias_sample(rel_bias, past, n_new, n_cmp, ncp):
    rows = NSA_HEADS * n_new
    return pl.pallas_call(
        functools.partial(_bias_sample_kernel, past=past, n_new=n_new, n_cmp=n_cmp),
        out_shape=(jax.ShapeDtypeStruct((rows, ncp), _F32),
                   jax.ShapeDtypeStruct((rows, WINDOW + PAGE_SIZE), _F32),
                   jax.ShapeDtypeStruct((rows, WINDOW + PAGE_SIZE), _F32)),
        in_specs=[pl.BlockSpec(memory_space=pltpu.SMEM)],
        name="t5_bias_sample",
    )(rel_bias)


def _proj_kernel(x_ref, g_ref, w_ref, b_ref, *refs, tr, prompt):
    if prompt:
        (qn_ref, nsa_ref, nsab_ref, kw_ref, kwb_ref, qf_ref, fox_ref, foxb_ref, misc_ref, dec_ref,
         qnt_ref, qft_ref, dect_ref, misct_ref, vst_ref, vwt_ref, vft_ref, carry_ref) = refs
    else:
        qn_ref, nsa_ref, nsab_ref, kw_ref, kwb_ref, qf_ref, fox_ref, foxb_ref, misc_ref = refs
    x = x_ref[0]
    h = _rms(x, g_ref[...]).astype(_BF)
    qn = _dot(h, w_ref[:, C_QN:C_NSA]) * Q_SCALE
    qn_ref[0] = qn.astype(_BF)
    z_nsa = _dot(h, w_ref[:, C_NSA:C_KW])
    nsa_ref[0] = z_nsa
    nsab_ref[0] = z_nsa.astype(_BF)
    z_kw = _dot(h, w_ref[:, C_KW:C_QF])
    kw_ref[0] = z_kw
    kwb_ref[0] = z_kw.astype(_BF)
    qf = _dot(h, w_ref[:, C_QF:C_FOX]) * Q_SCALE
    qf_ref[0] = qf.astype(_BF)
    z_fox = _dot(h, w_ref[:, C_FOX:C_MISC])
    fox_ref[0] = z_fox
    foxb_ref[0] = z_fox.astype(_BF)
    zm = _dot(h, w_ref[:, C_MISC:C_TOT]) + b_ref[...]
    lane = _iota((tr, LANES), 1)
    sg = jax.nn.sigmoid(zm)
    ls = jnp.minimum(zm, 0.0) - jnp.log1p(jnp.exp(-jnp.abs(zm)))
    if not prompt:
        misc_ref[0] = jnp.where(lane < L_LOGF, sg, jnp.where(lane < L_CUM, ls, 0.0))
        return

    @pl.when(pl.program_id(1) == 0)
    def _():
        carry_ref[...] = jnp.zeros(carry_ref.shape, _F32)
    row = _iota((tr, tr), 0)
    col = _iota((tr, tr), 1)
    tri = jnp.where(col <= row, 1.0, 0.0).astype(_BF)
    cs = _dot3(tri, ls) + carry_ref[0:1, :]
    carry_ref[...] = jnp.broadcast_to(cs[tr - 1:tr, :], carry_ref.shape)
    pieces = _split3(cs * LOG2E)
    r = _iota((LANES, LANES), 0) - L_CUM
    c = _iota((LANES, LANES), 1)
    head_row = (r >= 0) & (r < FOX_HEADS)
    dec = jnp.zeros((tr, LANES), _F32)
    for j, piece in enumerate(pieces):
        put = jnp.where(head_row & (c == L_NEG + N_PIECE * r + j), -1.0,
                        jnp.where(head_row & (c == L_POS + N_PIECE * r + j), 1.0, 0.0)).astype(_BF)
        dec = dec + _dot(piece, put)
    dec_ref[0] = dec.astype(_BF)
    misc = jnp.where(lane < L_LOGF, sg, jnp.where(lane < L_CUM, ls, jnp.where(lane < L_CUM + FOX_HEADS, cs, 0.0)))
    misc_ref[0] = misc
    qnt_ref[0] = qn.T.astype(_BF)
    qft_ref[0] = qf.T.astype(_BF)
    dect_ref[0] = dec.T.astype(_BF)
    misct_ref[0] = misc.T
    vst_ref[0, 0] = z_nsa[:, 3 * LANES:4 * LANES].T.astype(_BF)
    vwt_ref[0, 0] = z_kw[:, LANES:2 * LANES].T.astype(_BF)
    vft_ref[0, 0] = z_fox[:, FOX_HEADS * HEAD_DIM:].T.astype(_BF)


def _proj(x, g, w, b, tr, prompt):
    nb, t, d = x.shape
    grid = (nb, t // tr)
    row = lambda width: pl.BlockSpec((1, tr, width), lambda bi, i: (bi, i, 0))
    const = lambda shape: pl.BlockSpec(shape, lambda bi, i: (0,) * len(shape))
    shp = lambda width, dt: jax.ShapeDtypeStruct((nb, t, width), dt)
    out_specs = [row(512), row(512), row(512), row(256), row(256), row(512), row(1024), row(1024), row(LANES)]
    out_shape = [shp(512, _BF), shp(512, _F32), shp(512, _BF), shp(256, _F32), shp(256, _BF), shp(512, _BF),
                 shp(1024, _F32), shp(1024, _BF), shp(LANES, _F32)]
    scratch = []
    if prompt:
        assert TK % tr == 0 and t % TK == 0
        per = TK // tr
        colm = lambda rows: pl.BlockSpec((1, rows, tr), lambda bi, i: (bi, 0, i))
        tile = lambda rows: pl.BlockSpec((1, 1, rows, tr), lambda bi, i: (bi, i // per, 0, i % per))
        tshp = lambda rows, dt: jax.ShapeDtypeStruct((nb, rows, t), dt)
        t4 = lambda rows: jax.ShapeDtypeStruct((nb, t // TK, rows, TK), _BF)
        out_specs += [row(LANES), colm(512), colm(512), colm(LANES), colm(LANES), tile(LANES), tile(LANES), tile(512)]
        out_shape += [shp(LANES, _BF), tshp(512, _BF), tshp(512, _BF), tshp(LANES, _BF), tshp(LANES, _F32),
                      t4(LANES), t4(LANES), t4(512)]
        scratch = [pltpu.VMEM((SUBLANES, LANES), _F32)]
    return pl.pallas_call(
        functools.partial(_proj_kernel, tr=tr, prompt=prompt),
        grid=grid,
        in_specs=[row(d), const((1, d)), const((d, C_TOT)), const((1, LANES))],
        out_specs=tuple(out_specs),
        out_shape=tuple(out_shape),
        scratch_shapes=scratch,
        compiler_params=_cparams(("arbitrary", "arbitrary")),
        name="in_proj",
    )(x, g, w, b)


def _compress(lhs_bf, w1p, pe8, w1raw, w2p, n_valid):
    nc = lhs_bf.shape[0]
    hcat = _dot(lhs_bf, w1p)
    cst = _dot(pe8, w1raw)[0:1]

    def hidden(g):
        a = hcat[:, g * 256:g * 256 + CMP_HIDDEN]
        b = hcat[:, g * 256 + CMP_HIDDEN:(g + 1) * 256]
        return _gelu_tanh(a + pltpu.roll(b, nc - 1, 0) + cst)

    hh = jnp.concatenate([hidden(0), hidden(1)], axis=1).astype(_BF)
    out = _dot(hh, w2p)
    return jnp.where(_iota(out.shape, 0) < n_valid, out, 0.0)


def _compress_prompt_kernel(xk_ref, xv_ref, w1k_ref, w1v_ref, pe_ref, w1r_ref, w2_ref, kc_ref, vc_ref, *, nc, ncp):
    for idx, (x_ref, w1, out_ref) in enumerate(((xk_ref, w1k_ref, kc_ref), (xv_ref, w1v_ref, vc_ref))):
        pieces = [x_ref[0, pl.ds(p, nc, stride=CMP_STRIDE), :] for p in range(CMP_STRIDE)]
        lhs = jnp.concatenate(pieces, axis=1).astype(_BF)
        out = _compress(lhs, w1[...], pe_ref[idx], w1r_ref[idx], w2_ref[idx], nc - 1)
        if ncp > nc:
            out = jnp.concatenate([out, jnp.zeros((ncp - nc, LANES), _F32)], axis=0)
        out_ref[0] = (out.T if idx == 1 else out).astype(_BF)


def _compress_prompt(nsa_state, cw, nc, ncp):
    b, t, _ = nsa_state.shape
    const = lambda a: pl.BlockSpec(a.shape, lambda bi: (0,) * a.ndim)
    return pl.pallas_call(
        functools.partial(_compress_prompt_kernel, nc=nc, ncp=ncp),
        grid=(b,),
        in_specs=[pl.BlockSpec((1, t, LANES), lambda bi: (bi, 0, 0)), pl.BlockSpec((1, t, LANES), lambda bi: (bi, 0, 1)),
                  const(cw["w1k"]), const(cw["w1v"]), const(cw["pe8"]), const(cw["w1raw"]), const(cw["w2p"])],
        out_specs=(pl.BlockSpec((1, ncp, LANES), lambda bi: (bi, 0, 0)), pl.BlockSpec((1, LANES, ncp), lambda bi: (bi, 0, 0))),
        out_shape=(jax.ShapeDtypeStruct((b, ncp, LANES), _BF), jax.ShapeDtypeStruct((b, LANES, ncp), _BF)),
        compiler_params=_cparams(("arbitrary",)),
        name="compress_prompt",
    )(nsa_state, nsa_state, cw["w1k"], cw["w1v"], cw["pe8"], cw["w1raw"], cw["w2p"])


def _attend_tile_t(lhst_ref, k_t, v_of, m_ref, acc_ref, blocks, add_fn=None, feat=slice(None), keys_of=None):
    n = len(blocks)
    cols = lambda b: slice(b * CB, (b + 1) * CB)
    keys = (lambda b: slice(None)) if keys_of is None else keys_of
    score = lambda b: _dot(k_t[keys(b), :], lhst_ref[feat, cols(b)])
    scores = [score(b) for b in blocks[:AHEAD]]
    for i, b in enumerate(blocks):
        if i + AHEAD < n:
            scores.append(score(blocks[i + AHEAD]))
        s = scores[i]
        scores[i] = None
        if add_fn is not None:
            s = add_fn(b, s)
        m_old = m_ref[0:1, cols(b)]
        m_new = jnp.maximum(m_old, jnp.max(s, axis=0, keepdims=True))
        alpha = jnp.exp2(m_old - m_new)
        p = jnp.exp2(s - m_new).astype(_BF)
        acc_ref[:, cols(b)] = alpha * acc_ref[:, cols(b)] + _dot(v_of(b)[:, keys(b)], p)
        m_ref[0:1, cols(b)] = m_new


def _attend_init_t(m_ref, acc_ref):
    m_ref[...] = jnp.full(m_ref.shape, M_INIT, _F32)
    acc_ref[...] = jnp.zeros(acc_ref.shape, _F32)


def _edit_blocks(s, fn, a0=0):
    rows = []
    for a in range(s.shape[0] // LANES):
        pieces = []
        for q in range(s.shape[1] // LANES):
            piece = s[a * LANES:(a + 1) * LANES, q * LANES:(q + 1) * LANES]
            new = fn(a0 + a, q, piece)
            pieces.append(piece if new is None else new)
        rows.append(jnp.concatenate(pieces, axis=1))
    return jnp.concatenate(rows, axis=0)


def _head_values(v_t):
    ones = jnp.ones((MV - HEAD_DIM, v_t.shape[1]), _BF)
    return [jnp.concatenate([v_t[a * HEAD_DIM:(a + 1) * HEAD_DIM], ones], axis=0) for a in range(2)]


def _softmax_init(m_ref, l_ref, acc_ref):
    m_ref[...] = jnp.full(m_ref.shape, M_INIT, _F32)
    l_ref[...] = jnp.zeros(l_ref.shape, _F32)
    acc_ref[...] = jnp.zeros(acc_ref.shape, _F32)


def _online_update(s, pv_fn, m_ref, l_ref, acc_ref):
    m_old = m_ref[...]
    m_new = jnp.maximum(m_old, jnp.max(s, axis=-1, keepdims=True))
    alpha = jnp.exp2(m_old - m_new)
    p = jnp.exp2(s - m_new)
    l_ref[...] = alpha * l_ref[...] + jnp.sum(p, axis=-1, keepdims=True)
    acc_ref[...] = alpha * acc_ref[...] + pv_fn(p.astype(_BF))
    m_ref[...] = m_new


def _masked_softmax_full(s):
    m = jnp.max(s, axis=-1, keepdims=True)
    p = jnp.exp2(s - m)
    l = jnp.sum(p, axis=-1, keepdims=True)
    return jnp.where(m > NEG_HALF, p / l, 0.0)


def _overlap(ncp, n_cmp, n_blk, blk_axis=1):
    shape = (ncp, n_blk) if blk_axis == 1 else (n_blk, ncp)
    c = _iota(shape, 1 - blk_axis)
    j = _iota(shape, blk_axis)
    r = SLC_BLOCK // CMP_STRIDE
    hit = (c >= r * j - (CMP_BLOCK // CMP_STRIDE - 1)) & (c <= r * j + r - 1) & (c < n_cmp)
    return jnp.where(hit, 1.0, 0.0).astype(_BF)


def _select_blocks(imp, t_pos, n_slc, n_top, blk_axis=1):
    j = _iota(imp.shape, blk_axis)
    cur = _div(t_pos, SLC_BLOCK)
    forced = (j == 0) | (j == cur) | (j == cur - 1)
    score = jnp.where(forced, FORCE_SCORE, imp)
    score = jnp.where(j * SLC_BLOCK > t_pos, -FORCE_SCORE, score)
    score = jnp.where(j >= n_slc, M_INIT, score)
    sel = jnp.zeros(imp.shape, jnp.bool_)
    for _ in range(n_top):
        mx = jnp.max(score, axis=blk_axis, keepdims=True)
        idx = jnp.min(jnp.where(score == mx, j, 1 << 20), axis=blk_axis, keepdims=True)
        hit = j == idx
        sel = sel | hit
        score = jnp.where(hit, REMOVED, score)
    return jnp.where(sel, 0.0, NEG)


def _block_onehot(k0, tk, n_lanes):
    s = _iota((tk, n_lanes), 0)
    j = _iota((tk, n_lanes), 1)
    return jnp.where(j == _div(k0 + s, SLC_BLOCK), 1.0, 0.0).astype(_BF)


def _nsa_prompt_kernel(qnt_ref, misct_ref, kc_ref, vct_ref, ks_ref, vst_ref, kw_ref, vwt_ref, dt_ref, nbh_ref, nbl_ref,
                       on_ref, lhst_ref, oct_ref, imp_ref, ms_ref, as_ref, mw_ref, aw_ref, *, n_cmp, ncp, n_slc):
    qt = pl.program_id(1)
    q0 = qt * TQ
    n_col = NSA_HEADS * TQ
    n_cb = TQ // CB
    blocks = list(range(n_col // CB))
    cols = lambda b: slice(b * CB, (b + 1) * CB)
    head_cols = lambda h: slice(h * TQ, (h + 1) * TQ)
    low = _iota((LANES, TQ), 0) < HEAD_DIM

    for g in range(NSA_KV_HEADS):
        for r in range(NSA_GROUP):
            blk = qnt_ref[0, r * LANES:(r + 1) * LANES, :]
            lhst_ref[0:LANES, head_cols(g * NSA_GROUP + r)] = jnp.where(low if g == 0 else jnp.logical_not(low),
                                                                       blk, jnp.zeros_like(blk))

    c = _iota((ncp, LANES), 0)
    u = _iota((ncp, LANES), 1)
    place = ((u < NEAR_U) & (c == CPT * qt - NEAR_BACK + u)) | ((u == NEAR_U) & (c >= CPT * qt + CPT))
    place = jnp.where(place, 1.0, 0.0).astype(_BF)
    kc = kc_ref[0]
    vct = vct_ref[0]
    ovl_t = _overlap(ncp, n_cmp, LANES, blk_axis=0)

    kc_aug = jnp.concatenate([kc, place, place], axis=1)

    def cmp_branch(n_keys):
        def cmp_scores(b):
            rhs = jnp.concatenate([lhst_ref[0:LANES, cols(b)], nbh_ref[:, cols(b)], nbl_ref[:, cols(b)]], axis=0)
            return _dot(kc_aug[0:n_keys], rhs)

        scores = [cmp_scores(b) for b in blocks[:AHEAD]]
        group_mass = {}
        for b in blocks:
            if b + AHEAD < len(blocks):
                scores.append(cmp_scores(b + AHEAD))
            s = scores[b]
            scores[b] = None
            m = jnp.max(s, axis=0, keepdims=True)
            pb = jnp.exp2(s - m)
            rcp = jnp.where(m > NEG_HALF, 1.0 / jnp.sum(pb, axis=0, keepdims=True), 0.0)
            pb = pb.astype(_BF)
            oct_ref[:, cols(b)] = _dot(vct[:, 0:n_keys], pb) * rcp
            mass = _dot(ovl_t[:, 0:n_keys], pb) * rcp
            head, part = divmod(b, n_cb)
            g, r = divmod(head, NSA_GROUP)
            group_mass[(g, part)] = mass if r == 0 else group_mass[(g, part)] + mass
            if r == NSA_GROUP - 1:
                imp_ref[:, g * TQ + part * CB:g * TQ + (part + 1) * CB] = group_mass.pop((g, part))

    n_half = (ncp // 2) // LANES * LANES
    if n_half >= CPT:
        pl.when(CPT * (qt + 1) <= n_half)(functools.partial(cmp_branch, n_half))
        pl.when(CPT * (qt + 1) > n_half)(functools.partial(cmp_branch, ncp))
    else:
        cmp_branch(ncp)

    t_pos = q0 + _mod(_iota((1, NSA_KV_HEADS * TQ), 1), TQ)
    msel = _select_blocks(imp_ref[...], t_pos, n_slc, min(N_SELECT, n_slc), blk_axis=0).astype(_BF)
    for g in range(NSA_KV_HEADS):
        for r in range(NSA_GROUP):
            lhst_ref[LANES:2 * LANES, head_cols(g * NSA_GROUP + r)] = msel[:, g * TQ:(g + 1) * TQ]

    c2 = _iota((LANES, LANES), 0)
    i2 = _iota((LANES, LANES), 1)
    n_kb = TK // LANES

    def near_keys(kind):
        def key_blocks(b):
            part = b % n_cb
            q_lo, q_hi = part * (CB // LANES), (part + 1) * (CB // LANES) - 1
            if kind == 'diag':
                return 0, min(n_kb, q_hi + 1)
            if kind == 'wprev':
                return q_lo, n_kb
            return 0, n_kb
        return key_blocks

    def near_add(kind):
        def add(b, s):
            head, part = divmod(b, n_cb)
            a0 = near_keys(kind)(b)[0]

            def piece_fn(a, q, x):
                rel = part * (CB // LANES) + q - a + (0 if kind == 'diag' else n_kb)
                if kind == 'diag' and rel < 0:
                    return jnp.full(x.shape, NEG, _F32)
                if kind == 'wprev' and rel > n_kb:
                    return jnp.full(x.shape, NEG, _F32)
                if kind == 'wprev' and rel == n_kb:
                    return jnp.where(c2 > i2, x, NEG)
                if rel == 0:
                    return x + dt_ref[head, 0]
                if rel == 1:
                    return x + dt_ref[head, 1]
                return None
            return _edit_blocks(s, piece_fn, a0)
        return add

    def key_slice(kind):
        def keys_of(b):
            lo, hi = near_keys(kind)(b)
            return slice(lo * LANES, hi * LANES)
        return keys_of

    _attend_init_t(ms_ref, as_ref)

    def sel_tile(kt, kind):
        k0 = pl.multiple_of(kt * TK, TK)
        kaug = jnp.concatenate([ks_ref[0, pl.ds(k0, TK), :], _block_onehot(k0, TK, LANES)], axis=1)
        v_g = _head_values(vst_ref[0, kt])
        _attend_tile_t(lhst_ref, kaug, lambda b: v_g[b // (NSA_GROUP * n_cb)], ms_ref, as_ref, blocks,
                       None if kind is None else near_add(kind), keys_of=None if kind is None else key_slice(kind))

    n_far = jnp.maximum(qt - 1, 0)

    def far_pair(i, carry):
        sel_tile(2 * i, None)
        sel_tile(2 * i + 1, None)
        return carry
    lax.fori_loop(0, n_far // 2, far_pair, 0)

    @pl.when(n_far % 2 == 1)
    def _():
        sel_tile(n_far - 1, None)

    @pl.when(qt >= 1)
    def _():
        sel_tile(qt - 1, 'prev')
    sel_tile(qt, 'diag')

    _attend_init_t(mw_ref, aw_ref)
    qfeat = slice(0, LANES)

    def win_tile(kt, kind):
        k0 = pl.multiple_of(kt * TK, TK)
        v_g = _head_values(vwt_ref[0, kt])
        _attend_tile_t(lhst_ref, kw_ref[0, pl.ds(k0, TK), :], lambda b: v_g[b // (NSA_GROUP * n_cb)], mw_ref, aw_ref,
                       blocks, near_add(kind), feat=qfeat, keys_of=key_slice(kind))

    @pl.when(qt >= 1)
    def _():
        win_tile(qt - 1, 'wprev')
    win_tile(qt, 'diag')

    gates = misct_ref[0]
    for r in range(NSA_GROUP):
        parts = []
        for g in range(NSA_KV_HEADS):
            h = g * NSA_GROUP + r
            hc = head_cols(h)
            feat = slice(g * HEAD_DIM, (g + 1) * HEAD_DIM)
            own = slice(0, HEAD_DIM)
            den = slice(HEAD_DIM, HEAD_DIM + 1)
            gate = lambda kind: gates[L_GATE + kind * NSA_HEADS + h:L_GATE + kind * NSA_HEADS + h + 1, :]
            parts.append(gate(0) * oct_ref[feat, hc] + gate(1) * (as_ref[own, hc] / as_ref[den, hc])
                         + gate(2) * (aw_ref[own, hc] / aw_ref[den, hc]))
        on_ref[0, :, r * LANES:(r + 1) * LANES] = jnp.concatenate(parts, axis=0).T


def _nsa_prompt(qnt, misct, kcmp, vcmpt, nsab, vst, kwb, vwt, dt, nbh, nbl, n_cmp, ncp, n_slc):
    b, _, t = qnt.shape
    n_col = NSA_HEADS * TQ
    seq = lambda blk: pl.BlockSpec((1, t, LANES), lambda bi, qi, blk=blk: (bi, 0, blk))
    tiles = pl.BlockSpec((1, t // TK, LANES, TK), lambda bi, qi: (bi, 0, 0, 0))
    const = lambda a: pl.BlockSpec(a.shape, lambda bi, qi: (0,) * a.ndim)
    return pl.pallas_call(
        functools.partial(_nsa_prompt_kernel, n_cmp=n_cmp, ncp=ncp, n_slc=n_slc),
        grid=(b, t // TQ),
        in_specs=[pl.BlockSpec((1, 512, TQ), lambda bi, qi: (bi, 0, qi)),
                  pl.BlockSpec((1, LANES, TQ), lambda bi, qi: (bi, 0, qi)),
                  pl.BlockSpec((1, ncp, LANES), lambda bi, qi: (bi, 0, 0)),
                  pl.BlockSpec((1, LANES, ncp), lambda bi, qi: (bi, 0, 0)),
                  seq(2), tiles, seq(0), tiles, const(dt), const(nbh), const(nbl)],
        out_specs=pl.BlockSpec((1, TQ, 512), lambda bi, qi: (bi, qi, 0)),
        out_shape=jax.ShapeDtypeStruct((b, t, 512), _F32),
        scratch_shapes=[pltpu.VMEM((2 * LANES, n_col), _BF), pltpu.VMEM((LANES, n_col), _F32),
                        pltpu.VMEM((LANES, NSA_KV_HEADS * TQ), _F32),
                        pltpu.VMEM((SUBLANES, n_col), _F32), pltpu.VMEM((MV, n_col), _F32),
                        pltpu.VMEM((SUBLANES, n_col), _F32), pltpu.VMEM((MV, n_col), _F32)],
        compiler_params=_cparams(("arbitrary", "arbitrary")),
        name="nsa_prompt",
    )(qnt, misct, kcmp, vcmpt, nsab, vst, kwb, vwt, dt, nbh, nbl)


def _fox_prompt_kernel(qft_ref, dqt_ref, kf_ref, dk_ref, vft_ref, of_ref, lhst_ref, m_ref, acc_ref, *, tqf):
    p = pl.program_id(1)
    qt = pl.program_id(2)
    row = _iota((LANES, tqf), 0)
    low = row < HEAD_DIM
    qt_ = qft_ref[0]
    dq = dqt_ref[0]
    zero = jnp.zeros_like(qt_)
    one = jnp.ones_like(qt_)
    for a in range(2):
        head = 2 * p + a
        take_neg = (row >= L_NEG + N_PIECE * head) & (row < L_NEG + N_PIECE * (head + 1))
        take_pos = (row >= L_POS + N_PIECE * head) & (row < L_POS + N_PIECE * (head + 1))
        lhst_ref[0:LANES, a * tqf:(a + 1) * tqf] = jnp.where(low if a == 0 else jnp.logical_not(low), qt_, zero)
        lhst_ref[LANES:2 * LANES, a * tqf:(a + 1) * tqf] = jnp.where(take_neg, one, jnp.where(take_pos, dq, zero))
    _attend_init_t(m_ref, acc_ref)
    n_blk = 2 * tqf // CB
    per_q = tqf // TK
    lane = _iota((TK, LANES), 1)
    ones_pos = (jnp.clip(lane - (L_POS - 1), 0, 1) * jnp.clip(2 * L_POS - lane, 0, 1)).astype(_F32).astype(_BF)

    def tile(kt, blocks, add_fn):
        k0 = pl.multiple_of(kt * TK, TK)
        kaug = jnp.concatenate([kf_ref[0, pl.ds(k0, TK), :], jnp.where(lane < L_POS, dk_ref[0, pl.ds(k0, TK), :], ones_pos)],
                               axis=1)
        v_a = _head_values(vft_ref[0, kt])
        _attend_tile_t(lhst_ref, kaug, lambda b: v_a[b * CB // tqf], m_ref, acc_ref, blocks, add_fn)

    def far_body(kt, carry):
        tile(kt, list(range(n_blk)), None)
        return carry
    lax.fori_loop(0, qt * per_q, far_body, 0)

    for j in range(per_q):
        q_lo = lambda b: (b * CB) % tqf
        blocks = [b for b in range(n_blk) if q_lo(b) + CB > j * TK]

        def causal(b, s, j=j):
            if j * TK + TK - 1 <= q_lo(b):
                return s
            kk = j * TK + _iota((TK, CB), 0)
            qq = q_lo(b) + _iota((TK, CB), 1)
            return jnp.where(kk <= qq, s, NEG)
        tile(qt * per_q + j, blocks, causal)

    acc = acc_ref[...]
    o = [acc[0:HEAD_DIM, a * tqf:(a + 1) * tqf] / acc[HEAD_DIM:HEAD_DIM + 1, a * tqf:(a + 1) * tqf] for a in range(2)]
    of_ref[0] = jnp.concatenate(o, axis=0).T


def _fox_prompt(qft, dect, foxb, dec, vft):
    b, _, t = qft.shape
    tqf = min(TQF, t)
    n_pair = FOX_HEADS // 2
    return pl.pallas_call(
        functools.partial(_fox_prompt_kernel, tqf=tqf),
        grid=(b, n_pair, t // tqf),
        in_specs=[pl.BlockSpec((1, LANES, tqf), lambda bi, p, qi: (bi, p, qi)),
                  pl.BlockSpec((1, LANES, tqf), lambda bi, p, qi: (bi, 0, qi)),
                  pl.BlockSpec((1, t, LANES), lambda bi, p, qi: (bi, 0, p)),
                  pl.BlockSpec((1, t, LANES), lambda bi, p, qi: (bi, 0, 0)),
                  pl.BlockSpec((1, t // TK, LANES, TK), lambda bi, p, qi: (bi, 0, p, 0))],
        out_specs=pl.BlockSpec((1, tqf, LANES), lambda bi, p, qi: (bi, qi, p)),
        out_shape=jax.ShapeDtypeStruct((b, t, 512), _F32),
        scratch_shapes=[pltpu.VMEM((2 * LANES, 2 * tqf), _BF), pltpu.VMEM((SUBLANES, 2 * tqf), _F32),
                        pltpu.VMEM((MV, 2 * tqf), _F32)],
        compiler_params=_cparams(("arbitrary", "arbitrary", "arbitrary")),
        name="fox_prompt",
    )(qft, dect, foxb, dec, vft)


def _post_kernel(on_ref, of_ref, x_ref, gn_ref, gf_ref, wo_ref, gp_ref, y_ref):
    half = on_ref.shape[-1]
    a = _rms(on_ref[0], gn_ref[...]).astype(_BF)
    f = _rms(of_ref[0], gf_ref[...]).astype(_BF)
    mixed = _dot(a, wo_ref[0:half, :]) + _dot(f, wo_ref[half:2 * half, :])
    y_ref[0] = x_ref[0] + _rms(mixed, gp_ref[...])


def _post(o_n, o_f, x, gn, gf, wo, gp, tr):
    nb, t, d = x.shape
    half = o_n.shape[-1]
    row = lambda width: pl.BlockSpec((1, tr, width), lambda bi, i: (bi, i, 0))
    const = lambda shape: pl.BlockSpec(shape, lambda bi, i: (0,) * len(shape))
    return pl.pallas_call(
        _post_kernel,
        grid=(nb, t // tr),
        in_specs=[row(half), row(half), row(d), const((1, half)), const((1, half)), const((2 * half, d)), const((1, d))],
        out_specs=row(d),
        out_shape=jax.ShapeDtypeStruct((nb, t, d), _F32),
        compiler_params=_cparams(("arbitrary", "arbitrary")),
        name="out_proj",
    )(o_n, o_f, x, gn, gf, wo, gp)


def _ffn_kernel(x_ref, gpre_ref, wg_ref, wu_ref, wd_ref, gpost_ref, y_ref):
    x = x_ref[0]
    h = _rms(x, gpre_ref[...]).astype(_BF)
    act = (jax.nn.silu(_dot(h, wg_ref[...])) * _dot(h, wu_ref[...])).astype(_BF)
    y_ref[0] = x + _rms(_dot(act, wd_ref[...]), gpost_ref[...])


def _ffn(x, gpre, wg, wu, wd, gpost, tr):
    nb, t, d = x.shape
    dff = wg.shape[1]
    row = pl.BlockSpec((1, tr, d), lambda bi, i: (bi, i, 0))
    const = lambda shape: pl.BlockSpec(shape, lambda bi, i: (0,) * len(shape))
    return pl.pallas_call(
        _ffn_kernel,
        grid=(nb, t // tr),
        in_specs=[row, const((1, d)), const((d, dff)), const((d, dff)), const((dff, d)), const((1, d))],
        out_specs=row,
        out_shape=jax.ShapeDtypeStruct((nb, t, d), _F32),
        compiler_params=_cparams(("arbitrary", "arbitrary")),
        name="ffn",
    )(x, gpre, wg, wu, wd, gpost)


def _row_select(rows8, pieces):
    out = jnp.zeros((SUBLANES, pieces[0].shape[-1]), _F32)
    for i, piece in enumerate(pieces):
        out = jnp.where(rows8 == i, jnp.broadcast_to(piece, out.shape), out)
    return out


def _pad_rows(a, n):
    return jnp.concatenate([a, jnp.zeros((n - a.shape[0],) + a.shape[1:], a.dtype)], axis=0)


def _nsa_sample_kernel(pt_ref, *refs, pps, past, n_new, n_cmp, ncp, n_slc):
    page_refs = refs[:pps]
    (qn_ref, misc_ref, new_ref, win_ref, kwn_ref, w1k_ref, w1v_ref, pe_ref, w1r_ref, w2_ref,
     bc_ref, bs_ref, bw_ref, kwt_ref, on_ref, wout_ref, lk_ref, lv_ref, kst_ref, vst_ref) = refs[pps:]
    del pt_ref
    j = pl.program_id(1)
    n_steps = pl.num_programs(1)
    cpp = PAGE_SIZE // CMP_STRIDE
    ppc = WINDOW // PAGE_SIZE
    n_pages = past // PAGE_SIZE

    rr = _iota((PAGE_SIZE, PAGE_SIZE), 0)
    tok = _iota((PAGE_SIZE, PAGE_SIZE), 1)
    regroup = jnp.where(tok == CMP_STRIDE * _mod(rr, cpp) + _div(rr, cpp), 1.0, 0.0).astype(_BF)

    for k in range(pps):
        pg = j * pps + k
        ref = page_refs[k]
        kst_ref[pg] = ref[0, 0, 2].astype(_BF)
        vst_ref[pg] = ref[0, 0, 3].astype(_BF)
        c0 = pl.multiple_of(pg * cpp, cpp)
        for src, dst in ((0, lk_ref), (1, lv_ref)):
            x = _dot_nt(regroup, ref[0, 0, src].astype(_BF))
            for p in range(CMP_STRIDE):
                dst[pl.ds(c0, cpp), p * LANES:(p + 1) * LANES] = x[p * cpp:(p + 1) * cpp, :]

    @pl.when(j == n_steps - 1)
    def _():
        nc = past // CMP_STRIDE
        new = new_ref[0]
        ks_new = _pad_rows(new[:, 2 * LANES:3 * LANES], PAGE_SIZE).astype(_BF)
        vs_new = _pad_rows(new[:, 3 * LANES:4 * LANES], PAGE_SIZE).astype(_BF)

        def padc(a):
            return a if ncp == nc else _pad_rows(a, ncp)
        kc = padc(_compress(lk_ref[...].astype(_BF), w1k_ref[...], pe_ref[0], w1r_ref[0], w2_ref[0], n_cmp)).astype(_BF)
        vc = padc(_compress(lv_ref[...].astype(_BF), w1v_ref[...], pe_ref[1], w1r_ref[1], w2_ref[1], n_cmp)).astype(_BF)

        rows8 = _iota((SUBLANES, LANES), 0)
        lane = _iota((SUBLANES, LANES), 1)
        grp_low = rows8 < NSA_GROUP
        q = qn_ref[0].astype(_F32)
        blocks = []
        for i in range(n_new):
            blk = jnp.zeros((SUBLANES, LANES), _F32)
            for r in range(NSA_GROUP):
                piece = jnp.broadcast_to(q[i:i + 1, r * LANES:(r + 1) * LANES], (SUBLANES, LANES))
                blk = jnp.where(_mod(rows8, NSA_GROUP) == r, piece, blk)
            on_group = jnp.logical_not(jnp.logical_xor(lane < HEAD_DIM, grp_low))
            blocks.append(jnp.where(on_group, blk, jnp.zeros_like(blk)))
        lq = jnp.concatenate(blocks, axis=0).astype(_BF)

        pc = _masked_softmax_full(_dot_nt(lq, kc) + bc_ref[...])
        o_c = _dot(pc.astype(_BF), vc)

        nl = 2 * LANES
        ovl_t = _overlap(ncp, n_cmp, nl, blk_axis=0)
        sums = []
        for i in range(n_new):
            blk = pc[i * SUBLANES:(i + 1) * SUBLANES]
            rr8 = _iota(blk.shape, 0)
            for g in range(NSA_KV_HEADS):
                in_group = (rr8 < NSA_GROUP) if g == 0 else (rr8 >= NSA_GROUP)
                sums.append(jnp.sum(jnp.where(in_group, blk, 0.0), axis=0, keepdims=True))
        hi, lo = _split2(_pad_rows(_row_select(_iota((SUBLANES, ncp), 0), sums), LANES))
        imp_t = _dot_nt(ovl_t, hi) + _dot_nt(ovl_t, lo)
        t_pos = past + _div(_iota((1, LANES), 1), NSA_KV_HEADS)
        msel = _select_blocks(imp_t, t_pos, n_slc, min(N_SELECT, n_slc), blk_axis=0).T[0:SUBLANES]
        rows8n = _iota((SUBLANES, nl), 0)
        mrows = []
        for i in range(n_new):
            m0 = jnp.broadcast_to(msel[2 * i:2 * i + 1], (SUBLANES, nl))
            m1 = jnp.broadcast_to(msel[2 * i + 1:2 * i + 2], (SUBLANES, nl))
            mrows.append(jnp.where(rows8n < NSA_GROUP, m0, m1))
        mrows = jnp.concatenate(mrows, axis=0).astype(_BF)

        def block_mask(k0, n_keys):
            s_i = _iota((nl, n_keys), 1)
            j_i = _iota((nl, n_keys), 0)
            return _dot(mrows, jnp.where(j_i == _div(k0 + s_i, SLC_BLOCK), 1.0, 0.0).astype(_BF))

        n_chunk = n_pages // ppc
        kts = lambda ci: jnp.concatenate([kst_ref[ci * ppc + u] for u in range(ppc)], axis=1)
        vts = lambda ci: jnp.concatenate([vst_ref[ci * ppc + u] for u in range(ppc)], axis=1)
        parts = [_dot(lq, kts(ci)) + block_mask(ci * ppc * PAGE_SIZE, ppc * PAGE_SIZE) for ci in range(n_chunk)]
        parts[-1] = parts[-1] + bs_ref[:, 0:WINDOW]
        parts.append(_dot_nt(lq, ks_new) + block_mask(past, PAGE_SIZE) + bs_ref[:, WINDOW:WINDOW + PAGE_SIZE])
        m_s = jnp.max(parts[0], axis=-1, keepdims=True)
        for x in parts[1:]:
            m_s = jnp.maximum(m_s, jnp.max(x, axis=-1, keepdims=True))
        l_s = jnp.zeros_like(m_s)
        acc = jnp.zeros((lq.shape[0], LANES), _F32)
        for ci, x in enumerate(parts):
            pr = jnp.exp2(x - m_s)
            l_s = l_s + jnp.sum(pr, axis=-1, keepdims=True)
            acc = acc + (_dot_nt(pr.astype(_BF), vts(ci)) if ci < n_chunk else _dot(pr.astype(_BF), vs_new))
        o_s = acc / l_s

        kwn = kwn_ref[0]
        kw_new = _pad_rows(kwn[:, 0:LANES], PAGE_SIZE).astype(_BF)
        vw_new = _pad_rows(kwn[:, LANES:2 * LANES], PAGE_SIZE).astype(_BF)
        sw = jnp.concatenate([_dot(lq, win_ref[0, 0, 0].astype(_BF)), _dot_nt(lq, kw_new)], axis=1) + bw_ref[...]
        mw = jnp.max(sw, axis=-1, keepdims=True)
        pw = jnp.exp2(sw - mw)
        pwb = pw.astype(_BF)
        o_w = (_dot_nt(pwb[:, 0:WINDOW], win_ref[0, 0, 1].astype(_BF)) + _dot(pwb[:, WINDOW:], vw_new)) \
            / jnp.sum(pw, axis=-1, keepdims=True)

        lane_w = _iota((LANES, WINDOW), 1)
        for kv in range(2):
            buf = pltpu.roll(win_ref[0, 0, kv], WINDOW - n_new, 1)
            for i in range(n_new):
                col = jnp.broadcast_to(kwt_ref[0, kv * LANES:(kv + 1) * LANES, i:i + 1], (LANES, WINDOW))
                buf = jnp.where(lane_w == WINDOW - n_new + i, col, buf)
            wout_ref[0, kv] = buf

        misc = misc_ref[0]
        out_rows = []
        for i in range(n_new):
            sl = slice(i * SUBLANES, (i + 1) * SUBLANES)
            g_row = jnp.broadcast_to(misc[i:i + 1, :], (SUBLANES, LANES))

            def gcol(kind, g_row=g_row):
                pick = lane == L_GATE + kind * NSA_HEADS + rows8
                return jnp.sum(jnp.where(pick, g_row, 0.0), axis=-1, keepdims=True)
            o_blk = gcol(0) * o_c[sl] + gcol(1) * o_s[sl] + gcol(2) * o_w[sl]
            pieces = [jnp.where(lane[0:1] < HEAD_DIM, o_blk[r:r + 1], o_blk[NSA_GROUP + r:NSA_GROUP + r + 1])
                      for r in range(NSA_GROUP)]
            out_rows.append(jnp.concatenate(pieces, axis=1))
        on_ref[0] = _row_select(_iota((SUBLANES, 4 * LANES), 0), out_rows)


def _nsa_sample(layer, page_table, cache_t, qn, misc, new, win_t, kwn, kwn_t, cw, bc, bs, bw, past, n_new, n_cmp, ncp, n_slc, pps):
    n_seq, n_pages = page_table.shape
    n_steps = n_pages // pps
    nc = past // CMP_STRIDE

    def page_spec(k):
        return pl.BlockSpec((1, 1, 4, LANES, PAGE_SIZE), lambda s, j, pt, k=k: (layer, pt[s, j * pps + k], 0, 0, 0))
    per_seq = lambda a: pl.BlockSpec((1,) + a.shape[1:], lambda s, j, pt: (s,) + (0,) * (a.ndim - 1))
    const = lambda a: pl.BlockSpec(a.shape, lambda s, j, pt: (0,) * a.ndim)
    consts = [cw["w1k"], cw["w1v"], cw["pe8"], cw["w1raw"], cw["w2p"], bc, bs, bw]
    grid_spec = pltpu.PrefetchScalarGridSpec(
        num_scalar_prefetch=1,
        grid=(n_seq, n_steps),
        in_specs=[page_spec(k) for k in range(pps)] + [per_seq(qn), per_seq(misc), per_seq(new),
                                                        pl.BlockSpec((1, 1) + win_t.shape[2:], lambda s, j, pt: (layer, s, 0, 0, 0)),
                                                        per_seq(kwn)]
        + [const(a) for a in consts] + [per_seq(kwn_t)],
        out_specs=(pl.BlockSpec((1, SUBLANES, 4 * LANES), lambda s, j, pt: (s, 0, 0)),
                   pl.BlockSpec((1,) + win_t.shape[2:], lambda s, j, pt: (s, 0, 0, 0))),
        scratch_shapes=[pltpu.VMEM((nc, CMP_STRIDE * LANES), _F32), pltpu.VMEM((nc, CMP_STRIDE * LANES), _F32),
                        pltpu.VMEM((n_pages, LANES, PAGE_SIZE), _BF), pltpu.VMEM((n_pages, LANES, PAGE_SIZE), _BF)],
    )
    return pl.pallas_call(
        functools.partial(_nsa_sample_kernel, pps=pps, past=past, n_new=n_new, n_cmp=n_cmp, ncp=ncp, n_slc=n_slc),
        grid_spec=grid_spec,
        out_shape=(jax.ShapeDtypeStruct((n_seq, SUBLANES, 4 * LANES), _F32),
                   jax.ShapeDtypeStruct((n_seq,) + win_t.shape[2:], _F32)),
        compiler_params=_cparams(("arbitrary", "arbitrary")),
        name="nsa_sample",
    )(page_table, *([cache_t] * pps), qn, misc, new, win_t, kwn, *consts, kwn_t)


def _fox_sample_kernel(pt_ref, *refs, pps, n_new):
    kv_refs = refs[:pps]
    lf_refs = refs[pps:2 * pps]
    (qf_ref, kvn_ref, lfn_ref, of_ref, q_ref, e_ref, carry_ref, m_ref, l_ref, acc_ref) = refs[2 * pps:]
    del pt_ref
    j = pl.program_id(1)
    n_steps = pl.num_programs(1)
    rows = FOX_HEADS * n_new
    width = FOX_HEADS * HEAD_DIM
    srow = _iota((PAGE_SIZE, PAGE_SIZE), 0)
    scol = _iota((PAGE_SIZE, PAGE_SIZE), 1)
    later = jnp.where(srow > scol, 1.0, 0.0).astype(_BF)

    def suffix(lf):
        hi, mid, lo = _split3(lf)
        return _dot(hi, later) + _dot(mid, later) + _dot(lo, later)

    def decay(rt):
        return (jnp.concatenate([rt] * n_new, axis=0) - e_ref[...]) * LOG2E

    @pl.when(j == 0)
    def _():
        _softmax_init(m_ref, l_ref, acc_ref)
        rows8 = _iota((SUBLANES, width), 0)
        lane = _iota((SUBLANES, width), 1)
        q = qf_ref[0].astype(_F32)
        blocks = []
        for i in range(n_new):
            piece = jnp.broadcast_to(q[i:i + 1, :], (SUBLANES, width))
            blocks.append(jnp.where(_div(lane, HEAD_DIM) == rows8, piece, jnp.zeros_like(piece)))
        q_ref[...] = jnp.concatenate(blocks, axis=0).astype(_BF)
        lfn = lfn_ref[0]
        rt = suffix(lfn)
        e_ref[...] = jnp.concatenate([rt[:, i:i + 1] for i in range(n_new)], axis=0)
        carry_ref[...] = jnp.broadcast_to(jnp.sum(lfn, axis=-1, keepdims=True), carry_ref.shape)
        kvn = _pad_rows(kvn_ref[0], PAGE_SIZE)
        ri = _div(_iota((rows, PAGE_SIZE), 0), SUBLANES)
        ci = _iota((rows, PAGE_SIZE), 1)
        s = _dot_nt(q_ref[...], kvn[:, 0:width].astype(_BF)) + decay(rt) + jnp.where(ci <= ri, 0.0, NEG)
        v_new = kvn[:, width:2 * width].astype(_BF)
        _online_update(s, lambda p: _dot(p, v_new), m_ref, l_ref, acc_ref)

    kts, vts, rts = [], [], []
    carry = carry_ref[...]
    for k in range(pps):
        lf = lf_refs[k][0, 0]
        kts.append(kv_refs[k][0, 0, 0].astype(_BF))
        vts.append(kv_refs[k][0, 0, 1].astype(_BF))
        rts.append(suffix(lf) + carry)
        carry = carry + jnp.sum(lf, axis=-1, keepdims=True)
    carry_ref[...] = carry
    vt = jnp.concatenate(vts, axis=1)
    s = _dot(q_ref[...], jnp.concatenate(kts, axis=1)) + decay(jnp.concatenate(rts, axis=1))
    _online_update(s, lambda p: _dot_nt(p, vt), m_ref, l_ref, acc_ref)

    @pl.when(j == n_steps - 1)
    def _():
        o = acc_ref[...] / l_ref[...]
        rows8 = _iota((SUBLANES, width), 0)
        lane = _iota((SUBLANES, width), 1)
        out_rows = []
        for i in range(n_new):
            blk = jnp.where(_div(lane, HEAD_DIM) == rows8, o[i * SUBLANES:(i + 1) * SUBLANES], 0.0)
            out_rows.append(jnp.sum(blk, axis=0, keepdims=True))
        of_ref[0] = _row_select(rows8, out_rows)


def _fox_sample(layer, page_table, cache_t, cache_lft, qf, kvn, lfn, n_new, pps):
    n_seq, n_pages = page_table.shape
    n_steps = n_pages // pps
    rows = FOX_HEADS * n_new
    width = FOX_HEADS * HEAD_DIM

    def page_idx(s, j, pt, k):
        return pt[s, n_pages - 1 - (j * pps + k)]
    kv_spec = lambda k: pl.BlockSpec((1, 1, 2, width, PAGE_SIZE),
                                     lambda s, j, pt, k=k: (layer, page_idx(s, j, pt, k), 0, 0, 0))
    lf_spec = lambda k: pl.BlockSpec((1, 1, FOX_HEADS, PAGE_SIZE),
                                     lambda s, j, pt, k=k: (layer, page_idx(s, j, pt, k), 0, 0))
    per_seq = lambda a: pl.BlockSpec((1,) + a.shape[1:], lambda s, j, pt: (s,) + (0,) * (a.ndim - 1))
    grid_spec = pltpu.PrefetchScalarGridSpec(
        num_scalar_prefetch=1,
        grid=(n_seq, n_steps),
        in_specs=[kv_spec(k) for k in range(pps)] + [lf_spec(k) for k in range(pps)] + [per_seq(a) for a in (qf, kvn, lfn)],
        out_specs=pl.BlockSpec((1, SUBLANES, width), lambda s, j, pt: (s, 0, 0)),
        scratch_shapes=[pltpu.VMEM((rows, width), _BF), pltpu.VMEM((rows, 1), _F32), pltpu.VMEM((SUBLANES, LANES), _F32),
                        pltpu.VMEM((rows, 1), _F32), pltpu.VMEM((rows, 1), _F32), pltpu.VMEM((rows, width), _F32)],
    )
    return pl.pallas_call(
        functools.partial(_fox_sample_kernel, pps=pps, n_new=n_new),
        grid_spec=grid_spec,
        out_shape=jax.ShapeDtypeStruct((n_seq, SUBLANES, width), _F32),
        compiler_params=_cparams(("arbitrary", "arbitrary")),
        name="fox_sample",
    )(page_table, *([cache_t] * pps), *([cache_lft] * pps), qf, kvn, lfn)


def _nsa_perm():
    idx = np.zeros(NSA_HEADS * HEAD_DIM, np.int32)
    for r in range(NSA_GROUP):
        for g in range(NSA_KV_HEADS):
            for d in range(HEAD_DIM):
                idx[r * LANES + g * HEAD_DIM + d] = (g * NSA_GROUP + r) * HEAD_DIM + d
    return idx


def _layer_weights(l, w_in, b_gate, b_forget, cmp_pe, cmp_w1, cmp_w2, grp_norm_nsa, grp_norm_fox, w_o):
    perm = _nsa_perm()
    w = w_in[l]
    o_qn, o_nsa, o_kw, o_g, o_qf, o_kf, o_f = 0, 512, 1024, 1280, 1304, 1816, 2840
    f_cols = w[:, o_f:o_f + FOX_HEADS]
    misc = jnp.concatenate([w[:, o_g:o_g + 3 * NSA_HEADS], f_cols, f_cols,
                            jnp.zeros((w.shape[0], LANES - 3 * NSA_HEADS - 2 * FOX_HEADS), w.dtype)], axis=1)
    wp = jnp.concatenate([w[:, o_qn:o_nsa][:, perm], w[:, o_nsa:o_kw], w[:, o_kw:o_g], w[:, o_qf:o_kf],
                          w[:, o_kf:o_f], misc], axis=1).astype(_BF)
    bias = jnp.concatenate([b_gate[l].reshape(-1), b_forget[l], b_forget[l],
                            jnp.zeros((LANES - 3 * NSA_HEADS - 2 * FOX_HEADS,), _F32)]).reshape(1, LANES)

    def w1_layout(w1):
        w1 = w1.reshape(2, CMP_STRIDE, HEAD_DIM, CMP_HIDDEN)
        z = jnp.zeros((CMP_STRIDE, HEAD_DIM, CMP_HIDDEN), w1.dtype)
        g0 = jnp.concatenate([w1[0], w1[1], z, z], axis=-1)
        g1 = jnp.concatenate([z, z, w1[0], w1[1]], axis=-1)
        return jnp.concatenate([g0, g1], axis=1).reshape(CMP_STRIDE * LANES, 4 * CMP_HIDDEN).astype(_BF)

    def w2_layout(w2):
        z = jnp.zeros_like(w2)
        return jnp.concatenate([jnp.concatenate([w2, z], axis=1), jnp.concatenate([z, w2], axis=1)], axis=0).astype(_BF)
    cw = dict(
        w1k=w1_layout(cmp_w1[l, 0]), w1v=w1_layout(cmp_w1[l, 1]),
        pe8=jnp.broadcast_to(cmp_pe[l].reshape(2, 1, CMP_BLOCK * HEAD_DIM), (2, SUBLANES, CMP_BLOCK * HEAD_DIM)).astype(_BF),
        w1raw=cmp_w1[l].astype(_BF),
        w2p=jnp.stack([w2_layout(cmp_w2[l, 0]), w2_layout(cmp_w2[l, 1])]),
    )
    gn = grp_norm_nsa[l][perm].reshape(1, -1)
    gf = grp_norm_fox[l].reshape(1, -1)
    wo = jnp.concatenate([w_o[l][:NSA_HEADS * HEAD_DIM][perm], w_o[l][NSA_HEADS * HEAD_DIM:]], axis=0).astype(_BF)
    return wp, bias, cw, gn, gf, wo


def kernel(x_prompt, x_sample, cache_nsa_kv, cache_fox_kv, cache_fox_logf, state_win_kv, page_table, rel_bias,
           norm_mix_pre, norm_mix_post, norm_ffn_pre, norm_ffn_post, w_in, b_gate, b_forget, cmp_pe, cmp_w1, cmp_w2,
           grp_norm_nsa, grp_norm_fox, w_o, w_ffn_gate, w_ffn_up, w_ffn_down):
    depth = w_in.shape[0]
    b, t, d = x_prompt.shape
    n_seq, n_new, _ = x_sample.shape
    n_pages = page_table.shape[1]
    past = n_pages * PAGE_SIZE
    n_pool = cache_nsa_kv.shape[1]
    n_win = state_win_kv.shape[2]
    assert t % min(TQF, t) == 0 and min(TQF, t) % TK == 0 and t % TQ == 0 and TQ == TK == WINDOW and TQ % CB == 0
    assert CB % LANES == 0 and LANES > T5_FAR
    assert n_new <= SUBLANES and SLC_BLOCK >= n_new and n_win == WINDOW
    assert n_pages % (WINDOW // PAGE_SIZE) == 0 and past >= 2 * WINDOW
    assert (n_seq * n_new) % SUBLANES == 0

    nc_p = t // CMP_STRIDE
    ncp_p = -(-nc_p // LANES) * LANES
    n_slc_p = t // SLC_BLOCK
    assert n_slc_p <= LANES
    nc_s = past // CMP_STRIDE
    ncp_s = -(-nc_s // LANES) * LANES
    n_slc_s = past // SLC_BLOCK + 1
    assert n_slc_s <= 2 * LANES
    pps = math.gcd(n_pages, 32)
    pps_fox = math.gcd(n_pages, 32)

    dt, nbh, nbl = _bias_prompt(rel_bias)
    bc, bs, bw = _bias_sample(rel_bias, past, n_new, nc_s - 1, ncp_s)

    nsa_t = jnp.transpose(cache_nsa_kv, (0, 1, 3, 4, 5, 2)).reshape(depth, n_pool, 4, NSA_KV_HEADS * HEAD_DIM, PAGE_SIZE)
    fox_t = jnp.transpose(cache_fox_kv, (0, 1, 3, 4, 5, 2)).reshape(depth, n_pool, 2, FOX_HEADS * HEAD_DIM, PAGE_SIZE)
    logf_t = jnp.transpose(cache_fox_logf, (0, 1, 3, 2))
    win_t = jnp.transpose(state_win_kv, (0, 1, 3, 4, 5, 2)).reshape(depth, n_seq, 2, NSA_KV_HEADS * HEAD_DIM, n_win)

    row1 = lambda a: a.reshape(1, -1)
    pad_new = lambda a: jnp.pad(a, ((0, 0), (0, SUBLANES - n_new), (0, 0)))
    rs = n_seq * n_new
    xp, xs = x_prompt, x_sample.reshape(1, rs, d)
    outs_p, outs_s = [], []
    for l in range(depth):
        wp, bias, cw, gn, gf, wo = _layer_weights(l, w_in, b_gate, b_forget, cmp_pe, cmp_w1, cmp_w2,
                                                  grp_norm_nsa, grp_norm_fox, w_o)
        wg, wu, wd = w_ffn_gate[l].astype(_BF), w_ffn_up[l].astype(_BF), w_ffn_down[l].astype(_BF)
        g_pre, g_post = row1(norm_mix_pre[l]), row1(norm_mix_post[l])
        g_fpre, g_fpost = row1(norm_ffn_pre[l]), row1(norm_ffn_post[l])

        (_, nsa, nsab, kw, kwb, _, fox, foxb, misc, dec,
         qnt, qft, dect, misct, vst, vwt, vft) = _proj(xp, g_pre, wp, bias, TR_PROJ, True)
        kcmp, vcmpt = _compress_prompt(nsa, cw, nc_p, ncp_p)
        o_n = _nsa_prompt(qnt, misct, kcmp, vcmpt, nsab, vst, kwb, vwt, dt, nbh, nbl, nc_p - 1, ncp_p, n_slc_p)
        o_f = _fox_prompt(qft, dect, foxb, dec, vft)
        xp = _post(o_n, o_f, xp, gn, gf, wo, g_post, TR_POST)
        xp = _ffn(xp, g_fpre, wg, wu, wd, g_fpost, TR_FFN)
        outs_p.append((nsa.reshape(b, t, 4, NSA_KV_HEADS, HEAD_DIM), fox.reshape(b, t, 2, FOX_HEADS, HEAD_DIM),
                       misc[:, :, L_LOGF:L_LOGF + FOX_HEADS],
                       kw[:, t - WINDOW:].reshape(b, WINDOW, 2, NSA_KV_HEADS, HEAD_DIM)))

        tr_s = math.gcd(rs, TR_PROJ)
        qn, nsa, nsab, kw, kwb, qf, fox, foxb, misc = _proj(xs, g_pre, wp, bias, tr_s, False)
        per = lambda a: a.reshape(n_seq, n_new, a.shape[-1])
        nsa_s, fox_s, kw_s, misc_s = per(nsa), per(fox), per(kw), per(misc)
        logf_s = misc_s[:, :, L_LOGF:L_LOGF + FOX_HEADS]
        kw_st = jnp.pad(jnp.swapaxes(kw_s, 1, 2), ((0, 0), (0, 0), (0, LANES - n_new)))
        o_n, win_new = _nsa_sample(l, page_table, nsa_t, pad_new(per(qn)), pad_new(misc_s), pad_new(nsa_s), win_t,
                                   pad_new(kw_s), kw_st, cw, bc, bs, bw, past, n_new, nc_s - 1, ncp_s, n_slc_s, pps)
        kvn = pad_new(fox_s)
        lfn = jnp.pad(jnp.swapaxes(logf_s, 1, 2), ((0, 0), (0, 0), (0, PAGE_SIZE - n_new)))
        o_f = _fox_sample(l, page_table, fox_t, logf_t, pad_new(per(qf)), kvn, lfn, n_new, pps_fox)
        o_n = o_n[:, :n_new].reshape(1, rs, -1)
        o_f = o_f[:, :n_new].reshape(1, rs, -1)
        xs = _post(o_n, o_f, xs, gn, gf, wo, g_post, math.gcd(rs, TR_POST))
        xs = _ffn(xs, g_fpre, wg, wu, wd, g_fpost, math.gcd(rs, TR_FFN))
        win_new = jnp.transpose(win_new.reshape(n_seq, 2, NSA_KV_HEADS, HEAD_DIM, n_win), (0, 4, 1, 2, 3))
        outs_s.append((nsa_s.reshape(n_seq, n_new, 4, NSA_KV_HEADS, HEAD_DIM),
                       fox_s.reshape(n_seq, n_new, 2, FOX_HEADS, HEAD_DIM), logf_s, win_new))

    stack = lambda outs, i: jnp.stack([o[i] for o in outs], axis=0)
    return (xp, xs.reshape(n_seq, n_new, d), stack(outs_p, 0), stack(outs_p, 1), stack(outs_p, 2), stack(outs_p, 3),
            stack(outs_s, 0), stack(outs_s, 1), stack(outs_s, 2), stack(outs_s, 3))
```

```python
import functools
import math

import numpy as np
import jax
import jax.numpy as jnp
from jax import lax
from jax.experimental import pallas as pl
from jax.experimental.pallas import tpu as pltpu

HEAD_DIM = 64
NSA_HEADS = 8
FOX_HEADS = 8
NSA_KV_HEADS = 2
NSA_GROUP = NSA_HEADS // NSA_KV_HEADS
CMP_BLOCK = 32
CMP_STRIDE = 16
CMP_HIDDEN = 2 * HEAD_DIM
SLC_BLOCK = 64
N_SELECT = 16
WINDOW = 512
T5_BUCKETS = 32
T5_EXACT = T5_BUCKETS // 2
T5_MAX_DIST = 128
PAGE_SIZE = 128
NORM_EPS = 1e-6
FORCE_SCORE = 1e4
LOG2E = math.log2(math.e)
Q_SCALE = HEAD_DIM ** -0.5 * LOG2E

LANES = 128
SUBLANES = 8
VMEM_LIMIT = 56 * 1024 * 1024

_F32 = jnp.float32
_BF = jnp.bfloat16
NEG = -(2.0 ** 100)
NEG_HALF = -(2.0 ** 99)
M_INIT = -3.0e38
REMOVED = -3.4e38


def _t5_thresholds():
    n = np.arange(1, 4 * T5_MAX_DIST)
    large = T5_EXACT + (np.log(n / T5_EXACT) / math.log(T5_MAX_DIST / T5_EXACT) * (T5_BUCKETS - T5_EXACT)).astype(np.int64)
    return tuple(int(n[np.argmax(large >= k)]) for k in range(T5_EXACT + 1, T5_BUCKETS))


_T5_THR = _t5_thresholds()
T5_FAR = _T5_THR[-1]


def _log2(n):
    assert n & (n - 1) == 0
    return n.bit_length() - 1


TQ = 512
TQF = 4096
TK = 512
CB = 256
AHEAD = 3
MV = HEAD_DIM + SUBLANES
TR_PROJ = 256
TR_POST = 512
TR_FFN = 256
CPT = TQ // CMP_STRIDE
NEAR_BACK = -(-(T5_FAR + CMP_BLOCK - 1) // CMP_STRIDE) - 1
NEAR_U = CPT + NEAR_BACK

C_QN, C_NSA, C_KW, C_QF, C_FOX, C_MISC = 0, 512, 1024, 1280, 1792, 2816
C_TOT = 2944
L_GATE, L_LOGF, L_CUM = 0, 24, 32
N_PIECE = 3
L_NEG, L_POS = 0, N_PIECE * FOX_HEADS


def _dot(a, b):
    return jnp.dot(a, b, preferred_element_type=_F32)


def _dot_nt(a, b):
    return lax.dot_general(a, b, (((1,), (1,)), ((), ())), preferred_element_type=_F32)


def _split3(x):
    hi = x.astype(_BF)
    r1 = x - hi.astype(_F32)
    mid = r1.astype(_BF)
    lo = (r1 - mid.astype(_F32)).astype(_BF)
    return hi, mid, lo


def _split2(x):
    hi = x.astype(_BF)
    return hi, (x - hi.astype(_F32)).astype(_BF)


def _dot3(a_bf, x):
    hi, mid, lo = _split3(x)
    return _dot(a_bf, hi) + _dot(a_bf, mid) + _dot(a_bf, lo)


def _rms(x, g):
    ms = jnp.mean(x * x, axis=-1, keepdims=True)
    return x * lax.rsqrt(ms + NORM_EPS) * g


def _gelu_tanh(x):
    c = math.sqrt(2.0 / math.pi)
    return x * (0.5 * (1.0 + jnp.tanh(c * (x + 0.044715 * (x * x * x)))))


def _iota(shape, dim):
    return lax.broadcasted_iota(jnp.int32, shape, dim)


def _div(x, n):
    return jnp.right_shift(x, _log2(n))


def _mod(x, n):
    return x & (n - 1)


def _cparams(sem):
    return pltpu.CompilerParams(dimension_semantics=sem, vmem_limit_bytes=VMEM_LIMIT)


def _t5_rel(dist, rel_ref, h):
    d = jnp.minimum(dist, T5_MAX_DIST - 1)
    big = jnp.full(d.shape, T5_EXACT, jnp.int32)
    for thr in _T5_THR:
        big = big + jnp.where(d >= thr, 1, 0)
    bkt = jnp.where(d < T5_EXACT, d, big)
    far = rel_ref[T5_BUCKETS - 1, h]
    val = jnp.zeros(d.shape, _F32)
    for k in range(T5_BUCKETS - 1):
        val = jnp.where(bkt == k, (rel_ref[k, h] - far) * LOG2E, val)
    return val


def _t5_masked(dist, rel_ref, h):
    return jnp.where(dist < 0, NEG, _t5_rel(jnp.maximum(dist, 0), rel_ref, h))


def _bias_prompt_kernel(rel_ref, dt_ref, nbh_ref, nbl_ref):
    def body(h, carry):
        c = _iota((LANES, LANES), 0)
        i = _iota((LANES, LANES), 1)
        dt_ref[h, 0] = _t5_masked(i - c, rel_ref, h)
        dt_ref[h, 1] = _t5_masked(i - c + LANES, rel_ref, h)
        u = _iota((LANES, TQ), 0)
        i = _iota((LANES, TQ), 1)
        near = _t5_masked(i + (CMP_STRIDE * NEAR_BACK - (CMP_BLOCK - 1)) - CMP_STRIDE * u, rel_ref, h)
        nb = jnp.where(u < NEAR_U, near, jnp.where(u == NEAR_U, NEG, 0.0))
        hi, lo = _split2(nb)
        cols = pl.ds(pl.multiple_of(h * TQ, TQ), TQ)
        nbh_ref[:, cols] = hi
        nbl_ref[:, cols] = lo
        return carry
    lax.fori_loop(0, NSA_HEADS, body, 0)


def _bias_prompt(rel_bias):
    return pl.pallas_call(
        _bias_prompt_kernel,
        out_shape=(jax.ShapeDtypeStruct((NSA_HEADS, 2, LANES, LANES), _F32),
                   jax.ShapeDtypeStruct((LANES, NSA_HEADS * TQ), _BF),
                   jax.ShapeDtypeStruct((LANES, NSA_HEADS * TQ), _BF)),
        in_specs=[pl.BlockSpec(memory_space=pltpu.SMEM)],
        name="t5_bias_prompt",
    )(rel_bias)


def _bias_sample_kernel(rel_ref, bc_ref, bs_ref, bw_ref, *, past, n_new, n_cmp):
    def table(shape, dist_fn, extra_invalid=None):
        out = jnp.zeros(shape, _F32)
        r = _iota(shape, 0)
        c = _iota(shape, 1)
        i = _div(r, NSA_HEADS)
        dist = dist_fn(i, c)
        for h in range(NSA_HEADS):
            v = _t5_masked(dist, rel_ref, h)
            out = jnp.where(_mod(r, NSA_HEADS) == h, v, out)
        if extra_invalid is not None:
            out = jnp.where(extra_invalid(i, c, dist), NEG, out)
        return out

    bc_ref[...] = table(bc_ref.shape, lambda i, c: past + i - (CMP_STRIDE * c + CMP_BLOCK - 1),
                        lambda i, c, d: c >= n_cmp)
    bs_ref[...] = table(bs_ref.shape, lambda i, c: i + WINDOW - c)
    bw_ref[...] = table(bw_ref.shape, lambda i, c: i + WINDOW - c,
                        lambda i, c, d: (d >= WINDOW) | (c >= WINDOW + n_new))


def _bias_sample(rel_bias, past, n_new, n_cmp, ncp):
    rows = NSA_HEADS * n_new
    return pl.pallas_call(
        functools.partial(_bias_sample_kernel, past=past, n_new=n_new, n_cmp=n_cmp),
        out_shape=(jax.ShapeDtypeStruct((rows, ncp), _F32),
                   jax.ShapeDtypeStruct((rows, WINDOW + PAGE_SIZE), _F32),
                   jax.ShapeDtypeStruct((rows, WINDOW + PAGE_SIZE), _F32)),
        in_specs=[pl.BlockSpec(memory_space=pltpu.SMEM)],
        name="t5_bias_sample",
    )(rel_bias)


def _proj_kernel(x_ref, g_ref, w_ref, b_ref, *refs, tr, prompt):
    if prompt:
        (qn_ref, nsa_ref, nsab_ref, kw_ref, kwb_ref, qf_ref, fox_ref, foxb_ref, misc_ref, dec_ref,
         qnt_ref, qft_ref, dect_ref, misct_ref, vst_ref, vwt_ref, vft_ref, carry_ref) = refs
    else:
        qn_ref, nsa_ref, nsab_ref, kw_ref, kwb_ref, qf_ref, fox_ref, foxb_ref, misc_ref = refs
    x = x_ref[0]
    h = _rms(x, g_ref[...]).astype(_BF)
    qn = _dot(h, w_ref[:, C_QN:C_NSA]) * Q_SCALE
    qn_ref[0] = qn.astype(_BF)
    z_nsa = _dot(h, w_ref[:, C_NSA:C_KW])
    nsa_ref[0] = z_nsa
    nsab_ref[0] = z_nsa.astype(_BF)
    z_kw = _dot(h, w_ref[:, C_KW:C_QF])
    kw_ref[0] = z_kw
    kwb_ref[0] = z_kw.astype(_BF)
    qf = _dot(h, w_ref[:, C_QF:C_FOX]) * Q_SCALE
    qf_ref[0] = qf.astype(_BF)
    z_fox = _dot(h, w_ref[:, C_FOX:C_MISC])
    fox_ref[0] = z_fox
    foxb_ref[0] = z_fox.astype(_BF)
    zm = _dot(h, w_ref[:, C_MISC:C_TOT]) + b_ref[...]
    lane = _iota((tr, LANES), 1)
    sg = jax.nn.sigmoid(zm)
    ls = jnp.minimum(zm, 0.0) - jnp.log1p(jnp.exp(-jnp.abs(zm)))
    if not prompt:
        misc_ref[0] = jnp.where(lane < L_LOGF, sg, jnp.where(lane < L_CUM, ls, 0.0))
        return

    @pl.when(pl.program_id(1) == 0)
    def _():
        carry_ref[...] = jnp.zeros(carry_ref.shape, _F32)
    row = _iota((tr, tr), 0)
    col = _iota((tr, tr), 1)
    tri = jnp.where(col <= row, 1.0, 0.0).astype(_BF)
    cs = _dot3(tri, ls) + carry_ref[0:1, :]
    carry_ref[...] = jnp.broadcast_to(cs[tr - 1:tr, :], carry_ref.shape)
    pieces = _split3(cs * LOG2E)
    r = _iota((LANES, LANES), 0) - L_CUM
    c = _iota((LANES, LANES), 1)
    head_row = (r >= 0) & (r < FOX_HEADS)
    dec = jnp.zeros((tr, LANES), _F32)
    for j, piece in enumerate(pieces):
        put = jnp.where(head_row & (c == L_NEG + N_PIECE * r + j), -1.0,
                        jnp.where(head_row & (c == L_POS + N_PIECE * r + j), 1.0, 0.0)).astype(_BF)
        dec = dec + _dot(piece, put)
    dec_ref[0] = dec.astype(_BF)
    misc = jnp.where(lane < L_LOGF, sg, jnp.where(lane < L_CUM, ls, jnp.where(lane < L_CUM + FOX_HEADS, cs, 0.0)))
    misc_ref[0] = misc
    qnt_ref[0] = qn.T.astype(_BF)
    qft_ref[0] = qf.T.astype(_BF)
    dect_ref[0] = dec.T.astype(_BF)
    misct_ref[0] = misc.T
    vst_ref[0, 0] = z_nsa[:, 3 * LANES:4 * LANES].T.astype(_BF)
    vwt_ref[0, 0] = z_kw[:, LANES:2 * LANES].T.astype(_BF)
    vft_ref[0, 0] = z_fox[:, FOX_HEADS * HEAD_DIM:].T.astype(_BF)


def _proj(x, g, w, b, tr, prompt):
    nb, t, d = x.shape
    grid = (nb, t // tr)
    row = lambda width: pl.BlockSpec((1, tr, width), lambda bi, i: (bi, i, 0))
    const = lambda shape: pl.BlockSpec(shape, lambda bi, i: (0,) * len(shape))
    shp = lambda width, dt: jax.ShapeDtypeStruct((nb, t, width), dt)
    out_specs = [row(512), row(512), row(512), row(256), row(256), row(512), row(1024), row(1024), row(LANES)]
    out_shape = [shp(512, _BF), shp(512, _F32), shp(512, _BF), shp(256, _F32), shp(256, _BF), shp(512, _BF),
                 shp(1024, _F32), shp(1024, _BF), shp(LANES, _F32)]
    scratch = []
    if prompt:
        assert TK % tr == 0 and t % TK == 0
        per = TK // tr
        colm = lambda rows: pl.BlockSpec((1, rows, tr), lambda bi, i: (bi, 0, i))
        tile = lambda rows: pl.BlockSpec((1, 1, rows, tr), lambda bi, i: (bi, i // per, 0, i % per))
        tshp = lambda rows, dt: jax.ShapeDtypeStruct((nb, rows, t), dt)
        t4 = lambda rows: jax.ShapeDtypeStruct((nb, t // TK, rows, TK), _BF)
        out_specs += [row(LANES), colm(512), colm(512), colm(LANES), colm(LANES), tile(LANES), tile(LANES), tile(512)]
        out_shape += [shp(LANES, _BF), tshp(512, _BF), tshp(512, _BF), tshp(LANES, _BF), tshp(LANES, _F32),
                      t4(LANES), t4(LANES), t4(512)]
        scratch = [pltpu.VMEM((SUBLANES, LANES), _F32)]
    return pl.pallas_call(
        functools.partial(_proj_kernel, tr=tr, prompt=prompt),
        grid=grid,
        in_specs=[row(d), const((1, d)), const((d, C_TOT)), const((1, LANES))],
        out_specs=tuple(out_specs),
        out_shape=tuple(out_shape),
        scratch_shapes=scratch,
        compiler_params=_cparams(("arbitrary", "arbitrary")),
        name="in_proj",
    )(x, g, w, b)


def _compress(lhs_bf, w1p, pe8, w1raw, w2p, n_valid):
    nc = lhs_bf.shape[0]
    hcat = _dot(lhs_bf, w1p)
    cst = _dot(pe8, w1raw)[0:1]

    def hidden(g):
        a = hcat[:, g * 256:g * 256 + CMP_HIDDEN]
        b = hcat[:, g * 256 + CMP_HIDDEN:(g + 1) * 256]
        return _gelu_tanh(a + pltpu.roll(b, nc - 1, 0) + cst)

    hh = jnp.concatenate([hidden(0), hidden(1)], axis=1).astype(_BF)
    out = _dot(hh, w2p)
    return jnp.where(_iota(out.shape, 0) < n_valid, out, 0.0)


def _compress_prompt_kernel(xk_ref, xv_ref, w1k_ref, w1v_ref, pe_ref, w1r_ref, w2_ref, kc_ref, vc_ref, *, nc, ncp):
    for idx, (x_ref, w1, out_ref) in enumerate(((xk_ref, w1k_ref, kc_ref), (xv_ref, w1v_ref, vc_ref))):
        pieces = [x_ref[0, pl.ds(p, nc, stride=CMP_STRIDE), :] for p in range(CMP_STRIDE)]
        lhs = jnp.concatenate(pieces, axis=1).astype(_BF)
        out = _compress(lhs, w1[...], pe_ref[idx], w1r_ref[idx], w2_ref[idx], nc - 1)
        if ncp > nc:
            out = jnp.concatenate([out, jnp.zeros((ncp - nc, LANES), _F32)], axis=0)
        out_ref[0] = (out.T if idx == 1 else out).astype(_BF)


def _compress_prompt(nsa_state, cw, nc, ncp):
    b, t, _ = nsa_state.shape
    const = lambda a: pl.BlockSpec(a.shape, lambda bi: (0,) * a.ndim)
    return pl.pallas_call(
        functools.partial(_compress_prompt_kernel, nc=nc, ncp=ncp),
        grid=(b,),
        in_specs=[pl.BlockSpec((1, t, LANES), lambda bi: (bi, 0, 0)), pl.BlockSpec((1, t, LANES), lambda bi: (bi, 0, 1)),
                  const(cw["w1k"]), const(cw["w1v"]), const(cw["pe8"]), const(cw["w1raw"]), const(cw["w2p"])],
        out_specs=(pl.BlockSpec((1, ncp, LANES), lambda bi: (bi, 0, 0)), pl.BlockSpec((1, LANES, ncp), lambda bi: (bi, 0, 0))),
        out_shape=(jax.ShapeDtypeStruct((b, ncp, LANES), _BF), jax.ShapeDtypeStruct((b, LANES, ncp), _BF)),
        compiler_params=_cparams(("arbitrary",)),
        name="compress_prompt",
    )(nsa_state, nsa_state, cw["w1k"], cw["w1v"], cw["pe8"], cw["w1raw"], cw["w2p"])


def _attend_tile_t(lhst_ref, k_t, v_of, m_ref, acc_ref, blocks, add_fn=None, feat=slice(None), keys_of=None):
    n = len(blocks)
    cols = lambda b: slice(b * CB, (b + 1) * CB)
    keys = (lambda b: slice(None)) if keys_of is None else keys_of
    score = lambda b: _dot(k_t[keys(b), :], lhst_ref[feat, cols(b)])
    scores = [score(b) for b in blocks[:AHEAD]]
    for i, b in enumerate(blocks):
        if i + AHEAD < n:
            scores.append(score(blocks[i + AHEAD]))
        s = scores[i]
        scores[i] = None
        if add_fn is not None:
            s = add_fn(b, s)
        m_old = m_ref[0:1, cols(b)]
        m_new = jnp.maximum(m_old, jnp.max(s, axis=0, keepdims=True))
        alpha = jnp.exp2(m_old - m_new)
        p = jnp.exp2(s - m_new).astype(_BF)
        acc_ref[:, cols(b)] = alpha * acc_ref[:, cols(b)] + _dot(v_of(b)[:, keys(b)], p)
        m_ref[0:1, cols(b)] = m_new


def _attend_init_t(m_ref, acc_ref):
    m_ref[...] = jnp.full(m_ref.shape, M_INIT, _F32)
    acc_ref[...] = jnp.zeros(acc_ref.shape, _F32)


def _edit_blocks(s, fn, a0=0):
    rows = []
    for a in range(s.shape[0] // LANES):
        pieces = []
        for q in range(s.shape[1] // LANES):
            piece = s[a * LANES:(a + 1) * LANES, q * LANES:(q + 1) * LANES]
            new = fn(a0 + a, q, piece)
            pieces.append(piece if new is None else new)
        rows.append(jnp.concatenate(pieces, axis=1))
    return jnp.concatenate(rows, axis=0)


def _head_values(v_t):
    ones = jnp.ones((MV - HEAD_DIM, v_t.shape[1]), _BF)
    return [jnp.concatenate([v_t[a * HEAD_DIM:(a + 1) * HEAD_DIM], ones], axis=0) for a in range(2)]


def _softmax_init(m_ref, l_ref, acc_ref):
    m_ref[...] = jnp.full(m_ref.shape, M_INIT, _F32)
    l_ref[...] = jnp.zeros(l_ref.shape, _F32)
    acc_ref[...] = jnp.zeros(acc_ref.shape, _F32)


def _online_update(s, pv_fn, m_ref, l_ref, acc_ref):
    m_old = m_ref[...]
    m_new = jnp.maximum(m_old, jnp.max(s, axis=-1, keepdims=True))
    alpha = jnp.exp2(m_old - m_new)
    p = jnp.exp2(s - m_new)
    l_ref[...] = alpha * l_ref[...] + jnp.sum(p, axis=-1, keepdims=True)
    acc_ref[...] = alpha * acc_ref[...] + pv_fn(p.astype(_BF))
    m_ref[...] = m_new


def _masked_softmax_full(s):
    m = jnp.max(s, axis=-1, keepdims=True)
    p = jnp.exp2(s - m)
    l = jnp.sum(p, axis=-1, keepdims=True)
    return jnp.where(m > NEG_HALF, p / l, 0.0)


def _overlap(ncp, n_cmp, n_blk, blk_axis=1):
    shape = (ncp, n_blk) if blk_axis == 1 else (n_blk, ncp)
    c = _iota(shape, 1 - blk_axis)
    j = _iota(shape, blk_axis)
    r = SLC_BLOCK // CMP_STRIDE
    hit = (c >= r * j - (CMP_BLOCK // CMP_STRIDE - 1)) & (c <= r * j + r - 1) & (c < n_cmp)
    return jnp.where(hit, 1.0, 0.0).astype(_BF)


def _select_blocks(imp, t_pos, n_slc, n_top, blk_axis=1):
    j = _iota(imp.shape, blk_axis)
    cur = _div(t_pos, SLC_BLOCK)
    forced = (j == 0) | (j == cur) | (j == cur - 1)
    score = jnp.where(forced, FORCE_SCORE, imp)
    score = jnp.where(j * SLC_BLOCK > t_pos, -FORCE_SCORE, score)
    score = jnp.where(j >= n_slc, M_INIT, score)
    sel = jnp.zeros(imp.shape, jnp.bool_)
    for _ in range(n_top):
        mx = jnp.max(score, axis=blk_axis, keepdims=True)
        idx = jnp.min(jnp.where(score == mx, j, 1 << 20), axis=blk_axis, keepdims=True)
        hit = j == idx
        sel = sel | hit
        score = jnp.where(hit, REMOVED, score)
    return jnp.where(sel, 0.0, NEG)


def _block_onehot(k0, tk, n_lanes):
    s = _iota((tk, n_lanes), 0)
    j = _iota((tk, n_lanes), 1)
    return jnp.where(j == _div(k0 + s, SLC_BLOCK), 1.0, 0.0).astype(_BF)


def _nsa_prompt_kernel(qnt_ref, misct_ref, kc_ref, vct_ref, ks_ref, vst_ref, kw_ref, vwt_ref, dt_ref, nbh_ref, nbl_ref,
                       on_ref, lhst_ref, oct_ref, imp_ref, ms_ref, as_ref, mw_ref, aw_ref, *, n_cmp, ncp, n_slc):
    qt = pl.program_id(1)
    q0 = qt * TQ
    n_col = NSA_HEADS * TQ
    n_cb = TQ // CB
    blocks = list(range(n_col // CB))
    cols = lambda b: slice(b * CB, (b + 1) * CB)
    head_cols = lambda h: slice(h * TQ, (h + 1) * TQ)
    low = _iota((LANES, TQ), 0) < HEAD_DIM

    for g in range(NSA_KV_HEADS):
        for r in range(NSA_GROUP):
            blk = qnt_ref[0, r * LANES:(r + 1) * LANES, :]
            lhst_ref[0:LANES, head_cols(g * NSA_GROUP + r)] = jnp.where(low if g == 0 else jnp.logical_not(low),
                                                                       blk, jnp.zeros_like(blk))

    c = _iota((ncp, LANES), 0)
    u = _iota((ncp, LANES), 1)
    place = ((u < NEAR_U) & (c == CPT * qt - NEAR_BACK + u)) | ((u == NEAR_U) & (c >= CPT * qt + CPT))
    place = jnp.where(place, 1.0, 0.0).astype(_BF)
    kc = kc_ref[0]
    vct = vct_ref[0]
    ovl_t = _overlap(ncp, n_cmp, LANES, blk_axis=0)

    kc_aug = jnp.concatenate([kc, place, place], axis=1)

    def cmp_branch(n_keys):
        def cmp_scores(b):
            rhs = jnp.concatenate([lhst_ref[0:LANES, cols(b)], nbh_ref[:, cols(b)], nbl_ref[:, cols(b)]], axis=0)
            return _dot(kc_aug[0:n_keys], rhs)

        scores = [cmp_scores(b) for b in blocks[:AHEAD]]
        group_mass = {}
        for b in blocks:
            if b + AHEAD < len(blocks):
                scores.append(cmp_scores(b + AHEAD))
            s = scores[b]
            scores[b] = None
            m = jnp.max(s, axis=0, keepdims=True)
            pb = jnp.exp2(s - m)
            rcp = jnp.where(m > NEG_HALF, 1.0 / jnp.sum(pb, axis=0, keepdims=True), 0.0)
            pb = pb.astype(_BF)
            oct_ref[:, cols(b)] = _dot(vct[:, 0:n_keys], pb) * rcp
            mass = _dot(ovl_t[:, 0:n_keys], pb) * rcp
            head, part = divmod(b, n_cb)
            g, r = divmod(head, NSA_GROUP)
            group_mass[(g, part)] = mass if r == 0 else group_mass[(g, part)] + mass
            if r == NSA_GROUP - 1:
                imp_ref[:, g * TQ + part * CB:g * TQ + (part + 1) * CB] = group_mass.pop((g, part))

    n_half = (ncp // 2) // LANES * LANES
    if n_half >= CPT:
        pl.when(CPT * (qt + 1) <= n_half)(functools.partial(cmp_branch, n_half))
        pl.when(CPT * (qt + 1) > n_half)(functools.partial(cmp_branch, ncp))
    else:
        cmp_branch(ncp)

    c2 = _iota((LANES, LANES), 0)
    i2 = _iota((LANES, LANES), 1)
    n_kb = TK // LANES

    def near_keys(kind):
        def key_blocks(b):
            part = b % n_cb
            q_lo, q_hi = part * (CB // LANES), (part + 1) * (CB // LANES) - 1
            if kind == 'diag':
                return 0, min(n_kb, q_hi + 1)
            if kind == 'wprev':
                return q_lo, n_kb
            return 0, n_kb
        return key_blocks

    def near_add(kind):
        def add(b, s):
            head, part = divmod(b, n_cb)
            a0 = near_keys(kind)(b)[0]

            def piece_fn(a, q, x):
                rel = part * (CB // LANES) + q - a + (0 if kind == 'diag' else n_kb)
                if kind == 'diag' and rel < 0:
                    return jnp.full(x.shape, NEG, _F32)
                if kind == 'wprev' and rel > n_kb:
                    return jnp.full(x.shape, NEG, _F32)
                if kind == 'wprev' and rel == n_kb:
                    return jnp.where(c2 > i2, x, NEG)
                if rel == 0:
                    return x + dt_ref[head, 0]
                if rel == 1:
                    return x + dt_ref[head, 1]
                return None
            return _edit_blocks(s, piece_fn, a0)
        return add

    def key_slice(kind):
        def keys_of(b):
            lo, hi = near_keys(kind)(b)
            return slice(lo * LANES, hi * LANES)
        return keys_of

    _attend_init_t(mw_ref, aw_ref)
    qfeat = slice(0, LANES)

    def win_tile(kt, kind):
        k0 = pl.multiple_of(kt * TK, TK)
        v_g = _head_values(vwt_ref[0, kt])
        _attend_tile_t(lhst_ref, kw_ref[0, pl.ds(k0, TK), :], lambda b: v_g[b // (NSA_GROUP * n_cb)], mw_ref, aw_ref,
                       blocks, near_add(kind), feat=qfeat, keys_of=key_slice(kind))

    win_tile(qt, 'diag')

    t_pos = q0 + _mod(_iota((1, NSA_KV_HEADS * TQ), 1), TQ)
    msel = _select_blocks(imp_ref[...], t_pos, n_slc, min(N_SELECT, n_slc), blk_axis=0).astype(_BF)
    for g in range(NSA_KV_HEADS):
        for r in range(NSA_GROUP):
            lhst_ref[LANES:2 * LANES, head_cols(g * NSA_GROUP + r)] = msel[:, g * TQ:(g + 1) * TQ]

    @pl.when(qt >= 1)
    def _():
        win_tile(qt - 1, 'wprev')


    _attend_init_t(ms_ref, as_ref)

    def sel_tile(kt, kind):
        k0 = pl.multiple_of(kt * TK, TK)
        kaug = jnp.concatenate([ks_ref[0, pl.ds(k0, TK), :], _block_onehot(k0, TK, LANES)], axis=1)
        v_g = _head_values(vst_ref[0, kt])
        _attend_tile_t(lhst_ref, kaug, lambda b: v_g[b // (NSA_GROUP * n_cb)], ms_ref, as_ref, blocks,
                       None if kind is None else near_add(kind), keys_of=None if kind is None else key_slice(kind))

    n_far = jnp.maximum(qt - 1, 0)

    def far_pair(i, carry):
        sel_tile(2 * i, None)
        sel_tile(2 * i + 1, None)
        return carry
    lax.fori_loop(0, n_far // 2, far_pair, 0)

    @pl.when(n_far % 2 == 1)
    def _():
        sel_tile(n_far - 1, None)

    @pl.when(qt >= 1)
    def _():
        sel_tile(qt - 1, 'prev')
    sel_tile(qt, 'diag')

    gates = misct_ref[0]
    for r in range(NSA_GROUP):
        parts = []
        for g in range(NSA_KV_HEADS):
            h = g * NSA_GROUP + r
            hc = head_cols(h)
            feat = slice(g * HEAD_DIM, (g + 1) * HEAD_DIM)
            own = slice(0, HEAD_DIM)
            den = slice(HEAD_DIM, HEAD_DIM + 1)
            gate = lambda kind: gates[L_GATE + kind * NSA_HEADS + h:L_GATE + kind * NSA_HEADS + h + 1, :]
            parts.append(gate(0) * oct_ref[feat, hc] + gate(1) * (as_ref[own, hc] / as_ref[den, hc])
                         + gate(2) * (aw_ref[own, hc] / aw_ref[den, hc]))
        on_ref[0, :, r * LANES:(r + 1) * LANES] = jnp.concatenate(parts, axis=0).T


def _nsa_prompt(qnt, misct, kcmp, vcmpt, nsab, vst, kwb, vwt, dt, nbh, nbl, n_cmp, ncp, n_slc):
    b, _, t = qnt.shape
    n_col = NSA_HEADS * TQ
    seq = lambda blk: pl.BlockSpec((1, t, LANES), lambda bi, qi, blk=blk: (bi, 0, blk))
    tiles = pl.BlockSpec((1, t // TK, LANES, TK), lambda bi, qi: (bi, 0, 0, 0))
    const = lambda a: pl.BlockSpec(a.shape, lambda bi, qi: (0,) * a.ndim)
    return pl.pallas_call(
        functools.partial(_nsa_prompt_kernel, n_cmp=n_cmp, ncp=ncp, n_slc=n_slc),
        grid=(b, t // TQ),
        in_specs=[pl.BlockSpec((1, 512, TQ), lambda bi, qi: (bi, 0, qi)),
                  pl.BlockSpec((1, LANES, TQ), lambda bi, qi: (bi, 0, qi)),
                  pl.BlockSpec((1, ncp, LANES), lambda bi, qi: (bi, 0, 0)),
                  pl.BlockSpec((1, LANES, ncp), lambda bi, qi: (bi, 0, 0)),
                  seq(2), tiles, seq(0), tiles, const(dt), const(nbh), const(nbl)],
        out_specs=pl.BlockSpec((1, TQ, 512), lambda bi, qi: (bi, qi, 0)),
        out_shape=jax.ShapeDtypeStruct((b, t, 512), _F32),
        scratch_shapes=[pltpu.VMEM((2 * LANES, n_col), _BF), pltpu.VMEM((LANES, n_col), _F32),
                        pltpu.VMEM((LANES, NSA_KV_HEADS * TQ), _F32),
                        pltpu.VMEM((SUBLANES, n_col), _F32), pltpu.VMEM((MV, n_col), _F32),
                        pltpu.VMEM((SUBLANES, n_col), _F32), pltpu.VMEM((MV, n_col), _F32)],
        compiler_params=_cparams(("arbitrary", "arbitrary")),
        name="nsa_prompt",
    )(qnt, misct, kcmp, vcmpt, nsab, vst, kwb, vwt, dt, nbh, nbl)


def _fox_prompt_kernel(qft_ref, dqt_ref, kf_ref, dk_ref, vft_ref, of_ref, lhst_ref, m_ref, acc_ref, *, tqf):
    p = pl.program_id(1)
    qt = pl.program_id(2)
    row = _iota((LANES, tqf), 0)
    low = row < HEAD_DIM
    qt_ = qft_ref[0]
    dq = dqt_ref[0]
    zero = jnp.zeros_like(qt_)
    one = jnp.ones_like(qt_)
    for a in range(2):
        head = 2 * p + a
        take_neg = (row >= L_NEG + N_PIECE * head) & (row < L_NEG + N_PIECE * (head + 1))
        take_pos = (row >= L_POS + N_PIECE * head) & (row < L_POS + N_PIECE * (head + 1))
        lhst_ref[0:LANES, a * tqf:(a + 1) * tqf] = jnp.where(low if a == 0 else jnp.logical_not(low), qt_, zero)
        lhst_ref[LANES:2 * LANES, a * tqf:(a + 1) * tqf] = jnp.where(take_neg, one, jnp.where(take_pos, dq, zero))
    _attend_init_t(m_ref, acc_ref)
    n_blk = 2 * tqf // CB
    per_q = tqf // TK
    lane = _iota((TK, LANES), 1)
    ones_pos = (jnp.clip(lane - (L_POS - 1), 0, 1) * jnp.clip(2 * L_POS - lane, 0, 1)).astype(_F32).astype(_BF)

    def tile(kt, blocks, add_fn):
        k0 = pl.multiple_of(kt * TK, TK)
        kaug = jnp.concatenate([kf_ref[0, pl.ds(k0, TK), :], jnp.where(lane < L_POS, dk_ref[0, pl.ds(k0, TK), :], ones_pos)],
                               axis=1)
        v_a = _head_values(vft_ref[0, kt])
        _attend_tile_t(lhst_ref, kaug, lambda b: v_a[b * CB // tqf], m_ref, acc_ref, blocks, add_fn)

    def far_body(kt, carry):
        tile(kt, list(range(n_blk)), None)
        return carry
    lax.fori_loop(0, qt * per_q, far_body, 0)

    for j in range(per_q):
        q_lo = lambda b: (b * CB) % tqf
        blocks = [b for b in range(n_blk) if q_lo(b) + CB > j * TK]

        def causal(b, s, j=j):
            if j * TK + TK - 1 <= q_lo(b):
                return s
            kk = j * TK + _iota((TK, CB), 0)
            qq = q_lo(b) + _iota((TK, CB), 1)
            return jnp.where(kk <= qq, s, NEG)
        tile(qt * per_q + j, blocks, causal)

    acc = acc_ref[...]
    o = [acc[0:HEAD_DIM, a * tqf:(a + 1) * tqf] / acc[HEAD_DIM:HEAD_DIM + 1, a * tqf:(a + 1) * tqf] for a in range(2)]
    of_ref[0] = jnp.concatenate(o, axis=0).T


def _fox_prompt(qft, dect, foxb, dec, vft):
    b, _, t = qft.shape
    tqf = min(TQF, t)
    n_pair = FOX_HEADS // 2
    return pl.pallas_call(
        functools.partial(_fox_prompt_kernel, tqf=tqf),
        grid=(b, n_pair, t // tqf),
        in_specs=[pl.BlockSpec((1, LANES, tqf), lambda bi, p, qi: (bi, p, qi)),
                  pl.BlockSpec((1, LANES, tqf), lambda bi, p, qi: (bi, 0, qi)),
                  pl.BlockSpec((1, t, LANES), lambda bi, p, qi: (bi, 0, p)),
                  pl.BlockSpec((1, t, LANES), lambda bi, p, qi: (bi, 0, 0)),
                  pl.BlockSpec((1, t // TK, LANES, TK), lambda bi, p, qi: (bi, 0, p, 0))],
        out_specs=pl.BlockSpec((1, tqf, LANES), lambda bi, p, qi: (bi, qi, p)),
        out_shape=jax.ShapeDtypeStruct((b, t, 512), _F32),
        scratch_shapes=[pltpu.VMEM((2 * LANES, 2 * tqf), _BF), pltpu.VMEM((SUBLANES, 2 * tqf), _F32),
                        pltpu.VMEM((MV, 2 * tqf), _F32)],
        compiler_params=_cparams(("arbitrary", "arbitrary", "arbitrary")),
        name="fox_prompt",
    )(qft, dect, foxb, dec, vft)


def _post_kernel(on_ref, of_ref, x_ref, gn_ref, gf_ref, wo_ref, gp_ref, y_ref):
    half = on_ref.shape[-1]
    a = _rms(on_ref[0], gn_ref[...]).astype(_BF)
    f = _rms(of_ref[0], gf_ref[...]).astype(_BF)
    mixed = _dot(a, wo_ref[0:half, :]) + _dot(f, wo_ref[half:2 * half, :])
    y_ref[0] = x_ref[0] + _rms(mixed, gp_ref[...])


def _post(o_n, o_f, x, gn, gf, wo, gp, tr):
    nb, t, d = x.shape
    half = o_n.shape[-1]
    row = lambda width: pl.BlockSpec((1, tr, width), lambda bi, i: (bi, i, 0))
    const = lambda shape: pl.BlockSpec(shape, lambda bi, i: (0,) * len(shape))
    return pl.pallas_call(
        _post_kernel,
        grid=(nb, t // tr),
        in_specs=[row(half), row(half), row(d), const((1, half)), const((1, half)), const((2 * half, d)), const((1, d))],
        out_specs=row(d),
        out_shape=jax.ShapeDtypeStruct((nb, t, d), _F32),
        compiler_params=_cparams(("arbitrary", "arbitrary")),
        name="out_proj",
    )(o_n, o_f, x, gn, gf, wo, gp)


def _ffn_kernel(x_ref, gpre_ref, wg_ref, wu_ref, wd_ref, gpost_ref, y_ref):
    x = x_ref[0]
    h = _rms(x, gpre_ref[...]).astype(_BF)
    act = (jax.nn.silu(_dot(h, wg_ref[...])) * _dot(h, wu_ref[...])).astype(_BF)
    y_ref[0] = x + _rms(_dot(act, wd_ref[...]), gpost_ref[...])


def _ffn(x, gpre, wg, wu, wd, gpost, tr):
    nb, t, d = x.shape
    dff = wg.shape[1]
    row = pl.BlockSpec((1, tr, d), lambda bi, i: (bi, i, 0))
    const = lambda shape: pl.BlockSpec(shape, lambda bi, i: (0,) * len(shape))
    return pl.pallas_call(
        _ffn_kernel,
        grid=(nb, t // tr),
        in_specs=[row, const((1, d)), const((d, dff)), const((d, dff)), const((dff, d)), const((1, d))],
        out_specs=row,
        out_shape=jax.ShapeDtypeStruct((nb, t, d), _F32),
        compiler_params=_cparams(("arbitrary", "arbitrary")),
        name="ffn",
    )(x, gpre, wg, wu, wd, gpost)


def _row_select(rows8, pieces):
    out = jnp.zeros((SUBLANES, pieces[0].shape[-1]), _F32)
    for i, piece in enumerate(pieces):
        out = jnp.where(rows8 == i, jnp.broadcast_to(piece, out.shape), out)
    return out


def _pad_rows(a, n):
    return jnp.concatenate([a, jnp.zeros((n - a.shape[0],) + a.shape[1:], a.dtype)], axis=0)


def _nsa_sample_kernel(pt_ref, *refs, pps, past, n_new, n_cmp, ncp, n_slc):
    page_refs = refs[:pps]
    (qn_ref, misc_ref, new_ref, win_ref, kwn_ref, w1k_ref, w1v_ref, pe_ref, w1r_ref, w2_ref,
     bc_ref, bs_ref, bw_ref, kwt_ref, on_ref, wout_ref, lk_ref, lv_ref, kst_ref, vst_ref) = refs[pps:]
    del pt_ref
    j = pl.program_id(1)
    n_steps = pl.num_programs(1)
    cpp = PAGE_SIZE // CMP_STRIDE
    ppc = WINDOW // PAGE_SIZE
    n_pages = past // PAGE_SIZE

    rr = _iota((PAGE_SIZE, PAGE_SIZE), 0)
    tok = _iota((PAGE_SIZE, PAGE_SIZE), 1)
    regroup = jnp.where(tok == CMP_STRIDE * _mod(rr, cpp) + _div(rr, cpp), 1.0, 0.0).astype(_BF)

    for k in range(pps):
        pg = j * pps + k
        ref = page_refs[k]
        kst_ref[pg] = ref[0, 0, 2].astype(_BF)
        vst_ref[pg] = ref[0, 0, 3].astype(_BF)
        c0 = pl.multiple_of(pg * cpp, cpp)
        kv_c = jnp.concatenate([ref[0, 0, 0], ref[0, 0, 1]], axis=0).astype(_BF)
        x = _dot_nt(regroup, kv_c)
        for p in range(CMP_STRIDE):
            lk_ref[pl.ds(c0, cpp), p * LANES:(p + 1) * LANES] = x[p * cpp:(p + 1) * cpp, 0:LANES]
            lv_ref[pl.ds(c0, cpp), p * LANES:(p + 1) * LANES] = x[p * cpp:(p + 1) * cpp, LANES:2 * LANES]

    @pl.when(j == n_steps - 1)
    def _():
        nc = past // CMP_STRIDE
        new = new_ref[0]
        ks_new = _pad_rows(new[:, 2 * LANES:3 * LANES], PAGE_SIZE).astype(_BF)
        vs_new = _pad_rows(new[:, 3 * LANES:4 * LANES], PAGE_SIZE).astype(_BF)

        def padc(a):
            return a if ncp == nc else _pad_rows(a, ncp)
        kc = padc(_compress(lk_ref[...].astype(_BF), w1k_ref[...], pe_ref[0], w1r_ref[0], w2_ref[0], n_cmp)).astype(_BF)
        vc = padc(_compress(lv_ref[...].astype(_BF), w1v_ref[...], pe_ref[1], w1r_ref[1], w2_ref[1], n_cmp)).astype(_BF)

        rows8 = _iota((SUBLANES, LANES), 0)
        lane = _iota((SUBLANES, LANES), 1)
        grp_low = rows8 < NSA_GROUP
        q = qn_ref[0].astype(_F32)
        blocks = []
        for i in range(n_new):
            blk = jnp.zeros((SUBLANES, LANES), _F32)
            for r in range(NSA_GROUP):
                piece = jnp.broadcast_to(q[i:i + 1, r * LANES:(r + 1) * LANES], (SUBLANES, LANES))
                blk = jnp.where(_mod(rows8, NSA_GROUP) == r, piece, blk)
            on_group = jnp.logical_not(jnp.logical_xor(lane < HEAD_DIM, grp_low))
            blocks.append(jnp.where(on_group, blk, jnp.zeros_like(blk)))
        lq = jnp.concatenate(blocks, axis=0).astype(_BF)

        pc = _masked_softmax_full(_dot_nt(lq, kc) + bc_ref[...])
        o_c = _dot(pc.astype(_BF), vc)

        nl = 2 * LANES
        ovl_t = _overlap(ncp, n_cmp, nl, blk_axis=0)
        sums = []
        for i in range(n_new):
            blk = pc[i * SUBLANES:(i + 1) * SUBLANES]
            rr8 = _iota(blk.shape, 0)
            for g in range(NSA_KV_HEADS):
                in_group = (rr8 < NSA_GROUP) if g == 0 else (rr8 >= NSA_GROUP)
                sums.append(jnp.sum(jnp.where(in_group, blk, 0.0), axis=0, keepdims=True))
        hi, lo = _split2(_pad_rows(_row_select(_iota((SUBLANES, ncp), 0), sums), LANES))
        imp_t = _dot_nt(ovl_t, hi) + _dot_nt(ovl_t, lo)
        t_pos = past + _div(_iota((1, LANES), 1), NSA_KV_HEADS)
        msel = _select_blocks(imp_t, t_pos, n_slc, min(N_SELECT, n_slc), blk_axis=0).T[0:SUBLANES]
        rows8n = _iota((SUBLANES, nl), 0)
        mrows = []
        for i in range(n_new):
            m0 = jnp.broadcast_to(msel[2 * i:2 * i + 1], (SUBLANES, nl))
            m1 = jnp.broadcast_to(msel[2 * i + 1:2 * i + 2], (SUBLANES, nl))
            mrows.append(jnp.where(rows8n < NSA_GROUP, m0, m1))
        mrows = jnp.concatenate(mrows, axis=0).astype(_BF)

        def block_mask(k0, n_keys):
            s_i = _iota((nl, n_keys), 1)
            j_i = _iota((nl, n_keys), 0)
            return _dot(mrows, jnp.where(j_i == _div(k0 + s_i, SLC_BLOCK), 1.0, 0.0).astype(_BF))

        n_chunk = n_pages // ppc
        kts = lambda ci: jnp.concatenate([kst_ref[ci * ppc + u] for u in range(ppc)], axis=1)
        vts = lambda ci: jnp.concatenate([vst_ref[ci * ppc + u] for u in range(ppc)], axis=1)
        parts = [_dot(lq, kts(ci)) + block_mask(ci * ppc * PAGE_SIZE, ppc * PAGE_SIZE) for ci in range(n_chunk)]
        parts[-1] = parts[-1] + bs_ref[:, 0:WINDOW]
        parts.append(_dot_nt(lq, ks_new) + block_mask(past, PAGE_SIZE) + bs_ref[:, WINDOW:WINDOW + PAGE_SIZE])
        m_s = jnp.max(parts[0], axis=-1, keepdims=True)
        for x in parts[1:]:
            m_s = jnp.maximum(m_s, jnp.max(x, axis=-1, keepdims=True))
        l_s = jnp.zeros_like(m_s)
        acc = jnp.zeros((lq.shape[0], LANES), _F32)
        for ci, x in enumerate(parts):
            pr = jnp.exp2(x - m_s)
            l_s = l_s + jnp.sum(pr, axis=-1, keepdims=True)
            acc = acc + (_dot_nt(pr.astype(_BF), vts(ci)) if ci < n_chunk else _dot(pr.astype(_BF), vs_new))
        o_s = acc / l_s

        kwn = kwn_ref[0]
        kw_new = _pad_rows(kwn[:, 0:LANES], PAGE_SIZE).astype(_BF)
        vw_new = _pad_rows(kwn[:, LANES:2 * LANES], PAGE_SIZE).astype(_BF)
        sw = jnp.concatenate([_dot(lq, win_ref[0, 0, 0].astype(_BF)), _dot_nt(lq, kw_new)], axis=1) + bw_ref[...]
        mw = jnp.max(sw, axis=-1, keepdims=True)
        pw = jnp.exp2(sw - mw)
        pwb = pw.astype(_BF)
        o_w = (_dot_nt(pwb[:, 0:WINDOW], win_ref[0, 0, 1].astype(_BF)) + _dot(pwb[:, WINDOW:], vw_new)) \
            / jnp.sum(pw, axis=-1, keepdims=True)

        lane_w = _iota((LANES, WINDOW), 1)
        for kv in range(2):
            buf = pltpu.roll(win_ref[0, 0, kv], WINDOW - n_new, 1)
            for i in range(n_new):
                col = jnp.broadcast_to(kwt_ref[0, kv * LANES:(kv + 1) * LANES, i:i + 1], (LANES, WINDOW))
                buf = jnp.where(lane_w == WINDOW - n_new + i, col, buf)
            wout_ref[0, kv] = buf

        misc = misc_ref[0]
        out_rows = []
        for i in range(n_new):
            sl = slice(i * SUBLANES, (i + 1) * SUBLANES)
            g_row = jnp.broadcast_to(misc[i:i + 1, :], (SUBLANES, LANES))

            def gcol(kind, g_row=g_row):
                pick = lane == L_GATE + kind * NSA_HEADS + rows8
                return jnp.sum(jnp.where(pick, g_row, 0.0), axis=-1, keepdims=True)
            o_blk = gcol(0) * o_c[sl] + gcol(1) * o_s[sl] + gcol(2) * o_w[sl]
            pieces = [jnp.where(lane[0:1] < HEAD_DIM, o_blk[r:r + 1], o_blk[NSA_GROUP + r:NSA_GROUP + r + 1])
                      for r in range(NSA_GROUP)]
            out_rows.append(jnp.concatenate(pieces, axis=1))
        on_ref[0] = _row_select(_iota((SUBLANES, 4 * LANES), 0), out_rows)


def _nsa_sample(layer, page_table, cache_t, qn, misc, new, win_t, kwn, kwn_t, cw, bc, bs, bw, past, n_new, n_cmp, ncp, n_slc, pps):
    n_seq, n_pages = page_table.shape
    n_steps = n_pages // pps
    nc = past // CMP_STRIDE

    def page_spec(k):
        return pl.BlockSpec((1, 1, 4, LANES, PAGE_SIZE), lambda s, j, pt, k=k: (layer, pt[s, j * pps + k], 0, 0, 0))
    per_seq = lambda a: pl.BlockSpec((1,) + a.shape[1:], lambda s, j, pt: (s,) + (0,) * (a.ndim - 1))
    const = lambda a: pl.BlockSpec(a.shape, lambda s, j, pt: (0,) * a.ndim)
    consts = [cw["w1k"], cw["w1v"], cw["pe8"], cw["w1raw"], cw["w2p"], bc, bs, bw]
    grid_spec = pltpu.PrefetchScalarGridSpec(
        num_scalar_prefetch=1,
        grid=(n_seq, n_steps),
        in_specs=[page_spec(k) for k in range(pps)] + [per_seq(qn), per_seq(misc), per_seq(new),
                                                        pl.BlockSpec((1, 1) + win_t.shape[2:], lambda s, j, pt: (layer, s, 0, 0, 0)),
                                                        per_seq(kwn)]
        + [const(a) for a in consts] + [per_seq(kwn_t)],
        out_specs=(pl.BlockSpec((1, SUBLANES, 4 * LANES), lambda s, j, pt: (s, 0, 0)),
                   pl.BlockSpec((1,) + win_t.shape[2:], lambda s, j, pt: (s, 0, 0, 0))),
        scratch_shapes=[pltpu.VMEM((nc, CMP_STRIDE * LANES), _F32), pltpu.VMEM((nc, CMP_STRIDE * LANES), _F32),
                        pltpu.VMEM((n_pages, LANES, PAGE_SIZE), _BF), pltpu.VMEM((n_pages, LANES, PAGE_SIZE), _BF)],
    )
    return pl.pallas_call(
        functools.partial(_nsa_sample_kernel, pps=pps, past=past, n_new=n_new, n_cmp=n_cmp, ncp=ncp, n_slc=n_slc),
        grid_spec=grid_spec,
        out_shape=(jax.ShapeDtypeStruct((n_seq, SUBLANES, 4 * LANES), _F32),
                   jax.ShapeDtypeStruct((n_seq,) + win_t.shape[2:], _F32)),
        compiler_params=_cparams(("arbitrary", "arbitrary")),
        name="nsa_sample",
    )(page_table, *([cache_t] * pps), qn, misc, new, win_t, kwn, *consts, kwn_t)


def _fox_sample_kernel(pt_ref, *refs, pps, n_new):
    kv_refs = refs[:pps]
    lf_refs = refs[pps:2 * pps]
    (qf_ref, kvn_ref, lfn_ref, of_ref, q_ref, e_ref, carry_ref, m_ref, l_ref, acc_ref) = refs[2 * pps:]
    del pt_ref
    j = pl.program_id(1)
    n_steps = pl.num_programs(1)
    rows = FOX_HEADS * n_new
    width = FOX_HEADS * HEAD_DIM
    srow = _iota((PAGE_SIZE, PAGE_SIZE), 0)
    scol = _iota((PAGE_SIZE, PAGE_SIZE), 1)
    later = jnp.where(srow > scol, 1.0, 0.0).astype(_BF)

    def suffix(lf):
        hi, mid, lo = _split3(lf)
        return _dot(hi, later) + _dot(mid, later) + _dot(lo, later)

    def decay(rt):
        return (jnp.concatenate([rt] * n_new, axis=0) - e_ref[...]) * LOG2E

    @pl.when(j == 0)
    def _():
        _softmax_init(m_ref, l_ref, acc_ref)
        rows8 = _iota((SUBLANES, width), 0)
        lane = _iota((SUBLANES, width), 1)
        q = qf_ref[0].astype(_F32)
        blocks = []
        for i in range(n_new):
            piece = jnp.broadcast_to(q[i:i + 1, :], (SUBLANES, width))
            blocks.append(jnp.where(_div(lane, HEAD_DIM) == rows8, piece, jnp.zeros_like(piece)))
        q_ref[...] = jnp.concatenate(blocks, axis=0).astype(_BF)
        lfn = lfn_ref[0]
        rt = suffix(lfn)
        e_ref[...] = jnp.concatenate([rt[:, i:i + 1] for i in range(n_new)], axis=0)
        carry_ref[...] = jnp.broadcast_to(jnp.sum(lfn, axis=-1, keepdims=True), carry_ref.shape)
        kvn = _pad_rows(kvn_ref[0], PAGE_SIZE)
        ri = _div(_iota((rows, PAGE_SIZE), 0), SUBLANES)
        ci = _iota((rows, PAGE_SIZE), 1)
        s = _dot_nt(q_ref[...], kvn[:, 0:width].astype(_BF)) + decay(rt) + jnp.where(ci <= ri, 0.0, NEG)
        v_new = kvn[:, width:2 * width].astype(_BF)
        _online_update(s, lambda p: _dot(p, v_new), m_ref, l_ref, acc_ref)

    kts, vts, rts = [], [], []
    carry = carry_ref[...]
    for k in range(pps):
        lf = lf_refs[k][0, 0]
        kts.append(kv_refs[k][0, 0, 0].astype(_BF))
        vts.append(kv_refs[k][0, 0, 1].astype(_BF))
        rts.append(suffix(lf) + carry)
        carry = carry + jnp.sum(lf, axis=-1, keepdims=True)
    carry_ref[...] = carry
    vt = jnp.concatenate(vts, axis=1)
    s = _dot(q_ref[...], jnp.concatenate(kts, axis=1)) + decay(jnp.concatenate(rts, axis=1))
    _online_update(s, lambda p: _dot_nt(p, vt), m_ref, l_ref, acc_ref)

    @pl.when(j == n_steps - 1)
    def _():
        o = acc_ref[...] / l_ref[...]
        rows8 = _iota((SUBLANES, width), 0)
        lane = _iota((SUBLANES, width), 1)
        out_rows = []
        for i in range(n_new):
            blk = jnp.where(_div(lane, HEAD_DIM) == rows8, o[i * SUBLANES:(i + 1) * SUBLANES], 0.0)
            out_rows.append(jnp.sum(blk, axis=0, keepdims=True))
        of_ref[0] = _row_select(rows8, out_rows)


def _fox_sample(layer, page_table, cache_t, cache_lft, qf, kvn, lfn, n_new, pps):
    n_seq, n_pages = page_table.shape
    n_steps = n_pages // pps
    rows = FOX_HEADS * n_new
    width = FOX_HEADS * HEAD_DIM

    def page_idx(s, j, pt, k):
        return pt[s, n_pages - 1 - (j * pps + k)]
    kv_spec = lambda k: pl.BlockSpec((1, 1, 2, width, PAGE_SIZE),
                                     lambda s, j, pt, k=k: (layer, page_idx(s, j, pt, k), 0, 0, 0))
    lf_spec = lambda k: pl.BlockSpec((1, 1, FOX_HEADS, PAGE_SIZE),
                                     lambda s, j, pt, k=k: (layer, page_idx(s, j, pt, k), 0, 0))
    per_seq = lambda a: pl.BlockSpec((1,) + a.shape[1:], lambda s, j, pt: (s,) + (0,) * (a.ndim - 1))
    grid_spec = pltpu.PrefetchScalarGridSpec(
        num_scalar_prefetch=1,
        grid=(n_seq, n_steps),
        in_specs=[kv_spec(k) for k in range(pps)] + [lf_spec(k) for k in range(pps)] + [per_seq(a) for a in (qf, kvn, lfn)],
        out_specs=pl.BlockSpec((1, SUBLANES, width), lambda s, j, pt: (s, 0, 0)),
        scratch_shapes=[pltpu.VMEM((rows, width), _BF), pltpu.VMEM((rows, 1), _F32), pltpu.VMEM((SUBLANES, LANES), _F32),
                        pltpu.VMEM((rows, 1), _F32), pltpu.VMEM((rows, 1), _F32), pltpu.VMEM((rows, width), _F32)],
    )
    return pl.pallas_call(
        functools.partial(_fox_sample_kernel, pps=pps, n_new=n_new),
        grid_spec=grid_spec,
        out_shape=jax.ShapeDtypeStruct((n_seq, SUBLANES, width), _F32),
        compiler_params=_cparams(("arbitrary", "arbitrary")),
        name="fox_sample",
    )(page_table, *([cache_t] * pps), *([cache_lft] * pps), qf, kvn, lfn)


def _nsa_perm():
    idx = np.zeros(NSA_HEADS * HEAD_DIM, np.int32)
    for r in range(NSA_GROUP):
        for g in range(NSA_KV_HEADS):
            for d in range(HEAD_DIM):
                idx[r * LANES + g * HEAD_DIM + d] = (g * NSA_GROUP + r) * HEAD_DIM + d
    return idx


def _layer_weights(l, w_in, b_gate, b_forget, cmp_pe, cmp_w1, cmp_w2, grp_norm_nsa, grp_norm_fox, w_o):
    perm = _nsa_perm()
    w = w_in[l]
    o_qn, o_nsa, o_kw, o_g, o_qf, o_kf, o_f = 0, 512, 1024, 1280, 1304, 1816, 2840
    f_cols = w[:, o_f:o_f + FOX_HEADS]
    misc = jnp.concatenate([w[:, o_g:o_g + 3 * NSA_HEADS], f_cols, f_cols,
                            jnp.zeros((w.shape[0], LANES - 3 * NSA_HEADS - 2 * FOX_HEADS), w.dtype)], axis=1)
    wp = jnp.concatenate([w[:, o_qn:o_nsa][:, perm], w[:, o_nsa:o_kw], w[:, o_kw:o_g], w[:, o_qf:o_kf],
                          w[:, o_kf:o_f], misc], axis=1).astype(_BF)
    bias = jnp.concatenate([b_gate[l].reshape(-1), b_forget[l], b_forget[l],
                            jnp.zeros((LANES - 3 * NSA_HEADS - 2 * FOX_HEADS,), _F32)]).reshape(1, LANES)

    def w1_layout(w1):
        w1 = w1.reshape(2, CMP_STRIDE, HEAD_DIM, CMP_HIDDEN)
        z = jnp.zeros((CMP_STRIDE, HEAD_DIM, CMP_HIDDEN), w1.dtype)
        g0 = jnp.concatenate([w1[0], w1[1], z, z], axis=-1)
        g1 = jnp.concatenate([z, z, w1[0], w1[1]], axis=-1)
        return jnp.concatenate([g0, g1], axis=1).reshape(CMP_STRIDE * LANES, 4 * CMP_HIDDEN).astype(_BF)

    def w2_layout(w2):
        z = jnp.zeros_like(w2)
        return jnp.concatenate([jnp.concatenate([w2, z], axis=1), jnp.concatenate([z, w2], axis=1)], axis=0).astype(_BF)
    cw = dict(
        w1k=w1_layout(cmp_w1[l, 0]), w1v=w1_layout(cmp_w1[l, 1]),
        pe8=jnp.broadcast_to(cmp_pe[l].reshape(2, 1, CMP_BLOCK * HEAD_DIM), (2, SUBLANES, CMP_BLOCK * HEAD_DIM)).astype(_BF),
        w1raw=cmp_w1[l].astype(_BF),
        w2p=jnp.stack([w2_layout(cmp_w2[l, 0]), w2_layout(cmp_w2[l, 1])]),
    )
    gn = grp_norm_nsa[l][perm].reshape(1, -1)
    gf = grp_norm_fox[l].reshape(1, -1)
    wo = jnp.concatenate([w_o[l][:NSA_HEADS * HEAD_DIM][perm], w_o[l][NSA_HEADS * HEAD_DIM:]], axis=0).astype(_BF)
    return wp, bias, cw, gn, gf, wo


def kernel(x_prompt, x_sample, cache_nsa_kv, cache_fox_kv, cache_fox_logf, state_win_kv, page_table, rel_bias,
           norm_mix_pre, norm_mix_post, norm_ffn_pre, norm_ffn_post, w_in, b_gate, b_forget, cmp_pe, cmp_w1, cmp_w2,
           grp_norm_nsa, grp_norm_fox, w_o, w_ffn_gate, w_ffn_up, w_ffn_down):
    depth = w_in.shape[0]
    b, t, d = x_prompt.shape
    n_seq, n_new, _ = x_sample.shape
    n_pages = page_table.shape[1]
    past = n_pages * PAGE_SIZE
    n_pool = cache_nsa_kv.shape[1]
    n_win = state_win_kv.shape[2]
    assert t % min(TQF, t) == 0 and min(TQF, t) % TK == 0 and t % TQ == 0 and TQ == TK == WINDOW and TQ % CB == 0
    assert CB % LANES == 0 and LANES > T5_FAR
    assert n_new <= SUBLANES and SLC_BLOCK >= n_new and n_win == WINDOW
    assert n_pages % (WINDOW // PAGE_SIZE) == 0 and past >= 2 * WINDOW
    assert (n_seq * n_new) % SUBLANES == 0

    nc_p = t // CMP_STRIDE
    ncp_p = -(-nc_p // LANES) * LANES
    n_slc_p = t // SLC_BLOCK
    assert n_slc_p <= LANES
    nc_s = past // CMP_STRIDE
    ncp_s = -(-nc_s // LANES) * LANES
    n_slc_s = past // SLC_BLOCK + 1
    assert n_slc_s <= 2 * LANES
    pps = math.gcd(n_pages, 32)
    pps_fox = math.gcd(n_pages, 32)

    dt, nbh, nbl = _bias_prompt(rel_bias)
    bc, bs, bw = _bias_sample(rel_bias, past, n_new, nc_s - 1, ncp_s)

    nsa_t = jnp.transpose(cache_nsa_kv, (0, 1, 3, 4, 5, 2)).reshape(depth, n_pool, 4, NSA_KV_HEADS * HEAD_DIM, PAGE_SIZE)
    fox_t = jnp.transpose(cache_fox_kv, (0, 1, 3, 4, 5, 2)).reshape(depth, n_pool, 2, FOX_HEADS * HEAD_DIM, PAGE_SIZE)
    logf_t = jnp.transpose(cache_fox_logf, (0, 1, 3, 2))
    win_t = jnp.transpose(state_win_kv, (0, 1, 3, 4, 5, 2)).reshape(depth, n_seq, 2, NSA_KV_HEADS * HEAD_DIM, n_win)

    row1 = lambda a: a.reshape(1, -1)
    pad_new = lambda a: jnp.pad(a, ((0, 0), (0, SUBLANES - n_new), (0, 0)))
    rs = n_seq * n_new
    xp, xs = x_prompt, x_sample.reshape(1, rs, d)
    outs_p, outs_s = [], []
    for l in range(depth):
        wp, bias, cw, gn, gf, wo = _layer_weights(l, w_in, b_gate, b_forget, cmp_pe, cmp_w1, cmp_w2,
                                                  grp_norm_nsa, grp_norm_fox, w_o)
        wg, wu, wd = w_ffn_gate[l].astype(_BF), w_ffn_up[l].astype(_BF), w_ffn_down[l].astype(_BF)
        g_pre, g_post = row1(norm_mix_pre[l]), row1(norm_mix_post[l])
        g_fpre, g_fpost = row1(norm_ffn_pre[l]), row1(norm_ffn_post[l])

        (_, nsa, nsab, kw, kwb, _, fox, foxb, misc, dec,
         qnt, qft, dect, misct, vst, vwt, vft) = _proj(xp, g_pre, wp, bias, TR_PROJ, True)
        kcmp, vcmpt = _compress_prompt(nsa, cw, nc_p, ncp_p)
        o_n = _nsa_prompt(qnt, misct, kcmp, vcmpt, nsab, vst, kwb, vwt, dt, nbh, nbl, nc_p - 1, ncp_p, n_slc_p)
        o_f = _fox_prompt(qft, dect, foxb, dec, vft)
        xp = _post(o_n, o_f, xp, gn, gf, wo, g_post, TR_POST)
        xp = _ffn(xp, g_fpre, wg, wu, wd, g_fpost, TR_FFN)
        outs_p.append((nsa.reshape(b, t, 4, NSA_KV_HEADS, HEAD_DIM), fox.reshape(b, t, 2, FOX_HEADS, HEAD_DIM),
                       misc[:, :, L_LOGF:L_LOGF + FOX_HEADS],
                       kw[:, t - WINDOW:].reshape(b, WINDOW, 2, NSA_KV_HEADS, HEAD_DIM)))

        tr_s = math.gcd(rs, TR_PROJ)
        qn, nsa, nsab, kw, kwb, qf, fox, foxb, misc = _proj(xs, g_pre, wp, bias, tr_s, False)
        per = lambda a: a.reshape(n_seq, n_new, a.shape[-1])
        nsa_s, fox_s, kw_s, misc_s = per(nsa), per(fox), per(kw), per(misc)
        logf_s = misc_s[:, :, L_LOGF:L_LOGF + FOX_HEADS]
        kw_st = jnp.pad(jnp.swapaxes(kw_s, 1, 2), ((0, 0), (0, 0), (0, LANES - n_new)))
        o_n, win_new = _nsa_sample(l, page_table, nsa_t, pad_new(per(qn)), pad_new(misc_s), pad_new(nsa_s), win_t,
                                   pad_new(kw_s), kw_st, cw, bc, bs, bw, past, n_new, nc_s - 1, ncp_s, n_slc_s, pps)
        kvn = pad_new(fox_s)
        lfn = jnp.pad(jnp.swapaxes(logf_s, 1, 2), ((0, 0), (0, 0), (0, PAGE_SIZE - n_new)))
        o_f = _fox_sample(l, page_table, fox_t, logf_t, pad_new(per(qf)), kvn, lfn, n_new, pps_fox)
        o_n = o_n[:, :n_new].reshape(1, rs, -1)
        o_f = o_f[:, :n_new].reshape(1, rs, -1)
        xs = _post(o_n, o_f, xs, gn, gf, wo, g_post, math.gcd(rs, TR_POST))
        xs = _ffn(xs, g_fpre, wg, wu, wd, g_fpost, math.gcd(rs, TR_FFN))
        win_new = jnp.transpose(win_new.reshape(n_seq, 2, NSA_KV_HEADS, HEAD_DIM, n_win), (0, 4, 1, 2, 3))
        outs_s.append((nsa_s.reshape(n_seq, n_new, 4, NSA_KV_HEADS, HEAD_DIM),
                       fox_s.reshape(n_seq, n_new, 2, FOX_HEADS, HEAD_DIM), logf_s, win_new))

    stack = lambda outs, i: jnp.stack([o[i] for o in outs], axis=0)
    return (xp, xs.reshape(n_seq, n_new, d), stack(outs_p, 0), stack(outs_p, 1), stack(outs_p, 2), stack(outs_p, 3),
            stack(outs_s, 0), stack(outs_s, 1), stack(outs_s, 2), stack(outs_s, 3))
```

```python
import functools
import math

import numpy as np
import jax
import jax.numpy as jnp
from jax import lax
from jax.experimental import pallas as pl
from jax.experimental.pallas import tpu as pltpu

HEAD_DIM = 64
NSA_HEADS = 8
FOX_HEADS = 8
NSA_KV_HEADS = 2
NSA_GROUP = NSA_HEADS // NSA_KV_HEADS
CMP_BLOCK = 32
CMP_STRIDE = 16
CMP_HIDDEN = 2 * HEAD_DIM
SLC_BLOCK = 64
N_SELECT = 16
WINDOW = 512
T5_BUCKETS = 32
T5_EXACT = T5_BUCKETS // 2
T5_MAX_DIST = 128
PAGE_SIZE = 128
NORM_EPS = 1e-6
FORCE_SCORE = 1e4
LOG2E = math.log2(math.e)
Q_SCALE = HEAD_DIM ** -0.5 * LOG2E

LANES = 128
SUBLANES = 8
VMEM_LIMIT = 56 * 1024 * 1024

_F32 = jnp.float32
_BF = jnp.bfloat16
NEG = -(2.0 ** 100)
NEG_HALF = -(2.0 ** 99)
M_INIT = -3.0e38
REMOVED = -3.4e38


def _t5_thresholds():
    n = np.arange(1, 4 * T5_MAX_DIST)
    large = T5_EXACT + (np.log(n / T5_EXACT) / math.log(T5_MAX_DIST / T5_EXACT) * (T5_BUCKETS - T5_EXACT)).astype(np.int64)
    return tuple(int(n[np.argmax(large >= k)]) for k in range(T5_EXACT + 1, T5_BUCKETS))


_T5_THR = _t5_thresholds()
T5_FAR = _T5_THR[-1]


def _log2(n):
    assert n & (n - 1) == 0
    return n.bit_length() - 1


TQ = 512
TQF = 4096
TK = 512
CB = 256
AHEAD = 3
MV = HEAD_DIM + SUBLANES
TR_PROJ = 256
TR_POST = 512
TR_FFN = 256
CPT = TQ // CMP_STRIDE
NEAR_BACK = -(-(T5_FAR + CMP_BLOCK - 1) // CMP_STRIDE) - 1
NEAR_U = CPT + NEAR_BACK

C_QN, C_NSA, C_KW, C_QF, C_FOX, C_MISC = 0, 512, 1024, 1280, 1792, 2816
C_TOT = 2944
L_GATE, L_LOGF, L_CUM = 0, 24, 32
N_PIECE = 3
L_NEG, L_POS = 0, N_PIECE * FOX_HEADS


def _dot(a, b):
    return jnp.dot(a, b, preferred_element_type=_F32)


def _dot_nt(a, b):
    return lax.dot_general(a, b, (((1,), (1,)), ((), ())), preferred_element_type=_F32)


def _split3(x):
    hi = x.astype(_BF)
    r1 = x - hi.astype(_F32)
    mid = r1.astype(_BF)
    lo = (r1 - mid.astype(_F32)).astype(_BF)
    return hi, mid, lo


def _split2(x):
    hi = x.astype(_BF)
    return hi, (x - hi.astype(_F32)).astype(_BF)


def _dot3(a_bf, x):
    hi, mid, lo = _split3(x)
    return _dot(a_bf, hi) + _dot(a_bf, mid) + _dot(a_bf, lo)


def _rms(x, g):
    ms = jnp.mean(x * x, axis=-1, keepdims=True)
    return x * lax.rsqrt(ms + NORM_EPS) * g


def _gelu_tanh(x):
    c = math.sqrt(2.0 / math.pi)
    return x * (0.5 * (1.0 + jnp.tanh(c * (x + 0.044715 * (x * x * x)))))


def _iota(shape, dim):
    return lax.broadcasted_iota(jnp.int32, shape, dim)


def _div(x, n):
    return jnp.right_shift(x, _log2(n))


def _mod(x, n):
    return x & (n - 1)


def _cparams(sem):
    return pltpu.CompilerParams(dimension_semantics=sem, vmem_limit_bytes=VMEM_LIMIT)


def _t5_rel(dist, rel_ref, h):
    d = jnp.minimum(dist, T5_MAX_DIST - 1)
    big = jnp.full(d.shape, T5_EXACT, jnp.int32)
    for thr in _T5_THR:
        big = big + jnp.where(d >= thr, 1, 0)
    bkt = jnp.where(d < T5_EXACT, d, big)
    far = rel_ref[T5_BUCKETS - 1, h]
    val = jnp.zeros(d.shape, _F32)
    for k in range(T5_BUCKETS - 1):
        val = jnp.where(bkt == k, (rel_ref[k, h] - far) * LOG2E, val)
    return val


def _t5_masked(dist, rel_ref, h):
    return jnp.where(dist < 0, NEG, _t5_rel(jnp.maximum(dist, 0), rel_ref, h))


def _bias_prompt_kernel(rel_ref, dt_ref, nbh_ref, nbl_ref):
    def body(h, carry):
        c = _iota((LANES, LANES), 0)
        i = _iota((LANES, LANES), 1)
        dt_ref[h, 0] = _t5_masked(i - c, rel_ref, h)
        dt_ref[h, 1] = _t5_masked(i - c + LANES, rel_ref, h)
        u = _iota((LANES, TQ), 0)
        i = _iota((LANES, TQ), 1)
        near = _t5_masked(i + (CMP_STRIDE * NEAR_BACK - (CMP_BLOCK - 1)) - CMP_STRIDE * u, rel_ref, h)
        nb = jnp.where(u < NEAR_U, near, jnp.where(u == NEAR_U, NEG, 0.0))
        hi, lo = _split2(nb)
        cols = pl.ds(pl.multiple_of(h * TQ, TQ), TQ)
        nbh_ref[:, cols] = hi
        nbl_ref[:, cols] = lo
        return carry
    lax.fori_loop(0, NSA_HEADS, body, 0)


def _bias_prompt(rel_bias):
    return pl.pallas_call(
        _bias_prompt_kernel,
        out_shape=(jax.ShapeDtypeStruct((NSA_HEADS, 2, LANES, LANES), _F32),
                   jax.ShapeDtypeStruct((LANES, NSA_HEADS * TQ), _BF),
                   jax.ShapeDtypeStruct((LANES, NSA_HEADS * TQ), _BF)),
        in_specs=[pl.BlockSpec(memory_space=pltpu.SMEM)],
        name="t5_bias_prompt",
    )(rel_bias)


def _bias_sample_kernel(rel_ref, bc_ref, bs_ref, bw_ref, *, past, n_new, n_cmp):
    def table(shape, dist_fn, extra_invalid=None):
        out = jnp.zeros(shape, _F32)
        r = _iota(shape, 0)
        c = _iota(shape, 1)
        i = _div(r, NSA_HEADS)
        dist = dist_fn(i, c)
        for h in range(NSA_HEADS):
            v = _t5_masked(dist, rel_ref, h)
            out = jnp.where(_mod(r, NSA_HEADS) == h, v, out)
        if extra_invalid is not None:
            out = jnp.where(extra_invalid(i, c, dist), NEG, out)
        return out

    bc_ref[...] = table(bc_ref.shape, lambda i, c: past + i - (CMP_STRIDE * c + CMP_BLOCK - 1),
                        lambda i, c, d: c >= n_cmp)
    bs_ref[...] = table(bs_ref.shape, lambda i, c: i + WINDOW - c)
    bw_ref[...] = table(bw_ref.shape, lambda i, c: i + WINDOW - c,
                        lambda i, c, d: (d >= WINDOW) | (c >= WINDOW + n_new))


def _bias_sample(rel_bias, past, n_new, n_cmp, ncp):
    rows = NSA_HEADS * n_new
    return pl.pallas_call(
        functools.partial(_bias_sample_kernel, past=past, n_new=n_new, n_cmp=n_cmp),
        out_shape=(jax.ShapeDtypeStruct((rows, ncp), _F32),
                   jax.ShapeDtypeStruct((rows, WINDOW + PAGE_SIZE), _F32),
                   jax.ShapeDtypeStruct((rows, WINDOW + PAGE_SIZE), _F32)),
        in_specs=[pl.BlockSpec(memory_space=pltpu.SMEM)],
        name="t5_bias_sample",
    )(rel_bias)


def _proj_kernel(x_ref, g_ref, w_ref, b_ref, *refs, tr, prompt):
    if prompt:
        (qn_ref, nsa_ref, nsab_ref, kw_ref, kwb_ref, qf_ref, fox_ref, foxb_ref, misc_ref, dec_ref,
         qnt_ref, qft_ref, dect_ref, misct_ref, vst_ref, vwt_ref, vft_ref, carry_ref) = refs
    else:
        qn_ref, nsa_ref, nsab_ref, kw_ref, kwb_ref, qf_ref, fox_ref, foxb_ref, misc_ref = refs
    x = x_ref[0]
    h = _rms(x, g_ref[...]).astype(_BF)
    qn = _dot(h, w_ref[:, C_QN:C_NSA]) * Q_SCALE
    qn_ref[0] = qn.astype(_BF)
    z_nsa = _dot(h, w_ref[:, C_NSA:C_KW])
    nsa_ref[0] = z_nsa
    nsab_ref[0] = z_nsa.astype(_BF)
    z_kw = _dot(h, w_ref[:, C_KW:C_QF])
    kw_ref[0] = z_kw
    kwb_ref[0] = z_kw.astype(_BF)
    qf = _dot(h, w_ref[:, C_QF:C_FOX]) * Q_SCALE
    qf_ref[0] = qf.astype(_BF)
    z_fox = _dot(h, w_ref[:, C_FOX:C_MISC])
    fox_ref[0] = z_fox
    foxb_ref[0] = z_fox.astype(_BF)
    zm = _dot(h, w_ref[:, C_MISC:C_TOT]) + b_ref[...]
    lane = _iota((tr, LANES), 1)
    sg = jax.nn.sigmoid(zm)
    ls = jnp.minimum(zm, 0.0) - jnp.log1p(jnp.exp(-jnp.abs(zm)))
    if not prompt:
        misc_ref[0] = jnp.where(lane < L_LOGF, sg, jnp.where(lane < L_CUM, ls, 0.0))
        return

    @pl.when(pl.program_id(1) == 0)
    def _():
        carry_ref[...] = jnp.zeros(carry_ref.shape, _F32)
    row = _iota((tr, tr), 0)
    col = _iota((tr, tr), 1)
    tri = jnp.where(col <= row, 1.0, 0.0).astype(_BF)
    cs = _dot3(tri, ls) + carry_ref[0:1, :]
    carry_ref[...] = jnp.broadcast_to(cs[tr - 1:tr, :], carry_ref.shape)
    pieces = _split3(cs * LOG2E)
    r = _iota((LANES, LANES), 0) - L_CUM
    c = _iota((LANES, LANES), 1)
    head_row = (r >= 0) & (r < FOX_HEADS)
    dec = jnp.zeros((tr, LANES), _F32)
    for j, piece in enumerate(pieces):
        put = jnp.where(head_row & (c == L_NEG + N_PIECE * r + j), -1.0,
                        jnp.where(head_row & (c == L_POS + N_PIECE * r + j), 1.0, 0.0)).astype(_BF)
        dec = dec + _dot(piece, put)
    dec_ref[0] = dec.astype(_BF)
    misc = jnp.where(lane < L_LOGF, sg, jnp.where(lane < L_CUM, ls, jnp.where(lane < L_CUM + FOX_HEADS, cs, 0.0)))
    misc_ref[0] = misc
    qnt_ref[0] = qn.T.astype(_BF)
    qft_ref[0] = qf.T.astype(_BF)
    dect_ref[0] = dec.T.astype(_BF)
    misct_ref[0] = misc.T
    vst_ref[0, 0] = z_nsa[:, 3 * LANES:4 * LANES].T.astype(_BF)
    vwt_ref[0, 0] = z_kw[:, LANES:2 * LANES].T.astype(_BF)
    vft_ref[0, 0] = z_fox[:, FOX_HEADS * HEAD_DIM:].T.astype(_BF)


def _proj(x, g, w, b, tr, prompt):
    nb, t, d = x.shape
    grid = (nb, t // tr)
    row = lambda width: pl.BlockSpec((1, tr, width), lambda bi, i: (bi, i, 0))
    const = lambda shape: pl.BlockSpec(shape, lambda bi, i: (0,) * len(shape))
    shp = lambda width, dt: jax.ShapeDtypeStruct((nb, t, width), dt)
    out_specs = [row(512), row(512), row(512), row(256), row(256), row(512), row(1024), row(1024), row(LANES)]
    out_shape = [shp(512, _BF), shp(512, _F32), shp(512, _BF), shp(256, _F32), shp(256, _BF), shp(512, _BF),
                 shp(1024, _F32), shp(1024, _BF), shp(LANES, _F32)]
    scratch = []
    if prompt:
        assert TK % tr == 0 and t % TK == 0
        per = TK // tr
        colm = lambda rows: pl.BlockSpec((1, rows, tr), lambda bi, i: (bi, 0, i))
        tile = lambda rows: pl.BlockSpec((1, 1, rows, tr), lambda bi, i: (bi, i // per, 0, i % per))
        tshp = lambda rows, dt: jax.ShapeDtypeStruct((nb, rows, t), dt)
        t4 = lambda rows: jax.ShapeDtypeStruct((nb, t // TK, rows, TK), _BF)
        out_specs += [row(LANES), colm(512), colm(512), colm(LANES), colm(LANES), tile(LANES), tile(LANES), tile(512)]
        out_shape += [shp(LANES, _BF), tshp(512, _BF), tshp(512, _BF), tshp(LANES, _BF), tshp(LANES, _F32),
                      t4(LANES), t4(LANES), t4(512)]
        scratch = [pltpu.VMEM((SUBLANES, LANES), _F32)]
    return pl.pallas_call(
        functools.partial(_proj_kernel, tr=tr, prompt=prompt),
        grid=grid,
        in_specs=[row(d), const((1, d)), const((d, C_TOT)), const((1, LANES))],
        out_specs=tuple(out_specs),
        out_shape=tuple(out_shape),
        scratch_shapes=scratch,
        compiler_params=_cparams(("arbitrary", "arbitrary")),
        name="in_proj",
    )(x, g, w, b)


def _compress(lhs_bf, w1p, pe8, w1raw, w2p, n_valid):
    nc = lhs_bf.shape[0]
    hcat = _dot(lhs_bf, w1p)
    cst = _dot(pe8, w1raw)[0:1]

    def hidden(g):
        a = hcat[:, g * 256:g * 256 + CMP_HIDDEN]
        b = hcat[:, g * 256 + CMP_HIDDEN:(g + 1) * 256]
        return _gelu_tanh(a + pltpu.roll(b, nc - 1, 0) + cst)

    hh = jnp.concatenate([hidden(0), hidden(1)], axis=1).astype(_BF)
    out = _dot(hh, w2p)
    return jnp.where(_iota(out.shape, 0) < n_valid, out, 0.0)


def _compress_prompt_kernel(xk_ref, xv_ref, w1k_ref, w1v_ref, pe_ref, w1r_ref, w2_ref, kc_ref, vc_ref, *, nc, ncp):
    for idx, (x_ref, w1, out_ref) in enumerate(((xk_ref, w1k_ref, kc_ref), (xv_ref, w1v_ref, vc_ref))):
        pieces = [x_ref[0, pl.ds(p, nc, stride=CMP_STRIDE), :] for p in range(CMP_STRIDE)]
        lhs = jnp.concatenate(pieces, axis=1).astype(_BF)
        out = _compress(lhs, w1[...], pe_ref[idx], w1r_ref[idx], w2_ref[idx], nc - 1)
        if ncp > nc:
            out = jnp.concatenate([out, jnp.zeros((ncp - nc, LANES), _F32)], axis=0)
        out_ref[0] = (out.T if idx == 1 else out).astype(_BF)


def _compress_prompt(nsa_state, cw, nc, ncp):
    b, t, _ = nsa_state.shape
    const = lambda a: pl.BlockSpec(a.shape, lambda bi: (0,) * a.ndim)
    return pl.pallas_call(
        functools.partial(_compress_prompt_kernel, nc=nc, ncp=ncp),
        grid=(b,),
        in_specs=[pl.BlockSpec((1, t, LANES), lambda bi: (bi, 0, 0)), pl.BlockSpec((1, t, LANES), lambda bi: (bi, 0, 1)),
                  const(cw["w1k"]), const(cw["w1v"]), const(cw["pe8"]), const(cw["w1raw"]), const(cw["w2p"])],
        out_specs=(pl.BlockSpec((1, ncp, LANES), lambda bi: (bi, 0, 0)), pl.BlockSpec((1, LANES, ncp), lambda bi: (bi, 0, 0))),
        out_shape=(jax.ShapeDtypeStruct((b, ncp, LANES), _BF), jax.ShapeDtypeStruct((b, LANES, ncp), _BF)),
        compiler_params=_cparams(("arbitrary",)),
        name="compress_prompt",
    )(nsa_state, nsa_state, cw["w1k"], cw["w1v"], cw["pe8"], cw["w1raw"], cw["w2p"])


def _attend_tile_t(lhst_ref, k_t, v_of, m_ref, acc_ref, blocks, add_fn=None, feat=slice(None), keys_of=None):
    n = len(blocks)
    cols = lambda b: slice(b * CB, (b + 1) * CB)
    keys = (lambda b: slice(None)) if keys_of is None else keys_of
    score = lambda b: _dot(k_t[keys(b), :], lhst_ref[feat, cols(b)])
    scores = [score(b) for b in blocks[:AHEAD]]
    for i, b in enumerate(blocks):
        if i + AHEAD < n:
            scores.append(score(blocks[i + AHEAD]))
        s = scores[i]
        scores[i] = None
        if add_fn is not None:
            s = add_fn(b, s)
        m_old = m_ref[0:1, cols(b)]
        m_new = jnp.maximum(m_old, jnp.max(s, axis=0, keepdims=True))
        alpha = jnp.exp2(m_old - m_new)
        p = jnp.exp2(s - m_new).astype(_BF)
        acc_ref[:, cols(b)] = alpha * acc_ref[:, cols(b)] + _dot(v_of(b)[:, keys(b)], p)
        m_ref[0:1, cols(b)] = m_new


def _attend_init_t(m_ref, acc_ref):
    m_ref[...] = jnp.full(m_ref.shape, M_INIT, _F32)
    acc_ref[...] = jnp.zeros(acc_ref.shape, _F32)


def _edit_blocks(s, fn, a0=0):
    rows = []
    for a in range(s.shape[0] // LANES):
        pieces = []
        for q in range(s.shape[1] // LANES):
            piece = s[a * LANES:(a + 1) * LANES, q * LANES:(q + 1) * LANES]
            new = fn(a0 + a, q, piece)
            pieces.append(piece if new is None else new)
        rows.append(jnp.concatenate(pieces, axis=1))
    return jnp.concatenate(rows, axis=0)


def _head_values(v_t):
    ones = jnp.ones((MV - HEAD_DIM, v_t.shape[1]), _BF)
    return [jnp.concatenate([v_t[a * HEAD_DIM:(a + 1) * HEAD_DIM], ones], axis=0) for a in range(2)]


def _softmax_init(m_ref, l_ref, acc_ref):
    m_ref[...] = jnp.full(m_ref.shape, M_INIT, _F32)
    l_ref[...] = jnp.zeros(l_ref.shape, _F32)
    acc_ref[...] = jnp.zeros(acc_ref.shape, _F32)


def _online_update(s, pv_fn, m_ref, l_ref, acc_ref):
    m_old = m_ref[...]
    m_new = jnp.maximum(m_old, jnp.max(s, axis=-1, keepdims=True))
    alpha = jnp.exp2(m_old - m_new)
    p = jnp.exp2(s - m_new)
    l_ref[...] = alpha * l_ref[...] + jnp.sum(p, axis=-1, keepdims=True)
    acc_ref[...] = alpha * acc_ref[...] + pv_fn(p.astype(_BF))
    m_ref[...] = m_new


def _masked_softmax_full(s):
    m = jnp.max(s, axis=-1, keepdims=True)
    p = jnp.exp2(s - m)
    l = jnp.sum(p, axis=-1, keepdims=True)
    return jnp.where(m > NEG_HALF, p / l, 0.0)


def _overlap(ncp, n_cmp, n_blk, blk_axis=1):
    shape = (ncp, n_blk) if blk_axis == 1 else (n_blk, ncp)
    c = _iota(shape, 1 - blk_axis)
    j = _iota(shape, blk_axis)
    r = SLC_BLOCK // CMP_STRIDE
    hit = (c >= r * j - (CMP_BLOCK // CMP_STRIDE - 1)) & (c <= r * j + r - 1) & (c < n_cmp)
    return jnp.where(hit, 1.0, 0.0).astype(_BF)


def _select_blocks(imp, t_pos, n_slc, n_top, blk_axis=1):
    j = _iota(imp.shape, blk_axis)
    cur = _div(t_pos, SLC_BLOCK)
    forced = (j == 0) | (j == cur) | (j == cur - 1)
    score = jnp.where(forced, FORCE_SCORE, imp)
    score = jnp.where(j * SLC_BLOCK > t_pos, -FORCE_SCORE, score)
    score = jnp.where(j >= n_slc, M_INIT, score)
    sel = jnp.zeros(imp.shape, jnp.bool_)
    for _ in range(n_top):
        mx = jnp.max(score, axis=blk_axis, keepdims=True)
        idx = jnp.min(jnp.where(score == mx, j, 1 << 20), axis=blk_axis, keepdims=True)
        hit = j == idx
        sel = sel | hit
        score = jnp.where(hit, REMOVED, score)
    return jnp.where(sel, 0.0, NEG)


def _block_onehot(k0, tk, n_lanes):
    s = _iota((tk, n_lanes), 0)
    j = _iota((tk, n_lanes), 1)
    return jnp.where(j == _div(k0 + s, SLC_BLOCK), 1.0, 0.0).astype(_BF)


def _nsa_prompt_kernel(qnt_ref, misct_ref, kc_ref, vct_ref, ks_ref, vst_ref, kw_ref, vwt_ref, dt_ref, nbh_ref, nbl_ref,
                       on_ref, lhst_ref, oct_ref, imp_ref, ms_ref, as_ref, mw_ref, aw_ref, *, n_cmp, ncp, n_slc):
    qt = pl.program_id(1)
    q0 = qt * TQ
    n_col = NSA_HEADS * TQ
    n_cb = TQ // CB
    blocks = list(range(n_col // CB))
    cols = lambda b: slice(b * CB, (b + 1) * CB)
    head_cols = lambda h: slice(h * TQ, (h + 1) * TQ)
    low = _iota((LANES, TQ), 0) < HEAD_DIM

    for g in range(NSA_KV_HEADS):
        for r in range(NSA_GROUP):
            blk = qnt_ref[0, r * LANES:(r + 1) * LANES, :]
            lhst_ref[0:LANES, head_cols(g * NSA_GROUP + r)] = jnp.where(low if g == 0 else jnp.logical_not(low),
                                                                       blk, jnp.zeros_like(blk))

    c = _iota((ncp, LANES), 0)
    u = _iota((ncp, LANES), 1)
    place = ((u < NEAR_U) & (c == CPT * qt - NEAR_BACK + u)) | ((u == NEAR_U) & (c >= CPT * qt + CPT))
    place = jnp.where(place, 1.0, 0.0).astype(_BF)
    kc = kc_ref[0]
    vct = vct_ref[0]
    ovl_t = _overlap(ncp, n_cmp, LANES, blk_axis=0)

    kc_aug = jnp.concatenate([kc, place, place], axis=1)

    def cmp_branch(n_keys):
        def cmp_scores(b):
            rhs = jnp.concatenate([lhst_ref[0:LANES, cols(b)], nbh_ref[:, cols(b)], nbl_ref[:, cols(b)]], axis=0)
            return _dot(kc_aug[0:n_keys], rhs)

        scores = [cmp_scores(b) for b in blocks[:AHEAD]]
        group_mass = {}
        for b in blocks:
            if b + AHEAD < len(blocks):
                scores.append(cmp_scores(b + AHEAD))
            s = scores[b]
            scores[b] = None
            m = jnp.max(s, axis=0, keepdims=True)
            pb = jnp.exp2(s - m)
            rcp = jnp.where(m > NEG_HALF, 1.0 / jnp.sum(pb, axis=0, keepdims=True), 0.0)
            pb = pb.astype(_BF)
            oct_ref[:, cols(b)] = _dot(vct[:, 0:n_keys], pb) * rcp
            mass = _dot(ovl_t[:, 0:n_keys], pb) * rcp
            head, part = divmod(b, n_cb)
            g, r = divmod(head, NSA_GROUP)
            group_mass[(g, part)] = mass if r == 0 else group_mass[(g, part)] + mass
            if r == NSA_GROUP - 1:
                imp_ref[:, g * TQ + part * CB:g * TQ + (part + 1) * CB] = group_mass.pop((g, part))

    n_vis = CPT * (qt + 1)
    steps = list(range(LANES, ncp + 1, LANES)) if CPT <= LANES else [ncp]
    for lo, hi in zip([0] + steps[:-1], steps):
        pl.when((n_vis > lo) & ((n_vis <= hi) | (hi == ncp)))(functools.partial(cmp_branch, hi))

    c2 = _iota((LANES, LANES), 0)
    i2 = _iota((LANES, LANES), 1)
    n_kb = TK // LANES

    def near_keys(kind):
        def key_blocks(b):
            part = b % n_cb
            q_lo, q_hi = part * (CB // LANES), (part + 1) * (CB // LANES) - 1
            if kind == 'diag':
                return 0, min(n_kb, q_hi + 1)
            if kind == 'wprev':
                return q_lo, n_kb
            return 0, n_kb
        return key_blocks

    def near_add(kind):
        def add(b, s):
            head, part = divmod(b, n_cb)
            a0 = near_keys(kind)(b)[0]

            def piece_fn(a, q, x):
                rel = part * (CB // LANES) + q - a + (0 if kind == 'diag' else n_kb)
                if kind == 'diag' and rel < 0:
                    return jnp.full(x.shape, NEG, _F32)
                if kind == 'wprev' and rel > n_kb:
                    return jnp.full(x.shape, NEG, _F32)
                if kind == 'wprev' and rel == n_kb:
                    return jnp.where(c2 > i2, x, NEG)
                if rel == 0:
                    return x + dt_ref[head, 0]
                if rel == 1:
                    return x + dt_ref[head, 1]
                return None
            return _edit_blocks(s, piece_fn, a0)
        return add

    def key_slice(kind):
        def keys_of(b):
            lo, hi = near_keys(kind)(b)
            return slice(lo * LANES, hi * LANES)
        return keys_of

    _attend_init_t(mw_ref, aw_ref)
    qfeat = slice(0, LANES)

    def win_tile(kt, kind):
        k0 = pl.multiple_of(kt * TK, TK)
        v_g = _head_values(vwt_ref[0, kt])
        _attend_tile_t(lhst_ref, kw_ref[0, pl.ds(k0, TK), :], lambda b: v_g[b // (NSA_GROUP * n_cb)], mw_ref, aw_ref,
                       blocks, near_add(kind), feat=qfeat, keys_of=key_slice(kind))

    win_tile(qt, 'diag')

    t_pos = q0 + _mod(_iota((1, NSA_KV_HEADS * TQ), 1), TQ)
    msel = _select_blocks(imp_ref[...], t_pos, n_slc, min(N_SELECT, n_slc), blk_axis=0).astype(_BF)
    for g in range(NSA_KV_HEADS):
        for r in range(NSA_GROUP):
            lhst_ref[LANES:2 * LANES, head_cols(g * NSA_GROUP + r)] = msel[:, g * TQ:(g + 1) * TQ]

    @pl.when(qt >= 1)
    def _():
        win_tile(qt - 1, 'wprev')


    _attend_init_t(ms_ref, as_ref)

    def sel_tile(kt, kind):
        k0 = pl.multiple_of(kt * TK, TK)
        kaug = jnp.concatenate([ks_ref[0, pl.ds(k0, TK), :], _block_onehot(k0, TK, LANES)], axis=1)
        v_g = _head_values(vst_ref[0, kt])
        _attend_tile_t(lhst_ref, kaug, lambda b: v_g[b // (NSA_GROUP * n_cb)], ms_ref, as_ref, blocks,
                       None if kind is None else near_add(kind), keys_of=None if kind is None else key_slice(kind))

    n_far = jnp.maximum(qt - 1, 0)

    def far_pair(i, carry):
        sel_tile(2 * i, None)
        sel_tile(2 * i + 1, None)
        return carry
    lax.fori_loop(0, n_far // 2, far_pair, 0)

    @pl.when(n_far % 2 == 1)
    def _():
        sel_tile(n_far - 1, None)

    @pl.when(qt >= 1)
    def _():
        sel_tile(qt - 1, 'prev')
    sel_tile(qt, 'diag')

    gates = misct_ref[0]
    for r in range(NSA_GROUP):
        parts = []
        for g in range(NSA_KV_HEADS):
            h = g * NSA_GROUP + r
            hc = head_cols(h)
            feat = slice(g * HEAD_DIM, (g + 1) * HEAD_DIM)
            own = slice(0, HEAD_DIM)
            den = slice(HEAD_DIM, HEAD_DIM + 1)
            gate = lambda kind: gates[L_GATE + kind * NSA_HEADS + h:L_GATE + kind * NSA_HEADS + h + 1, :]
            parts.append(gate(0) * oct_ref[feat, hc] + gate(1) * (as_ref[own, hc] / as_ref[den, hc])
                         + gate(2) * (aw_ref[own, hc] / aw_ref[den, hc]))
        on_ref[0, :, r * LANES:(r + 1) * LANES] = jnp.concatenate(parts, axis=0).T


def _nsa_prompt(qnt, misct, kcmp, vcmpt, nsab, vst, kwb, vwt, dt, nbh, nbl, n_cmp, ncp, n_slc):
    b, _, t = qnt.shape
    n_col = NSA_HEADS * TQ
    seq = lambda blk: pl.BlockSpec((1, t, LANES), lambda bi, qi, blk=blk: (bi, 0, blk))
    tiles = pl.BlockSpec((1, t // TK, LANES, TK), lambda bi, qi: (bi, 0, 0, 0))
    const = lambda a: pl.BlockSpec(a.shape, lambda bi, qi: (0,) * a.ndim)
    return pl.pallas_call(
        functools.partial(_nsa_prompt_kernel, n_cmp=n_cmp, ncp=ncp, n_slc=n_slc),
        grid=(b, t // TQ),
        in_specs=[pl.BlockSpec((1, 512, TQ), lambda bi, qi: (bi, 0, qi)),
                  pl.BlockSpec((1, LANES, TQ), lambda bi, qi: (bi, 0, qi)),
                  pl.BlockSpec((1, ncp, LANES), lambda bi, qi: (bi, 0, 0)),
                  pl.BlockSpec((1, LANES, ncp), lambda bi, qi: (bi, 0, 0)),
                  seq(2), tiles, seq(0), tiles, const(dt), const(nbh), const(nbl)],
        out_specs=pl.BlockSpec((1, TQ, 512), lambda bi, qi: (bi, qi, 0)),
        out_shape=jax.ShapeDtypeStruct((b, t, 512), _F32),
        scratch_shapes=[pltpu.VMEM((2 * LANES, n_col), _BF), pltpu.VMEM((LANES, n_col), _F32),
                        pltpu.VMEM((LANES, NSA_KV_HEADS * TQ), _F32),
                        pltpu.VMEM((SUBLANES, n_col), _F32), pltpu.VMEM((MV, n_col), _F32),
                        pltpu.VMEM((SUBLANES, n_col), _F32), pltpu.VMEM((MV, n_col), _F32)],
        compiler_params=_cparams(("arbitrary", "arbitrary")),
        name="nsa_prompt",
    )(qnt, misct, kcmp, vcmpt, nsab, vst, kwb, vwt, dt, nbh, nbl)


def _fox_prompt_kernel(qft_ref, dqt_ref, kf_ref, dk_ref, vft_ref, of_ref, lhst_ref, m_ref, acc_ref, *, tqf):
    p = pl.program_id(1)
    qt = pl.program_id(2)
    row = _iota((LANES, tqf), 0)
    low = row < HEAD_DIM
    qt_ = qft_ref[0]
    dq = dqt_ref[0]
    zero = jnp.zeros_like(qt_)
    one = jnp.ones_like(qt_)
    for a in range(2):
        head = 2 * p + a
        take_neg = (row >= L_NEG + N_PIECE * head) & (row < L_NEG + N_PIECE * (head + 1))
        take_pos = (row >= L_POS + N_PIECE * head) & (row < L_POS + N_PIECE * (head + 1))
        lhst_ref[0:LANES, a * tqf:(a + 1) * tqf] = jnp.where(low if a == 0 else jnp.logical_not(low), qt_, zero)
        lhst_ref[LANES:2 * LANES, a * tqf:(a + 1) * tqf] = jnp.where(take_neg, one, jnp.where(take_pos, dq, zero))
    _attend_init_t(m_ref, acc_ref)
    n_blk = 2 * tqf // CB
    per_q = tqf // TK
    lane = _iota((TK, LANES), 1)
    ones_pos = (jnp.clip(lane - (L_POS - 1), 0, 1) * jnp.clip(2 * L_POS - lane, 0, 1)).astype(_F32).astype(_BF)

    def tile(kt, blocks, add_fn):
        k0 = pl.multiple_of(kt * TK, TK)
        kaug = jnp.concatenate([kf_ref[0, pl.ds(k0, TK), :], jnp.where(lane < L_POS, dk_ref[0, pl.ds(k0, TK), :], ones_pos)],
                               axis=1)
        v_a = _head_values(vft_ref[0, kt])
        _attend_tile_t(lhst_ref, kaug, lambda b: v_a[b * CB // tqf], m_ref, acc_ref, blocks, add_fn)

    def far_body(kt, carry):
        tile(kt, list(range(n_blk)), None)
        return carry
    lax.fori_loop(0, qt * per_q, far_body, 0)

    for j in range(per_q):
        q_lo = lambda b: (b * CB) % tqf
        blocks = [b for b in range(n_blk) if q_lo(b) + CB > j * TK]

        def causal(b, s, j=j):
            if j * TK + TK - 1 <= q_lo(b):
                return s
            kk = j * TK + _iota((TK, CB), 0)
            qq = q_lo(b) + _iota((TK, CB), 1)
            return jnp.where(kk <= qq, s, NEG)
        tile(qt * per_q + j, blocks, causal)

    acc = acc_ref[...]
    o = [acc[0:HEAD_DIM, a * tqf:(a + 1) * tqf] / acc[HEAD_DIM:HEAD_DIM + 1, a * tqf:(a + 1) * tqf] for a in range(2)]
    of_ref[0] = jnp.concatenate(o, axis=0).T


def _fox_prompt(qft, dect, foxb, dec, vft):
    b, _, t = qft.shape
    tqf = min(TQF, t)
    n_pair = FOX_HEADS // 2
    return pl.pallas_call(
        functools.partial(_fox_prompt_kernel, tqf=tqf),
        grid=(b, n_pair, t // tqf),
        in_specs=[pl.BlockSpec((1, LANES, tqf), lambda bi, p, qi: (bi, p, qi)),
                  pl.BlockSpec((1, LANES, tqf), lambda bi, p, qi: (bi, 0, qi)),
                  pl.BlockSpec((1, t, LANES), lambda bi, p, qi: (bi, 0, p)),
                  pl.BlockSpec((1, t, LANES), lambda bi, p, qi: (bi, 0, 0)),
                  pl.BlockSpec((1, t // TK, LANES, TK), lambda bi, p, qi: (bi, 0, p, 0))],
        out_specs=pl.BlockSpec((1, tqf, LANES), lambda bi, p, qi: (bi, qi, p)),
        out_shape=jax.ShapeDtypeStruct((b, t, 512), _F32),
        scratch_shapes=[pltpu.VMEM((2 * LANES, 2 * tqf), _BF), pltpu.VMEM((SUBLANES, 2 * tqf), _F32),
                        pltpu.VMEM((MV, 2 * tqf), _F32)],
        compiler_params=_cparams(("arbitrary", "arbitrary", "arbitrary")),
        name="fox_prompt",
    )(qft, dect, foxb, dec, vft)


def _post_kernel(on_ref, of_ref, x_ref, gn_ref, gf_ref, wo_ref, gp_ref, y_ref):
    half = on_ref.shape[-1]
    a = _rms(on_ref[0], gn_ref[...]).astype(_BF)
    f = _rms(of_ref[0], gf_ref[...]).astype(_BF)
    mixed = _dot(a, wo_ref[0:half, :]) + _dot(f, wo_ref[half:2 * half, :])
    y_ref[0] = x_ref[0] + _rms(mixed, gp_ref[...])


def _post(o_n, o_f, x, gn, gf, wo, gp, tr):
    nb, t, d = x.shape
    half = o_n.shape[-1]
    row = lambda width: pl.BlockSpec((1, tr, width), lambda bi, i: (bi, i, 0))
    const = lambda shape: pl.BlockSpec(shape, lambda bi, i: (0,) * len(shape))
    return pl.pallas_call(
        _post_kernel,
        grid=(nb, t // tr),
        in_specs=[row(half), row(half), row(d), const((1, half)), const((1, half)), const((2 * half, d)), const((1, d))],
        out_specs=row(d),
        out_shape=jax.ShapeDtypeStruct((nb, t, d), _F32),
        compiler_params=_cparams(("arbitrary", "arbitrary")),
        name="out_proj",
    )(o_n, o_f, x, gn, gf, wo, gp)


def _ffn_kernel(x_ref, gpre_ref, wg_ref, wu_ref, wd_ref, gpost_ref, y_ref):
    x = x_ref[0]
    h = _rms(x, gpre_ref[...]).astype(_BF)
    act = (jax.nn.silu(_dot(h, wg_ref[...])) * _dot(h, wu_ref[...])).astype(_BF)
    y_ref[0] = x + _rms(_dot(act, wd_ref[...]), gpost_ref[...])


def _ffn(x, gpre, wg, wu, wd, gpost, tr):
    nb, t, d = x.shape
    dff = wg.shape[1]
    row = pl.BlockSpec((1, tr, d), lambda bi, i: (bi, i, 0))
    const = lambda shape: pl.BlockSpec(shape, lambda bi, i: (0,) * len(shape))
    return pl.pallas_call(
        _ffn_kernel,
        grid=(nb, t // tr),
        in_specs=[row, const((1, d)), const((d, dff)), const((d, dff)), const((dff, d)), const((1, d))],
        out_specs=row,
        out_shape=jax.ShapeDtypeStruct((nb, t, d), _F32),
        compiler_params=_cparams(("arbitrary", "arbitrary")),
        name="ffn",
    )(x, gpre, wg, wu, wd, gpost)


def _row_select(rows8, pieces):
    out = jnp.zeros((SUBLANES, pieces[0].shape[-1]), _F32)
    for i, piece in enumerate(pieces):
        out = jnp.where(rows8 == i, jnp.broadcast_to(piece, out.shape), out)
    return out


def _pad_rows(a, n):
    return jnp.concatenate([a, jnp.zeros((n - a.shape[0],) + a.shape[1:], a.dtype)], axis=0)


def _nsa_sample_kernel(pt_ref, *refs, pps, past, n_new, n_cmp, ncp, n_slc):
    page_refs = refs[:pps]
    (qn_ref, misc_ref, new_ref, win_ref, kwn_ref, w1k_ref, w1v_ref, pe_ref, w1r_ref, w2_ref,
     bc_ref, bs_ref, bw_ref, kwt_ref, on_ref, wout_ref, lk_ref, lv_ref, kst_ref, vst_ref) = refs[pps:]
    del pt_ref
    j = pl.program_id(1)
    n_steps = pl.num_programs(1)
    cpp = PAGE_SIZE // CMP_STRIDE
    ppc = WINDOW // PAGE_SIZE
    n_pages = past // PAGE_SIZE

    rr = _iota((PAGE_SIZE, PAGE_SIZE), 0)
    tok = _iota((PAGE_SIZE, PAGE_SIZE), 1)
    regroup = jnp.where(tok == CMP_STRIDE * _mod(rr, cpp) + _div(rr, cpp), 1.0, 0.0).astype(_BF)

    for k in range(pps):
        pg = j * pps + k
        ref = page_refs[k]
        kst_ref[pg] = ref[0, 0, 2].astype(_BF)
        vst_ref[pg] = ref[0, 0, 3].astype(_BF)
        c0 = pl.multiple_of(pg * cpp, cpp)
        kv_c = jnp.concatenate([ref[0, 0, 0], ref[0, 0, 1]], axis=0).astype(_BF)
        x = _dot_nt(regroup, kv_c)
        for p in range(CMP_STRIDE):
            lk_ref[pl.ds(c0, cpp), p * LANES:(p + 1) * LANES] = x[p * cpp:(p + 1) * cpp, 0:LANES]
            lv_ref[pl.ds(c0, cpp), p * LANES:(p + 1) * LANES] = x[p * cpp:(p + 1) * cpp, LANES:2 * LANES]

    @pl.when(j == n_steps - 1)
    def _():
        nc = past // CMP_STRIDE
        new = new_ref[0]
        ks_new = _pad_rows(new[:, 2 * LANES:3 * LANES], PAGE_SIZE).astype(_BF)
        vs_new = _pad_rows(new[:, 3 * LANES:4 * LANES], PAGE_SIZE).astype(_BF)

        def padc(a):
            return a if ncp == nc else _pad_rows(a, ncp)
        kc = padc(_compress(lk_ref[...].astype(_BF), w1k_ref[...], pe_ref[0], w1r_ref[0], w2_ref[0], n_cmp)).astype(_BF)
        vc = padc(_compress(lv_ref[...].astype(_BF), w1v_ref[...], pe_ref[1], w1r_ref[1], w2_ref[1], n_cmp)).astype(_BF)

        rows8 = _iota((SUBLANES, LANES), 0)
        lane = _iota((SUBLANES, LANES), 1)
        grp_low = rows8 < NSA_GROUP
        q = qn_ref[0].astype(_F32)
        blocks = []
        for i in range(n_new):
            blk = jnp.zeros((SUBLANES, LANES), _F32)
            for r in range(NSA_GROUP):
                piece = jnp.broadcast_to(q[i:i + 1, r * LANES:(r + 1) * LANES], (SUBLANES, LANES))
                blk = jnp.where(_mod(rows8, NSA_GROUP) == r, piece, blk)
            on_group = jnp.logical_not(jnp.logical_xor(lane < HEAD_DIM, grp_low))
            blocks.append(jnp.where(on_group, blk, jnp.zeros_like(blk)))
        lq = jnp.concatenate(blocks, axis=0).astype(_BF)

        pc = _masked_softmax_full(_dot_nt(lq, kc) + bc_ref[...])
        o_c = _dot(pc.astype(_BF), vc)

        nl = 2 * LANES
        ovl_t = _overlap(ncp, n_cmp, nl, blk_axis=0)
        sums = []
        for i in range(n_new):
            blk = pc[i * SUBLANES:(i + 1) * SUBLANES]
            rr8 = _iota(blk.shape, 0)
            for g in range(NSA_KV_HEADS):
                in_group = (rr8 < NSA_GROUP) if g == 0 else (rr8 >= NSA_GROUP)
                sums.append(jnp.sum(jnp.where(in_group, blk, 0.0), axis=0, keepdims=True))
        hi, lo = _split2(_pad_rows(_row_select(_iota((SUBLANES, ncp), 0), sums), LANES))
        imp_t = _dot_nt(ovl_t, hi) + _dot_nt(ovl_t, lo)
        t_pos = past + _div(_iota((1, LANES), 1), NSA_KV_HEADS)
        msel = _select_blocks(imp_t, t_pos, n_slc, min(N_SELECT, n_slc), blk_axis=0).T[0:SUBLANES]
        rows8n = _iota((SUBLANES, nl), 0)
        mrows = []
        for i in range(n_new):
            m0 = jnp.broadcast_to(msel[2 * i:2 * i + 1], (SUBLANES, nl))
            m1 = jnp.broadcast_to(msel[2 * i + 1:2 * i + 2], (SUBLANES, nl))
            mrows.append(jnp.where(rows8n < NSA_GROUP, m0, m1))
        mrows = jnp.concatenate(mrows, axis=0).astype(_BF)

        def block_mask(k0, n_keys):
            s_i = _iota((nl, n_keys), 1)
            j_i = _iota((nl, n_keys), 0)
            return _dot(mrows, jnp.where(j_i == _div(k0 + s_i, SLC_BLOCK), 1.0, 0.0).astype(_BF))

        n_chunk = n_pages // ppc
        kts = lambda ci: jnp.concatenate([kst_ref[ci * ppc + u] for u in range(ppc)], axis=1)
        vts = lambda ci: jnp.concatenate([vst_ref[ci * ppc + u] for u in range(ppc)], axis=1)
        parts = [_dot(lq, kts(ci)) + block_mask(ci * ppc * PAGE_SIZE, ppc * PAGE_SIZE) for ci in range(n_chunk)]
        parts[-1] = parts[-1] + bs_ref[:, 0:WINDOW]
        parts.append(_dot_nt(lq, ks_new) + block_mask(past, PAGE_SIZE) + bs_ref[:, WINDOW:WINDOW + PAGE_SIZE])
        m_s = jnp.max(parts[0], axis=-1, keepdims=True)
        for x in parts[1:]:
            m_s = jnp.maximum(m_s, jnp.max(x, axis=-1, keepdims=True))
        l_s = jnp.zeros_like(m_s)
        acc = jnp.zeros((lq.shape[0], LANES), _F32)
        for ci, x in enumerate(parts):
            pr = jnp.exp2(x - m_s)
            l_s = l_s + jnp.sum(pr, axis=-1, keepdims=True)
            acc = acc + (_dot_nt(pr.astype(_BF), vts(ci)) if ci < n_chunk else _dot(pr.astype(_BF), vs_new))
        o_s = acc / l_s

        kwn = kwn_ref[0]
        kw_new = _pad_rows(kwn[:, 0:LANES], PAGE_SIZE).astype(_BF)
        vw_new = _pad_rows(kwn[:, LANES:2 * LANES], PAGE_SIZE).astype(_BF)
        sw = jnp.concatenate([_dot(lq, win_ref[0, 0, 0].astype(_BF)), _dot_nt(lq, kw_new)], axis=1) + bw_ref[...]
        mw = jnp.max(sw, axis=-1, keepdims=True)
        pw = jnp.exp2(sw - mw)
        pwb = pw.astype(_BF)
        o_w = (_dot_nt(pwb[:, 0:WINDOW], win_ref[0, 0, 1].astype(_BF)) + _dot(pwb[:, WINDOW:], vw_new)) \
            / jnp.sum(pw, axis=-1, keepdims=True)

        lane_w = _iota((LANES, WINDOW), 1)
        for kv in range(2):
            buf = pltpu.roll(win_ref[0, 0, kv], WINDOW - n_new, 1)
            for i in range(n_new):
                col = jnp.broadcast_to(kwt_ref[0, kv * LANES:(kv + 1) * LANES, i:i + 1], (LANES, WINDOW))
                buf = jnp.where(lane_w == WINDOW - n_new + i, col, buf)
            wout_ref[0, kv] = buf

        misc = misc_ref[0]
        out_rows = []
        for i in range(n_new):
            sl = slice(i * SUBLANES, (i + 1) * SUBLANES)
            g_row = jnp.broadcast_to(misc[i:i + 1, :], (SUBLANES, LANES))

            def gcol(kind, g_row=g_row):
                pick = lane == L_GATE + kind * NSA_HEADS + rows8
                return jnp.sum(jnp.where(pick, g_row, 0.0), axis=-1, keepdims=True)
            o_blk = gcol(0) * o_c[sl] + gcol(1) * o_s[sl] + gcol(2) * o_w[sl]
            pieces = [jnp.where(lane[0:1] < HEAD_DIM, o_blk[r:r + 1], o_blk[NSA_GROUP + r:NSA_GROUP + r + 1])
                      for r in range(NSA_GROUP)]
            out_rows.append(jnp.concatenate(pieces, axis=1))
        on_ref[0] = _row_select(_iota((SUBLANES, 4 * LANES), 0), out_rows)


def _nsa_sample(layer, page_table, cache_t, qn, misc, new, win_t, kwn, kwn_t, cw, bc, bs, bw, past, n_new, n_cmp, ncp, n_slc, pps):
    n_seq, n_pages = page_table.shape
    n_steps = n_pages // pps
    nc = past // CMP_STRIDE

    def page_spec(k):
        return pl.BlockSpec((1, 1, 4, LANES, PAGE_SIZE), lambda s, j, pt, k=k: (layer, pt[s, j * pps + k], 0, 0, 0))
    per_seq = lambda a: pl.BlockSpec((1,) + a.shape[1:], lambda s, j, pt: (s,) + (0,) * (a.ndim - 1))
    const = lambda a: pl.BlockSpec(a.shape, lambda s, j, pt: (0,) * a.ndim)
    consts = [cw["w1k"], cw["w1v"], cw["pe8"], cw["w1raw"], cw["w2p"], bc, bs, bw]
    grid_spec = pltpu.PrefetchScalarGridSpec(
        num_scalar_prefetch=1,
        grid=(n_seq, n_steps),
        in_specs=[page_spec(k) for k in range(pps)] + [per_seq(qn), per_seq(misc), per_seq(new),
                                                        pl.BlockSpec((1, 1) + win_t.shape[2:], lambda s, j, pt: (layer, s, 0, 0, 0)),
                                                        per_seq(kwn)]
        + [const(a) for a in consts] + [per_seq(kwn_t)],
        out_specs=(pl.BlockSpec((1, SUBLANES, 4 * LANES), lambda s, j, pt: (s, 0, 0)),
                   pl.BlockSpec((1,) + win_t.shape[2:], lambda s, j, pt: (s, 0, 0, 0))),
        scratch_shapes=[pltpu.VMEM((nc, CMP_STRIDE * LANES), _F32), pltpu.VMEM((nc, CMP_STRIDE * LANES), _F32),
                        pltpu.VMEM((n_pages, LANES, PAGE_SIZE), _BF), pltpu.VMEM((n_pages, LANES, PAGE_SIZE), _BF)],
    )
    return pl.pallas_call(
        functools.partial(_nsa_sample_kernel, pps=pps, past=past, n_new=n_new, n_cmp=n_cmp, ncp=ncp, n_slc=n_slc),
        grid_spec=grid_spec,
        out_shape=(jax.ShapeDtypeStruct((n_seq, SUBLANES, 4 * LANES), _F32),
                   jax.ShapeDtypeStruct((n_seq,) + win_t.shape[2:], _F32)),
        compiler_params=_cparams(("arbitrary", "arbitrary")),
        name="nsa_sample",
    )(page_table, *([cache_t] * pps), qn, misc, new, win_t, kwn, *consts, kwn_t)


def _fox_sample_kernel(pt_ref, *refs, pps, n_new):
    kv_refs = refs[:pps]
    lf_refs = refs[pps:2 * pps]
    (qf_ref, kvn_ref, lfn_ref, of_ref, q_ref, e_ref, carry_ref, m_ref, l_ref, acc_ref) = refs[2 * pps:]
    del pt_ref
    j = pl.program_id(1)
    n_steps = pl.num_programs(1)
    rows = FOX_HEADS * n_new
    width = FOX_HEADS * HEAD_DIM
    srow = _iota((PAGE_SIZE, PAGE_SIZE), 0)
    scol = _iota((PAGE_SIZE, PAGE_SIZE), 1)
    later = jnp.where(srow > scol, 1.0, 0.0).astype(_BF)

    def suffix(lf):
        hi, mid, lo = _split3(lf)
        return _dot(hi, later) + _dot(mid, later) + _dot(lo, later)

    def decay(rt):
        return (jnp.concatenate([rt] * n_new, axis=0) - e_ref[...]) * LOG2E

    @pl.when(j == 0)
    def _():
        _softmax_init(m_ref, l_ref, acc_ref)
        rows8 = _iota((SUBLANES, width), 0)
        lane = _iota((SUBLANES, width), 1)
        q = qf_ref[0].astype(_F32)
        blocks = []
        for i in range(n_new):
            piece = jnp.broadcast_to(q[i:i + 1, :], (SUBLANES, width))
            blocks.append(jnp.where(_div(lane, HEAD_DIM) == rows8, piece, jnp.zeros_like(piece)))
        q_ref[...] = jnp.concatenate(blocks, axis=0).astype(_BF)
        lfn = lfn_ref[0]
        rt = suffix(lfn)
        e_ref[...] = jnp.concatenate([rt[:, i:i + 1] for i in range(n_new)], axis=0)
        carry_ref[...] = jnp.broadcast_to(jnp.sum(lfn, axis=-1, keepdims=True), carry_ref.shape)
        kvn = _pad_rows(kvn_ref[0], PAGE_SIZE)
        ri = _div(_iota((rows, PAGE_SIZE), 0), SUBLANES)
        ci = _iota((rows, PAGE_SIZE), 1)
        s = _dot_nt(q_ref[...], kvn[:, 0:width].astype(_BF)) + decay(rt) + jnp.where(ci <= ri, 0.0, NEG)
        v_new = kvn[:, width:2 * width].astype(_BF)
        _online_update(s, lambda p: _dot(p, v_new), m_ref, l_ref, acc_ref)

    kts, vts, rts = [], [], []
    carry = carry_ref[...]
    for k in range(pps):
        lf = lf_refs[k][0, 0]
        kts.append(kv_refs[k][0, 0, 0].astype(_BF))
        vts.append(kv_refs[k][0, 0, 1].astype(_BF))
        rts.append(suffix(lf) + carry)
        carry = carry + jnp.sum(lf, axis=-1, keepdims=True)
    carry_ref[...] = carry
    vt = jnp.concatenate(vts, axis=1)
    s = _dot(q_ref[...], jnp.concatenate(kts, axis=1)) + decay(jnp.concatenate(rts, axis=1))
    _online_update(s, lambda p: _dot_nt(p, vt), m_ref, l_ref, acc_ref)

    @pl.when(j == n_steps - 1)
    def _():
        o = acc_ref[...] / l_ref[...]
        rows8 = _iota((SUBLANES, width), 0)
        lane = _iota((SUBLANES, width), 1)
        out_rows = []
        for i in range(n_new):
            blk = jnp.where(_div(lane, HEAD_DIM) == rows8, o[i * SUBLANES:(i + 1) * SUBLANES], 0.0)
            out_rows.append(jnp.sum(blk, axis=0, keepdims=True))
        of_ref[0] = _row_select(rows8, out_rows)


def _fox_sample(layer, page_table, cache_t, cache_lft, qf, kvn, lfn, n_new, pps):
    n_seq, n_pages = page_table.shape
    n_steps = n_pages // pps
    rows = FOX_HEADS * n_new
    width = FOX_HEADS * HEAD_DIM

    def page_idx(s, j, pt, k):
        return pt[s, n_pages - 1 - (j * pps + k)]
    kv_spec = lambda k: pl.BlockSpec((1, 1, 2, width, PAGE_SIZE),
                                     lambda s, j, pt, k=k: (layer, page_idx(s, j, pt, k), 0, 0, 0))
    lf_spec = lambda k: pl.BlockSpec((1, 1, FOX_HEADS, PAGE_SIZE),
                                     lambda s, j, pt, k=k: (layer, page_idx(s, j, pt, k), 0, 0))
    per_seq = lambda a: pl.BlockSpec((1,) + a.shape[1:], lambda s, j, pt: (s,) + (0,) * (a.ndim - 1))
    grid_spec = pltpu.PrefetchScalarGridSpec(
        num_scalar_prefetch=1,
        grid=(n_seq, n_steps),
        in_specs=[kv_spec(k) for k in range(pps)] + [lf_spec(k) for k in range(pps)] + [per_seq(a) for a in (qf, kvn, lfn)],
        out_specs=pl.BlockSpec((1, SUBLANES, width), lambda s, j, pt: (s, 0, 0)),
        scratch_shapes=[pltpu.VMEM((rows, width), _BF), pltpu.VMEM((rows, 1), _F32), pltpu.VMEM((SUBLANES, LANES), _F32),
                        pltpu.VMEM((rows, 1), _F32), pltpu.VMEM((rows, 1), _F32), pltpu.VMEM((rows, width), _F32)],
    )
    return pl.pallas_call(
        functools.partial(_fox_sample_kernel, pps=pps, n_new=n_new),
        grid_spec=grid_spec,
        out_shape=jax.ShapeDtypeStruct((n_seq, SUBLANES, width), _F32),
        compiler_params=_cparams(("arbitrary", "arbitrary")),
        name="fox_sample",
    )(page_table, *([cache_t] * pps), *([cache_lft] * pps), qf, kvn, lfn)


def _nsa_perm():
    idx = np.zeros(NSA_HEADS * HEAD_DIM, np.int32)
    for r in range(NSA_GROUP):
        for g in range(NSA_KV_HEADS):
            for d in range(HEAD_DIM):
                idx[r * LANES + g * HEAD_DIM + d] = (g * NSA_GROUP + r) * HEAD_DIM + d
    return idx


def _layer_weights(l, w_in, b_gate, b_forget, cmp_pe, cmp_w1, cmp_w2, grp_norm_nsa, grp_norm_fox, w_o):
    perm = _nsa_perm()
    w = w_in[l]
    o_qn, o_nsa, o_kw, o_g, o_qf, o_kf, o_f = 0, 512, 1024, 1280, 1304, 1816, 2840
    f_cols = w[:, o_f:o_f + FOX_HEADS]
    misc = jnp.concatenate([w[:, o_g:o_g + 3 * NSA_HEADS], f_cols, f_cols,
                            jnp.zeros((w.shape[0], LANES - 3 * NSA_HEADS - 2 * FOX_HEADS), w.dtype)], axis=1)
    wp = jnp.concatenate([w[:, o_qn:o_nsa][:, perm], w[:, o_nsa:o_kw], w[:, o_kw:o_g], w[:, o_qf:o_kf],
                          w[:, o_kf:o_f], misc], axis=1).astype(_BF)
    bias = jnp.concatenate([b_gate[l].reshape(-1), b_forget[l], b_forget[l],
                            jnp.zeros((LANES - 3 * NSA_HEADS - 2 * FOX_HEADS,), _F32)]).reshape(1, LANES)

    def w1_layout(w1):
        w1 = w1.reshape(2, CMP_STRIDE, HEAD_DIM, CMP_HIDDEN)
        z = jnp.zeros((CMP_STRIDE, HEAD_DIM, CMP_HIDDEN), w1.dtype)
        g0 = jnp.concatenate([w1[0], w1[1], z, z], axis=-1)
        g1 = jnp.concatenate([z, z, w1[0], w1[1]], axis=-1)
        return jnp.concatenate([g0, g1], axis=1).reshape(CMP_STRIDE * LANES, 4 * CMP_HIDDEN).astype(_BF)

    def w2_layout(w2):
        z = jnp.zeros_like(w2)
        return jnp.concatenate([jnp.concatenate([w2, z], axis=1), jnp.concatenate([z, w2], axis=1)], axis=0).astype(_BF)
    cw = dict(
        w1k=w1_layout(cmp_w1[l, 0]), w1v=w1_layout(cmp_w1[l, 1]),
        pe8=jnp.broadcast_to(cmp_pe[l].reshape(2, 1, CMP_BLOCK * HEAD_DIM), (2, SUBLANES, CMP_BLOCK * HEAD_DIM)).astype(_BF),
        w1raw=cmp_w1[l].astype(_BF),
        w2p=jnp.stack([w2_layout(cmp_w2[l, 0]), w2_layout(cmp_w2[l, 1])]),
    )
    gn = grp_norm_nsa[l][perm].reshape(1, -1)
    gf = grp_norm_fox[l].reshape(1, -1)
    wo = jnp.concatenate([w_o[l][:NSA_HEADS * HEAD_DIM][perm], w_o[l][NSA_HEADS * HEAD_DIM:]], axis=0).astype(_BF)
    return wp, bias, cw, gn, gf, wo


def kernel(x_prompt, x_sample, cache_nsa_kv, cache_fox_kv, cache_fox_logf, state_win_kv, page_table, rel_bias,
           norm_mix_pre, norm_mix_post, norm_ffn_pre, norm_ffn_post, w_in, b_gate, b_forget, cmp_pe, cmp_w1, cmp_w2,
           grp_norm_nsa, grp_norm_fox, w_o, w_ffn_gate, w_ffn_up, w_ffn_down):
    depth = w_in.shape[0]
    b, t, d = x_prompt.shape
    n_seq, n_new, _ = x_sample.shape
    n_pages = page_table.shape[1]
    past = n_pages * PAGE_SIZE
    n_pool = cache_nsa_kv.shape[1]
    n_win = state_win_kv.shape[2]
    assert t % min(TQF, t) == 0 and min(TQF, t) % TK == 0 and t % TQ == 0 and TQ == TK == WINDOW and TQ % CB == 0
    assert CB % LANES == 0 and LANES > T5_FAR
    assert n_new <= SUBLANES and SLC_BLOCK >= n_new and n_win == WINDOW
    assert n_pages % (WINDOW // PAGE_SIZE) == 0 and past >= 2 * WINDOW
    assert (n_seq * n_new) % SUBLANES == 0

    nc_p = t // CMP_STRIDE
    ncp_p = -(-nc_p // LANES) * LANES
    n_slc_p = t // SLC_BLOCK
    assert n_slc_p <= LANES
    nc_s = past // CMP_STRIDE
    ncp_s = -(-nc_s // LANES) * LANES
    n_slc_s = past // SLC_BLOCK + 1
    assert n_slc_s <= 2 * LANES
    pps = math.gcd(n_pages, 32)
    pps_fox = math.gcd(n_pages, 32)

    dt, nbh, nbl = _bias_prompt(rel_bias)
    bc, bs, bw = _bias_sample(rel_bias, past, n_new, nc_s - 1, ncp_s)

    nsa_t = jnp.transpose(cache_nsa_kv, (0, 1, 3, 4, 5, 2)).reshape(depth, n_pool, 4, NSA_KV_HEADS * HEAD_DIM, PAGE_SIZE)
    fox_t = jnp.transpose(cache_fox_kv, (0, 1, 3, 4, 5, 2)).reshape(depth, n_pool, 2, FOX_HEADS * HEAD_DIM, PAGE_SIZE)
    logf_t = jnp.transpose(cache_fox_logf, (0, 1, 3, 2))
    win_t = jnp.transpose(state_win_kv, (0, 1, 3, 4, 5, 2)).reshape(depth, n_seq, 2, NSA_KV_HEADS * HEAD_DIM, n_win)

    row1 = lambda a: a.reshape(1, -1)
    pad_new = lambda a: jnp.pad(a, ((0, 0), (0, SUBLANES - n_new), (0, 0)))
    rs = n_seq * n_new
    xp, xs = x_prompt, x_sample.reshape(1, rs, d)
    outs_p, outs_s = [], []
    for l in range(depth):
        wp, bias, cw, gn, gf, wo = _layer_weights(l, w_in, b_gate, b_forget, cmp_pe, cmp_w1, cmp_w2,
                                                  grp_norm_nsa, grp_norm_fox, w_o)
        wg, wu, wd = w_ffn_gate[l].astype(_BF), w_ffn_up[l].astype(_BF), w_ffn_down[l].astype(_BF)
        g_pre, g_post = row1(norm_mix_pre[l]), row1(norm_mix_post[l])
        g_fpre, g_fpost = row1(norm_ffn_pre[l]), row1(norm_ffn_post[l])

        (_, nsa, nsab, kw, kwb, _, fox, foxb, misc, dec,
         qnt, qft, dect, misct, vst, vwt, vft) = _proj(xp, g_pre, wp, bias, TR_PROJ, True)
        kcmp, vcmpt = _compress_prompt(nsa, cw, nc_p, ncp_p)
        o_n = _nsa_prompt(qnt, misct, kcmp, vcmpt, nsab, vst, kwb, vwt, dt, nbh, nbl, nc_p - 1, ncp_p, n_slc_p)
        o_f = _fox_prompt(qft, dect, foxb, dec, vft)
        xp = _post(o_n, o_f, xp, gn, gf, wo, g_post, TR_POST)
        xp = _ffn(xp, g_fpre, wg, wu, wd, g_fpost, TR_FFN)
        outs_p.append((nsa.reshape(b, t, 4, NSA_KV_HEADS, HEAD_DIM), fox.reshape(b, t, 2, FOX_HEADS, HEAD_DIM),
                       misc[:, :, L_LOGF:L_LOGF + FOX_HEADS],
                       kw[:, t - WINDOW:].reshape(b, WINDOW, 2, NSA_KV_HEADS, HEAD_DIM)))

        tr_s = math.gcd(rs, TR_PROJ)
        qn, nsa, nsab, kw, kwb, qf, fox, foxb, misc = _proj(xs, g_pre, wp, bias, tr_s, False)
        per = lambda a: a.reshape(n_seq, n_new, a.shape[-1])
        nsa_s, fox_s, kw_s, misc_s = per(nsa), per(fox), per(kw), per(misc)
        logf_s = misc_s[:, :, L_LOGF:L_LOGF + FOX_HEADS]
        kw_st = jnp.pad(jnp.swapaxes(kw_s, 1, 2), ((0, 0), (0, 0), (0, LANES - n_new)))
        o_n, win_new = _nsa_sample(l, page_table, nsa_t, pad_new(per(qn)), pad_new(misc_s), pad_new(nsa_s), win_t,
                                   pad_new(kw_s), kw_st, cw, bc, bs, bw, past, n_new, nc_s - 1, ncp_s, n_slc_s, pps)
        kvn = pad_new(fox_s)
        lfn = jnp.pad(jnp.swapaxes(logf_s, 1, 2), ((0, 0), (0, 0), (0, PAGE_SIZE - n_new)))
        o_f = _fox_sample(l, page_table, fox_t, logf_t, pad_new(per(qf)), kvn, lfn, n_new, pps_fox)
        o_n = o_n[:, :n_new].reshape(1, rs, -1)
        o_f = o_f[:, :n_new].reshape(1, rs, -1)
        xs = _post(o_n, o_f, xs, gn, gf, wo, g_post, math.gcd(rs, TR_POST))
        xs = _ffn(xs, g_fpre, wg, wu, wd, g_fpost, math.gcd(rs, TR_FFN))
        win_new = jnp.transpose(win_new.reshape(n_seq, 2, NSA_KV_HEADS, HEAD_DIM, n_win), (0, 4, 1, 2, 3))
        outs_s.append((nsa_s.reshape(n_seq, n_new, 4, NSA_KV_HEADS, HEAD_DIM),
                       fox_s.reshape(n_seq, n_new, 2, FOX_HEADS, HEAD_DIM), logf_s, win_new))

    stack = lambda outs, i: jnp.stack([o[i] for o in outs], axis=0)
    return (xp, xs.reshape(n_seq, n_new, d), stack(outs_p, 0), stack(outs_p, 1), stack(outs_p, 2), stack(outs_p, 3),
            stack(outs_s, 0), stack(outs_s, 1), stack(outs_s, 2), stack(outs_s, 3))
```
